```python
import jax
import jax.numpy as jnp
from jax import lax
import numpy as np

D_MODEL = 1024
BATCH = 4
SEQ = 8192
DEPTH = 4

CTX_LEN = 256
GRID_W = 64
HEAD_DIM = 64
NA_HEADS = 4
NA_KH = 8
NA_KW = 16
WA_HEADS = 4
WA_KV_HEADS = 2
WA_WINDOW = 128
WA_BLOCK = 128
MLA_HEADS = 4
MLA_Q_RANK = 256
MLA_KV_RANK = 128
MLA_NOPE = 32
MLA_ROPE = 32
MLA_V = 64
MLA_BLOCK = 128
GLA_HEADS = 4
GLA_DK = 32
GLA_DV = 64
GLA_GATE_RANK = 16
GLA_GATE_NORM = 16.0
GLA_CHUNK = 64
N_EXPERTS = 128
TOP_K = 8
D_EXPERT = 256
D_SHARED = 256
ROUTED_SCALE = 2.5
MOE_BLOCK = 128

ROPE_THETA = 10000.0
LN_EPS = 1e-5
RMS_EPS = 1e-6
DEEPNORM_ALPHA = (2 * DEPTH) ** 0.25
DEEPNORM_BETA = (8 * DEPTH) ** -0.25
MIX_WIDTH = NA_HEADS * HEAD_DIM + WA_HEADS * HEAD_DIM + MLA_HEADS * MLA_V + GLA_HEADS * GLA_DV
IN_SIZES = (
    NA_HEADS * HEAD_DIM, NA_HEADS * HEAD_DIM, NA_HEADS * HEAD_DIM,
    WA_HEADS * HEAD_DIM, WA_KV_HEADS * HEAD_DIM, WA_KV_HEADS * HEAD_DIM,
    MLA_Q_RANK, MLA_KV_RANK, MLA_ROPE,
    GLA_HEADS * GLA_DK, GLA_HEADS * GLA_DK, GLA_HEADS * GLA_DV,
    GLA_HEADS * GLA_DV, GLA_GATE_RANK, GLA_GATE_RANK,
)
IN_WIDTH = sum(IN_SIZES)

kernel_name = 'hybrid_dit_parallel_heads_moe'


def layer_norm(x):
    xf = x.astype(jnp.float32)
    mu = jnp.mean(xf, -1, keepdims=True)
    var = jnp.mean(jnp.square(xf - mu), -1, keepdims=True)
    return ((xf - mu) * lax.rsqrt(var + LN_EPS)).astype(x.dtype)


def rms_norm(x, g):
    xf = x.astype(jnp.float32)
    y = xf * lax.rsqrt(jnp.mean(xf * xf, -1, keepdims=True) + RMS_EPS)
    return y.astype(x.dtype) * g


def modulate(x, shift, scale):
    return layer_norm(x) * (1 + scale) + shift


def post_norm(z, g, b):
    return layer_norm(z) * g + b


def heads(a, n):
    return a.reshape(a.shape[:-1] + (n, a.shape[-1] // n))


def rope_1d(x, pos):
    half = x.shape[-1] // 2
    inv_freq = ROPE_THETA ** (-jnp.arange(half, dtype=jnp.float32) / half)
    ang = pos.astype(jnp.float32)[:, None, None] * inv_freq
    cos, sin = jnp.cos(ang), jnp.sin(ang)
    x1, x2 = x[..., :half].astype(jnp.float32), x[..., half:].astype(jnp.float32)
    return jnp.concatenate([x1 * cos - x2 * sin, x2 * cos + x1 * sin], -1).astype(x.dtype)


def rope_2d(x):
    t = jnp.arange(x.shape[1])
    d = x.shape[-1] // 2
    return jnp.concatenate([rope_1d(x[..., :d], t // GRID_W), rope_1d(x[..., d:], t % GRID_W)], -1)


def context_attention(qc, kc, vc, sink=None):
    s = jnp.einsum('bqhd,bkhd->bhqk', qc, kc).astype(jnp.float32) * qc.shape[-1] ** -0.5
    if sink is not None:
        s = jnp.concatenate([s, jnp.broadcast_to(sink.astype(jnp.float32)[None, :, None, None], s.shape[:-1] + (1,))], -1)
    p = jax.nn.softmax(s, axis=-1)[..., :kc.shape[1]].astype(vc.dtype)
    o = jnp.einsum('bhqk,bkhd->bqhd', p, vc)
    return o.reshape(o.shape[:2] + (-1,))


def neighbourhood_attention(q, k, v, kc, vc, rpb):
    B, T, H, d = q.shape
    rows = T // GRID_W
    kh = min(NA_KH, rows)
    scale = d ** -0.5

    def to_grid(a):
        return a.reshape(B, rows, GRID_W, H, d).transpose(0, 3, 1, 2, 4)

    kg, vg = to_grid(k), to_grid(v)
    q_rows = jnp.moveaxis(to_grid(q), 2, 0)
    r_idx = jnp.arange(rows)
    c_idx = jnp.arange(GRID_W)
    row_start = jnp.clip(r_idx - kh // 2, 0, rows - kh)
    col_keys = jnp.clip(c_idx - NA_KW // 2, 0, GRID_W - NA_KW)[:, None] + jnp.arange(NA_KW)
    col_off = col_keys - c_idx[:, None] + (NA_KW - 1)
    kc_t, vc_t = kc.transpose(0, 2, 1, 3), vc.transpose(0, 2, 1, 3)
    n_nb = kh * NA_KW

    def one_row(inp):
        r, rs, q_row = inp
        k_nb = lax.dynamic_slice_in_dim(kg, rs, kh, axis=2)[:, :, :, col_keys]
        v_nb = lax.dynamic_slice_in_dim(vg, rs, kh, axis=2)[:, :, :, col_keys]
        row_off = rs + jnp.arange(kh) - r + (NA_KH - 1)
        bias = rpb[:, row_off][:, :, col_off].transpose(0, 2, 1, 3)
        s_nb = jnp.einsum('bhcd,bhicjd->bhcij', q_row, k_nb).astype(jnp.float32) * scale + bias.astype(jnp.float32)
        s_ctx = jnp.einsum('bhcd,bhnd->bhcn', q_row, kc_t).astype(jnp.float32) * scale
        p = jax.nn.softmax(jnp.concatenate([s_nb.reshape(B, H, GRID_W, n_nb), s_ctx], -1), axis=-1).astype(v.dtype)
        return (jnp.einsum('bhcij,bhicjd->bhcd', p[..., :n_nb].reshape(B, H, GRID_W, kh, NA_KW), v_nb)
                + jnp.einsum('bhcn,bhnd->bhcd', p[..., n_nb:], vc_t))

    o = lax.map(one_row, (r_idx, row_start, q_rows))
    return o.transpose(1, 0, 3, 2, 4).reshape(B, T, H * d)


def window_attention(q, k, v, kc, vc, sink):
    B, T, H, d = q.shape
    hkv = k.shape[2]
    grp = H // hkv
    nb = T // WA_BLOCK
    band = WA_BLOCK + 2 * WA_WINDOW
    scale = d ** -0.5
    pad = ((0, 0), (WA_WINDOW, WA_WINDOW), (0, 0), (0, 0))
    kp, vp = jnp.pad(k, pad), jnp.pad(v, pad)
    qb = q.reshape(B, nb, WA_BLOCK, hkv, grp, d).transpose(1, 0, 2, 3, 4, 5)
    k_off = jnp.arange(band)
    rel = k_off[None, :] - WA_WINDOW - jnp.arange(WA_BLOCK)[:, None]
    sink_l = jnp.broadcast_to(sink.reshape(hkv, grp).astype(jnp.float32)[None, :, :, None, None], (B, hkv, grp, WA_BLOCK, 1))

    def one_block(inp):
        i, q_blk = inp
        start = i * WA_BLOCK
        k_blk = lax.dynamic_slice_in_dim(kp, start, band, axis=1)
        v_blk = lax.dynamic_slice_in_dim(vp, start, band, axis=1)
        kpos = start - WA_WINDOW + k_off
        valid = (jnp.abs(rel) <= WA_WINDOW) & ((kpos >= 0) & (kpos < T))[None, :]
        s = jnp.einsum('bqhgd,bkhd->bhgqk', q_blk, k_blk).astype(jnp.float32) * scale
        s = jnp.where(valid, s, -jnp.inf)
        s_ctx = jnp.einsum('bqhgd,bchd->bhgqc', q_blk, kc).astype(jnp.float32) * scale
        p = jax.nn.softmax(jnp.concatenate([s, s_ctx, sink_l], -1), axis=-1).astype(v.dtype)
        return (jnp.einsum('bhgqk,bkhd->bqhgd', p[..., :band], v_blk)
                + jnp.einsum('bhgqc,bchd->bqhgd', p[..., band:band + kc.shape[1]], vc))

    o = lax.map(one_block, (jnp.arange(nb), qb))
    return o.transpose(1, 0, 2, 3, 4, 5).reshape(B, T, H * d)


def mla_q(cq, g_q, w_uq, rotary):
    q = heads(rms_norm(cq, g_q) @ w_uq, MLA_HEADS)
    q_nope, q_rope = q[..., :MLA_NOPE], q[..., MLA_NOPE:]
    if rotary:
        q_rope = rope_2d(q_rope)
    return jnp.concatenate([q_nope, q_rope], -1)


def mla_kv(ckv, k_rope, g_kv, w_ukv, rotary):
    kv = heads(rms_norm(ckv, g_kv) @ w_ukv, MLA_HEADS)
    k_nope, v = kv[..., :MLA_NOPE], kv[..., MLA_NOPE:]
    kr = k_rope[:, :, None, :]
    if rotary:
        kr = rope_2d(kr)
    k = jnp.concatenate([k_nope, jnp.broadcast_to(kr, k_nope.shape[:-1] + (MLA_ROPE,))], -1)
    return k, v


def mla_attention(q, k, v, kc, vc):
    B, T, H, dqk = q.shape
    nb = T // MLA_BLOCK
    scale = dqk ** -0.5
    k_all = jnp.concatenate([k, kc], axis=1)
    v_all = jnp.concatenate([v, vc], axis=1)
    qb = q.reshape(B, nb, MLA_BLOCK, H, dqk).transpose(1, 0, 2, 3, 4)

    def one_block(q_blk):
        s = jnp.einsum('bqhd,bkhd->bhqk', q_blk, k_all).astype(jnp.float32) * scale
        p = jax.nn.softmax(s, axis=-1).astype(v.dtype)
        return jnp.einsum('bhqk,bkhd->bqhd', p, v_all)

    o = lax.map(one_block, qb)
    return o.transpose(1, 0, 2, 3, 4).reshape(B, T, H * v.shape[-1])


def gla_chunked(q, k, v, g, s0, with_output):
    B, H, T, dk = q.shape
    n = T // GLA_CHUNK

    def ch(a):
        return a.reshape(B, H, n, GLA_CHUNK, a.shape[-1])

    q, k, v, g = ch(q), ch(k), ch(v), ch(g)
    b = jnp.cumsum(g, axis=3)
    b_end = b[:, :, :, -1]
    u = jnp.einsum('bhnck,bhncv->bhnkv', k * jnp.exp(b_end[:, :, :, None] - b), v)

    def step(s, inp):
        dec, du = inp
        return s * dec[..., None] + du, s

    s_final, s_before = lax.scan(step, s0, (jnp.moveaxis(jnp.exp(b_end), 2, 0), jnp.moveaxis(u, 2, 0)))
    if not with_output:
        return None, s_final
    s_before = jnp.moveaxis(s_before, 0, 2)
    q_dec = q * jnp.exp(b)
    att = jnp.einsum('bhnik,bhnjk->bhnij', q_dec, k * jnp.exp(-b))
    att = jnp.where(jnp.tril(jnp.ones((GLA_CHUNK, GLA_CHUNK), dtype=bool)), att, 0.0)
    o = jnp.einsum('bhnij,bhnjv->bhniv', att, v) + jnp.einsum('bhnik,bhnkv->bhniv', q_dec, s_before)
    return o.reshape(B, H, T, v.shape[-1]), s_final


def gla_mixer(lat, ctx, w_gf, b_gf, w_gb, b_gb, g_norm, ctx_out):
    def prep(q, k, v, zf, zb):
        def f(a, dh):
            return heads(a, GLA_HEADS).transpose(0, 2, 1, 3).astype(jnp.float32)
        gf = jax.nn.log_sigmoid(zf @ w_gf + b_gf) / GLA_GATE_NORM
        gb = jax.nn.log_sigmoid(zb @ w_gb + b_gb) / GLA_GATE_NORM
        return f(q, GLA_DK) * GLA_DK ** -0.5, f(k, GLA_DK), f(v, GLA_DV), f(gf, GLA_DK), f(gb, GLA_DK)

    q, k, v, gf, gb = prep(lat[0], lat[1], lat[2], lat[4], lat[5])
    qc, kc, vc, gfc, gbc = prep(ctx[0], ctx[1], ctx[2], ctx[4], ctx[5])
    B = q.shape[0]
    s0 = jnp.zeros((B, GLA_HEADS, GLA_DK, GLA_DV), jnp.float32)

    def flip(a):
        return jnp.flip(a, axis=2)

    oc_f, sc_f = gla_chunked(qc, kc, vc, gfc, s0, ctx_out)
    oc_b, sc_b = gla_chunked(flip(qc), flip(kc), flip(vc), flip(gbc), s0, ctx_out)
    o_f, _ = gla_chunked(q, k, v, gf, sc_f, True)
    o_b, _ = gla_chunked(flip(q), flip(k), flip(v), flip(gb), sc_b, True)

    def finish(of, ob, r):
        o = rms_norm((of + flip(ob)).transpose(0, 2, 1, 3), g_norm).astype(r.dtype)
        return o.reshape(r.shape[:-1] + (-1,)) * jax.nn.silu(r)

    o_lat = finish(o_f, o_b, lat[3])
    o_ctx = finish(oc_f, oc_b, ctx[3]) if ctx_out else None
    return o_lat, o_ctx


def moe_ffn(h, router_w, router_bias, w1, w3, w2, sw1, sw3, sw2):
    N, D = h.shape
    E = w1.shape[0]
    scores = jax.nn.sigmoid((h @ router_w).astype(jnp.float32))
    _, sel = lax.top_k(scores + router_bias.astype(jnp.float32), TOP_K)
    s_sel = jnp.take_along_axis(scores, sel, axis=-1)
    gates = ROUTED_SCALE * s_sel / jnp.sum(s_sel, -1, keepdims=True)
    e_flat = sel.reshape(-1)
    tok_flat = jnp.repeat(jnp.arange(N, dtype=jnp.int32), TOP_K)
    g_flat = gates.reshape(-1).astype(h.dtype)
    order = jnp.argsort(e_flat)
    e_s, tok_s, g_s = e_flat[order], tok_flat[order], g_flat[order]
    counts = jnp.bincount(e_flat, length=E)
    padded = (counts + MOE_BLOCK - 1) // MOE_BLOCK * MOE_BLOCK
    pad_end = jnp.cumsum(padded)
    pad_start = pad_end - padded
    grp_start = jnp.cumsum(counts) - counts
    dest = pad_start[e_s] + jnp.arange(e_s.shape[0]) - grp_start[e_s]
    n_blocks = -(-(N * TOP_K) // MOE_BLOCK) + E
    buf_tok = jnp.full((n_blocks * MOE_BLOCK,), N, jnp.int32).at[dest].set(tok_s)
    buf_gate = jnp.zeros((n_blocks * MOE_BLOCK,), h.dtype).at[dest].set(g_s)
    blk_exp = jnp.minimum(jnp.searchsorted(pad_end, jnp.arange(n_blocks) * MOE_BLOCK, side='right'), E - 1)
    h_pad = jnp.concatenate([h, jnp.zeros((1, D), h.dtype)], 0)

    def expert_block(acc, inp):
        e, tok, g = inp
        xb = h_pad[tok]
        y = (jax.nn.silu(xb @ w1[e]) * (xb @ w3[e])) @ w2[e]
        return acc.at[tok].add(y * g[:, None]), None

    acc, _ = lax.scan(expert_block, jnp.zeros((N + 1, D), h.dtype),
                      (blk_exp, buf_tok.reshape(n_blocks, MOE_BLOCK), buf_gate.reshape(n_blocks, MOE_BLOCK)))
    shared = (jax.nn.silu(h @ sw1) * (h @ sw3)) @ sw2
    return acc[:N] + shared


def setup_inputs(seed: int = 0) -> dict:
    key = jax.random.key(seed)
    keys = iter(jax.random.split(key, 40))

    def nrm(shape, scale):
        return scale * jax.random.normal(next(keys), shape, jnp.float32)

    L, D, E, F = DEPTH, D_MODEL, N_EXPERTS, D_EXPERT
    return {
        'x': nrm((BATCH, SEQ, D), 1.0),
        'c': nrm((BATCH, D), 1.0),
        'ctx': nrm((BATCH, CTX_LEN, D), 1.0),
        'c_ctx': nrm((D,), 1.0),
        'w_ada': nrm((L, D, 6 * D), 0.5 * D ** -0.5),
        'b_ada': nrm((L, 6 * D), 0.02),
        'w_in': nrm((L, D, IN_WIDTH), D ** -0.5),
        'na_rpb': nrm((L, NA_HEADS, 2 * NA_KH - 1, 2 * NA_KW - 1), 0.3),
        'wa_sink': nrm((L, WA_HEADS), 1.0),
        'mla_g_q': 1.0 + nrm((L, MLA_Q_RANK), 0.02),
        'mla_g_kv': 1.0 + nrm((L, MLA_KV_RANK), 0.02),
        'mla_w_uq': nrm((L, MLA_Q_RANK, MLA_HEADS * (MLA_NOPE + MLA_ROPE)), MLA_Q_RANK ** -0.5),
        'mla_w_ukv': nrm((L, MLA_KV_RANK, MLA_HEADS * (MLA_NOPE + MLA_V)), MLA_KV_RANK ** -0.5),
        'gla_w_gf': nrm((L, GLA_GATE_RANK, GLA_HEADS * GLA_DK), GLA_GATE_RANK ** -0.5),
        'gla_b_gf': nrm((L, GLA_HEADS * GLA_DK), 0.5),
        'gla_w_gb': nrm((L, GLA_GATE_RANK, GLA_HEADS * GLA_DK), GLA_GATE_RANK ** -0.5),
        'gla_b_gb': nrm((L, GLA_HEADS * GLA_DK), 0.5),
        'gla_g_norm': 1.0 + nrm((L, GLA_DV), 0.02),
        'w_out': nrm((L, MIX_WIDTH, D), DEEPNORM_BETA * MIX_WIDTH ** -0.5),
        'ln1_g': 1.0 + nrm((L, D), 0.02),
        'ln1_b': nrm((L, D), 0.02),
        'ln2_g': 1.0 + nrm((L, D), 0.02),
        'ln2_b': nrm((L, D), 0.02),
        'router_w': nrm((L, D, E), D ** -0.5),
        'router_bias': nrm((L, E), 0.01),
        'exp_w1': nrm((L, E, D, F), D ** -0.5),
        'exp_w3': nrm((L, E, D, F), D ** -0.5),
        'exp_w2': nrm((L, E, F, D), DEEPNORM_BETA * F ** -0.5),
        'sh_w1': nrm((L, D, D_SHARED), D ** -0.5),
        'sh_w3': nrm((L, D, D_SHARED), D ** -0.5),
        'sh_w2': nrm((L, D_SHARED, D), DEEPNORM_BETA * D_SHARED ** -0.5),
    }


def reference(x, c, ctx, c_ctx, w_ada, b_ada, w_in, na_rpb, wa_sink, mla_g_q, mla_g_kv, mla_w_uq, mla_w_ukv,
              gla_w_gf, gla_b_gf, gla_w_gb, gla_b_gb, gla_g_norm, w_out, ln1_g, ln1_b, ln2_g, ln2_b,
              router_w, router_bias, exp_w1, exp_w3, exp_w2, sh_w1, sh_w3, sh_w2):
    B, T, D = x.shape
    C = ctx.shape[1]
    offs = np.cumsum(IN_SIZES)[:-1].tolist()
    grp = WA_HEADS // WA_KV_HEADS
    xc = ctx
    for l in range(DEPTH):
        last = l == DEPTH - 1
        mod = jnp.split((jax.nn.silu(c) @ w_ada[l] + b_ada[l])[:, None, :], 6, axis=-1)
        mod_c = jnp.split(jax.nn.silu(c_ctx) @ w_ada[l] + b_ada[l], 6, axis=-1)
        h = modulate(x, mod[0], mod[1])
        hc = modulate(xc, mod_c[0], mod_c[1])
        P = jnp.split(h @ w_in[l], offs, axis=-1)
        Pc = jnp.split(hc @ w_in[l], offs, axis=-1)

        kc_a, vc_a = heads(Pc[1], NA_HEADS), heads(Pc[2], NA_HEADS)
        o_a = neighbourhood_attention(heads(P[0], NA_HEADS), heads(P[1], NA_HEADS), heads(P[2], NA_HEADS),
                                      kc_a, vc_a, na_rpb[l])
        kc_b, vc_b = heads(Pc[4], WA_KV_HEADS), heads(Pc[5], WA_KV_HEADS)
        o_b = window_attention(rope_2d(heads(P[3], WA_HEADS)), rope_2d(heads(P[4], WA_KV_HEADS)),
                               heads(P[5], WA_KV_HEADS), kc_b, vc_b, wa_sink[l])
        k_c, v_c = mla_kv(P[7], P[8], mla_g_kv[l], mla_w_ukv[l], True)
        kc_c, vc_c = mla_kv(Pc[7], Pc[8], mla_g_kv[l], mla_w_ukv[l], False)
        o_c = mla_attention(mla_q(P[6], mla_g_q[l], mla_w_uq[l], True), k_c, v_c, kc_c, vc_c)
        o_d, oc_d = gla_mixer(P[9:15], Pc[9:15], gla_w_gf[l], gla_b_gf[l], gla_w_gb[l], gla_b_gb[l],
                              gla_g_norm[l], not last)

        o = jnp.concatenate([o_a, o_b, o_c, o_d], -1) @ w_out[l]
        x = post_norm(DEEPNORM_ALPHA * x + mod[2] * o, ln1_g[l], ln1_b[l])
        if not last:
            oc = jnp.concatenate([
                context_attention(heads(Pc[0], NA_HEADS), kc_a, vc_a),
                context_attention(heads(Pc[3], WA_HEADS), jnp.repeat(kc_b, grp, axis=2),
                                  jnp.repeat(vc_b, grp, axis=2), wa_sink[l]),
                context_attention(mla_q(Pc[6], mla_g_q[l], mla_w_uq[l], False), kc_c, vc_c),
                oc_d], -1) @ w_out[l]
            xc = post_norm(DEEPNORM_ALPHA * xc + mod_c[2] * oc, ln1_g[l], ln1_b[l])

        h = modulate(x, mod[3], mod[4])
        moe_w = (router_w[l], router_bias[l], exp_w1[l], exp_w3[l], exp_w2[l], sh_w1[l], sh_w3[l], sh_w2[l])
        if last:
            f = moe_ffn(h.reshape(B * T, D), *moe_w).reshape(B, T, D)
        else:
            hc = modulate(xc, mod_c[3], mod_c[4])
            f_all = moe_ffn(jnp.concatenate([h.reshape(B * T, D), hc.reshape(B * C, D)], 0), *moe_w)
            f = f_all[:B * T].reshape(B, T, D)
            xc = post_norm(DEEPNORM_ALPHA * xc + mod_c[5] * f_all[B * T:].reshape(B, C, D), ln2_g[l], ln2_b[l])
        x = post_norm(DEEPNORM_ALPHA * x + mod[5] * f, ln2_g[l], ln2_b[l])
    return x
```

```python
import functools

import numpy as np
import jax
import jax.numpy as jnp
from jax import lax
from jax.experimental import pallas as pl
from jax.experimental.pallas import tpu as pltpu

F32 = jnp.float32
BF16 = jnp.bfloat16
U32 = jnp.uint32
I32 = jnp.int32

GRID_W = 64
HEAD_DIM = 64
N_HEADS = 4
NA_KH, NA_KW = 8, 16
WA_WINDOW = 128
MLA_NOPE, MLA_ROPE = 32, 32
GLA_DK, GLA_DV = 32, 64
GLA_GATE_RANK = 16
GLA_GATE_NORM = 16.0
GLA_CHUNK = 64
TOP_K = 8
ROUTED_SCALE = 2.5
ROPE_THETA = 10000.0
LN_EPS = 1e-5
RMS_EPS = 1e-6
NEG = -1e30

VMEM_LIMIT = 56 * 1024 * 1024
TOK_TILE = 512
ATT_TQ = 512
MLA_TK = 1024
GLA_BLOCK = 256
MOE_BM = 256
GATHER_R = 256
CMB_TILE = 256

PW = 2688
_NT = (((1,), (1,)), ((), ()))


def _cparams(sem, vmem=VMEM_LIMIT):
    return pltpu.CompilerParams(dimension_semantics=sem, vmem_limit_bytes=vmem)


def _sigmoid(x):
    return 1.0 / (1.0 + jnp.exp(-x))


def _layer_norm(x):
    mu = jnp.mean(x, -1, keepdims=True)
    xc = x - mu
    var = jnp.mean(xc * xc, -1, keepdims=True)
    return xc * lax.rsqrt(var + LN_EPS)


def _rms(x):
    return x * lax.rsqrt(jnp.mean(x * x, -1, keepdims=True) + RMS_EPS)


def _bdot(a, b):
    return jnp.dot(a.astype(BF16), b.astype(BF16), preferred_element_type=F32)


def _bdot_nt(a, b):
    return lax.dot_general(a.astype(BF16), b.astype(BF16), _NT, preferred_element_type=F32)


def _split3(a):
    a1 = a.astype(BF16)
    r = a - a1.astype(F32)
    a2 = r.astype(BF16)
    a3 = (r - a2.astype(F32)).astype(BF16)
    return a1, a2, a3


def _mods_kernel(cc_ref, w_ref, b_ref, o_ref):
    cc = cc_ref[...]
    s = cc * _sigmoid(cc)
    o_ref[0] = _bdot(s, w_ref[0]) + b_ref[0]


def _mods(cc, w_ada, b_ada):
    L, D, W = w_ada.shape
    tn = 1536
    return pl.pallas_call(
        _mods_kernel,
        grid=(L, W // tn),
        in_specs=[pl.BlockSpec((8, D), lambda l, j: (0, 0)),
                  pl.BlockSpec((1, D, tn), lambda l, j: (l, 0, j)),
                  pl.BlockSpec((1, 1, tn), lambda l, j: (l, 0, j))],
        out_specs=pl.BlockSpec((1, 8, tn), lambda l, j: (l, 0, j)),
        out_shape=jax.ShapeDtypeStruct((L, 8, W), F32),
        compiler_params=_cparams(("arbitrary", "arbitrary")),
        name="adaln_mods",
    )(cc, w_ada, b_ada.reshape(L, 1, W))


def _rope(z, cos, sin, half):
    w = z.shape[-1]
    lane = lax.broadcasted_iota(I32, z.shape, 1)
    first = (lane % (2 * half)) < half
    partner = jnp.where(first, pltpu.roll(z, w - half, 1), pltpu.roll(z, half, 1))
    return z * cos + partner * sin


def _log_sigmoid(x):
    return jnp.minimum(x, 0.0) - jnp.log(1.0 + jnp.exp(-jnp.abs(x)))


def _inproj_kernel(x_ref, shift_ref, scale_ref, win_ref, cw_ref, sw_ref, cq_ref, sq_ref, ck_ref, sk_ref,
                   gq_ref, gkv_ref, wuq_ref, wkk_ref, wv_ref, wg_ref, bg_ref,
                   pa_ref, pb_ref, pc_ref, pd_ref):
    h = _layer_norm(x_ref[...]) * (1.0 + scale_ref[0]) + shift_ref[0]
    p = jnp.dot(h.astype(BF16), win_ref[...], preferred_element_type=F32)
    pa_ref[...] = p[:, 0:768].astype(BF16)
    cw, sw = cw_ref[...], sw_ref[...]
    pb_ref[:, 0:128] = _rope(p[:, 768:896], cw, sw, 16).astype(BF16)
    pb_ref[:, 128:256] = _rope(p[:, 896:1024], cw, sw, 16).astype(BF16)
    pb_ref[:, 256:384] = _rope(p[:, 1024:1152], cw, sw, 16).astype(BF16)
    pb_ref[:, 384:512] = p[:, 1152:1280].astype(BF16)
    cqn = _rms(p[:, 1280:1536]) * gq_ref[...]
    q = jnp.dot(cqn.astype(BF16), wuq_ref[...], preferred_element_type=F32)
    cq, sq = cq_ref[...], sq_ref[...]
    pc_ref[:, 0:128] = _rope(q[:, 0:128], cq, sq, 8).astype(BF16)
    pc_ref[:, 128:256] = _rope(q[:, 128:256], cq, sq, 8).astype(BF16)
    ckvn = (_rms(p[:, 1536:1664]) * gkv_ref[...]).astype(BF16)
    kr = _rope(p[:, 1664:1792], ck_ref[...], sk_ref[...], 8).astype(BF16)
    kin = jnp.concatenate([ckvn, kr], axis=-1)
    pc_ref[:, 256:512] = jnp.dot(kin, wkk_ref[...], preferred_element_type=F32).astype(BF16)
    pc_ref[:, 512:768] = jnp.dot(ckvn, wv_ref[...], preferred_element_type=F32).astype(BF16)
    pd_ref[:, 0:128] = p[:, 1792:1920] * (GLA_DK ** -0.5)
    pd_ref[:, 128:768] = p[:, 1920:2560]
    pre = jnp.dot(p[:, 2560:2688].astype(BF16), wg_ref[...], preferred_element_type=F32) + bg_ref[...]
    pd_ref[:, 768:1024] = _log_sigmoid(pre) * (1.0 / GLA_GATE_NORM)


def _inproj(xall, mods_l, win, tabs, gq, gkv, wuq, wkk, wv, wg, bg, *, n_lat, T):
    N, D = xall.shape
    tm = TOK_TILE
    nlat_t = n_lat // tm
    per_b = T // tm
    nb = n_lat // T

    def midx(k):
        return lambda i: (jnp.where(i < nlat_t, i // per_b, nb) * 6 + k, 0, 0)

    def tidx(i):
        return (jnp.where(i < nlat_t, i % per_b, per_b), 0)

    const = lambda i: (0, 0)
    tab_spec = pl.BlockSpec((tm, 128), tidx)
    in_specs = [pl.BlockSpec((tm, D), lambda i: (i, 0)),
                pl.BlockSpec((1, 1, D), midx(0)), pl.BlockSpec((1, 1, D), midx(1)),
                pl.BlockSpec((D, PW), const)] + [tab_spec] * 6 + [
                pl.BlockSpec((1, 256), const), pl.BlockSpec((1, 128), const),
                pl.BlockSpec((256, 256), const), pl.BlockSpec((256, 256), const),
                pl.BlockSpec((128, 256), const), pl.BlockSpec((128, 256), const),
                pl.BlockSpec((1, 256), const)]
    out_specs = [pl.BlockSpec((tm, 768), lambda i: (i, 0)), pl.BlockSpec((tm, 512), lambda i: (i, 0)),
                 pl.BlockSpec((tm, 768), lambda i: (i, 0)), pl.BlockSpec((tm, 1024), lambda i: (i, 0))]
    out_shape = [jax.ShapeDtypeStruct((N, 768), BF16), jax.ShapeDtypeStruct((N, 512), BF16),
                 jax.ShapeDtypeStruct((N, 768), BF16), jax.ShapeDtypeStruct((N, 1024), F32)]
    return pl.pallas_call(
        _inproj_kernel, grid=(N // tm,), in_specs=in_specs, out_specs=out_specs, out_shape=out_shape,
        compiler_params=_cparams(("arbitrary",)), name="inproj",
    )(xall, mods_l, mods_l, win, *tabs, gq, gkv, wuq, wkk, wv, wg, bg)


def _softmax_av(parts, extra_logit=None):
    m = functools.reduce(jnp.maximum, [jnp.max(s, -1, keepdims=True) for s, _ in parts])
    if extra_logit is not None:
        m = jnp.maximum(m, extra_logit)
    l = 0.0 if extra_logit is None else jnp.exp(extra_logit - m)
    o = None
    for s, v in parts:
        e = jnp.exp(s - m)
        l = l + jnp.sum(e, -1, keepdims=True)
        c = jnp.dot(e.astype(BF16), v, preferred_element_type=F32)
        o = c if o is None else o + c
    return o * (1.0 / l)


def _na_kernel(q_ref, k_ref, v_ref, kc_ref, vc_ref, bias_ref, o_ref, *, rows):
    j = pl.program_id(1)
    ws = pl.multiple_of(jnp.clip(8 * j - 4, 0, rows - 16) * GRID_W, 256)
    win = 16 * GRID_W
    outs = []
    for h in range(N_HEADS):
        sl = slice(HEAD_DIM * h, HEAD_DIM * (h + 1))
        q = q_ref[:, sl]
        s = lax.dot_general(q, k_ref[pl.ds(ws, win), sl], _NT, preferred_element_type=F32) + bias_ref[0, h]
        sc = lax.dot_general(q, kc_ref[:, sl], _NT, preferred_element_type=F32)
        outs.append(_softmax_av([(s, v_ref[pl.ds(ws, win), sl]), (sc, vc_ref[:, sl])]))
    o_ref[...] = jnp.concatenate(outs, -1).astype(BF16)


def _na_bias_tables(rpb, rows):
    nj = rows // 8
    ro_all, co_all, ok_all = [], [], []
    for j in (0, 1, nj - 1):
        ws = int(np.clip(8 * j - 4, 0, rows - 16))
        r = 8 * j + np.arange(8)[:, None, None, None]
        qc = np.arange(GRID_W)[None, :, None, None]
        kr = ws + np.arange(16)[None, None, :, None]
        kc = np.arange(GRID_W)[None, None, None, :]
        rs = np.clip(r - NA_KH // 2, 0, rows - NA_KH)
        cs = np.clip(qc - NA_KW // 2, 0, GRID_W - NA_KW)
        ok = (kr >= rs) & (kr < rs + NA_KH) & (kc >= cs) & (kc < cs + NA_KW)
        ro = np.clip(kr - r + (NA_KH - 1), 0, 2 * NA_KH - 2)
        co = np.clip(kc - qc + (NA_KW - 1), 0, 2 * NA_KW - 2)
        shp = (8, GRID_W, 16, GRID_W)
        ro_all.append(np.broadcast_to(ro, shp).reshape(512, 1024))
        co_all.append(np.broadcast_to(co, shp).reshape(512, 1024))
        ok_all.append(np.broadcast_to(ok, shp).reshape(512, 1024))
    ro, co, ok = np.stack(ro_all), np.stack(co_all), np.stack(ok_all)
    b = rpb[:, ro, co]
    b = jnp.where(ok[None], b, NEG)
    return jnp.transpose(b, (1, 0, 2, 3))


def _na_attention(pa, bias, *, B, T, C):
    rows = T // GRID_W
    nj = T // ATT_TQ
    cb = B * T // C
    kern = functools.partial(_na_kernel, rows=rows)
    return pl.pallas_call(
        kern, grid=(B, nj),
        in_specs=[pl.BlockSpec((ATT_TQ, 256), lambda b, j: (b * nj + j, 0)),
                  pl.BlockSpec((T, 256), lambda b, j: (b, 1)),
                  pl.BlockSpec((T, 256), lambda b, j: (b, 2)),
                  pl.BlockSpec((C, 256), lambda b, j: (cb + b, 1)),
                  pl.BlockSpec((C, 256), lambda b, j: (cb + b, 2)),
                  pl.BlockSpec((1, N_HEADS, 512, 1024),
                               lambda b, j: (jnp.where(j == 0, 0, jnp.where(j == nj - 1, 2, 1)), 0, 0, 0))],
        out_specs=pl.BlockSpec((ATT_TQ, 256), lambda b, j: (b * nj + j, 0)),
        out_shape=jax.ShapeDtypeStruct((B * T, 256), BF16),
        compiler_params=_cparams(("arbitrary", "arbitrary")), name="na_attention",
    )(pa, pa, pa, pa, pa, bias)


def _wa_kernel(sink_ref, q_ref, k_ref, v_ref, kc_ref, vc_ref, o_ref, *, T):
    i = pl.program_id(1)
    start = i * ATT_TQ
    win = ATT_TQ + 2 * WA_WINDOW
    ws = pl.multiple_of(jnp.clip(start - WA_WINDOW, 0, T - win), 128)
    rel = (lax.broadcasted_iota(I32, (ATT_TQ, win), 1) + (ws - start)) - lax.broadcasted_iota(I32, (ATT_TQ, win), 0)
    valid = jnp.abs(rel) <= WA_WINDOW
    outs = []
    for h in range(N_HEADS):
        sl = slice(HEAD_DIM * h, HEAD_DIM * (h + 1))
        g = h // 2
        ksl = slice(HEAD_DIM * g, HEAD_DIM * (g + 1))
        q = q_ref[:, sl]
        s = lax.dot_general(q, k_ref[pl.ds(ws, win), ksl], _NT, preferred_element_type=F32)
        s = jnp.where(valid, s, NEG)
        sc = lax.dot_general(q, kc_ref[:, ksl], _NT, preferred_element_type=F32)
        outs.append(_softmax_av([(s, v_ref[pl.ds(ws, win), ksl]), (sc, vc_ref[:, ksl])], extra_logit=sink_ref[h]))
    o_ref[...] = jnp.concatenate(outs, -1).astype(BF16)


def _wa_attention(pb, sink, *, B, T, C):
    nj = T // ATT_TQ
    cb = B * T // C
    kern = functools.partial(_wa_kernel, T=T)
    return pl.pallas_call(
        kern, grid=(B, nj),
        in_specs=[pl.BlockSpec(memory_space=pltpu.SMEM),
                  pl.BlockSpec((ATT_TQ, 256), lambda b, j: (b * nj + j, 0)),
                  pl.BlockSpec((T, 128), lambda b, j: (b, 2)),
                  pl.BlockSpec((T, 128), lambda b, j: (b, 3)),
                  pl.BlockSpec((C, 128), lambda b, j: (cb + b, 2)),
                  pl.BlockSpec((C, 128), lambda b, j: (cb + b, 3))],
        out_specs=pl.BlockSpec((ATT_TQ, 256), lambda b, j: (b * nj + j, 0)),
        out_shape=jax.ShapeDtypeStruct((B * T, 256), BF16),
        compiler_params=_cparams(("arbitrary", "arbitrary")), name="wa_attention",
    )(sink, pb, pb, pb, pb, pb)


def _mla_kernel(q_ref, k_ref, v_ref, kc_ref, vc_ref, o_ref, *, T):
    nk = T // MLA_TK
    outs = []
    for h in range(N_HEADS):
        sl = slice(HEAD_DIM * h, HEAD_DIM * (h + 1))
        q = q_ref[:, sl]
        sc = lax.dot_general(q, kc_ref[:, sl], _NT, preferred_element_type=F32)
        m0 = jnp.max(sc, -1, keepdims=True)
        e0 = jnp.exp(sc - m0)
        l0 = jnp.sum(e0, -1, keepdims=True)
        a0 = jnp.dot(e0.astype(BF16), vc_ref[:, sl], preferred_element_type=F32)

        def body(c, carry, sl=sl, q=q):
            m, l, acc = carry
            ks = pl.multiple_of(c * MLA_TK, MLA_TK)
            s = lax.dot_general(q, k_ref[pl.ds(ks, MLA_TK), sl], _NT, preferred_element_type=F32)
            mn = jnp.maximum(m, jnp.max(s, -1, keepdims=True))
            a = jnp.exp(m - mn)
            e = jnp.exp(s - mn)
            l = l * a + jnp.sum(e, -1, keepdims=True)
            acc = acc * a + jnp.dot(e.astype(BF16), v_ref[pl.ds(ks, MLA_TK), sl], preferred_element_type=F32)
            return mn, l, acc

        m, l, acc = lax.fori_loop(0, nk, body, (m0, l0, a0))
        outs.append(acc * (1.0 / l))
    o_ref[...] = jnp.concatenate(outs, -1).astype(BF16)


def _mla_attention(pc, *, B, T, C):
    nj = T // ATT_TQ
    cb = B * T // C
    kern = functools.partial(_mla_kernel, T=T)
    return pl.pallas_call(
        kern, grid=(B, nj),
        in_specs=[pl.BlockSpec((ATT_TQ, 256), lambda b, j: (b * nj + j, 0)),
                  pl.BlockSpec((T, 256), lambda b, j: (b, 1)),
                  pl.BlockSpec((T, 256), lambda b, j: (b, 2)),
                  pl.BlockSpec((C, 256), lambda b, j: (cb + b, 1)),
                  pl.BlockSpec((C, 256), lambda b, j: (cb + b, 2))],
        out_specs=pl.BlockSpec((ATT_TQ, 256), lambda b, j: (b * nj + j, 0)),
        out_shape=jax.ShapeDtypeStruct((B * T, 256), BF16),
        compiler_params=_cparams(("arbitrary", "arbitrary")), name="mla_attention",
    )(pc, pc, pc, pc, pc)


def _ctx_kernel(sink_ref, pa_ref, pb_ref, pc_ref, oa_ref, ob_ref, oc_ref):
    def attend(p_ref, koff, voff, kv_heads, out_ref, sink):
        outs = []
        for h in range(N_HEADS):
            g = h * kv_heads // N_HEADS
            q = p_ref[:, HEAD_DIM * h:HEAD_DIM * (h + 1)]
            k = p_ref[:, koff + HEAD_DIM * g:koff + HEAD_DIM * (g + 1)]
            v = p_ref[:, voff + HEAD_DIM * g:voff + HEAD_DIM * (g + 1)]
            s = lax.dot_general(q, k, _NT, preferred_element_type=F32)
            outs.append(_softmax_av([(s, v)], extra_logit=sink_ref[h] if sink else None))
        out_ref[...] = jnp.concatenate(outs, -1).astype(BF16)

    attend(pa_ref, 256, 512, 4, oa_ref, False)
    attend(pb_ref, 256, 384, 2, ob_ref, True)
    attend(pc_ref, 256, 512, 4, oc_ref, False)


def _ctx_attention(sink, pa, pb, pc, *, B, T, C):
    cb = B * T // C
    row = lambda b: (cb + b, 0)
    return pl.pallas_call(
        _ctx_kernel, grid=(B,),
        in_specs=[pl.BlockSpec(memory_space=pltpu.SMEM),
                  pl.BlockSpec((C, 768), row), pl.BlockSpec((C, 512), row), pl.BlockSpec((C, 768), row)],
        out_specs=[pl.BlockSpec((C, 256), lambda b: (b, 0))] * 3,
        out_shape=[jax.ShapeDtypeStruct((B * C, 256), BF16)] * 3,
        compiler_params=_cparams(("arbitrary",)), name="ctx_attention",
    )(sink, pa, pb, pc)


def _gla_dir(pd_ref, o_ref, st_ref, reverse):
    nchunk = GLA_BLOCK // GLA_CHUNK
    ri = lax.broadcasted_iota(I32, (GLA_CHUNK, GLA_CHUNK), 0)
    ci = lax.broadcasted_iota(I32, (GLA_CHUNK, GLA_CHUNK), 1)
    keep = (ci >= ri) if reverse else (ci <= ri)
    tri = jnp.where(keep, 1.0, 0.0).astype(BF16)
    gcol = 896 if reverse else 768
    order = range(nchunk - 1, -1, -1) if reverse else range(nchunk)
    for c in order:
        rs = slice(c * GLA_CHUNK, (c + 1) * GLA_CHUNK)
        g1, g2, g3 = _split3(pd_ref[rs, gcol:gcol + 128])
        b = (jnp.dot(tri, g1, preferred_element_type=F32) + jnp.dot(tri, g2, preferred_element_type=F32)
             + jnp.dot(tri, g3, preferred_element_type=F32))
        bend = b[0:1, :] if reverse else b[GLA_CHUNK - 1:GLA_CHUNK, :]
        q = pd_ref[rs, 0:128]
        k = pd_ref[rs, 128:256]
        qd = (q * jnp.exp(b)).astype(BF16)
        ki = (k * jnp.exp(-b)).astype(BF16)
        kd = (k * jnp.exp(bend - b)).astype(BF16)
        dec = jnp.exp(bend)
        outs = []
        for h in range(N_HEADS):
            ks = slice(GLA_DK * h, GLA_DK * (h + 1))
            v = pd_ref[rs, 256 + GLA_DV * h:256 + GLA_DV * (h + 1)]
            vb = v.astype(BF16)
            att = lax.dot_general(qd[:, ks], ki[:, ks], _NT, preferred_element_type=F32)
            att = jnp.where(keep, att, 0.0).astype(BF16)
            st = st_ref[h]
            o = (jnp.dot(att, vb, preferred_element_type=F32)
                 + lax.dot_general(qd[:, ks], st.astype(BF16), _NT, preferred_element_type=F32))
            ut = jnp.dot(v.T.astype(BF16), kd[:, ks], preferred_element_type=F32)
            st_ref[h] = st * dec[:, ks] + ut
            outs.append(o)
        o_ref[0, rs, :] = jnp.concatenate(outs, -1)


def _gla_kernel(pd_ref, o_ref, st_ref):
    d = pl.program_id(1)
    s = pl.program_id(2)

    @pl.when(s == 0)
    def _():
        st_ref[...] = jnp.zeros_like(st_ref)

    @pl.when(d == 0)
    def _():
        _gla_dir(pd_ref, o_ref, st_ref, False)

    @pl.when(d == 1)
    def _():
        _gla_dir(pd_ref, o_ref, st_ref, True)


def _gla(pd, *, B, T, C):
    N = pd.shape[0]
    assert C == GLA_BLOCK
    nlb = T // GLA_BLOCK
    cb = B * T // GLA_BLOCK

    def rb(b, d, s):
        lat = b * nlb + jnp.where(d == 0, s - 1, nlb - s)
        return jnp.where(s == 0, cb + b, lat)

    return pl.pallas_call(
        _gla_kernel, grid=(B, 2, nlb + 1),
        in_specs=[pl.BlockSpec((GLA_BLOCK, 1024), lambda b, d, s: (rb(b, d, s), 0))],
        out_specs=pl.BlockSpec((1, GLA_BLOCK, 256), lambda b, d, s: (d, rb(b, d, s), 0)),
        out_shape=jax.ShapeDtypeStruct((2, N, 256), F32),
        scratch_shapes=[pltpu.VMEM((N_HEADS, GLA_DV, GLA_DK), F32)],
        compiler_params=_cparams(("arbitrary", "arbitrary", "arbitrary")), name="gla_scan",
    )(pd)


def _pack_bf16_pairs(h):
    w = h.shape[-1] // 2
    bits = lax.bitcast_convert_type(h.astype(BF16).astype(F32), U32)
    return (bits[:, w:] & jnp.uint32(0xFFFF0000)) | (bits[:, :w] >> 16)


def _unpack_bf16_pairs(u):
    lo = lax.bitcast_convert_type(u << 16, F32)
    hi = lax.bitcast_convert_type(u & jnp.uint32(0xFFFF0000), F32)
    return jnp.concatenate([lo, hi], -1).astype(BF16)


def _outproj_kernel(x_ref, oa_ref, ob_ref, oc_ref, oac_ref, obc_ref, occ_ref, ofb_ref, r_ref, gn_ref, wout_ref,
                    g1_ref, sh2_ref, sc2_ref, lng_ref, lnb_ref, rw1_ref, rw2_ref, rb_ref,
                    x1_ref, hp_ref, sel_ref, gate_ref, *, alpha, nlat_t):
    is_lat = pl.program_id(0) < nlat_t
    pick = lambda lat_ref, ctx_ref: jnp.where(is_lat, lat_ref[...], ctx_ref[...])
    s = ofb_ref[0] + ofb_ref[1]
    parts = []
    for h in range(N_HEADS):
        parts.append(_rms(s[:, GLA_DV * h:GLA_DV * (h + 1)]))
    r = r_ref[...]
    od = jnp.concatenate(parts, -1) * gn_ref[...] * (r * _sigmoid(r))
    ocat = jnp.concatenate([pick(oa_ref, oac_ref), pick(ob_ref, obc_ref), pick(oc_ref, occ_ref), od.astype(BF16)], -1)
    o = jnp.dot(ocat, wout_ref[...], preferred_element_type=F32)
    x1 = _layer_norm(alpha * x_ref[...] + g1_ref[0] * o) * lng_ref[...] + lnb_ref[...]
    x1_ref[...] = x1
    h2 = _layer_norm(x1) * (1.0 + sc2_ref[0]) + sh2_ref[0]
    hp_ref[...] = _pack_bf16_pairs(h2)
    a1, a2, _ = _split3(h2)
    w1, w2 = rw1_ref[...], rw2_ref[...]
    logits = (jnp.dot(a1, w1, preferred_element_type=F32) + jnp.dot(a1, w2, preferred_element_type=F32)
              + jnp.dot(a2, w1, preferred_element_type=F32))
    scores = _sigmoid(logits)
    work = scores + rb_ref[...]
    tm, ne = scores.shape
    lane = lax.broadcasted_iota(I32, (tm, ne), 1).astype(F32)
    lane8 = lax.broadcasted_iota(I32, (tm, TOP_K), 1)
    sel = jnp.zeros((tm, TOP_K), F32)
    gsel = jnp.zeros((tm, TOP_K), F32)
    for k in range(TOP_K):
        m = jnp.max(work, -1, keepdims=True)
        idx = jnp.min(jnp.where(work == m, lane, float(ne)), -1, keepdims=True)
        hit = lane == idx
        sc = jnp.sum(jnp.where(hit, scores, 0.0), -1, keepdims=True)
        sel = jnp.where(lane8 == k, idx, sel)
        gsel = jnp.where(lane8 == k, sc, gsel)
        work = jnp.where(hit, -jnp.inf, work)
    sel_ref[...] = sel.astype(I32)
    gate_ref[...] = ROUTED_SCALE * gsel / jnp.sum(gsel, -1, keepdims=True)


def _outproj(xall, o_lat, o_ctx, ofb, pd, gn, wout, mods_l, lng, lnb, rw1, rw2, rb, *, n_lat, T, alpha):
    N, D = xall.shape
    tm = TOK_TILE
    nlat_t, per_b, nb = n_lat // tm, T // tm, n_lat // T

    def midx(k):
        return lambda i: (jnp.where(i < nlat_t, i // per_b, nb) * 6 + k, 0, 0)

    const = lambda i: (0, 0)
    row = lambda i: (i, 0)
    lat_row = lambda i: (jnp.minimum(i, nlat_t - 1), 0)
    ctx_row = lambda i: (jnp.maximum(i - nlat_t, 0), 0)
    E = rw1.shape[1]
    kern = functools.partial(_outproj_kernel, alpha=alpha, nlat_t=nlat_t)
    return pl.pallas_call(
        kern, grid=(N // tm,),
        in_specs=[pl.BlockSpec((tm, D), row)] + [pl.BlockSpec((tm, 256), lat_row)] * 3
                 + [pl.BlockSpec((tm, 256), ctx_row)] * 3 + [
                  pl.BlockSpec((2, tm, 256), lambda i: (0, i, 0)),
                  pl.BlockSpec((tm, 256), lambda i: (i, 2)), pl.BlockSpec((1, 256), const),
                  pl.BlockSpec((D, D), const),
                  pl.BlockSpec((1, 1, D), midx(2)), pl.BlockSpec((1, 1, D), midx(3)), pl.BlockSpec((1, 1, D), midx(4)),
                  pl.BlockSpec((1, D), const), pl.BlockSpec((1, D), const),
                  pl.BlockSpec((D, E), const), pl.BlockSpec((D, E), const), pl.BlockSpec((1, E), const)],
        out_specs=[pl.BlockSpec((tm, D), row), pl.BlockSpec((tm, D // 2), row),
                   pl.BlockSpec((tm, TOP_K), row), pl.BlockSpec((tm, TOP_K), row)],
        out_shape=[jax.ShapeDtypeStruct((N, D), F32), jax.ShapeDtypeStruct((N, D // 2), U32),
                   jax.ShapeDtypeStruct((N, TOP_K), I32), jax.ShapeDtypeStruct((N, TOP_K), F32)],
        compiler_params=_cparams(("arbitrary",)), name="outproj_router",
    )(xall, *o_lat, *o_ctx, ofb, pd, gn, wout, mods_l, mods_l, mods_l, lng, lnb, rw1, rw2, rb)


def _gather_kernel(idx_ref, src_ref, dst_ref, sem):
    base = pl.program_id(0) * GATHER_R

    def copy(r, t):
        return pltpu.make_async_copy(src_ref.at[pl.ds(t, 1)], dst_ref.at[pl.ds(base + r, 1)], sem)

    def issue(r, c):
        copy(r, idx_ref[0, 0, r]).start()
        return c

    def wait(r, c):
        copy(r, 0).wait()
        return c

    lax.fori_loop(0, GATHER_R, issue, 0, unroll=8)
    lax.fori_loop(0, GATHER_R, wait, 0, unroll=8)


def _gather_rows(src, idx):
    n = idx.shape[0]
    steps = n // GATHER_R
    return pl.pallas_call(
        _gather_kernel, grid=(steps,),
        in_specs=[pl.BlockSpec((1, 1, GATHER_R), lambda i: (i, 0, 0), memory_space=pltpu.SMEM),
                  pl.BlockSpec(memory_space=pl.ANY)],
        out_specs=pl.BlockSpec(memory_space=pl.ANY),
        out_shape=jax.ShapeDtypeStruct((n, src.shape[1]), src.dtype),
        scratch_shapes=[pltpu.SemaphoreType.DMA(())],
        compiler_params=_cparams(("arbitrary",)), name="gather_rows",
    )(idx.reshape(steps, 1, GATHER_R), src)


def _swiglu(xb, w1, w3, w2):
    a = jnp.dot(xb, w1, preferred_element_type=F32)
    b = jnp.dot(xb, w3, preferred_element_type=F32)
    return jnp.dot((a * _sigmoid(a) * b).astype(BF16), w2, preferred_element_type=F32)


def _expert_kernel(be_ref, nu_ref, xs_ref, w1_ref, w3_ref, w2_ref, y_ref):
    del be_ref
    i = pl.program_id(0)

    @pl.when(i < nu_ref[0])
    def _():
        xb = _unpack_bf16_pairs(xs_ref[...])
        y_ref[...] = _swiglu(xb, w1_ref[0].astype(BF16), w3_ref[0].astype(BF16), w2_ref[0].astype(BF16))

    @pl.when(i >= nu_ref[0])
    def _():
        y_ref[...] = jnp.zeros_like(y_ref)


def _experts(xs, blk_exp, n_used, w1, w3, w2):
    rows, half = xs.shape
    E, D, F = w1.shape
    nblk = rows // MOE_BM
    grid_spec = pltpu.PrefetchScalarGridSpec(
        num_scalar_prefetch=2, grid=(nblk,),
        in_specs=[pl.BlockSpec((MOE_BM, half), lambda i, be, nu: (i, 0)),
                  pl.BlockSpec((1, D, F), lambda i, be, nu: (be[i], 0, 0)),
                  pl.BlockSpec((1, D, F), lambda i, be, nu: (be[i], 0, 0)),
                  pl.BlockSpec((1, F, D), lambda i, be, nu: (be[i], 0, 0))],
        out_specs=pl.BlockSpec((MOE_BM, D), lambda i, be, nu: (i, 0)))
    return pl.pallas_call(
        _expert_kernel, grid_spec=grid_spec,
        out_shape=jax.ShapeDtypeStruct((rows, D), F32),
        compiler_params=_cparams(("arbitrary",)), name="expert_ffn",
    )(blk_exp, n_used, xs, w1, w3, w2)


def _combine_kernel(yg_ref, gate_ref, hp_ref, sw1_ref, sw3_ref, sw2_ref, x_ref, g2_ref, lng_ref, lnb_ref,
                    o_ref, *, alpha):
    f = _swiglu(_unpack_bf16_pairs(hp_ref[...]), sw1_ref[...], sw3_ref[...], sw2_ref[...])
    gates = gate_ref[...]
    for k in range(TOP_K):
        f = f + gates[:, k:k + 1] * yg_ref[k]
    o_ref[...] = _layer_norm(alpha * x_ref[...] + g2_ref[0] * f) * lng_ref[...] + lnb_ref[...]


def _combine(yg, gates, hp, sw1, sw3, sw2, x1, mods_l, lng, lnb, *, n_lat, T, alpha):
    N, D = x1.shape
    tm = CMB_TILE
    nlat_t, per_b, nb = n_lat // tm, T // tm, n_lat // T
    F = sw1.shape[1]
    const = lambda i: (0, 0)
    row = lambda i: (i, 0)
    kern = functools.partial(_combine_kernel, alpha=alpha)
    return pl.pallas_call(
        kern, grid=(N // tm,),
        in_specs=[pl.BlockSpec((TOP_K, tm, D), lambda i: (0, i, 0)), pl.BlockSpec((tm, TOP_K), row),
                  pl.BlockSpec((tm, D // 2), row),
                  pl.BlockSpec((D, F), const), pl.BlockSpec((D, F), const), pl.BlockSpec((F, D), const),
                  pl.BlockSpec((tm, D), row),
                  pl.BlockSpec((1, 1, D), lambda i: (jnp.where(i < nlat_t, i // per_b, nb) * 6 + 5, 0, 0)),
                  pl.BlockSpec((1, D), const), pl.BlockSpec((1, D), const)],
        out_specs=pl.BlockSpec((tm, D), row),
        out_shape=jax.ShapeDtypeStruct((N, D), F32),
        compiler_params=_cparams(("arbitrary",)), name="moe_combine",
    )(yg, gates, hp, sw1, sw3, sw2, x1, mods_l, lng, lnb)


def _dispatch_plan(sel, n_experts):
    N = sel.shape[0]
    e_flat = sel.reshape(-1)
    n = e_flat.shape[0]
    order = jnp.argsort(e_flat, stable=True).astype(I32)
    counts = jnp.zeros((n_experts,), I32).at[e_flat].add(1)
    padded = (counts + MOE_BM - 1) // MOE_BM * MOE_BM
    pad_end = jnp.cumsum(padded)
    pad_start = pad_end - padded
    grp_start = jnp.cumsum(counts) - counts
    e_s = e_flat[order]
    dest = (pad_start[e_s] + jnp.arange(n, dtype=I32) - grp_start[e_s]).astype(I32)
    n_blocks = -(-n // MOE_BM) + n_experts
    rows = n_blocks * MOE_BM
    row_tok = jnp.zeros((rows,), I32).at[dest].set(order // TOP_K)
    pos = jnp.zeros((n,), I32).at[order].set(dest)
    pos_kmajor = pos.reshape(N, TOP_K).T.reshape(-1)
    blk_exp = jnp.minimum(jnp.searchsorted(pad_end, jnp.arange(n_blocks, dtype=I32) * MOE_BM, side='right'),
                          n_experts - 1).astype(I32)
    n_used = (pad_end[-1] // MOE_BM).astype(I32).reshape(1)
    return row_tok, pos_kmajor, blk_exp, n_used


def _rope_tables(T, tile):
    t = jnp.arange(T)
    row = (t // GRID_W).astype(F32)[:, None]
    col = (t % GRID_W).astype(F32)[:, None]

    def group(half):
        inv = ROPE_THETA ** (-jnp.arange(half, dtype=F32) / half)
        ar, ac = row * inv, col * inv
        cos = jnp.concatenate([jnp.cos(ar), jnp.cos(ar), jnp.cos(ac), jnp.cos(ac)], -1)
        sin = jnp.concatenate([-jnp.sin(ar), jnp.sin(ar), -jnp.sin(ac), jnp.sin(ac)], -1)
        return cos, sin

    ones32, zeros32 = jnp.ones((T, 32), F32), jnp.zeros((T, 32), F32)
    cw, sw = group(16)
    cw, sw = jnp.tile(cw, (1, 2)), jnp.tile(sw, (1, 2))
    c8, s8 = group(8)
    cq = jnp.tile(jnp.concatenate([ones32, c8], -1), (1, 2))
    sq = jnp.tile(jnp.concatenate([zeros32, s8], -1), (1, 2))
    ck = jnp.concatenate([c8, ones32, ones32, ones32], -1)
    sk = jnp.concatenate([s8, zeros32, zeros32, zeros32], -1)

    def pad(a, ident):
        return jnp.concatenate([a, jnp.full((tile, 128), ident, F32)], 0)

    return [pad(cw, 1.0), pad(sw, 0.0), pad(cq, 1.0), pad(sq, 0.0), pad(ck, 1.0), pad(sk, 0.0)]


def _layer_weights(w_in, mla_w_uq, mla_w_ukv, gla_w_gf, gla_w_gb, gla_b_gf, gla_b_gb):
    D = w_in.shape[0]
    z = lambda n: jnp.zeros((D, n), F32)
    win = jnp.concatenate([
        w_in[:, 0:256] * 0.125, w_in[:, 256:768],
        w_in[:, 768:1024] * 0.125, w_in[:, 1024:1280],
        w_in[:, 1280:1696], z(96),
        w_in[:, 1696:2464],
        w_in[:, 2464:2496], z(96)], -1).astype(BF16)
    wuq = (mla_w_uq * 0.125).astype(BF16)
    ukv = mla_w_ukv.reshape(-1, N_HEADS, MLA_NOPE + HEAD_DIM)
    kv_rank = ukv.shape[0]
    wk = jnp.concatenate([ukv[:, :, :MLA_NOPE], jnp.zeros((kv_rank, N_HEADS, MLA_ROPE), F32)], -1)
    place = jnp.concatenate([jnp.zeros((MLA_ROPE, MLA_NOPE), F32), jnp.eye(MLA_ROPE, dtype=F32)], -1)
    place = jnp.concatenate([jnp.tile(place, (1, N_HEADS)), jnp.zeros((128 - MLA_ROPE, 256), F32)], 0)
    wkk = jnp.concatenate([wk.reshape(kv_rank, 256), place], 0).astype(BF16)
    wv = ukv[:, :, MLA_NOPE:].reshape(kv_rank, 256).astype(BF16)
    r = GLA_GATE_RANK
    wg = jnp.zeros((128, 256), F32).at[0:r, 0:128].set(gla_w_gf).at[r:2 * r, 128:256].set(gla_w_gb).astype(BF16)
    bg = jnp.concatenate([gla_b_gf, gla_b_gb])[None, :]
    return win, wuq, wkk, wv, wg, bg


def kernel(x, c, ctx, c_ctx, w_ada, b_ada, w_in, na_rpb, wa_sink, mla_g_q, mla_g_kv, mla_w_uq, mla_w_ukv,
           gla_w_gf, gla_b_gf, gla_w_gb, gla_b_gb, gla_g_norm, w_out, ln1_g, ln1_b, ln2_g, ln2_b,
           router_w, router_bias, exp_w1, exp_w3, exp_w2, sh_w1, sh_w3, sh_w2):
    B, T, D = x.shape
    C = ctx.shape[1]
    L = w_ada.shape[0]
    E = router_w.shape[-1]
    n_lat = B * T
    alpha = (2 * L) ** 0.25
    dims = dict(B=B, T=T, C=C)

    cc = jnp.zeros((8, D), F32).at[:B].set(c).at[B].set(c_ctx)
    mods = _mods(cc, w_ada, b_ada)
    tabs = _rope_tables(T, TOK_TILE)
    xall = jnp.concatenate([x.reshape(n_lat, D), ctx.reshape(B * C, D)], 0)

    for l in range(L):
        mods_l = mods[l].reshape(8 * 6, 1, D)
        win, wuq, wkk, wv, wg, bg = _layer_weights(w_in[l], mla_w_uq[l], mla_w_ukv[l], gla_w_gf[l], gla_w_gb[l],
                                                   gla_b_gf[l], gla_b_gb[l])
        pa, pb, pc, pd = _inproj(xall, mods_l, win, tabs, mla_g_q[l][None], mla_g_kv[l][None], wuq, wkk, wv, wg, bg,
                                 n_lat=n_lat, T=T)
        oa = _na_attention(pa, _na_bias_tables(na_rpb[l], T // GRID_W), **dims)
        ob = _wa_attention(pb, wa_sink[l], **dims)
        oc = _mla_attention(pc, **dims)
        o_ctx = _ctx_attention(wa_sink[l], pa, pb, pc, **dims)
        ofb = _gla(pd, **dims)

        rw1 = router_w[l].astype(BF16)
        rw2 = (router_w[l] - rw1.astype(F32)).astype(BF16)
        x1, hp, sel, gates = _outproj(
            xall, (oa, ob, oc), o_ctx, ofb, pd, jnp.tile(gla_g_norm[l], N_HEADS)[None], w_out[l].astype(BF16), mods_l,
            ln1_g[l][None], ln1_b[l][None], rw1, rw2, router_bias[l][None], n_lat=n_lat, T=T, alpha=alpha)

        row_tok, pos_kmajor, blk_exp, n_used = _dispatch_plan(sel, E)
        xs = _gather_rows(hp, row_tok)
        y = _experts(xs, blk_exp, n_used, exp_w1[l], exp_w3[l], exp_w2[l])
        yg = _gather_rows(y, pos_kmajor).reshape(TOP_K, xall.shape[0], D)
        xall = _combine(yg, gates, hp, sh_w1[l].astype(BF16), sh_w3[l].astype(BF16), sh_w2[l].astype(BF16),
                        x1, mods_l, ln2_g[l][None], ln2_b[l][None], n_lat=n_lat, T=T, alpha=alpha)

    return xall[:n_lat].reshape(B, T, D)
```

```python
import functools

import numpy as np
import jax
import jax.numpy as jnp
from jax import lax
from jax.experimental import pallas as pl
from jax.experimental.pallas import tpu as pltpu
from jax.experimental.pallas import tpu_sc as plsc

F32 = jnp.float32
BF16 = jnp.bfloat16
U32 = jnp.uint32
I32 = jnp.int32

GRID_W = 64
HEAD_DIM = 64
N_HEADS = 4
NA_KH, NA_KW = 8, 16
WA_WINDOW = 128
MLA_NOPE, MLA_ROPE = 32, 32
GLA_DK, GLA_DV = 32, 64
GLA_GATE_RANK = 16
GLA_GATE_NORM = 16.0
GLA_CHUNK = 64
TOP_K = 8
ROUTED_SCALE = 2.5
ROPE_THETA = 10000.0
LN_EPS = 1e-5
RMS_EPS = 1e-6
NEG = -1e30

VMEM_LIMIT = 56 * 1024 * 1024
TOK_TILE = 512
ATT_TQ = 512
MLA_TK = 1024
GLA_BLOCK = 256
MOE_BM = 256
CMB_TILE = 256
SC_CORES = 2
SC_WORKERS = 32
SC_DISPATCH_CHUNK = 32
SC_GATHER_CHUNK = 64

PW = 2688
_NT = (((1,), (1,)), ((), ()))


def _cparams(sem, vmem=VMEM_LIMIT):
    return pltpu.CompilerParams(dimension_semantics=sem, vmem_limit_bytes=vmem)


def _sigmoid(x):
    return 1.0 / (1.0 + jnp.exp(-x))


def _layer_norm(x):
    mu = jnp.mean(x, -1, keepdims=True)
    xc = x - mu
    var = jnp.mean(xc * xc, -1, keepdims=True)
    return xc * lax.rsqrt(var + LN_EPS)


def _rms(x):
    return x * lax.rsqrt(jnp.mean(x * x, -1, keepdims=True) + RMS_EPS)


def _bdot(a, b):
    return jnp.dot(a.astype(BF16), b.astype(BF16), preferred_element_type=F32)


def _bdot_nt(a, b):
    return lax.dot_general(a.astype(BF16), b.astype(BF16), _NT, preferred_element_type=F32)


def _split3(a):
    a1 = a.astype(BF16)
    r = a - a1.astype(F32)
    a2 = r.astype(BF16)
    a3 = (r - a2.astype(F32)).astype(BF16)
    return a1, a2, a3


def _mods_kernel(cc_ref, w_ref, b_ref, o_ref):
    cc = cc_ref[...]
    s = cc * _sigmoid(cc)
    o_ref[0] = _bdot(s, w_ref[0]) + b_ref[0]


def _mods(cc, w_ada, b_ada):
    L, D, W = w_ada.shape
    tn = 1536
    return pl.pallas_call(
        _mods_kernel,
        grid=(L, W // tn),
        in_specs=[pl.BlockSpec((8, D), lambda l, j: (0, 0)),
                  pl.BlockSpec((1, D, tn), lambda l, j: (l, 0, j)),
                  pl.BlockSpec((1, 1, tn), lambda l, j: (l, 0, j))],
        out_specs=pl.BlockSpec((1, 8, tn), lambda l, j: (l, 0, j)),
        out_shape=jax.ShapeDtypeStruct((L, 8, W), F32),
        compiler_params=_cparams(("arbitrary", "arbitrary")),
        name="adaln_mods",
    )(cc, w_ada, b_ada.reshape(L, 1, W))


def _rope(z, cos, sin, half):
    w = z.shape[-1]
    lane = lax.broadcasted_iota(I32, z.shape, 1)
    first = (lane % (2 * half)) < half
    partner = jnp.where(first, pltpu.roll(z, w - half, 1), pltpu.roll(z, half, 1))
    return z * cos + partner * sin


def _log_sigmoid(x):
    return jnp.minimum(x, 0.0) - jnp.log(1.0 + jnp.exp(-jnp.abs(x)))


def _inproj_kernel(x_ref, shift_ref, scale_ref, win_ref, cw_ref, sw_ref, cq_ref, sq_ref, ck_ref, sk_ref,
                   gq_ref, gkv_ref, wuq_ref, wkk_ref, wv_ref, wg_ref, bg_ref,
                   pa_ref, pb_ref, pc_ref, pd_ref):
    h = _layer_norm(x_ref[...]) * (1.0 + scale_ref[0]) + shift_ref[0]
    p = jnp.dot(h.astype(BF16), win_ref[...], preferred_element_type=F32)
    pa_ref[...] = p[:, 0:768].astype(BF16)
    cw, sw = cw_ref[...], sw_ref[...]
    pb_ref[:, 0:128] = _rope(p[:, 768:896], cw, sw, 16).astype(BF16)
    pb_ref[:, 128:256] = _rope(p[:, 896:1024], cw, sw, 16).astype(BF16)
    pb_ref[:, 256:384] = _rope(p[:, 1024:1152], cw, sw, 16).astype(BF16)
    pb_ref[:, 384:512] = p[:, 1152:1280].astype(BF16)
    cqn = _rms(p[:, 1280:1536]) * gq_ref[...]
    q = jnp.dot(cqn.astype(BF16), wuq_ref[...], preferred_element_type=F32)
    cq, sq = cq_ref[...], sq_ref[...]
    pc_ref[:, 0:128] = _rope(q[:, 0:128], cq, sq, 8).astype(BF16)
    pc_ref[:, 128:256] = _rope(q[:, 128:256], cq, sq, 8).astype(BF16)
    ckvn = (_rms(p[:, 1536:1664]) * gkv_ref[...]).astype(BF16)
    kr = _rope(p[:, 1664:1792], ck_ref[...], sk_ref[...], 8).astype(BF16)
    kin = jnp.concatenate([ckvn, kr], axis=-1)
    pc_ref[:, 256:512] = jnp.dot(kin, wkk_ref[...], preferred_element_type=F32).astype(BF16)
    pc_ref[:, 512:768] = jnp.dot(ckvn, wv_ref[...], preferred_element_type=F32).astype(BF16)
    pd_ref[:, 0:128] = p[:, 1792:1920] * (GLA_DK ** -0.5)
    pd_ref[:, 128:768] = p[:, 1920:2560]
    pre = jnp.dot(p[:, 2560:2688].astype(BF16), wg_ref[...], preferred_element_type=F32) + bg_ref[...]
    pd_ref[:, 768:1024] = _log_sigmoid(pre) * (1.0 / GLA_GATE_NORM)


def _inproj(xall, mods_l, win, tabs, gq, gkv, wuq, wkk, wv, wg, bg, *, n_lat, T):
    N, D = xall.shape
    tm = TOK_TILE
    nlat_t = n_lat // tm
    per_b = T // tm
    nb = n_lat // T

    def midx(k):
        return lambda i: (jnp.where(i < nlat_t, i // per_b, nb) * 6 + k, 0, 0)

    def tidx(i):
        return (jnp.where(i < nlat_t, i % per_b, per_b), 0)

    const = lambda i: (0, 0)
    tab_spec = pl.BlockSpec((tm, 128), tidx)
    in_specs = [pl.BlockSpec((tm, D), lambda i: (i, 0)),
                pl.BlockSpec((1, 1, D), midx(0)), pl.BlockSpec((1, 1, D), midx(1)),
                pl.BlockSpec((D, PW), const)] + [tab_spec] * 6 + [
                pl.BlockSpec((1, 256), const), pl.BlockSpec((1, 128), const),
                pl.BlockSpec((256, 256), const), pl.BlockSpec((256, 256), const),
                pl.BlockSpec((128, 256), const), pl.BlockSpec((128, 256), const),
                pl.BlockSpec((1, 256), const)]
    out_specs = [pl.BlockSpec((tm, 768), lambda i: (i, 0)), pl.BlockSpec((tm, 512), lambda i: (i, 0)),
                 pl.BlockSpec((tm, 768), lambda i: (i, 0)), pl.BlockSpec((tm, 1024), lambda i: (i, 0))]
    out_shape = [jax.ShapeDtypeStruct((N, 768), BF16), jax.ShapeDtypeStruct((N, 512), BF16),
                 jax.ShapeDtypeStruct((N, 768), BF16), jax.ShapeDtypeStruct((N, 1024), F32)]
    return pl.pallas_call(
        _inproj_kernel, grid=(N // tm,), in_specs=in_specs, out_specs=out_specs, out_shape=out_shape,
        compiler_params=_cparams(("arbitrary",)), name="inproj",
    )(xall, mods_l, mods_l, win, *tabs, gq, gkv, wuq, wkk, wv, wg, bg)


def _softmax_av(parts, extra_logit=None):
    m = functools.reduce(jnp.maximum, [jnp.max(s, -1, keepdims=True) for s, _ in parts])
    if extra_logit is not None:
        m = jnp.maximum(m, extra_logit)
    l = 0.0 if extra_logit is None else jnp.exp(extra_logit - m)
    o = None
    for s, v in parts:
        e = jnp.exp(s - m)
        l = l + jnp.sum(e, -1, keepdims=True)
        c = jnp.dot(e.astype(BF16), v, preferred_element_type=F32)
        o = c if o is None else o + c
    return o * (1.0 / l)


def _na_kernel(q_ref, k_ref, v_ref, kc_ref, vc_ref, bias_ref, o_ref, *, rows):
    j = pl.program_id(1)
    ws = pl.multiple_of(jnp.clip(8 * j - 4, 0, rows - 16) * GRID_W, 256)
    win = 16 * GRID_W
    outs = []
    for h in range(N_HEADS):
        sl = slice(HEAD_DIM * h, HEAD_DIM * (h + 1))
        q = q_ref[:, sl]
        s = lax.dot_general(q, k_ref[pl.ds(ws, win), sl], _NT, preferred_element_type=F32) + bias_ref[0, h]
        sc = lax.dot_general(q, kc_ref[:, sl], _NT, preferred_element_type=F32)
        outs.append(_softmax_av([(s, v_ref[pl.ds(ws, win), sl]), (sc, vc_ref[:, sl])]))
    o_ref[...] = jnp.concatenate(outs, -1).astype(BF16)


def _na_bias_tables(rpb, rows):
    nj = rows // 8
    ro_all, ok_all = [], []
    for j in (0, 1, nj - 1):
        ws = int(np.clip(8 * j - 4, 0, rows - 16))
        r = 8 * j + np.arange(8)[:, None, None, None]
        qc = np.arange(GRID_W)[None, :, None, None]
        kr = ws + np.arange(16)[None, None, :, None]
        kc = np.arange(GRID_W)[None, None, None, :]
        rs = np.clip(r - NA_KH // 2, 0, rows - NA_KH)
        cs = np.clip(qc - NA_KW // 2, 0, GRID_W - NA_KW)
        ok = (kr >= rs) & (kr < rs + NA_KH) & (kc >= cs) & (kc < cs + NA_KW)
        ro = np.clip(kr - r + (NA_KH - 1), 0, 2 * NA_KH - 2)
        shp = (8, GRID_W, 16, GRID_W)
        ro_all.append(np.broadcast_to(ro, shp).reshape(512, 1024))
        ok_all.append(np.broadcast_to(ok, shp).reshape(512, 1024))
    ok = np.stack(ok_all)
    hi = lax.Precision.HIGHEST
    co = np.clip(np.arange(GRID_W)[None, :] - np.arange(GRID_W)[:, None] + (NA_KW - 1), 0, 2 * NA_KW - 2)
    col_hot = (co[..., None] == np.arange(2 * NA_KW - 1)).astype(np.float32)
    cb = jnp.einsum('hdo,qko->hdqk', rpb, col_hot, precision=hi)
    ro = np.stack(ro_all).reshape(3, 8, GRID_W, 16, GRID_W)[:, :, 0, :, 0]
    row_hot = (ro[..., None] == np.arange(2 * NA_KH - 1)).astype(np.float32)
    b = jnp.einsum('vrsd,hdqk->vhrqsk', row_hot, cb, precision=hi)
    b = b.reshape(3, rpb.shape[0], 512, 1024)
    return jnp.where(ok[:, None], b, NEG)


def _na_attention(pa, bias, *, B, T, C):
    rows = T // GRID_W
    nj = T // ATT_TQ
    cb = B * T // C
    kern = functools.partial(_na_kernel, rows=rows)
    return pl.pallas_call(
        kern, grid=(B, nj),
        in_specs=[pl.BlockSpec((ATT_TQ, 256), lambda b, j: (b * nj + j, 0)),
                  pl.BlockSpec((T, 256), lambda b, j: (b, 1)),
                  pl.BlockSpec((T, 256), lambda b, j: (b, 2)),
                  pl.BlockSpec((C, 256), lambda b, j: (cb + b, 1)),
                  pl.BlockSpec((C, 256), lambda b, j: (cb + b, 2)),
                  pl.BlockSpec((1, N_HEADS, 512, 1024),
                               lambda b, j: (jnp.where(j == 0, 0, jnp.where(j == nj - 1, 2, 1)), 0, 0, 0))],
        out_specs=pl.BlockSpec((ATT_TQ, 256), lambda b, j: (b * nj + j, 0)),
        out_shape=jax.ShapeDtypeStruct((B * T, 256), BF16),
        compiler_params=_cparams(("arbitrary", "arbitrary")), name="na_attention",
    )(pa, pa, pa, pa, pa, bias)


def _wa_kernel(sink_ref, q_ref, k_ref, v_ref, kc_ref, vc_ref, o_ref, *, T):
    i = pl.program_id(1)
    start = i * ATT_TQ
    win = ATT_TQ + 2 * WA_WINDOW
    ws = pl.multiple_of(jnp.clip(start - WA_WINDOW, 0, T - win), 128)
    rel = (lax.broadcasted_iota(I32, (ATT_TQ, win), 1) + (ws - start)) - lax.broadcasted_iota(I32, (ATT_TQ, win), 0)
    valid = jnp.abs(rel) <= WA_WINDOW
    outs = []
    for h in range(N_HEADS):
        sl = slice(HEAD_DIM * h, HEAD_DIM * (h + 1))
        g = h // 2
        ksl = slice(HEAD_DIM * g, HEAD_DIM * (g + 1))
        q = q_ref[:, sl]
        s = lax.dot_general(q, k_ref[pl.ds(ws, win), ksl], _NT, preferred_element_type=F32)
        s = jnp.where(valid, s, NEG)
        sc = lax.dot_general(q, kc_ref[:, ksl], _NT, preferred_element_type=F32)
        outs.append(_softmax_av([(s, v_ref[pl.ds(ws, win), ksl]), (sc, vc_ref[:, ksl])], extra_logit=sink_ref[h]))
    o_ref[...] = jnp.concatenate(outs, -1).astype(BF16)


def _wa_attention(pb, sink, *, B, T, C):
    nj = T // ATT_TQ
    cb = B * T // C
    kern = functools.partial(_wa_kernel, T=T)
    return pl.pallas_call(
        kern, grid=(B, nj),
        in_specs=[pl.BlockSpec(memory_space=pltpu.SMEM),
                  pl.BlockSpec((ATT_TQ, 256), lambda b, j: (b * nj + j, 0)),
                  pl.BlockSpec((T, 128), lambda b, j: (b, 2)),
                  pl.BlockSpec((T, 128), lambda b, j: (b, 3)),
                  pl.BlockSpec((C, 128), lambda b, j: (cb + b, 2)),
                  pl.BlockSpec((C, 128), lambda b, j: (cb + b, 3))],
        out_specs=pl.BlockSpec((ATT_TQ, 256), lambda b, j: (b * nj + j, 0)),
        out_shape=jax.ShapeDtypeStruct((B * T, 256), BF16),
        compiler_params=_cparams(("arbitrary", "arbitrary")), name="wa_attention",
    )(sink, pb, pb, pb, pb, pb)


def _mla_kernel(q_ref, k_ref, v_ref, kc_ref, vc_ref, o_ref, *, T):
    nk = T // MLA_TK
    outs = []
    for h in range(N_HEADS):
        sl = slice(HEAD_DIM * h, HEAD_DIM * (h + 1))
        q = q_ref[:, sl]
        sc = lax.dot_general(q, kc_ref[:, sl], _NT, preferred_element_type=F32)
        m0 = jnp.max(sc, -1, keepdims=True)
        e0 = jnp.exp(sc - m0)
        l0 = jnp.sum(e0, -1, keepdims=True)
        a0 = jnp.dot(e0.astype(BF16), vc_ref[:, sl], preferred_element_type=F32)

        def body(c, carry, sl=sl, q=q):
            m, l, acc = carry
            ks = pl.multiple_of(c * MLA_TK, MLA_TK)
            s = lax.dot_general(q, k_ref[pl.ds(ks, MLA_TK), sl], _NT, preferred_element_type=F32)
            mn = jnp.maximum(m, jnp.max(s, -1, keepdims=True))
            a = jnp.exp(m - mn)
            e = jnp.exp(s - mn)
            l = l * a + jnp.sum(e, -1, keepdims=True)
            acc = acc * a + jnp.dot(e.astype(BF16), v_ref[pl.ds(ks, MLA_TK), sl], preferred_element_type=F32)
            return mn, l, acc

        m, l, acc = lax.fori_loop(0, nk, body, (m0, l0, a0))
        outs.append(acc * (1.0 / l))
    o_ref[...] = jnp.concatenate(outs, -1).astype(BF16)


def _mla_attention(pc, *, B, T, C):
    nj = T // ATT_TQ
    cb = B * T // C
    kern = functools.partial(_mla_kernel, T=T)
    return pl.pallas_call(
        kern, grid=(B, nj),
        in_specs=[pl.BlockSpec((ATT_TQ, 256), lambda b, j: (b * nj + j, 0)),
                  pl.BlockSpec((T, 256), lambda b, j: (b, 1)),
                  pl.BlockSpec((T, 256), lambda b, j: (b, 2)),
                  pl.BlockSpec((C, 256), lambda b, j: (cb + b, 1)),
                  pl.BlockSpec((C, 256), lambda b, j: (cb + b, 2))],
        out_specs=pl.BlockSpec((ATT_TQ, 256), lambda b, j: (b * nj + j, 0)),
        out_shape=jax.ShapeDtypeStruct((B * T, 256), BF16),
        compiler_params=_cparams(("arbitrary", "arbitrary")), name="mla_attention",
    )(pc, pc, pc, pc, pc)


def _ctx_kernel(sink_ref, pa_ref, pb_ref, pc_ref, oa_ref, ob_ref, oc_ref):
    def attend(p_ref, koff, voff, kv_heads, out_ref, sink):
        outs = []
        for h in range(N_HEADS):
            g = h * kv_heads // N_HEADS
            q = p_ref[:, HEAD_DIM * h:HEAD_DIM * (h + 1)]
            k = p_ref[:, koff + HEAD_DIM * g:koff + HEAD_DIM * (g + 1)]
            v = p_ref[:, voff + HEAD_DIM * g:voff + HEAD_DIM * (g + 1)]
            s = lax.dot_general(q, k, _NT, preferred_element_type=F32)
            outs.append(_softmax_av([(s, v)], extra_logit=sink_ref[h] if sink else None))
        out_ref[...] = jnp.concatenate(outs, -1).astype(BF16)

    attend(pa_ref, 256, 512, 4, oa_ref, False)
    attend(pb_ref, 256, 384, 2, ob_ref, True)
    attend(pc_ref, 256, 512, 4, oc_ref, False)


def _ctx_attention(sink, pa, pb, pc, *, B, T, C):
    cb = B * T // C
    row = lambda b: (cb + b, 0)
    return pl.pallas_call(
        _ctx_kernel, grid=(B,),
        in_specs=[pl.BlockSpec(memory_space=pltpu.SMEM),
                  pl.BlockSpec((C, 768), row), pl.BlockSpec((C, 512), row), pl.BlockSpec((C, 768), row)],
        out_specs=[pl.BlockSpec((C, 256), lambda b: (b, 0))] * 3,
        out_shape=[jax.ShapeDtypeStruct((B * C, 256), BF16)] * 3,
        compiler_params=_cparams(("arbitrary",)), name="ctx_attention",
    )(sink, pa, pb, pc)


def _gla_dir(pd_ref, o_ref, st_ref, reverse):
    nchunk = GLA_BLOCK // GLA_CHUNK
    ri = lax.broadcasted_iota(I32, (GLA_CHUNK, GLA_CHUNK), 0)
    ci = lax.broadcasted_iota(I32, (GLA_CHUNK, GLA_CHUNK), 1)
    keep = (ci >= ri) if reverse else (ci <= ri)
    tri = jnp.where(keep, 1.0, 0.0).astype(BF16)
    gcol = 896 if reverse else 768
    order = range(nchunk - 1, -1, -1) if reverse else range(nchunk)
    for c in order:
        rs = slice(c * GLA_CHUNK, (c + 1) * GLA_CHUNK)
        g1, g2, g3 = _split3(pd_ref[rs, gcol:gcol + 128])
        b = (jnp.dot(tri, g1, preferred_element_type=F32) + jnp.dot(tri, g2, preferred_element_type=F32)
             + jnp.dot(tri, g3, preferred_element_type=F32))
        bend = b[0:1, :] if reverse else b[GLA_CHUNK - 1:GLA_CHUNK, :]
        q = pd_ref[rs, 0:128]
        k = pd_ref[rs, 128:256]
        qd = (q * jnp.exp(b)).astype(BF16)
        ki = (k * jnp.exp(-b)).astype(BF16)
        kd = (k * jnp.exp(bend - b)).astype(BF16)
        dec = jnp.exp(bend)
        outs = []
        for h in range(N_HEADS):
            ks = slice(GLA_DK * h, GLA_DK * (h + 1))
            v = pd_ref[rs, 256 + GLA_DV * h:256 + GLA_DV * (h + 1)]
            vb = v.astype(BF16)
            att = lax.dot_general(qd[:, ks], ki[:, ks], _NT, preferred_element_type=F32)
            att = jnp.where(keep, att, 0.0).astype(BF16)
            st = st_ref[h]
            o = (jnp.dot(att, vb, preferred_element_type=F32)
                 + lax.dot_general(qd[:, ks], st.astype(BF16), _NT, preferred_element_type=F32))
            ut = jnp.dot(v.T.astype(BF16), kd[:, ks], preferred_element_type=F32)
            st_ref[h] = st * dec[:, ks] + ut
            outs.append(o)
        o_ref[0, rs, :] = jnp.concatenate(outs, -1)


def _gla_kernel(pd_ref, o_ref, st_ref):
    d = pl.program_id(1)
    s = pl.program_id(2)

    @pl.when(s == 0)
    def _():
        st_ref[...] = jnp.zeros_like(st_ref)

    @pl.when(d == 0)
    def _():
        _gla_dir(pd_ref, o_ref, st_ref, False)

    @pl.when(d == 1)
    def _():
        _gla_dir(pd_ref, o_ref, st_ref, True)


def _gla(pd, *, B, T, C):
    N = pd.shape[0]
    assert C == GLA_BLOCK
    nlb = T // GLA_BLOCK
    cb = B * T // GLA_BLOCK

    def rb(b, d, s):
        lat = b * nlb + jnp.where(d == 0, s - 1, nlb - s)
        return jnp.where(s == 0, cb + b, lat)

    return pl.pallas_call(
        _gla_kernel, grid=(B, 2, nlb + 1),
        in_specs=[pl.BlockSpec((GLA_BLOCK, 1024), lambda b, d, s: (rb(b, d, s), 0))],
        out_specs=pl.BlockSpec((1, GLA_BLOCK, 256), lambda b, d, s: (d, rb(b, d, s), 0)),
        out_shape=jax.ShapeDtypeStruct((2, N, 256), F32),
        scratch_shapes=[pltpu.VMEM((N_HEADS, GLA_DV, GLA_DK), F32)],
        compiler_params=_cparams(("arbitrary", "arbitrary", "arbitrary")), name="gla_scan",
    )(pd)


def _pack_bf16_pairs(h):
    w = h.shape[-1] // 2
    bits = lax.bitcast_convert_type(h.astype(BF16).astype(F32), U32)
    return (bits[:, w:] & jnp.uint32(0xFFFF0000)) | (bits[:, :w] >> 16)


def _unpack_bf16_pairs(u):
    lo = lax.bitcast_convert_type(u << 16, F32)
    hi = lax.bitcast_convert_type(u & jnp.uint32(0xFFFF0000), F32)
    return jnp.concatenate([lo, hi], -1).astype(BF16)


def _outproj_kernel(x_ref, oa_ref, ob_ref, oc_ref, oac_ref, obc_ref, occ_ref, ofb_ref, r_ref, gn_ref, wout_ref,
                    g1_ref, sh2_ref, sc2_ref, lng_ref, lnb_ref, rw1_ref, rw2_ref, rb_ref,
                    x1_ref, hp_ref, sel_ref, gate_ref, rank_ref, count_ref, cnt_ref, *, alpha, nlat_t):
    is_lat = pl.program_id(0) < nlat_t
    pick = lambda lat_ref, ctx_ref: jnp.where(is_lat, lat_ref[...], ctx_ref[...])
    s = ofb_ref[0] + ofb_ref[1]
    parts = []
    for h in range(N_HEADS):
        parts.append(_rms(s[:, GLA_DV * h:GLA_DV * (h + 1)]))
    r = r_ref[...]
    od = jnp.concatenate(parts, -1) * gn_ref[...] * (r * _sigmoid(r))
    ocat = jnp.concatenate([pick(oa_ref, oac_ref), pick(ob_ref, obc_ref), pick(oc_ref, occ_ref), od.astype(BF16)], -1)
    o = jnp.dot(ocat, wout_ref[...], preferred_element_type=F32)
    x1 = _layer_norm(alpha * x_ref[...] + g1_ref[0] * o) * lng_ref[...] + lnb_ref[...]
    x1_ref[...] = x1
    h2 = _layer_norm(x1) * (1.0 + sc2_ref[0]) + sh2_ref[0]
    hp_ref[...] = _pack_bf16_pairs(h2)
    a1, a2, _ = _split3(h2)
    w1, w2 = rw1_ref[...], rw2_ref[...]
    logits = (jnp.dot(a1, w1, preferred_element_type=F32) + jnp.dot(a1, w2, preferred_element_type=F32)
              + jnp.dot(a2, w1, preferred_element_type=F32))
    scores = _sigmoid(logits)
    work = scores + rb_ref[...]
    tm, ne = scores.shape
    lane = lax.broadcasted_iota(I32, (tm, ne), 1).astype(F32)
    lane8 = lax.broadcasted_iota(I32, (tm, TOP_K), 1)
    sel = jnp.zeros((tm, TOP_K), F32)
    gsel = jnp.zeros((tm, TOP_K), F32)
    chosen = jnp.zeros((tm, ne), F32)
    picks = []
    for k in range(TOP_K):
        m = jnp.max(work, -1, keepdims=True)
        idx = jnp.min(jnp.where(work == m, lane, float(ne)), -1, keepdims=True)
        hit = lane == idx
        picks.append(idx)
        sc = jnp.sum(jnp.where(hit, scores, 0.0), -1, keepdims=True)
        sel = jnp.where(lane8 == k, idx, sel)
        gsel = jnp.where(lane8 == k, sc, gsel)
        chosen = jnp.where(hit, 1.0, chosen)
        work = jnp.where(hit, -jnp.inf, work)
    sel_ref[...] = sel.astype(I32)
    gate_ref[...] = ROUTED_SCALE * gsel / jnp.sum(gsel, -1, keepdims=True)
    @pl.when(pl.program_id(0) == 0)
    def _():
        cnt_ref[...] = jnp.zeros_like(cnt_ref)

    ri = lax.broadcasted_iota(I32, (tm, tm), 0)
    ci = lax.broadcasted_iota(I32, (tm, tm), 1)
    before = jnp.where(ci < ri, 1.0, 0.0).astype(BF16)
    prefix = jnp.dot(before, chosen.astype(BF16), preferred_element_type=F32) + cnt_ref[...]
    rank = jnp.zeros((tm, TOP_K), F32)
    for k in range(TOP_K):
        rk = jnp.sum(jnp.where(lane == picks[k], prefix, 0.0), -1, keepdims=True)
        rank = jnp.where(lane8 == k, rk, rank)
    rank_ref[...] = rank.astype(I32)
    cnt_ref[...] = cnt_ref[...] + jnp.sum(chosen, 0, keepdims=True)
    count_ref[...] = cnt_ref[...].astype(I32)


def _outproj(xall, o_lat, o_ctx, ofb, pd, gn, wout, mods_l, lng, lnb, rw1, rw2, rb, *, n_lat, T, alpha):
    N, D = xall.shape
    tm = TOK_TILE
    nlat_t, per_b, nb = n_lat // tm, T // tm, n_lat // T

    def midx(k):
        return lambda i: (jnp.where(i < nlat_t, i // per_b, nb) * 6 + k, 0, 0)

    const = lambda i: (0, 0)
    row = lambda i: (i, 0)
    lat_row = lambda i: (jnp.minimum(i, nlat_t - 1), 0)
    ctx_row = lambda i: (jnp.maximum(i - nlat_t, 0), 0)
    E = rw1.shape[1]
    kern = functools.partial(_outproj_kernel, alpha=alpha, nlat_t=nlat_t)
    return pl.pallas_call(
        kern, grid=(N // tm,),
        in_specs=[pl.BlockSpec((tm, D), row)] + [pl.BlockSpec((tm, 256), lat_row)] * 3
                 + [pl.BlockSpec((tm, 256), ctx_row)] * 3 + [
                  pl.BlockSpec((2, tm, 256), lambda i: (0, i, 0)),
                  pl.BlockSpec((tm, 256), lambda i: (i, 2)), pl.BlockSpec((1, 256), const),
                  pl.BlockSpec((D, D), const),
                  pl.BlockSpec((1, 1, D), midx(2)), pl.BlockSpec((1, 1, D), midx(3)), pl.BlockSpec((1, 1, D), midx(4)),
                  pl.BlockSpec((1, D), const), pl.BlockSpec((1, D), const),
                  pl.BlockSpec((D, E), const), pl.BlockSpec((D, E), const), pl.BlockSpec((1, E), const)],
        out_specs=[pl.BlockSpec((tm, D), row), pl.BlockSpec((tm, D // 2), row),
                   pl.BlockSpec((tm, TOP_K), row), pl.BlockSpec((tm, TOP_K), row), pl.BlockSpec((tm, TOP_K), row),
                   pl.BlockSpec((1, E), const)],
        out_shape=[jax.ShapeDtypeStruct((N, D), F32), jax.ShapeDtypeStruct((N, D // 2), U32),
                   jax.ShapeDtypeStruct((N, TOP_K), I32), jax.ShapeDtypeStruct((N, TOP_K), F32),
                   jax.ShapeDtypeStruct((N, TOP_K), I32), jax.ShapeDtypeStruct((1, E), I32)],
        scratch_shapes=[pltpu.VMEM((1, E), F32)],
        compiler_params=_cparams(("arbitrary",)), name="outproj_router",
    )(xall, *o_lat, *o_ctx, ofb, pd, gn, wout, mods_l, mods_l, mods_l, lng, lnb, rw1, rw2, rb)


def _sc_mesh():
    return plsc.VectorSubcoreMesh(core_axis_name="c", subcore_axis_name="s")


def _sc_worker():
    return lax.axis_index("s") * SC_CORES + lax.axis_index("c")


def _sc_dispatch(hp, dest_k, rows):
    N, W = hp.shape
    tpw = N // SC_WORKERS
    ch = SC_DISPATCH_CHUNK
    assert N % SC_WORKERS == 0 and tpw % ch == 0
    nch = tpw // ch
    idx = dest_k.reshape(TOP_K, SC_WORKERS, nch, ch).transpose(1, 2, 0, 3)

    @functools.partial(
        pl.kernel, mesh=_sc_mesh(), out_type=jax.ShapeDtypeStruct((rows, W), hp.dtype),
        scratch_types=[pltpu.VMEM((nch, TOP_K, ch), I32), pltpu.VMEM((ch, W), hp.dtype), pltpu.SemaphoreType.DMA],
        compiler_params=pltpu.CompilerParams(use_tc_tiling_on_sc=True), name="sc_dispatch")
    def scatter(hp_hbm, idx_hbm, out_hbm, idx_v, rows_v, sem):
        wid = _sc_worker()
        base = wid * tpw
        pltpu.sync_copy(idx_hbm.at[wid], idx_v)

        @pl.loop(0, nch)
        def _(c):
            pltpu.sync_copy(hp_hbm.at[pl.ds(base + c * ch, ch)], rows_v)
            copies = [pltpu.async_copy(rows_v, out_hbm.at[idx_v.at[c, k]], sem) for k in range(TOP_K)]
            for cp in copies:
                cp.wait()

    return scatter(hp, idx)


def _sc_gather(src, idx):
    R, W = idx.shape[0], src.shape[1]
    per_w = R // SC_WORKERS
    ch = SC_GATHER_CHUNK
    assert R % SC_WORKERS == 0 and per_w % ch == 0
    nch = per_w // ch

    @functools.partial(
        pl.kernel, mesh=_sc_mesh(), out_type=jax.ShapeDtypeStruct((R, W), src.dtype),
        scratch_types=[pltpu.VMEM((nch, ch), I32), pltpu.VMEM((ch, W), src.dtype), pltpu.SemaphoreType.DMA],
        compiler_params=pltpu.CompilerParams(use_tc_tiling_on_sc=True), name="sc_gather")
    def gather(src_hbm, idx_hbm, out_hbm, idx_v, rows_v, sem):
        wid = _sc_worker()
        base = wid * per_w
        pltpu.sync_copy(idx_hbm.at[wid], idx_v)

        @pl.loop(0, nch)
        def _(c):
            pltpu.async_copy(src_hbm.at[idx_v.at[c]], rows_v, sem).wait()
            pltpu.sync_copy(rows_v, out_hbm.at[pl.ds(base + c * ch, ch)])

    return gather(src, idx.reshape(SC_WORKERS, nch, ch))


def _swiglu(xb, w1, w3, w2):
    a = jnp.dot(xb, w1, preferred_element_type=F32)
    b = jnp.dot(xb, w3, preferred_element_type=F32)
    return jnp.dot((a * _sigmoid(a) * b).astype(BF16), w2, preferred_element_type=F32)


def _expert_kernel(be_ref, nu_ref, xs_ref, w1_ref, w3_ref, w2_ref, y_ref):
    del be_ref
    i = pl.program_id(0)

    @pl.when(i < nu_ref[0])
    def _():
        xb = _unpack_bf16_pairs(xs_ref[...])
        y = _swiglu(xb, w1_ref[0].astype(BF16), w3_ref[0].astype(BF16), w2_ref[0].astype(BF16))
        y_ref[...] = _pack_bf16_pairs(y)

    @pl.when(i >= nu_ref[0])
    def _():
        y_ref[...] = jnp.zeros_like(y_ref)


def _experts(xs, blk_exp, n_used, w1, w3, w2):
    rows, half = xs.shape
    E, D, F = w1.shape
    nblk = rows // MOE_BM
    grid_spec = pltpu.PrefetchScalarGridSpec(
        num_scalar_prefetch=2, grid=(nblk,),
        in_specs=[pl.BlockSpec((MOE_BM, half), lambda i, be, nu: (i, 0)),
                  pl.BlockSpec((1, D, F), lambda i, be, nu: (be[i], 0, 0)),
                  pl.BlockSpec((1, D, F), lambda i, be, nu: (be[i], 0, 0)),
                  pl.BlockSpec((1, F, D), lambda i, be, nu: (be[i], 0, 0))],
        out_specs=pl.BlockSpec((MOE_BM, half), lambda i, be, nu: (i, 0)))
    return pl.pallas_call(
        _expert_kernel, grid_spec=grid_spec,
        out_shape=jax.ShapeDtypeStruct((rows, half), U32),
        compiler_params=_cparams(("arbitrary",)), name="expert_ffn",
    )(blk_exp, n_used, xs, w1, w3, w2)


def _combine_kernel(yg_ref, gate_ref, hp_ref, sw1_ref, sw3_ref, sw2_ref, x_ref, g2_ref, lng_ref, lnb_ref,
                    o_ref, *, alpha):
    f = _swiglu(_unpack_bf16_pairs(hp_ref[...]), sw1_ref[...], sw3_ref[...], sw2_ref[...])
    gates = gate_ref[...]
    lo = hi = 0.0
    for k in range(TOP_K):
        u = yg_ref[k]
        g = gates[:, k:k + 1]
        lo = lo + g * lax.bitcast_convert_type(u << 16, F32)
        hi = hi + g * lax.bitcast_convert_type(u & jnp.uint32(0xFFFF0000), F32)
    f = f + jnp.concatenate([lo, hi], -1)
    o_ref[...] = _layer_norm(alpha * x_ref[...] + g2_ref[0] * f) * lng_ref[...] + lnb_ref[...]


def _combine(yg, gates, hp, sw1, sw3, sw2, x1, mods_l, lng, lnb, *, n_lat, T, alpha):
    N, D = x1.shape
    tm = CMB_TILE
    nlat_t, per_b, nb = n_lat // tm, T // tm, n_lat // T
    F = sw1.shape[1]
    const = lambda i: (0, 0)
    row = lambda i: (i, 0)
    kern = functools.partial(_combine_kernel, alpha=alpha)
    return pl.pallas_call(
        kern, grid=(N // tm,),
        in_specs=[pl.BlockSpec((TOP_K, tm, D // 2), lambda i: (0, i, 0)), pl.BlockSpec((tm, TOP_K), row),
                  pl.BlockSpec((tm, D // 2), row),
                  pl.BlockSpec((D, F), const), pl.BlockSpec((D, F), const), pl.BlockSpec((F, D), const),
                  pl.BlockSpec((tm, D), row),
                  pl.BlockSpec((1, 1, D), lambda i: (jnp.where(i < nlat_t, i // per_b, nb) * 6 + 5, 0, 0)),
                  pl.BlockSpec((1, D), const), pl.BlockSpec((1, D), const)],
        out_specs=pl.BlockSpec((tm, D), row),
        out_shape=jax.ShapeDtypeStruct((N, D), F32),
        compiler_params=_cparams(("arbitrary",)), name="moe_combine",
    )(yg, gates, hp, sw1, sw3, sw2, x1, mods_l, lng, lnb)


def _dispatch_plan(sel, rank, counts):
    N = sel.shape[0]
    n_experts = counts.shape[0]
    padded = (counts + MOE_BM - 1) // MOE_BM * MOE_BM
    pad_end = jnp.cumsum(padded)
    pad_start = pad_end - padded
    start_of = jnp.sum(jnp.where(sel[..., None] == jnp.arange(n_experts, dtype=I32), pad_start, 0), -1)
    dest_k = (start_of + rank).astype(I32).T
    n_blocks = -(-(N * TOP_K) // MOE_BM) + n_experts
    blk_exp = jnp.minimum(jnp.searchsorted(pad_end, jnp.arange(n_blocks, dtype=I32) * MOE_BM, side='right'),
                          n_experts - 1).astype(I32)
    n_used = (pad_end[-1] // MOE_BM).astype(I32).reshape(1)
    return dest_k, blk_exp, n_used, n_blocks * MOE_BM


def _rope_tables(T, tile):
    t = jnp.arange(T)
    row = (t // GRID_W).astype(F32)[:, None]
    col = (t % GRID_W).astype(F32)[:, None]

    def group(half):
        inv = ROPE_THETA ** (-jnp.arange(half, dtype=F32) / half)
        ar, ac = row * inv, col * inv
        cos = jnp.concatenate([jnp.cos(ar), jnp.cos(ar), jnp.cos(ac), jnp.cos(ac)], -1)
        sin = jnp.concatenate([-jnp.sin(ar), jnp.sin(ar), -jnp.sin(ac), jnp.sin(ac)], -1)
        return cos, sin

    ones32, zeros32 = jnp.ones((T, 32), F32), jnp.zeros((T, 32), F32)
    cw, sw = group(16)
    cw, sw = jnp.tile(cw, (1, 2)), jnp.tile(sw, (1, 2))
    c8, s8 = group(8)
    cq = jnp.tile(jnp.concatenate([ones32, c8], -1), (1, 2))
    sq = jnp.tile(jnp.concatenate([zeros32, s8], -1), (1, 2))
    ck = jnp.concatenate([c8, ones32, ones32, ones32], -1)
    sk = jnp.concatenate([s8, zeros32, zeros32, zeros32], -1)

    def pad(a, ident):
        return jnp.concatenate([a, jnp.full((tile, 128), ident, F32)], 0)

    return [pad(cw, 1.0), pad(sw, 0.0), pad(cq, 1.0), pad(sq, 0.0), pad(ck, 1.0), pad(sk, 0.0)]


def _layer_weights(w_in, mla_w_uq, mla_w_ukv, gla_w_gf, gla_w_gb, gla_b_gf, gla_b_gb):
    D = w_in.shape[0]
    z = lambda n: jnp.zeros((D, n), F32)
    win = jnp.concatenate([
        w_in[:, 0:256] * 0.125, w_in[:, 256:768],
        w_in[:, 768:1024] * 0.125, w_in[:, 1024:1280],
        w_in[:, 1280:1696], z(96),
        w_in[:, 1696:2464],
        w_in[:, 2464:2496], z(96)], -1).astype(BF16)
    wuq = (mla_w_uq * 0.125).astype(BF16)
    ukv = mla_w_ukv.reshape(-1, N_HEADS, MLA_NOPE + HEAD_DIM)
    kv_rank = ukv.shape[0]
    wk = jnp.concatenate([ukv[:, :, :MLA_NOPE], jnp.zeros((kv_rank, N_HEADS, MLA_ROPE), F32)], -1)
    place = jnp.concatenate([jnp.zeros((MLA_ROPE, MLA_NOPE), F32), jnp.eye(MLA_ROPE, dtype=F32)], -1)
    place = jnp.concatenate([jnp.tile(place, (1, N_HEADS)), jnp.zeros((128 - MLA_ROPE, 256), F32)], 0)
    wkk = jnp.concatenate([wk.reshape(kv_rank, 256), place], 0).astype(BF16)
    wv = ukv[:, :, MLA_NOPE:].reshape(kv_rank, 256).astype(BF16)
    r = GLA_GATE_RANK
    wg = jnp.zeros((128, 256), F32).at[0:r, 0:128].set(gla_w_gf).at[r:2 * r, 128:256].set(gla_w_gb).astype(BF16)
    bg = jnp.concatenate([gla_b_gf, gla_b_gb])[None, :]
    return win, wuq, wkk, wv, wg, bg


def kernel(x, c, ctx, c_ctx, w_ada, b_ada, w_in, na_rpb, wa_sink, mla_g_q, mla_g_kv, mla_w_uq, mla_w_ukv,
           gla_w_gf, gla_b_gf, gla_w_gb, gla_b_gb, gla_g_norm, w_out, ln1_g, ln1_b, ln2_g, ln2_b,
           router_w, router_bias, exp_w1, exp_w3, exp_w2, sh_w1, sh_w3, sh_w2):
    B, T, D = x.shape
    C = ctx.shape[1]
    L = w_ada.shape[0]
    E = router_w.shape[-1]
    n_lat = B * T
    alpha = (2 * L) ** 0.25
    dims = dict(B=B, T=T, C=C)

    cc = jnp.zeros((8, D), F32).at[:B].set(c).at[B].set(c_ctx)
    mods = _mods(cc, w_ada, b_ada)
    tabs = _rope_tables(T, TOK_TILE)
    xall = jnp.concatenate([x.reshape(n_lat, D), ctx.reshape(B * C, D)], 0)

    for l in range(L):
        mods_l = mods[l].reshape(8 * 6, 1, D)
        win, wuq, wkk, wv, wg, bg = _layer_weights(w_in[l], mla_w_uq[l], mla_w_ukv[l], gla_w_gf[l], gla_w_gb[l],
                                                   gla_b_gf[l], gla_b_gb[l])
        pa, pb, pc, pd = _inproj(xall, mods_l, win, tabs, mla_g_q[l][None], mla_g_kv[l][None], wuq, wkk, wv, wg, bg,
                                 n_lat=n_lat, T=T)
        oa = _na_attention(pa, _na_bias_tables(na_rpb[l], T // GRID_W), **dims)
        ob = _wa_attention(pb, wa_sink[l], **dims)
        oc = _mla_attention(pc, **dims)
        o_ctx = _ctx_attention(wa_sink[l], pa, pb, pc, **dims)
        ofb = _gla(pd, **dims)

        rw1 = router_w[l].astype(BF16)
        rw2 = (router_w[l] - rw1.astype(F32)).astype(BF16)
        x1, hp, sel, gates, rank, counts = _outproj(
            xall, (oa, ob, oc), o_ctx, ofb, pd, jnp.tile(gla_g_norm[l], N_HEADS)[None], w_out[l].astype(BF16), mods_l,
            ln1_g[l][None], ln1_b[l][None], rw1, rw2, router_bias[l][None], n_lat=n_lat, T=T, alpha=alpha)

        dest_k, blk_exp, n_used, rows = _dispatch_plan(sel, rank, counts.reshape(E))
        xs = _sc_dispatch(hp, dest_k, rows)
        y = _experts(xs, blk_exp, n_used, exp_w1[l], exp_w3[l], exp_w2[l])
        yg = _sc_gather(y, dest_k.reshape(-1)).reshape(TOP_K, xall.shape[0], D // 2)
        xall = _combine(yg, gates, hp, sh_w1[l].astype(BF16), sh_w3[l].astype(BF16), sh_w2[l].astype(BF16),
                        x1, mods_l, ln2_g[l][None], ln2_b[l][None], n_lat=n_lat, T=T, alpha=alpha)

    return xall[:n_lat].reshape(B, T, D)
```

```python
import functools

import numpy as np
import jax
import jax.numpy as jnp
from jax import lax
from jax.experimental import pallas as pl
from jax.experimental.pallas import tpu as pltpu
from jax.experimental.pallas import tpu_sc as plsc

F32 = jnp.float32
BF16 = jnp.bfloat16
U32 = jnp.uint32
I32 = jnp.int32

GRID_W = 64
HEAD_DIM = 64
N_HEADS = 4
NA_KH, NA_KW = 8, 16
WA_WINDOW = 128
MLA_NOPE, MLA_ROPE = 32, 32
GLA_DK, GLA_DV = 32, 64
GLA_GATE_RANK = 16
GLA_GATE_NORM = 16.0
GLA_CHUNK = 64
TOP_K = 8
ROUTED_SCALE = 2.5
ROPE_THETA = 10000.0
LN_EPS = 1e-5
RMS_EPS = 1e-6
NEG = -1e30

VMEM_LIMIT = 56 * 1024 * 1024
TOK_TILE = 512
ATT_TQ = 512
MLA_TK = 1024
GLA_BLOCK = 256
MOE_BM = 512
CMB_TILE = 256
SC_CORES = 2
SC_WORKERS = 32
SC_DISPATCH_CHUNK = 32
SC_GATHER_CHUNK = 64

PW = 2688
_NT = (((1,), (1,)), ((), ()))


def _cparams(sem, vmem=VMEM_LIMIT):
    return pltpu.CompilerParams(dimension_semantics=sem, vmem_limit_bytes=vmem)


def _sigmoid(x):
    return 1.0 / (1.0 + jnp.exp(-x))


def _layer_norm(x):
    mu = jnp.mean(x, -1, keepdims=True)
    xc = x - mu
    var = jnp.mean(xc * xc, -1, keepdims=True)
    return xc * lax.rsqrt(var + LN_EPS)


def _rms(x):
    return x * lax.rsqrt(jnp.mean(x * x, -1, keepdims=True) + RMS_EPS)


def _bdot(a, b):
    return jnp.dot(a.astype(BF16), b.astype(BF16), preferred_element_type=F32)


def _bdot_nt(a, b):
    return lax.dot_general(a.astype(BF16), b.astype(BF16), _NT, preferred_element_type=F32)


def _split3(a):
    a1 = a.astype(BF16)
    r = a - a1.astype(F32)
    a2 = r.astype(BF16)
    a3 = (r - a2.astype(F32)).astype(BF16)
    return a1, a2, a3


def _mods_kernel(cc_ref, w_ref, b_ref, o_ref):
    cc = cc_ref[...]
    s = cc * _sigmoid(cc)
    o_ref[0] = _bdot(s, w_ref[0]) + b_ref[0]


def _mods(cc, w_ada, b_ada):
    L, D, W = w_ada.shape
    tn = 1536
    return pl.pallas_call(
        _mods_kernel,
        grid=(L, W // tn),
        in_specs=[pl.BlockSpec((8, D), lambda l, j: (0, 0)),
                  pl.BlockSpec((1, D, tn), lambda l, j: (l, 0, j)),
                  pl.BlockSpec((1, 1, tn), lambda l, j: (l, 0, j))],
        out_specs=pl.BlockSpec((1, 8, tn), lambda l, j: (l, 0, j)),
        out_shape=jax.ShapeDtypeStruct((L, 8, W), F32),
        compiler_params=_cparams(("arbitrary", "arbitrary")),
        name="adaln_mods",
    )(cc, w_ada, b_ada.reshape(L, 1, W))


def _rope(z, cos, sin, half):
    w = z.shape[-1]
    lane = lax.broadcasted_iota(I32, z.shape, 1)
    first = (lane % (2 * half)) < half
    partner = jnp.where(first, pltpu.roll(z, w - half, 1), pltpu.roll(z, half, 1))
    return z * cos + partner * sin


def _log_sigmoid(x):
    return jnp.minimum(x, 0.0) - jnp.log(1.0 + jnp.exp(-jnp.abs(x)))


def _inproj_kernel(x_ref, shift_ref, scale_ref, win_ref, cw_ref, sw_ref, cq_ref, sq_ref, ck_ref, sk_ref,
                   gq_ref, gkv_ref, wuq_ref, wkk_ref, wv_ref, wg_ref, bg_ref,
                   pa_ref, pb_ref, pc_ref, pd_ref):
    h = _layer_norm(x_ref[...]) * (1.0 + scale_ref[0]) + shift_ref[0]
    p = jnp.dot(h.astype(BF16), win_ref[...], preferred_element_type=F32)
    pa_ref[...] = p[:, 0:768].astype(BF16)
    cw, sw = cw_ref[...], sw_ref[...]
    pb_ref[:, 0:128] = _rope(p[:, 768:896], cw, sw, 16).astype(BF16)
    pb_ref[:, 128:256] = _rope(p[:, 896:1024], cw, sw, 16).astype(BF16)
    pb_ref[:, 256:384] = _rope(p[:, 1024:1152], cw, sw, 16).astype(BF16)
    pb_ref[:, 384:512] = p[:, 1152:1280].astype(BF16)
    cqn = _rms(p[:, 1280:1536]) * gq_ref[...]
    q = jnp.dot(cqn.astype(BF16), wuq_ref[...], preferred_element_type=F32)
    cq, sq = cq_ref[...], sq_ref[...]
    pc_ref[:, 0:128] = _rope(q[:, 0:128], cq, sq, 8).astype(BF16)
    pc_ref[:, 128:256] = _rope(q[:, 128:256], cq, sq, 8).astype(BF16)
    ckvn = (_rms(p[:, 1536:1664]) * gkv_ref[...]).astype(BF16)
    kr = _rope(p[:, 1664:1792], ck_ref[...], sk_ref[...], 8).astype(BF16)
    kin = jnp.concatenate([ckvn, kr], axis=-1)
    pc_ref[:, 256:512] = jnp.dot(kin, wkk_ref[...], preferred_element_type=F32).astype(BF16)
    pc_ref[:, 512:768] = jnp.dot(ckvn, wv_ref[...], preferred_element_type=F32).astype(BF16)
    pd_ref[:, 0:128] = p[:, 1792:1920] * (GLA_DK ** -0.5)
    pd_ref[:, 128:768] = p[:, 1920:2560]
    pre = jnp.dot(p[:, 2560:2688].astype(BF16), wg_ref[...], preferred_element_type=F32) + bg_ref[...]
    pd_ref[:, 768:1024] = _log_sigmoid(pre) * (1.0 / GLA_GATE_NORM)


def _inproj(xall, mods_l, win, tabs, gq, gkv, wuq, wkk, wv, wg, bg, *, n_lat, T):
    N, D = xall.shape
    tm = TOK_TILE
    nlat_t = n_lat // tm
    per_b = T // tm
    nb = n_lat // T

    def midx(k):
        return lambda i: (jnp.where(i < nlat_t, i // per_b, nb) * 6 + k, 0, 0)

    def tidx(i):
        return (jnp.where(i < nlat_t, i % per_b, per_b), 0)

    const = lambda i: (0, 0)
    tab_spec = pl.BlockSpec((tm, 128), tidx)
    in_specs = [pl.BlockSpec((tm, D), lambda i: (i, 0)),
                pl.BlockSpec((1, 1, D), midx(0)), pl.BlockSpec((1, 1, D), midx(1)),
                pl.BlockSpec((D, PW), const)] + [tab_spec] * 6 + [
                pl.BlockSpec((1, 256), const), pl.BlockSpec((1, 128), const),
                pl.BlockSpec((256, 256), const), pl.BlockSpec((256, 256), const),
                pl.BlockSpec((128, 256), const), pl.BlockSpec((128, 256), const),
                pl.BlockSpec((1, 256), const)]
    out_specs = [pl.BlockSpec((tm, 768), lambda i: (i, 0)), pl.BlockSpec((tm, 512), lambda i: (i, 0)),
                 pl.BlockSpec((tm, 768), lambda i: (i, 0)), pl.BlockSpec((tm, 1024), lambda i: (i, 0))]
    out_shape = [jax.ShapeDtypeStruct((N, 768), BF16), jax.ShapeDtypeStruct((N, 512), BF16),
                 jax.ShapeDtypeStruct((N, 768), BF16), jax.ShapeDtypeStruct((N, 1024), F32)]
    return pl.pallas_call(
        _inproj_kernel, grid=(N // tm,), in_specs=in_specs, out_specs=out_specs, out_shape=out_shape,
        compiler_params=_cparams(("arbitrary",)), name="inproj",
    )(xall, mods_l, mods_l, win, *tabs, gq, gkv, wuq, wkk, wv, wg, bg)


def _softmax_av(parts, extra_logit=None):
    m = functools.reduce(jnp.maximum, [jnp.max(s, -1, keepdims=True) for s, _ in parts])
    if extra_logit is not None:
        m = jnp.maximum(m, extra_logit)
    l = 0.0 if extra_logit is None else jnp.exp(extra_logit - m)
    o = None
    for s, v in parts:
        e = jnp.exp(s - m)
        l = l + jnp.sum(e, -1, keepdims=True)
        c = jnp.dot(e.astype(BF16), v, preferred_element_type=F32)
        o = c if o is None else o + c
    return o * (1.0 / l)


def _na_kernel(q_ref, k_ref, v_ref, kc_ref, vc_ref, bias_ref, o_ref, *, rows):
    j = pl.program_id(1)
    ws = pl.multiple_of(jnp.clip(8 * j - 4, 0, rows - 16) * GRID_W, 256)
    win = 16 * GRID_W
    outs = []
    for h in range(N_HEADS):
        sl = slice(HEAD_DIM * h, HEAD_DIM * (h + 1))
        q = q_ref[:, sl]
        s = lax.dot_general(q, k_ref[pl.ds(ws, win), sl], _NT, preferred_element_type=F32) + bias_ref[0, h]
        sc = lax.dot_general(q, kc_ref[:, sl], _NT, preferred_element_type=F32)
        outs.append(_softmax_av([(s, v_ref[pl.ds(ws, win), sl]), (sc, vc_ref[:, sl])]))
    o_ref[...] = jnp.concatenate(outs, -1).astype(BF16)


def _na_bias_tables(rpb, rows):
    nj = rows // 8
    ro_all, ok_all = [], []
    for j in (0, 1, nj - 1):
        ws = int(np.clip(8 * j - 4, 0, rows - 16))
        r = 8 * j + np.arange(8)[:, None, None, None]
        qc = np.arange(GRID_W)[None, :, None, None]
        kr = ws + np.arange(16)[None, None, :, None]
        kc = np.arange(GRID_W)[None, None, None, :]
        rs = np.clip(r - NA_KH // 2, 0, rows - NA_KH)
        cs = np.clip(qc - NA_KW // 2, 0, GRID_W - NA_KW)
        ok = (kr >= rs) & (kr < rs + NA_KH) & (kc >= cs) & (kc < cs + NA_KW)
        ro = np.clip(kr - r + (NA_KH - 1), 0, 2 * NA_KH - 2)
        shp = (8, GRID_W, 16, GRID_W)
        ro_all.append(np.broadcast_to(ro, shp).reshape(512, 1024))
        ok_all.append(np.broadcast_to(ok, shp).reshape(512, 1024))
    ok = np.stack(ok_all)
    hi = lax.Precision.HIGHEST
    co = np.clip(np.arange(GRID_W)[None, :] - np.arange(GRID_W)[:, None] + (NA_KW - 1), 0, 2 * NA_KW - 2)
    col_hot = (co[..., None] == np.arange(2 * NA_KW - 1)).astype(np.float32)
    cb = jnp.einsum('hdo,qko->hdqk', rpb, col_hot, precision=hi)
    ro = np.stack(ro_all).reshape(3, 8, GRID_W, 16, GRID_W)[:, :, 0, :, 0]
    row_hot = (ro[..., None] == np.arange(2 * NA_KH - 1)).astype(np.float32)
    b = jnp.einsum('vrsd,hdqk->vhrqsk', row_hot, cb, precision=hi)
    b = b.reshape(3, rpb.shape[0], 512, 1024)
    return jnp.where(ok[:, None], b, NEG)


def _na_attention(pa, bias, *, B, T, C):
    rows = T // GRID_W
    nj = T // ATT_TQ
    cb = B * T // C
    kern = functools.partial(_na_kernel, rows=rows)
    return pl.pallas_call(
        kern, grid=(B, nj),
        in_specs=[pl.BlockSpec((ATT_TQ, 256), lambda b, j: (b * nj + j, 0)),
                  pl.BlockSpec((T, 256), lambda b, j: (b, 1)),
                  pl.BlockSpec((T, 256), lambda b, j: (b, 2)),
                  pl.BlockSpec((C, 256), lambda b, j: (cb + b, 1)),
                  pl.BlockSpec((C, 256), lambda b, j: (cb + b, 2)),
                  pl.BlockSpec((1, N_HEADS, 512, 1024),
                               lambda b, j: (jnp.where(j == 0, 0, jnp.where(j == nj - 1, 2, 1)), 0, 0, 0))],
        out_specs=pl.BlockSpec((ATT_TQ, 256), lambda b, j: (b * nj + j, 0)),
        out_shape=jax.ShapeDtypeStruct((B * T, 256), BF16),
        compiler_params=_cparams(("arbitrary", "arbitrary")), name="na_attention",
    )(pa, pa, pa, pa, pa, bias)


def _wa_kernel(sink_ref, q_ref, k_ref, v_ref, kc_ref, vc_ref, o_ref, *, T):
    i = pl.program_id(1)
    start = i * ATT_TQ
    win = ATT_TQ + 2 * WA_WINDOW
    ws = pl.multiple_of(jnp.clip(start - WA_WINDOW, 0, T - win), 128)
    rel = (lax.broadcasted_iota(I32, (ATT_TQ, win), 1) + (ws - start)) - lax.broadcasted_iota(I32, (ATT_TQ, win), 0)
    valid = jnp.abs(rel) <= WA_WINDOW
    outs = []
    for h in range(N_HEADS):
        sl = slice(HEAD_DIM * h, HEAD_DIM * (h + 1))
        g = h // 2
        ksl = slice(HEAD_DIM * g, HEAD_DIM * (g + 1))
        q = q_ref[:, sl]
        s = lax.dot_general(q, k_ref[pl.ds(ws, win), ksl], _NT, preferred_element_type=F32)
        s = jnp.where(valid, s, NEG)
        sc = lax.dot_general(q, kc_ref[:, ksl], _NT, preferred_element_type=F32)
        outs.append(_softmax_av([(s, v_ref[pl.ds(ws, win), ksl]), (sc, vc_ref[:, ksl])], extra_logit=sink_ref[h]))
    o_ref[...] = jnp.concatenate(outs, -1).astype(BF16)


def _wa_attention(pb, sink, *, B, T, C):
    nj = T // ATT_TQ
    cb = B * T // C
    kern = functools.partial(_wa_kernel, T=T)
    return pl.pallas_call(
        kern, grid=(B, nj),
        in_specs=[pl.BlockSpec(memory_space=pltpu.SMEM),
                  pl.BlockSpec((ATT_TQ, 256), lambda b, j: (b * nj + j, 0)),
                  pl.BlockSpec((T, 128), lambda b, j: (b, 2)),
                  pl.BlockSpec((T, 128), lambda b, j: (b, 3)),
                  pl.BlockSpec((C, 128), lambda b, j: (cb + b, 2)),
                  pl.BlockSpec((C, 128), lambda b, j: (cb + b, 3))],
        out_specs=pl.BlockSpec((ATT_TQ, 256), lambda b, j: (b * nj + j, 0)),
        out_shape=jax.ShapeDtypeStruct((B * T, 256), BF16),
        compiler_params=_cparams(("arbitrary", "arbitrary")), name="wa_attention",
    )(sink, pb, pb, pb, pb, pb)


def _mla_kernel(q_ref, k_ref, v_ref, kc_ref, vc_ref, o_ref, *, T):
    nk = T // MLA_TK
    heads = [slice(HEAD_DIM * h, HEAD_DIM * (h + 1)) for h in range(N_HEADS)]
    qs = [q_ref[:, sl] for sl in heads]
    init = []
    for q, sl in zip(qs, heads):
        sc = lax.dot_general(q, kc_ref[:, sl], _NT, preferred_element_type=F32)
        m0 = jnp.max(sc, -1, keepdims=True)
        e0 = jnp.exp(sc - m0)
        init += [m0, jnp.sum(e0, -1, keepdims=True),
                 jnp.dot(e0.astype(BF16), vc_ref[:, sl], preferred_element_type=F32)]

    def body(c, carry):
        ks = pl.multiple_of(c * MLA_TK, MLA_TK)
        out = []
        for h, (q, sl) in enumerate(zip(qs, heads)):
            m, l, acc = carry[3 * h:3 * h + 3]
            s = lax.dot_general(q, k_ref[pl.ds(ks, MLA_TK), sl], _NT, preferred_element_type=F32)
            mn = jnp.maximum(m, jnp.max(s, -1, keepdims=True))
            a = jnp.exp(m - mn)
            e = jnp.exp(s - mn)
            l = l * a + jnp.sum(e, -1, keepdims=True)
            acc = acc * a + jnp.dot(e.astype(BF16), v_ref[pl.ds(ks, MLA_TK), sl], preferred_element_type=F32)
            out += [mn, l, acc]
        return tuple(out)

    fin = lax.fori_loop(0, nk, body, tuple(init))
    outs = [fin[3 * h + 2] * (1.0 / fin[3 * h + 1]) for h in range(N_HEADS)]
    o_ref[...] = jnp.concatenate(outs, -1).astype(BF16)


def _mla_attention(pc, *, B, T, C):
    nj = T // ATT_TQ
    cb = B * T // C
    kern = functools.partial(_mla_kernel, T=T)
    return pl.pallas_call(
        kern, grid=(B, nj),
        in_specs=[pl.BlockSpec((ATT_TQ, 256), lambda b, j: (b * nj + j, 0)),
                  pl.BlockSpec((T, 256), lambda b, j: (b, 1)),
                  pl.BlockSpec((T, 256), lambda b, j: (b, 2)),
                  pl.BlockSpec((C, 256), lambda b, j: (cb + b, 1)),
                  pl.BlockSpec((C, 256), lambda b, j: (cb + b, 2))],
        out_specs=pl.BlockSpec((ATT_TQ, 256), lambda b, j: (b * nj + j, 0)),
        out_shape=jax.ShapeDtypeStruct((B * T, 256), BF16),
        compiler_params=_cparams(("arbitrary", "arbitrary")), name="mla_attention",
    )(pc, pc, pc, pc, pc)


def _ctx_kernel(sink_ref, pa_ref, pb_ref, pc_ref, oa_ref, ob_ref, oc_ref):
    def attend(p_ref, koff, voff, kv_heads, out_ref, sink):
        outs = []
        for h in range(N_HEADS):
            g = h * kv_heads // N_HEADS
            q = p_ref[:, HEAD_DIM * h:HEAD_DIM * (h + 1)]
            k = p_ref[:, koff + HEAD_DIM * g:koff + HEAD_DIM * (g + 1)]
            v = p_ref[:, voff + HEAD_DIM * g:voff + HEAD_DIM * (g + 1)]
            s = lax.dot_general(q, k, _NT, preferred_element_type=F32)
            outs.append(_softmax_av([(s, v)], extra_logit=sink_ref[h] if sink else None))
        out_ref[...] = jnp.concatenate(outs, -1).astype(BF16)

    attend(pa_ref, 256, 512, 4, oa_ref, False)
    attend(pb_ref, 256, 384, 2, ob_ref, True)
    attend(pc_ref, 256, 512, 4, oc_ref, False)


def _ctx_attention(sink, pa, pb, pc, *, B, T, C):
    cb = B * T // C
    row = lambda b: (cb + b, 0)
    return pl.pallas_call(
        _ctx_kernel, grid=(B,),
        in_specs=[pl.BlockSpec(memory_space=pltpu.SMEM),
                  pl.BlockSpec((C, 768), row), pl.BlockSpec((C, 512), row), pl.BlockSpec((C, 768), row)],
        out_specs=[pl.BlockSpec((C, 256), lambda b: (b, 0))] * 3,
        out_shape=[jax.ShapeDtypeStruct((B * C, 256), BF16)] * 3,
        compiler_params=_cparams(("arbitrary",)), name="ctx_attention",
    )(sink, pa, pb, pc)


def _gla_dir(pd_ref, o_ref, st_ref, reverse):
    nchunk = GLA_BLOCK // GLA_CHUNK
    ri = lax.broadcasted_iota(I32, (GLA_CHUNK, GLA_CHUNK), 0)
    ci = lax.broadcasted_iota(I32, (GLA_CHUNK, GLA_CHUNK), 1)
    keep = (ci >= ri) if reverse else (ci <= ri)
    tri = jnp.where(keep, 1.0, 0.0).astype(BF16)
    gcol = 896 if reverse else 768
    order = range(nchunk - 1, -1, -1) if reverse else range(nchunk)
    for c in order:
        rs = slice(c * GLA_CHUNK, (c + 1) * GLA_CHUNK)
        g1, g2, g3 = _split3(pd_ref[rs, gcol:gcol + 128])
        b = (jnp.dot(tri, g1, preferred_element_type=F32) + jnp.dot(tri, g2, preferred_element_type=F32)
             + jnp.dot(tri, g3, preferred_element_type=F32))
        bend = b[0:1, :] if reverse else b[GLA_CHUNK - 1:GLA_CHUNK, :]
        q = pd_ref[rs, 0:128]
        k = pd_ref[rs, 128:256]
        qd = (q * jnp.exp(b)).astype(BF16)
        ki = (k * jnp.exp(-b)).astype(BF16)
        kd = (k * jnp.exp(bend - b)).astype(BF16)
        dec = jnp.exp(bend)
        outs = []
        for h in range(N_HEADS):
            ks = slice(GLA_DK * h, GLA_DK * (h + 1))
            v = pd_ref[rs, 256 + GLA_DV * h:256 + GLA_DV * (h + 1)]
            vb = v.astype(BF16)
            att = lax.dot_general(qd[:, ks], ki[:, ks], _NT, preferred_element_type=F32)
            att = jnp.where(keep, att, 0.0).astype(BF16)
            st = st_ref[h]
            o = (jnp.dot(att, vb, preferred_element_type=F32)
                 + lax.dot_general(qd[:, ks], st.astype(BF16), _NT, preferred_element_type=F32))
            ut = jnp.dot(v.T.astype(BF16), kd[:, ks], preferred_element_type=F32)
            st_ref[h] = st * dec[:, ks] + ut
            outs.append(o)
        o_ref[0, rs, :] = jnp.concatenate(outs, -1)


def _gla_kernel(pd_ref, o_ref, st_ref):
    d = pl.program_id(1)
    s = pl.program_id(2)

    @pl.when(s == 0)
    def _():
        st_ref[...] = jnp.zeros_like(st_ref)

    @pl.when(d == 0)
    def _():
        _gla_dir(pd_ref, o_ref, st_ref, False)

    @pl.when(d == 1)
    def _():
        _gla_dir(pd_ref, o_ref, st_ref, True)


def _gla(pd, *, B, T, C):
    N = pd.shape[0]
    assert C == GLA_BLOCK
    nlb = T // GLA_BLOCK
    cb = B * T // GLA_BLOCK

    def rb(b, d, s):
        lat = b * nlb + jnp.where(d == 0, s - 1, nlb - s)
        return jnp.where(s == 0, cb + b, lat)

    return pl.pallas_call(
        _gla_kernel, grid=(B, 2, nlb + 1),
        in_specs=[pl.BlockSpec((GLA_BLOCK, 1024), lambda b, d, s: (rb(b, d, s), 0))],
        out_specs=pl.BlockSpec((1, GLA_BLOCK, 256), lambda b, d, s: (d, rb(b, d, s), 0)),
        out_shape=jax.ShapeDtypeStruct((2, N, 256), F32),
        scratch_shapes=[pltpu.VMEM((N_HEADS, GLA_DV, GLA_DK), F32)],
        compiler_params=_cparams(("arbitrary", "arbitrary", "arbitrary")), name="gla_scan",
    )(pd)


def _pack_bf16_pairs(h):
    w = h.shape[-1] // 2
    bits = lax.bitcast_convert_type(h.astype(BF16).astype(F32), U32)
    return (bits[:, w:] & jnp.uint32(0xFFFF0000)) | (bits[:, :w] >> 16)


def _unpack_bf16_pairs(u):
    lo = lax.bitcast_convert_type(u << 16, F32)
    hi = lax.bitcast_convert_type(u & jnp.uint32(0xFFFF0000), F32)
    return jnp.concatenate([lo, hi], -1).astype(BF16)


def _outproj_kernel(x_ref, oa_ref, ob_ref, oc_ref, oac_ref, obc_ref, occ_ref, ofb_ref, r_ref, gn_ref, wout_ref,
                    g1_ref, sh2_ref, sc2_ref, lng_ref, lnb_ref, rw1_ref, rw2_ref, rb_ref,
                    x1_ref, hp_ref, sel_ref, gate_ref, rank_ref, count_ref, cnt_ref, *, alpha, nlat_t):
    is_lat = pl.program_id(0) < nlat_t
    pick = lambda lat_ref, ctx_ref: jnp.where(is_lat, lat_ref[...], ctx_ref[...])
    s = ofb_ref[0] + ofb_ref[1]
    parts = []
    for h in range(N_HEADS):
        parts.append(_rms(s[:, GLA_DV * h:GLA_DV * (h + 1)]))
    r = r_ref[...]
    od = jnp.concatenate(parts, -1) * gn_ref[...] * (r * _sigmoid(r))
    ocat = jnp.concatenate([pick(oa_ref, oac_ref), pick(ob_ref, obc_ref), pick(oc_ref, occ_ref), od.astype(BF16)], -1)
    o = jnp.dot(ocat, wout_ref[...], preferred_element_type=F32)
    x1 = _layer_norm(alpha * x_ref[...] + g1_ref[0] * o) * lng_ref[...] + lnb_ref[...]
    x1_ref[...] = x1
    h2 = _layer_norm(x1) * (1.0 + sc2_ref[0]) + sh2_ref[0]
    hp_ref[...] = _pack_bf16_pairs(h2)
    a1, a2, _ = _split3(h2)
    w1, w2 = rw1_ref[...], rw2_ref[...]
    logits = (jnp.dot(a1, w1, preferred_element_type=F32) + jnp.dot(a1, w2, preferred_element_type=F32)
              + jnp.dot(a2, w1, preferred_element_type=F32))
    scores = _sigmoid(logits)
    work = scores + rb_ref[...]
    tm, ne = scores.shape
    lane = lax.broadcasted_iota(I32, (tm, ne), 1).astype(F32)
    lane8 = lax.broadcasted_iota(I32, (tm, TOP_K), 1)
    sel = jnp.zeros((tm, TOP_K), F32)
    gsel = jnp.zeros((tm, TOP_K), F32)
    chosen = jnp.zeros((tm, ne), F32)
    picks = []
    for k in range(TOP_K):
        m = jnp.max(work, -1, keepdims=True)
        idx = jnp.min(jnp.where(work == m, lane, float(ne)), -1, keepdims=True)
        hit = lane == idx
        picks.append(idx)
        sc = jnp.sum(jnp.where(hit, scores, 0.0), -1, keepdims=True)
        sel = jnp.where(lane8 == k, idx, sel)
        gsel = jnp.where(lane8 == k, sc, gsel)
        chosen = jnp.where(hit, 1.0, chosen)
        work = jnp.where(hit, -jnp.inf, work)
    sel_ref[...] = sel.astype(I32)
    gate_ref[...] = ROUTED_SCALE * gsel / jnp.sum(gsel, -1, keepdims=True)
    @pl.when(pl.program_id(0) == 0)
    def _():
        cnt_ref[...] = jnp.zeros_like(cnt_ref)

    ri = lax.broadcasted_iota(I32, (tm, tm), 0)
    ci = lax.broadcasted_iota(I32, (tm, tm), 1)
    before = jnp.where(ci < ri, 1.0, 0.0).astype(BF16)
    prefix = jnp.dot(before, chosen.astype(BF16), preferred_element_type=F32) + cnt_ref[...]
    rank = jnp.zeros((tm, TOP_K), F32)
    for k in range(TOP_K):
        rk = jnp.sum(jnp.where(lane == picks[k], prefix, 0.0), -1, keepdims=True)
        rank = jnp.where(lane8 == k, rk, rank)
    rank_ref[...] = rank.astype(I32)
    cnt_ref[...] = cnt_ref[...] + jnp.sum(chosen, 0, keepdims=True)
    count_ref[...] = cnt_ref[...].astype(I32)


def _outproj(xall, o_lat, o_ctx, ofb, pd, gn, wout, mods_l, lng, lnb, rw1, rw2, rb, *, n_lat, T, alpha):
    N, D = xall.shape
    tm = TOK_TILE
    nlat_t, per_b, nb = n_lat // tm, T // tm, n_lat // T

    def midx(k):
        return lambda i: (jnp.where(i < nlat_t, i // per_b, nb) * 6 + k, 0, 0)

    const = lambda i: (0, 0)
    row = lambda i: (i, 0)
    lat_row = lambda i: (jnp.minimum(i, nlat_t - 1), 0)
    ctx_row = lambda i: (jnp.maximum(i - nlat_t, 0), 0)
    E = rw1.shape[1]
    kern = functools.partial(_outproj_kernel, alpha=alpha, nlat_t=nlat_t)
    return pl.pallas_call(
        kern, grid=(N // tm,),
        in_specs=[pl.BlockSpec((tm, D), row)] + [pl.BlockSpec((tm, 256), lat_row)] * 3
                 + [pl.BlockSpec((tm, 256), ctx_row)] * 3 + [
                  pl.BlockSpec((2, tm, 256), lambda i: (0, i, 0)),
                  pl.BlockSpec((tm, 256), lambda i: (i, 2)), pl.BlockSpec((1, 256), const),
                  pl.BlockSpec((D, D), const),
                  pl.BlockSpec((1, 1, D), midx(2)), pl.BlockSpec((1, 1, D), midx(3)), pl.BlockSpec((1, 1, D), midx(4)),
                  pl.BlockSpec((1, D), const), pl.BlockSpec((1, D), const),
                  pl.BlockSpec((D, E), const), pl.BlockSpec((D, E), const), pl.BlockSpec((1, E), const)],
        out_specs=[pl.BlockSpec((tm, D), row), pl.BlockSpec((tm, D // 2), row),
                   pl.BlockSpec((tm, TOP_K), row), pl.BlockSpec((tm, TOP_K), row), pl.BlockSpec((tm, TOP_K), row),
                   pl.BlockSpec((1, E), const)],
        out_shape=[jax.ShapeDtypeStruct((N, D), F32), jax.ShapeDtypeStruct((N, D // 2), U32),
                   jax.ShapeDtypeStruct((N, TOP_K), I32), jax.ShapeDtypeStruct((N, TOP_K), F32),
                   jax.ShapeDtypeStruct((N, TOP_K), I32), jax.ShapeDtypeStruct((1, E), I32)],
        scratch_shapes=[pltpu.VMEM((1, E), F32)],
        compiler_params=_cparams(("arbitrary",)), name="outproj_router",
    )(xall, *o_lat, *o_ctx, ofb, pd, gn, wout, mods_l, mods_l, mods_l, lng, lnb, rw1, rw2, rb)


def _sc_mesh():
    return plsc.VectorSubcoreMesh(core_axis_name="c", subcore_axis_name="s")


def _sc_worker():
    return lax.axis_index("s") * SC_CORES + lax.axis_index("c")


def _sc_dispatch(hp, dest_k, rows):
    N, W = hp.shape
    tpw = N // SC_WORKERS
    ch = SC_DISPATCH_CHUNK
    assert N % SC_WORKERS == 0 and tpw % ch == 0
    nch = tpw // ch
    idx = dest_k.reshape(TOP_K, SC_WORKERS, nch, ch).transpose(1, 2, 0, 3)

    @functools.partial(
        pl.kernel, mesh=_sc_mesh(), out_type=jax.ShapeDtypeStruct((rows, W), hp.dtype),
        scratch_types=[pltpu.VMEM((nch, TOP_K, ch), I32), pltpu.VMEM((ch, W), hp.dtype), pltpu.SemaphoreType.DMA],
        compiler_params=pltpu.CompilerParams(use_tc_tiling_on_sc=True), name="sc_dispatch")
    def scatter(hp_hbm, idx_hbm, out_hbm, idx_v, rows_v, sem):
        wid = _sc_worker()
        base = wid * tpw
        pltpu.sync_copy(idx_hbm.at[wid], idx_v)

        @pl.loop(0, nch)
        def _(c):
            pltpu.sync_copy(hp_hbm.at[pl.ds(base + c * ch, ch)], rows_v)
            copies = [pltpu.async_copy(rows_v, out_hbm.at[idx_v.at[c, k]], sem) for k in range(TOP_K)]
            for cp in copies:
                cp.wait()

    return scatter(hp, idx)


def _sc_gather(src, idx):
    R, W = idx.shape[0], src.shape[1]
    per_w = R // SC_WORKERS
    ch = SC_GATHER_CHUNK
    assert R % SC_WORKERS == 0 and per_w % ch == 0
    nch = per_w // ch

    @functools.partial(
        pl.kernel, mesh=_sc_mesh(), out_type=jax.ShapeDtypeStruct((R, W), src.dtype),
        scratch_types=[pltpu.VMEM((nch, ch), I32), pltpu.VMEM((ch, W), src.dtype), pltpu.SemaphoreType.DMA],
        compiler_params=pltpu.CompilerParams(use_tc_tiling_on_sc=True), name="sc_gather")
    def gather(src_hbm, idx_hbm, out_hbm, idx_v, rows_v, sem):
        wid = _sc_worker()
        base = wid * per_w
        pltpu.sync_copy(idx_hbm.at[wid], idx_v)

        @pl.loop(0, nch)
        def _(c):
            pltpu.async_copy(src_hbm.at[idx_v.at[c]], rows_v, sem).wait()
            pltpu.sync_copy(rows_v, out_hbm.at[pl.ds(base + c * ch, ch)])

    return gather(src, idx.reshape(SC_WORKERS, nch, ch))


def _swiglu(xb, w1, w3, w2):
    a = jnp.dot(xb, w1, preferred_element_type=F32)
    b = jnp.dot(xb, w3, preferred_element_type=F32)
    return jnp.dot((a * _sigmoid(a) * b).astype(BF16), w2, preferred_element_type=F32)


def _expert_kernel(be_ref, nu_ref, xs_ref, w1_ref, w3_ref, w2_ref, y_ref, w1b_ref, w3b_ref, w2b_ref):
    i = pl.program_id(0)
    used = i < nu_ref[0]

    @pl.when(used & ((i == 0) | (be_ref[i] != be_ref[jnp.maximum(i - 1, 0)])))
    def _():
        w1b_ref[...] = w1_ref[0, 0].astype(BF16)
        w3b_ref[...] = w3_ref[0, 0].astype(BF16)
        w2b_ref[...] = w2_ref[0, 0].astype(BF16)

    @pl.when(used)
    def _():
        xb = _unpack_bf16_pairs(xs_ref[...])
        y_ref[...] = _pack_bf16_pairs(_swiglu(xb, w1b_ref[...], w3b_ref[...], w2b_ref[...]))

    @pl.when(jnp.logical_not(used))
    def _():
        y_ref[...] = jnp.zeros_like(y_ref)


def _experts(xs, blk_exp, n_used, w1, w3, w2, layer):
    rows, half = xs.shape
    _, E, D, F = w1.shape
    nblk = rows // MOE_BM
    wmap = lambda i, be, nu: (layer, be[i], 0, 0)
    grid_spec = pltpu.PrefetchScalarGridSpec(
        num_scalar_prefetch=2, grid=(nblk,),
        in_specs=[pl.BlockSpec((MOE_BM, half), lambda i, be, nu: (i, 0)),
                  pl.BlockSpec((1, 1, D, F), wmap), pl.BlockSpec((1, 1, D, F), wmap), pl.BlockSpec((1, 1, F, D), wmap)],
        out_specs=pl.BlockSpec((MOE_BM, half), lambda i, be, nu: (i, 0)),
        scratch_shapes=[pltpu.VMEM((D, F), BF16), pltpu.VMEM((D, F), BF16), pltpu.VMEM((F, D), BF16)])
    return pl.pallas_call(
        _expert_kernel, grid_spec=grid_spec,
        out_shape=jax.ShapeDtypeStruct((rows, half), U32),
        compiler_params=_cparams(("arbitrary",)), name="expert_ffn",
    )(blk_exp, n_used, xs, w1, w3, w2)


def _combine_kernel(yg_ref, gate_ref, hp_ref, sw1_ref, sw3_ref, sw2_ref, x_ref, g2_ref, lng_ref, lnb_ref,
                    o_ref, *, alpha):
    f = _swiglu(_unpack_bf16_pairs(hp_ref[...]), sw1_ref[...], sw3_ref[...], sw2_ref[...])
    gates = gate_ref[...]
    lo = hi = 0.0
    for k in range(TOP_K):
        u = yg_ref[k]
        g = gates[:, k:k + 1]
        lo = lo + g * lax.bitcast_convert_type(u << 16, F32)
        hi = hi + g * lax.bitcast_convert_type(u & jnp.uint32(0xFFFF0000), F32)
    f = f + jnp.concatenate([lo, hi], -1)
    o_ref[...] = _layer_norm(alpha * x_ref[...] + g2_ref[0] * f) * lng_ref[...] + lnb_ref[...]


def _combine(yg, gates, hp, sw1, sw3, sw2, x1, mods_l, lng, lnb, *, n_lat, T, alpha):
    N, D = x1.shape
    tm = CMB_TILE
    nlat_t, per_b, nb = n_lat // tm, T // tm, n_lat // T
    F = sw1.shape[1]
    const = lambda i: (0, 0)
    row = lambda i: (i, 0)
    kern = functools.partial(_combine_kernel, alpha=alpha)
    return pl.pallas_call(
        kern, grid=(N // tm,),
        in_specs=[pl.BlockSpec((TOP_K, tm, D // 2), lambda i: (0, i, 0)), pl.BlockSpec((tm, TOP_K), row),
                  pl.BlockSpec((tm, D // 2), row),
                  pl.BlockSpec((D, F), const), pl.BlockSpec((D, F), const), pl.BlockSpec((F, D), const),
                  pl.BlockSpec((tm, D), row),
                  pl.BlockSpec((1, 1, D), lambda i: (jnp.where(i < nlat_t, i // per_b, nb) * 6 + 5, 0, 0)),
                  pl.BlockSpec((1, D), const), pl.BlockSpec((1, D), const)],
        out_specs=pl.BlockSpec((tm, D), row),
        out_shape=jax.ShapeDtypeStruct((N, D), F32),
        compiler_params=_cparams(("arbitrary",)), name="moe_combine",
    )(yg, gates, hp, sw1, sw3, sw2, x1, mods_l, lng, lnb)


def _dest_kernel(start_ref, sel_ref, rank_ref, o_ref):
    sel = sel_ref[...]

    def body(e, acc):
        return jnp.where(sel == e, start_ref[e], acc)

    o_ref[...] = lax.fori_loop(0, start_ref.shape[0], body, jnp.zeros_like(sel)) + rank_ref[...]


def _dispatch_plan(sel, rank, counts):
    N = sel.shape[0]
    n_experts = counts.shape[0]
    padded = (counts + MOE_BM - 1) // MOE_BM * MOE_BM
    pad_end = jnp.cumsum(padded)
    pad_start = (pad_end - padded).astype(I32)
    dense = (N * TOP_K // 128, 128)
    dest = pl.pallas_call(
        _dest_kernel,
        in_specs=[pl.BlockSpec(memory_space=pltpu.SMEM), pl.BlockSpec(memory_space=pltpu.VMEM),
                  pl.BlockSpec(memory_space=pltpu.VMEM)],
        out_specs=pl.BlockSpec(memory_space=pltpu.VMEM),
        out_shape=jax.ShapeDtypeStruct(dense, I32), name="moe_dest_rows",
    )(pad_start, sel.reshape(dense), rank.reshape(dense))
    dest_k = dest.reshape(N, TOP_K).T
    n_blocks = -(-(N * TOP_K) // MOE_BM) + n_experts
    blk_exp = jnp.minimum(jnp.searchsorted(pad_end, jnp.arange(n_blocks, dtype=I32) * MOE_BM, side='right'),
                          n_experts - 1).astype(I32)
    n_used = (pad_end[-1] // MOE_BM).astype(I32).reshape(1)
    return dest_k, blk_exp, n_used, n_blocks * MOE_BM


def _rope_tables(T, tile):
    t = jnp.arange(T)
    row = (t // GRID_W).astype(F32)[:, None]
    col = (t % GRID_W).astype(F32)[:, None]

    def group(half):
        inv = ROPE_THETA ** (-jnp.arange(half, dtype=F32) / half)
        ar, ac = row * inv, col * inv
        cos = jnp.concatenate([jnp.cos(ar), jnp.cos(ar), jnp.cos(ac), jnp.cos(ac)], -1)
        sin = jnp.concatenate([-jnp.sin(ar), jnp.sin(ar), -jnp.sin(ac), jnp.sin(ac)], -1)
        return cos, sin

    ones32, zeros32 = jnp.ones((T, 32), F32), jnp.zeros((T, 32), F32)
    cw, sw = group(16)
    cw, sw = jnp.tile(cw, (1, 2)), jnp.tile(sw, (1, 2))
    c8, s8 = group(8)
    cq = jnp.tile(jnp.concatenate([ones32, c8], -1), (1, 2))
    sq = jnp.tile(jnp.concatenate([zeros32, s8], -1), (1, 2))
    ck = jnp.concatenate([c8, ones32, ones32, ones32], -1)
    sk = jnp.concatenate([s8, zeros32, zeros32, zeros32], -1)

    def pad(a, ident):
        return jnp.concatenate([a, jnp.full((tile, 128), ident, F32)], 0)

    return [pad(cw, 1.0), pad(sw, 0.0), pad(cq, 1.0), pad(sq, 0.0), pad(ck, 1.0), pad(sk, 0.0)]


def _layer_weights(w_in, mla_w_uq, mla_w_ukv, gla_w_gf, gla_w_gb, gla_b_gf, gla_b_gb):
    D = w_in.shape[0]
    z = lambda n: jnp.zeros((D, n), F32)
    win = jnp.concatenate([
        w_in[:, 0:256] * 0.125, w_in[:, 256:768],
        w_in[:, 768:1024] * 0.125, w_in[:, 1024:1280],
        w_in[:, 1280:1696], z(96),
        w_in[:, 1696:2464],
        w_in[:, 2464:2496], z(96)], -1).astype(BF16)
    wuq = (mla_w_uq * 0.125).astype(BF16)
    ukv = mla_w_ukv.reshape(-1, N_HEADS, MLA_NOPE + HEAD_DIM)
    kv_rank = ukv.shape[0]
    wk = jnp.concatenate([ukv[:, :, :MLA_NOPE], jnp.zeros((kv_rank, N_HEADS, MLA_ROPE), F32)], -1)
    place = jnp.concatenate([jnp.zeros((MLA_ROPE, MLA_NOPE), F32), jnp.eye(MLA_ROPE, dtype=F32)], -1)
    place = jnp.concatenate([jnp.tile(place, (1, N_HEADS)), jnp.zeros((128 - MLA_ROPE, 256), F32)], 0)
    wkk = jnp.concatenate([wk.reshape(kv_rank, 256), place], 0).astype(BF16)
    wv = ukv[:, :, MLA_NOPE:].reshape(kv_rank, 256).astype(BF16)
    r = GLA_GATE_RANK
    wg = jnp.zeros((128, 256), F32).at[0:r, 0:128].set(gla_w_gf).at[r:2 * r, 128:256].set(gla_w_gb).astype(BF16)
    bg = jnp.concatenate([gla_b_gf, gla_b_gb])[None, :]
    return win, wuq, wkk, wv, wg, bg


def kernel(x, c, ctx, c_ctx, w_ada, b_ada, w_in, na_rpb, wa_sink, mla_g_q, mla_g_kv, mla_w_uq, mla_w_ukv,
           gla_w_gf, gla_b_gf, gla_w_gb, gla_b_gb, gla_g_norm, w_out, ln1_g, ln1_b, ln2_g, ln2_b,
           router_w, router_bias, exp_w1, exp_w3, exp_w2, sh_w1, sh_w3, sh_w2):
    B, T, D = x.shape
    C = ctx.shape[1]
    L = w_ada.shape[0]
    E = router_w.shape[-1]
    n_lat = B * T
    alpha = (2 * L) ** 0.25
    dims = dict(B=B, T=T, C=C)

    cc = jnp.zeros((8, D), F32).at[:B].set(c).at[B].set(c_ctx)
    mods = _mods(cc, w_ada, b_ada)
    tabs = _rope_tables(T, TOK_TILE)
    xall = jnp.concatenate([x.reshape(n_lat, D), ctx.reshape(B * C, D)], 0)

    for l in range(L):
        mods_l = mods[l].reshape(8 * 6, 1, D)
        win, wuq, wkk, wv, wg, bg = _layer_weights(w_in[l], mla_w_uq[l], mla_w_ukv[l], gla_w_gf[l], gla_w_gb[l],
                                                   gla_b_gf[l], gla_b_gb[l])
        pa, pb, pc, pd = _inproj(xall, mods_l, win, tabs, mla_g_q[l][None], mla_g_kv[l][None], wuq, wkk, wv, wg, bg,
                                 n_lat=n_lat, T=T)
        oa = _na_attention(pa, _na_bias_tables(na_rpb[l], T // GRID_W), **dims)
        ob = _wa_attention(pb, wa_sink[l], **dims)
        oc = _mla_attention(pc, **dims)
        o_ctx = _ctx_attention(wa_sink[l], pa, pb, pc, **dims)
        ofb = _gla(pd, **dims)

        rw1 = router_w[l].astype(BF16)
        rw2 = (router_w[l] - rw1.astype(F32)).astype(BF16)
        x1, hp, sel, gates, rank, counts = _outproj(
            xall, (oa, ob, oc), o_ctx, ofb, pd, jnp.tile(gla_g_norm[l], N_HEADS)[None], w_out[l].astype(BF16), mods_l,
            ln1_g[l][None], ln1_b[l][None], rw1, rw2, router_bias[l][None], n_lat=n_lat, T=T, alpha=alpha)

        dest_k, blk_exp, n_used, rows = _dispatch_plan(sel, rank, counts.reshape(E))
        xs = _sc_dispatch(hp, dest_k, rows)
        y = _experts(xs, blk_exp, n_used, exp_w1, exp_w3, exp_w2, l)
        yg = _sc_gather(y, dest_k.reshape(-1)).reshape(TOP_K, xall.shape[0], D // 2)
        xall = _combine(yg, gates, hp, sh_w1[l].astype(BF16), sh_w3[l].astype(BF16), sh_w2[l].astype(BF16),
                        x1, mods_l, ln2_g[l][None], ln2_b[l][None], n_lat=n_lat, T=T, alpha=alpha)

    return xall[:n_lat].reshape(B, T, D)
```

```python
import functools

import numpy as np
import jax
import jax.numpy as jnp
from jax import lax
from jax.experimental import pallas as pl
from jax.experimental.pallas import tpu as pltpu
from jax.experimental.pallas import tpu_sc as plsc

F32 = jnp.float32
BF16 = jnp.bfloat16
U32 = jnp.uint32
I32 = jnp.int32

GRID_W = 64
HEAD_DIM = 64
N_HEADS = 4
NA_KH, NA_KW = 8, 16
WA_WINDOW = 128
MLA_NOPE, MLA_ROPE = 32, 32
GLA_DK, GLA_DV = 32, 64
GLA_GATE_RANK = 16
GLA_GATE_NORM = 16.0
GLA_CHUNK = 64
TOP_K = 8
ROUTED_SCALE = 2.5
ROPE_THETA = 10000.0
LN_EPS = 1e-5
RMS_EPS = 1e-6
NEG = -1e30

VMEM_LIMIT = 56 * 1024 * 1024
TOK_TILE = 512
ATT_TQ = 512
MLA_TK = 1024
GLA_BLOCK = 256
MOE_BM = 1024
CMB_TILE = 256
SC_CORES = 2
SC_WORKERS = 32
SC_DISPATCH_CHUNK = 32
SC_GATHER_CHUNK = 64

PW = 2688
_NT = (((1,), (1,)), ((), ()))


def _cparams(sem, vmem=VMEM_LIMIT):
    return pltpu.CompilerParams(dimension_semantics=sem, vmem_limit_bytes=vmem)


def _sigmoid(x):
    return 1.0 / (1.0 + jnp.exp(-x))


def _layer_norm(x):
    mu = jnp.mean(x, -1, keepdims=True)
    xc = x - mu
    var = jnp.mean(xc * xc, -1, keepdims=True)
    return xc * lax.rsqrt(var + LN_EPS)


def _rms(x):
    return x * lax.rsqrt(jnp.mean(x * x, -1, keepdims=True) + RMS_EPS)


def _bdot(a, b):
    return jnp.dot(a.astype(BF16), b.astype(BF16), preferred_element_type=F32)


def _bdot_nt(a, b):
    return lax.dot_general(a.astype(BF16), b.astype(BF16), _NT, preferred_element_type=F32)


def _split3(a):
    a1 = a.astype(BF16)
    r = a - a1.astype(F32)
    a2 = r.astype(BF16)
    a3 = (r - a2.astype(F32)).astype(BF16)
    return a1, a2, a3


def _mods_kernel(cc_ref, w_ref, b_ref, o_ref):
    cc = cc_ref[...]
    s = cc * _sigmoid(cc)
    o_ref[0] = _bdot(s, w_ref[0]) + b_ref[0]


def _mods(cc, w_ada, b_ada):
    L, D, W = w_ada.shape
    tn = 1536
    return pl.pallas_call(
        _mods_kernel,
        grid=(L, W // tn),
        in_specs=[pl.BlockSpec((8, D), lambda l, j: (0, 0)),
                  pl.BlockSpec((1, D, tn), lambda l, j: (l, 0, j)),
                  pl.BlockSpec((1, 1, tn), lambda l, j: (l, 0, j))],
        out_specs=pl.BlockSpec((1, 8, tn), lambda l, j: (l, 0, j)),
        out_shape=jax.ShapeDtypeStruct((L, 8, W), F32),
        compiler_params=_cparams(("arbitrary", "arbitrary")),
        name="adaln_mods",
    )(cc, w_ada, b_ada.reshape(L, 1, W))


def _rope(z, cos, sin, half):
    w = z.shape[-1]
    lane = lax.broadcasted_iota(I32, z.shape, 1)
    first = (lane % (2 * half)) < half
    partner = jnp.where(first, pltpu.roll(z, w - half, 1), pltpu.roll(z, half, 1))
    return z * cos + partner * sin


def _log_sigmoid(x):
    return jnp.minimum(x, 0.0) - jnp.log(1.0 + jnp.exp(-jnp.abs(x)))


def _inproj_kernel(x_ref, shift_ref, scale_ref, win_ref, cw_ref, sw_ref, cq_ref, sq_ref, ck_ref, sk_ref,
                   gq_ref, gkv_ref, wuq_ref, wkk_ref, wv_ref, wg_ref, bg_ref,
                   pa_ref, pb_ref, pc_ref, pd_ref):
    h = _layer_norm(x_ref[...]) * (1.0 + scale_ref[0]) + shift_ref[0]
    p = jnp.dot(h.astype(BF16), win_ref[...], preferred_element_type=F32)
    pa_ref[...] = p[:, 0:768].astype(BF16)
    cw, sw = cw_ref[...], sw_ref[...]
    pb_ref[:, 0:128] = _rope(p[:, 768:896], cw, sw, 16).astype(BF16)
    pb_ref[:, 128:256] = _rope(p[:, 896:1024], cw, sw, 16).astype(BF16)
    pb_ref[:, 256:384] = _rope(p[:, 1024:1152], cw, sw, 16).astype(BF16)
    pb_ref[:, 384:512] = p[:, 1152:1280].astype(BF16)
    cqn = _rms(p[:, 1280:1536]) * gq_ref[...]
    q = jnp.dot(cqn.astype(BF16), wuq_ref[...], preferred_element_type=F32)
    cq, sq = cq_ref[...], sq_ref[...]
    pc_ref[:, 0:128] = _rope(q[:, 0:128], cq, sq, 8).astype(BF16)
    pc_ref[:, 128:256] = _rope(q[:, 128:256], cq, sq, 8).astype(BF16)
    ckvn = (_rms(p[:, 1536:1664]) * gkv_ref[...]).astype(BF16)
    kr = _rope(p[:, 1664:1792], ck_ref[...], sk_ref[...], 8).astype(BF16)
    kin = jnp.concatenate([ckvn, kr], axis=-1)
    pc_ref[:, 256:512] = jnp.dot(kin, wkk_ref[...], preferred_element_type=F32).astype(BF16)
    pc_ref[:, 512:768] = jnp.dot(ckvn, wv_ref[...], preferred_element_type=F32).astype(BF16)
    pd_ref[:, 0:128] = p[:, 1792:1920] * (GLA_DK ** -0.5)
    pd_ref[:, 128:768] = p[:, 1920:2560]
    pre = jnp.dot(p[:, 2560:2688].astype(BF16), wg_ref[...], preferred_element_type=F32) + bg_ref[...]
    pd_ref[:, 768:1024] = _log_sigmoid(pre) * (1.0 / GLA_GATE_NORM)


def _inproj(xall, mods_l, win, tabs, gq, gkv, wuq, wkk, wv, wg, bg, *, n_lat, T):
    N, D = xall.shape
    tm = TOK_TILE
    nlat_t = n_lat // tm
    per_b = T // tm
    nb = n_lat // T

    def midx(k):
        return lambda i: (jnp.where(i < nlat_t, i // per_b, nb) * 6 + k, 0, 0)

    def tidx(i):
        return (jnp.where(i < nlat_t, i % per_b, per_b), 0)

    const = lambda i: (0, 0)
    tab_spec = pl.BlockSpec((tm, 128), tidx)
    in_specs = [pl.BlockSpec((tm, D), lambda i: (i, 0)),
                pl.BlockSpec((1, 1, D), midx(0)), pl.BlockSpec((1, 1, D), midx(1)),
                pl.BlockSpec((D, PW), const)] + [tab_spec] * 6 + [
                pl.BlockSpec((1, 256), const), pl.BlockSpec((1, 128), const),
                pl.BlockSpec((256, 256), const), pl.BlockSpec((256, 256), const),
                pl.BlockSpec((128, 256), const), pl.BlockSpec((128, 256), const),
                pl.BlockSpec((1, 256), const)]
    out_specs = [pl.BlockSpec((tm, 768), lambda i: (i, 0)), pl.BlockSpec((tm, 512), lambda i: (i, 0)),
                 pl.BlockSpec((tm, 768), lambda i: (i, 0)), pl.BlockSpec((tm, 1024), lambda i: (i, 0))]
    out_shape = [jax.ShapeDtypeStruct((N, 768), BF16), jax.ShapeDtypeStruct((N, 512), BF16),
                 jax.ShapeDtypeStruct((N, 768), BF16), jax.ShapeDtypeStruct((N, 1024), F32)]
    return pl.pallas_call(
        _inproj_kernel, grid=(N // tm,), in_specs=in_specs, out_specs=out_specs, out_shape=out_shape,
        compiler_params=_cparams(("arbitrary",)), name="inproj",
    )(xall, mods_l, mods_l, win, *tabs, gq, gkv, wuq, wkk, wv, wg, bg)


def _softmax_av(parts, extra_logit=None):
    m = functools.reduce(jnp.maximum, [jnp.max(s, -1, keepdims=True) for s, _ in parts])
    if extra_logit is not None:
        m = jnp.maximum(m, extra_logit)
    l = 0.0 if extra_logit is None else jnp.exp(extra_logit - m)
    o = None
    for s, v in parts:
        e = jnp.exp(s - m)
        l = l + jnp.sum(e, -1, keepdims=True)
        c = jnp.dot(e.astype(BF16), v, preferred_element_type=F32)
        o = c if o is None else o + c
    return o * (1.0 / l)


def _na_kernel(q_ref, k_ref, v_ref, kc_ref, vc_ref, bias_ref, o_ref, *, rows):
    j = pl.program_id(1)
    ws = pl.multiple_of(jnp.clip(8 * j - 4, 0, rows - 16) * GRID_W, 256)
    win = 16 * GRID_W
    outs = []
    for h in range(N_HEADS):
        sl = slice(HEAD_DIM * h, HEAD_DIM * (h + 1))
        q = q_ref[:, sl]
        s = lax.dot_general(q, k_ref[pl.ds(ws, win), sl], _NT, preferred_element_type=F32) + bias_ref[0, h]
        sc = lax.dot_general(q, kc_ref[:, sl], _NT, preferred_element_type=F32)
        outs.append(_softmax_av([(s, v_ref[pl.ds(ws, win), sl]), (sc, vc_ref[:, sl])]))
    o_ref[...] = jnp.concatenate(outs, -1).astype(BF16)


def _na_bias_tables(rpb, rows):
    nj = rows // 8
    ro_all, ok_all = [], []
    for j in (0, 1, nj - 1):
        ws = int(np.clip(8 * j - 4, 0, rows - 16))
        r = 8 * j + np.arange(8)[:, None, None, None]
        qc = np.arange(GRID_W)[None, :, None, None]
        kr = ws + np.arange(16)[None, None, :, None]
        kc = np.arange(GRID_W)[None, None, None, :]
        rs = np.clip(r - NA_KH // 2, 0, rows - NA_KH)
        cs = np.clip(qc - NA_KW // 2, 0, GRID_W - NA_KW)
        ok = (kr >= rs) & (kr < rs + NA_KH) & (kc >= cs) & (kc < cs + NA_KW)
        ro = np.clip(kr - r + (NA_KH - 1), 0, 2 * NA_KH - 2)
        shp = (8, GRID_W, 16, GRID_W)
        ro_all.append(np.broadcast_to(ro, shp).reshape(512, 1024))
        ok_all.append(np.broadcast_to(ok, shp).reshape(512, 1024))
    ok = np.stack(ok_all)
    hi = lax.Precision.HIGHEST
    co = np.clip(np.arange(GRID_W)[None, :] - np.arange(GRID_W)[:, None] + (NA_KW - 1), 0, 2 * NA_KW - 2)
    col_hot = (co[..., None] == np.arange(2 * NA_KW - 1)).astype(np.float32)
    cb = jnp.einsum('hdo,qko->hdqk', rpb, col_hot, precision=hi)
    ro = np.stack(ro_all).reshape(3, 8, GRID_W, 16, GRID_W)[:, :, 0, :, 0]
    row_hot = (ro[..., None] == np.arange(2 * NA_KH - 1)).astype(np.float32)
    b = jnp.einsum('vrsd,hdqk->vhrqsk', row_hot, cb, precision=hi)
    b = b.reshape(3, rpb.shape[0], 512, 1024)
    return jnp.where(ok[:, None], b, NEG)


def _na_attention(pa, bias, *, B, T, C):
    rows = T // GRID_W
    nj = T // ATT_TQ
    cb = B * T // C
    kern = functools.partial(_na_kernel, rows=rows)
    return pl.pallas_call(
        kern, grid=(B, nj),
        in_specs=[pl.BlockSpec((ATT_TQ, 256), lambda b, j: (b * nj + j, 0)),
                  pl.BlockSpec((T, 256), lambda b, j: (b, 1)),
                  pl.BlockSpec((T, 256), lambda b, j: (b, 2)),
                  pl.BlockSpec((C, 256), lambda b, j: (cb + b, 1)),
                  pl.BlockSpec((C, 256), lambda b, j: (cb + b, 2)),
                  pl.BlockSpec((1, N_HEADS, 512, 1024),
                               lambda b, j: (jnp.where(j == 0, 0, jnp.where(j == nj - 1, 2, 1)), 0, 0, 0))],
        out_specs=pl.BlockSpec((ATT_TQ, 256), lambda b, j: (b * nj + j, 0)),
        out_shape=jax.ShapeDtypeStruct((B * T, 256), BF16),
        compiler_params=_cparams(("arbitrary", "arbitrary")), name="na_attention",
    )(pa, pa, pa, pa, pa, bias)


def _wa_kernel(sink_ref, q_ref, k_ref, v_ref, kc_ref, vc_ref, o_ref, *, T):
    i = pl.program_id(1)
    start = i * ATT_TQ
    win = ATT_TQ + 2 * WA_WINDOW
    ws = pl.multiple_of(jnp.clip(start - WA_WINDOW, 0, T - win), 128)
    rel = (lax.broadcasted_iota(I32, (ATT_TQ, win), 1) + (ws - start)) - lax.broadcasted_iota(I32, (ATT_TQ, win), 0)
    valid = jnp.abs(rel) <= WA_WINDOW
    outs = []
    for h in range(N_HEADS):
        sl = slice(HEAD_DIM * h, HEAD_DIM * (h + 1))
        g = h // 2
        ksl = slice(HEAD_DIM * g, HEAD_DIM * (g + 1))
        q = q_ref[:, sl]
        s = lax.dot_general(q, k_ref[pl.ds(ws, win), ksl], _NT, preferred_element_type=F32)
        s = jnp.where(valid, s, NEG)
        sc = lax.dot_general(q, kc_ref[:, ksl], _NT, preferred_element_type=F32)
        outs.append(_softmax_av([(s, v_ref[pl.ds(ws, win), ksl]), (sc, vc_ref[:, ksl])], extra_logit=sink_ref[h]))
    o_ref[...] = jnp.concatenate(outs, -1).astype(BF16)


def _wa_attention(pb, sink, *, B, T, C):
    nj = T // ATT_TQ
    cb = B * T // C
    kern = functools.partial(_wa_kernel, T=T)
    return pl.pallas_call(
        kern, grid=(B, nj),
        in_specs=[pl.BlockSpec(memory_space=pltpu.SMEM),
                  pl.BlockSpec((ATT_TQ, 256), lambda b, j: (b * nj + j, 0)),
                  pl.BlockSpec((T, 128), lambda b, j: (b, 2)),
                  pl.BlockSpec((T, 128), lambda b, j: (b, 3)),
                  pl.BlockSpec((C, 128), lambda b, j: (cb + b, 2)),
                  pl.BlockSpec((C, 128), lambda b, j: (cb + b, 3))],
        out_specs=pl.BlockSpec((ATT_TQ, 256), lambda b, j: (b * nj + j, 0)),
        out_shape=jax.ShapeDtypeStruct((B * T, 256), BF16),
        compiler_params=_cparams(("arbitrary", "arbitrary")), name="wa_attention",
    )(sink, pb, pb, pb, pb, pb)


def _mla_kernel(q_ref, k_ref, v_ref, kc_ref, vc_ref, o_ref, *, T):
    nk = T // MLA_TK
    heads = [slice(HEAD_DIM * h, HEAD_DIM * (h + 1)) for h in range(N_HEADS)]
    qs = [q_ref[:, sl] for sl in heads]
    init = []
    for q, sl in zip(qs, heads):
        sc = lax.dot_general(q, kc_ref[:, sl], _NT, preferred_element_type=F32)
        m0 = jnp.max(sc, -1, keepdims=True)
        e0 = jnp.exp(sc - m0)
        init += [m0, jnp.sum(e0, -1, keepdims=True),
                 jnp.dot(e0.astype(BF16), vc_ref[:, sl], preferred_element_type=F32)]

    def body(c, carry):
        ks = pl.multiple_of(c * MLA_TK, MLA_TK)
        out = []
        for h, (q, sl) in enumerate(zip(qs, heads)):
            m, l, acc = carry[3 * h:3 * h + 3]
            s = lax.dot_general(q, k_ref[pl.ds(ks, MLA_TK), sl], _NT, preferred_element_type=F32)
            mn = jnp.maximum(m, jnp.max(s, -1, keepdims=True))
            a = jnp.exp(m - mn)
            e = jnp.exp(s - mn)
            l = l * a + jnp.sum(e, -1, keepdims=True)
            acc = acc * a + jnp.dot(e.astype(BF16), v_ref[pl.ds(ks, MLA_TK), sl], preferred_element_type=F32)
            out += [mn, l, acc]
        return tuple(out)

    fin = lax.fori_loop(0, nk, body, tuple(init), unroll=2)
    outs = [fin[3 * h + 2] * (1.0 / fin[3 * h + 1]) for h in range(N_HEADS)]
    o_ref[...] = jnp.concatenate(outs, -1).astype(BF16)


def _mla_attention(pc, *, B, T, C):
    nj = T // ATT_TQ
    cb = B * T // C
    kern = functools.partial(_mla_kernel, T=T)
    return pl.pallas_call(
        kern, grid=(B, nj),
        in_specs=[pl.BlockSpec((ATT_TQ, 256), lambda b, j: (b * nj + j, 0)),
                  pl.BlockSpec((T, 256), lambda b, j: (b, 1)),
                  pl.BlockSpec((T, 256), lambda b, j: (b, 2)),
                  pl.BlockSpec((C, 256), lambda b, j: (cb + b, 1)),
                  pl.BlockSpec((C, 256), lambda b, j: (cb + b, 2))],
        out_specs=pl.BlockSpec((ATT_TQ, 256), lambda b, j: (b * nj + j, 0)),
        out_shape=jax.ShapeDtypeStruct((B * T, 256), BF16),
        compiler_params=_cparams(("arbitrary", "arbitrary")), name="mla_attention",
    )(pc, pc, pc, pc, pc)


def _ctx_kernel(sink_ref, pa_ref, pb_ref, pc_ref, oa_ref, ob_ref, oc_ref):
    def attend(p_ref, koff, voff, kv_heads, out_ref, sink):
        outs = []
        for h in range(N_HEADS):
            g = h * kv_heads // N_HEADS
            q = p_ref[:, HEAD_DIM * h:HEAD_DIM * (h + 1)]
            k = p_ref[:, koff + HEAD_DIM * g:koff + HEAD_DIM * (g + 1)]
            v = p_ref[:, voff + HEAD_DIM * g:voff + HEAD_DIM * (g + 1)]
            s = lax.dot_general(q, k, _NT, preferred_element_type=F32)
            outs.append(_softmax_av([(s, v)], extra_logit=sink_ref[h] if sink else None))
        out_ref[...] = jnp.concatenate(outs, -1).astype(BF16)

    attend(pa_ref, 256, 512, 4, oa_ref, False)
    attend(pb_ref, 256, 384, 2, ob_ref, True)
    attend(pc_ref, 256, 512, 4, oc_ref, False)


def _ctx_attention(sink, pa, pb, pc, *, B, T, C):
    cb = B * T // C
    row = lambda b: (cb + b, 0)
    return pl.pallas_call(
        _ctx_kernel, grid=(B,),
        in_specs=[pl.BlockSpec(memory_space=pltpu.SMEM),
                  pl.BlockSpec((C, 768), row), pl.BlockSpec((C, 512), row), pl.BlockSpec((C, 768), row)],
        out_specs=[pl.BlockSpec((C, 256), lambda b: (b, 0))] * 3,
        out_shape=[jax.ShapeDtypeStruct((B * C, 256), BF16)] * 3,
        compiler_params=_cparams(("arbitrary",)), name="ctx_attention",
    )(sink, pa, pb, pc)


def _gla_dir(pd_ref, o_ref, st_ref, reverse):
    nchunk = GLA_BLOCK // GLA_CHUNK
    L, DK, DV = GLA_CHUNK, N_HEADS * GLA_DK, N_HEADS * GLA_DV

    def iota(shape, dim):
        return lax.broadcasted_iota(I32, shape, dim)

    ri, ci = iota((L, L), 0), iota((L, L), 1)
    tri = jnp.where((ci >= ri) if reverse else (ci <= ri), 1.0, 0.0).astype(BF16)
    tri_t = jnp.where((ri >= ci) if reverse else (ri <= ci), 1.0, 0.0).astype(BF16)
    kcol, qrow = iota((L, N_HEADS * L), 1) % L, iota((L, N_HEADS * L), 0)
    keep = (kcol >= qrow) if reverse else (kcol <= qrow)
    k_owner = [iota((L, DK), 1) // GLA_DK == h for h in range(N_HEADS)]
    v_owner = [iota((L, DV), 1) // GLA_DV == h for h in range(N_HEADS)]
    s_owner = iota((DK, DV), 0) // GLA_DK == iota((DK, DV), 1) // GLA_DV
    gcol = 896 if reverse else 768
    end = 0 if reverse else L - 1
    order = range(nchunk - 1, -1, -1) if reverse else range(nchunk)

    def dot3(a_parts, b_parts):
        return sum(jnp.dot(a, b, preferred_element_type=F32) for a in a_parts for b in b_parts)

    per_chunk = []
    for c in order:
        rs = slice(c * L, (c + 1) * L)
        g = pd_ref[rs, gcol:gcol + DK]
        q = pd_ref[rs, 0:DK]
        k = pd_ref[rs, DK:2 * DK]
        vb = pd_ref[rs, 2 * DK:2 * DK + DV].astype(BF16)
        b = dot3([tri], _split3(g))
        bt = dot3(_split3(g.T), [tri_t])
        bend = bt[:, end:end + 1]
        qd = (q * jnp.exp(b)).astype(BF16)
        ki = (k * jnp.exp(-b)).astype(BF16)
        kdt = (k.T * jnp.exp(bend - bt)).astype(BF16)
        kbd = jnp.concatenate([jnp.where(m, ki, jnp.zeros_like(ki)) for m in k_owner], 0)
        vbd = jnp.concatenate([jnp.where(m, vb, jnp.zeros_like(vb)) for m in v_owner], 0)
        att = lax.dot_general(qd, kbd, _NT, preferred_element_type=F32)
        att = jnp.where(keep, att, 0.0).astype(BF16)
        intra = jnp.dot(att, vbd, preferred_element_type=F32)
        u = jnp.where(s_owner, jnp.dot(kdt, vb, preferred_element_type=F32), 0.0)
        per_chunk.append((rs, qd, intra, u, jnp.exp(bend)))
    st = st_ref[...]
    for rs, qd, intra, u, dec in per_chunk:
        o_ref[0, rs, :] = intra + jnp.dot(qd, st.astype(BF16), preferred_element_type=F32)
        st = st * dec + u
    st_ref[...] = st


def _gla_kernel(pd_ref, o_ref, st_ref):
    d = pl.program_id(1)
    s = pl.program_id(2)

    @pl.when(s == 0)
    def _():
        st_ref[...] = jnp.zeros_like(st_ref)

    @pl.when(d == 0)
    def _():
        _gla_dir(pd_ref, o_ref, st_ref, False)

    @pl.when(d == 1)
    def _():
        _gla_dir(pd_ref, o_ref, st_ref, True)


def _gla(pd, *, B, T, C):
    N = pd.shape[0]
    assert C == GLA_BLOCK
    nlb = T // GLA_BLOCK
    cb = B * T // GLA_BLOCK

    def rb(b, d, s):
        lat = b * nlb + jnp.where(d == 0, s - 1, nlb - s)
        return jnp.where(s == 0, cb + b, lat)

    return pl.pallas_call(
        _gla_kernel, grid=(B, 2, nlb + 1),
        in_specs=[pl.BlockSpec((GLA_BLOCK, 1024), lambda b, d, s: (rb(b, d, s), 0))],
        out_specs=pl.BlockSpec((1, GLA_BLOCK, 256), lambda b, d, s: (d, rb(b, d, s), 0)),
        out_shape=jax.ShapeDtypeStruct((2, N, 256), F32),
        scratch_shapes=[pltpu.VMEM((N_HEADS * GLA_DK, N_HEADS * GLA_DV), F32)],
        compiler_params=_cparams(("arbitrary", "arbitrary", "arbitrary")), name="gla_scan",
    )(pd)


def _pack_bf16_pairs(h):
    w = h.shape[-1] // 2
    bits = lax.bitcast_convert_type(h.astype(BF16).astype(F32), U32)
    return (bits[:, w:] & jnp.uint32(0xFFFF0000)) | (bits[:, :w] >> 16)


def _unpack_bf16_pairs(u):
    lo = lax.bitcast_convert_type(u << 16, F32)
    hi = lax.bitcast_convert_type(u & jnp.uint32(0xFFFF0000), F32)
    return jnp.concatenate([lo, hi], -1).astype(BF16)


def _outproj_kernel(x_ref, oa_ref, ob_ref, oc_ref, oac_ref, obc_ref, occ_ref, ofb_ref, r_ref, gn_ref, wout_ref,
                    g1_ref, sh2_ref, sc2_ref, lng_ref, lnb_ref, rw1_ref, rw2_ref, rb_ref,
                    x1_ref, hp_ref, sel_ref, gate_ref, rank_ref, count_ref, cnt_ref, *, alpha, nlat_t):
    is_lat = pl.program_id(0) < nlat_t
    pick = lambda lat_ref, ctx_ref: jnp.where(is_lat, lat_ref[...], ctx_ref[...])
    s = ofb_ref[0] + ofb_ref[1]
    parts = []
    for h in range(N_HEADS):
        parts.append(_rms(s[:, GLA_DV * h:GLA_DV * (h + 1)]))
    r = r_ref[...]
    od = jnp.concatenate(parts, -1) * gn_ref[...] * (r * _sigmoid(r))
    ocat = jnp.concatenate([pick(oa_ref, oac_ref), pick(ob_ref, obc_ref), pick(oc_ref, occ_ref), od.astype(BF16)], -1)
    o = jnp.dot(ocat, wout_ref[...], preferred_element_type=F32)
    x1 = _layer_norm(alpha * x_ref[...] + g1_ref[0] * o) * lng_ref[...] + lnb_ref[...]
    x1_ref[...] = x1
    h2 = _layer_norm(x1) * (1.0 + sc2_ref[0]) + sh2_ref[0]
    hp_ref[...] = _pack_bf16_pairs(h2)
    a1, a2, _ = _split3(h2)
    w1, w2 = rw1_ref[...], rw2_ref[...]
    logits = (jnp.dot(a1, w1, preferred_element_type=F32) + jnp.dot(a1, w2, preferred_element_type=F32)
              + jnp.dot(a2, w1, preferred_element_type=F32))
    scores = _sigmoid(logits)
    work = scores + rb_ref[...]
    tm, ne = scores.shape
    lane = lax.broadcasted_iota(I32, (tm, ne), 1).astype(F32)
    lane8 = lax.broadcasted_iota(I32, (tm, TOP_K), 1)
    sel = jnp.zeros((tm, TOP_K), F32)
    gsel = jnp.zeros((tm, TOP_K), F32)
    chosen = jnp.zeros((tm, ne), F32)
    picks = []
    for k in range(TOP_K):
        m = jnp.max(work, -1, keepdims=True)
        idx = jnp.min(jnp.where(work == m, lane, float(ne)), -1, keepdims=True)
        hit = lane == idx
        picks.append(idx)
        sc = jnp.sum(jnp.where(hit, scores, 0.0), -1, keepdims=True)
        sel = jnp.where(lane8 == k, idx, sel)
        gsel = jnp.where(lane8 == k, sc, gsel)
        chosen = jnp.where(hit, 1.0, chosen)
        work = jnp.where(hit, -jnp.inf, work)
    sel_ref[...] = sel.astype(I32)
    gate_ref[...] = ROUTED_SCALE * gsel / jnp.sum(gsel, -1, keepdims=True)
    @pl.when(pl.program_id(0) == 0)
    def _():
        cnt_ref[...] = jnp.zeros_like(cnt_ref)

    ri = lax.broadcasted_iota(I32, (tm, tm), 0)
    ci = lax.broadcasted_iota(I32, (tm, tm), 1)
    before = jnp.where(ci < ri, 1.0, 0.0).astype(BF16)
    prefix = jnp.dot(before, chosen.astype(BF16), preferred_element_type=F32) + cnt_ref[...]
    rank = jnp.zeros((tm, TOP_K), F32)
    for k in range(TOP_K):
        rk = jnp.sum(jnp.where(lane == picks[k], prefix, 0.0), -1, keepdims=True)
        rank = jnp.where(lane8 == k, rk, rank)
    rank_ref[...] = rank.astype(I32)
    cnt_ref[...] = cnt_ref[...] + jnp.sum(chosen, 0, keepdims=True)
    count_ref[...] = cnt_ref[...].astype(I32)


def _outproj(xall, o_lat, o_ctx, ofb, pd, gn, wout, mods_l, lng, lnb, rw1, rw2, rb, *, n_lat, T, alpha):
    N, D = xall.shape
    tm = TOK_TILE
    nlat_t, per_b, nb = n_lat // tm, T // tm, n_lat // T

    def midx(k):
        return lambda i: (jnp.where(i < nlat_t, i // per_b, nb) * 6 + k, 0, 0)

    const = lambda i: (0, 0)
    row = lambda i: (i, 0)
    lat_row = lambda i: (jnp.minimum(i, nlat_t - 1), 0)
    ctx_row = lambda i: (jnp.maximum(i - nlat_t, 0), 0)
    E = rw1.shape[1]
    kern = functools.partial(_outproj_kernel, alpha=alpha, nlat_t=nlat_t)
    return pl.pallas_call(
        kern, grid=(N // tm,),
        in_specs=[pl.BlockSpec((tm, D), row)] + [pl.BlockSpec((tm, 256), lat_row)] * 3
                 + [pl.BlockSpec((tm, 256), ctx_row)] * 3 + [
                  pl.BlockSpec((2, tm, 256), lambda i: (0, i, 0)),
                  pl.BlockSpec((tm, 256), lambda i: (i, 2)), pl.BlockSpec((1, 256), const),
                  pl.BlockSpec((D, D), const),
                  pl.BlockSpec((1, 1, D), midx(2)), pl.BlockSpec((1, 1, D), midx(3)), pl.BlockSpec((1, 1, D), midx(4)),
                  pl.BlockSpec((1, D), const), pl.BlockSpec((1, D), const),
                  pl.BlockSpec((D, E), const), pl.BlockSpec((D, E), const), pl.BlockSpec((1, E), const)],
        out_specs=[pl.BlockSpec((tm, D), row), pl.BlockSpec((tm, D // 2), row),
                   pl.BlockSpec((tm, TOP_K), row), pl.BlockSpec((tm, TOP_K), row), pl.BlockSpec((tm, TOP_K), row),
                   pl.BlockSpec((1, E), const)],
        out_shape=[jax.ShapeDtypeStruct((N, D), F32), jax.ShapeDtypeStruct((N, D // 2), U32),
                   jax.ShapeDtypeStruct((N, TOP_K), I32), jax.ShapeDtypeStruct((N, TOP_K), F32),
                   jax.ShapeDtypeStruct((N, TOP_K), I32), jax.ShapeDtypeStruct((1, E), I32)],
        scratch_shapes=[pltpu.VMEM((1, E), F32)],
        compiler_params=_cparams(("arbitrary",)), name="outproj_router",
    )(xall, *o_lat, *o_ctx, ofb, pd, gn, wout, mods_l, mods_l, mods_l, lng, lnb, rw1, rw2, rb)


def _sc_mesh():
    return plsc.VectorSubcoreMesh(core_axis_name="c", subcore_axis_name="s")


def _sc_worker():
    return lax.axis_index("s") * SC_CORES + lax.axis_index("c")


def _sc_dispatch(hp, dest_k, rows):
    N, W = hp.shape
    tpw = N // SC_WORKERS
    ch = SC_DISPATCH_CHUNK
    assert N % SC_WORKERS == 0 and tpw % ch == 0
    nch = tpw // ch
    idx = dest_k.reshape(TOP_K, SC_WORKERS, nch, ch).transpose(1, 2, 0, 3)

    @functools.partial(
        pl.kernel, mesh=_sc_mesh(), out_type=jax.ShapeDtypeStruct((rows, W), hp.dtype),
        scratch_types=[pltpu.VMEM((nch, TOP_K, ch), I32), pltpu.VMEM((ch, W), hp.dtype), pltpu.SemaphoreType.DMA],
        compiler_params=pltpu.CompilerParams(use_tc_tiling_on_sc=True), name="sc_dispatch")
    def scatter(hp_hbm, idx_hbm, out_hbm, idx_v, rows_v, sem):
        wid = _sc_worker()
        base = wid * tpw
        pltpu.sync_copy(idx_hbm.at[wid], idx_v)

        @pl.loop(0, nch)
        def _(c):
            pltpu.sync_copy(hp_hbm.at[pl.ds(base + c * ch, ch)], rows_v)
            copies = [pltpu.async_copy(rows_v, out_hbm.at[idx_v.at[c, k]], sem) for k in range(TOP_K)]
            for cp in copies:
                cp.wait()

    return scatter(hp, idx)


def _sc_gather(src, idx):
    R, W = idx.shape[0], src.shape[1]
    per_w = R // SC_WORKERS
    ch = SC_GATHER_CHUNK
    assert R % SC_WORKERS == 0 and per_w % ch == 0
    nch = per_w // ch

    @functools.partial(
        pl.kernel, mesh=_sc_mesh(), out_type=jax.ShapeDtypeStruct((R, W), src.dtype),
        scratch_types=[pltpu.VMEM((nch, ch), I32), pltpu.VMEM((ch, W), src.dtype), pltpu.SemaphoreType.DMA],
        compiler_params=pltpu.CompilerParams(use_tc_tiling_on_sc=True), name="sc_gather")
    def gather(src_hbm, idx_hbm, out_hbm, idx_v, rows_v, sem):
        wid = _sc_worker()
        base = wid * per_w
        pltpu.sync_copy(idx_hbm.at[wid], idx_v)

        @pl.loop(0, nch)
        def _(c):
            pltpu.async_copy(src_hbm.at[idx_v.at[c]], rows_v, sem).wait()
            pltpu.sync_copy(rows_v, out_hbm.at[pl.ds(base + c * ch, ch)])

    return gather(src, idx.reshape(SC_WORKERS, nch, ch))


def _swiglu(xb, w1, w3, w2):
    a = jnp.dot(xb, w1, preferred_element_type=F32)
    b = jnp.dot(xb, w3, preferred_element_type=F32)
    return jnp.dot((a * _sigmoid(a) * b).astype(BF16), w2, preferred_element_type=F32)


def _expert_kernel(be_ref, nu_ref, xs_ref, w1_ref, w3_ref, w2_ref, y_ref, w1b_ref, w3b_ref, w2b_ref):
    i = pl.program_id(0)
    used = i < nu_ref[0]

    @pl.when(used & ((i == 0) | (be_ref[i] != be_ref[jnp.maximum(i - 1, 0)])))
    def _():
        w1b_ref[...] = w1_ref[0, 0].astype(BF16)
        w3b_ref[...] = w3_ref[0, 0].astype(BF16)
        w2b_ref[...] = w2_ref[0, 0].astype(BF16)

    @pl.when(used)
    def _():
        xb = _unpack_bf16_pairs(xs_ref[...])
        y_ref[...] = _pack_bf16_pairs(_swiglu(xb, w1b_ref[...], w3b_ref[...], w2b_ref[...]))

    @pl.when(jnp.logical_not(used))
    def _():
        y_ref[...] = jnp.zeros_like(y_ref)


def _experts(xs, blk_exp, n_used, w1, w3, w2, layer):
    rows, half = xs.shape
    _, E, D, F = w1.shape
    nblk = rows // MOE_BM
    wmap = lambda i, be, nu: (layer, be[i], 0, 0)
    grid_spec = pltpu.PrefetchScalarGridSpec(
        num_scalar_prefetch=2, grid=(nblk,),
        in_specs=[pl.BlockSpec((MOE_BM, half), lambda i, be, nu: (i, 0)),
                  pl.BlockSpec((1, 1, D, F), wmap), pl.BlockSpec((1, 1, D, F), wmap), pl.BlockSpec((1, 1, F, D), wmap)],
        out_specs=pl.BlockSpec((MOE_BM, half), lambda i, be, nu: (i, 0)),
        scratch_shapes=[pltpu.VMEM((D, F), BF16), pltpu.VMEM((D, F), BF16), pltpu.VMEM((F, D), BF16)])
    return pl.pallas_call(
        _expert_kernel, grid_spec=grid_spec,
        out_shape=jax.ShapeDtypeStruct((rows, half), U32),
        compiler_params=_cparams(("arbitrary",)), name="expert_ffn",
    )(blk_exp, n_used, xs, w1, w3, w2)


def _combine_kernel(yg_ref, gate_ref, hp_ref, sw1_ref, sw3_ref, sw2_ref, x_ref, g2_ref, lng_ref, lnb_ref,
                    o_ref, *, alpha):
    f = _swiglu(_unpack_bf16_pairs(hp_ref[...]), sw1_ref[...], sw3_ref[...], sw2_ref[...])
    gates = gate_ref[...]
    lo = hi = 0.0
    for k in range(TOP_K):
        u = yg_ref[k]
        g = gates[:, k:k + 1]
        lo = lo + g * lax.bitcast_convert_type(u << 16, F32)
        hi = hi + g * lax.bitcast_convert_type(u & jnp.uint32(0xFFFF0000), F32)
    f = f + jnp.concatenate([lo, hi], -1)
    o_ref[...] = _layer_norm(alpha * x_ref[...] + g2_ref[0] * f) * lng_ref[...] + lnb_ref[...]


def _combine(yg, gates, hp, sw1, sw3, sw2, x1, mods_l, lng, lnb, *, n_lat, T, alpha, n_out):
    D = x1.shape[1]
    N = n_out
    tm = CMB_TILE
    nlat_t, per_b, nb = n_lat // tm, T // tm, n_lat // T
    F = sw1.shape[1]
    const = lambda i: (0, 0)
    row = lambda i: (i, 0)
    kern = functools.partial(_combine_kernel, alpha=alpha)
    return pl.pallas_call(
        kern, grid=(N // tm,),
        in_specs=[pl.BlockSpec((TOP_K, tm, D // 2), lambda i: (0, i, 0)), pl.BlockSpec((tm, TOP_K), row),
                  pl.BlockSpec((tm, D // 2), row),
                  pl.BlockSpec((D, F), const), pl.BlockSpec((D, F), const), pl.BlockSpec((F, D), const),
                  pl.BlockSpec((tm, D), row),
                  pl.BlockSpec((1, 1, D), lambda i: (jnp.where(i < nlat_t, i // per_b, nb) * 6 + 5, 0, 0)),
                  pl.BlockSpec((1, D), const), pl.BlockSpec((1, D), const)],
        out_specs=pl.BlockSpec((tm, D), row),
        out_shape=jax.ShapeDtypeStruct((N, D), F32),
        compiler_params=_cparams(("arbitrary",)), name="moe_combine",
    )(yg, gates, hp, sw1, sw3, sw2, x1, mods_l, lng, lnb)


def _dest_kernel(start_ref, sel_ref, rank_ref, o_ref):
    sel = sel_ref[...]

    def body(e, acc):
        return jnp.where(sel == e, start_ref[e], acc)

    o_ref[...] = lax.fori_loop(0, start_ref.shape[0], body, jnp.zeros_like(sel)) + rank_ref[...]


def _dispatch_plan(sel, rank, counts):
    N = sel.shape[0]
    n_experts = counts.shape[0]
    padded = (counts + MOE_BM - 1) // MOE_BM * MOE_BM
    pad_end = jnp.cumsum(padded)
    pad_start = (pad_end - padded).astype(I32)
    dense = (N * TOP_K // 128, 128)
    dest = pl.pallas_call(
        _dest_kernel,
        in_specs=[pl.BlockSpec(memory_space=pltpu.SMEM), pl.BlockSpec(memory_space=pltpu.VMEM),
                  pl.BlockSpec(memory_space=pltpu.VMEM)],
        out_specs=pl.BlockSpec(memory_space=pltpu.VMEM),
        out_shape=jax.ShapeDtypeStruct(dense, I32), name="moe_dest_rows",
    )(pad_start, sel.reshape(dense), rank.reshape(dense))
    dest_k = dest.reshape(N, TOP_K).T
    n_blocks = -(-(N * TOP_K) // MOE_BM) + n_experts
    blk_first = jnp.arange(n_blocks, dtype=I32) * MOE_BM
    blk_exp = jnp.minimum(jnp.sum((pad_end[None, :] <= blk_first[:, None]).astype(I32), -1), n_experts - 1)
    n_used = (pad_end[-1] // MOE_BM).astype(I32).reshape(1)
    return dest_k, blk_exp, n_used, n_blocks * MOE_BM


def _rope_tables(T, tile):
    t = jnp.arange(T)
    row = (t // GRID_W).astype(F32)[:, None]
    col = (t % GRID_W).astype(F32)[:, None]

    def group(half):
        inv = ROPE_THETA ** (-jnp.arange(half, dtype=F32) / half)
        ar, ac = row * inv, col * inv
        cos = jnp.concatenate([jnp.cos(ar), jnp.cos(ar), jnp.cos(ac), jnp.cos(ac)], -1)
        sin = jnp.concatenate([-jnp.sin(ar), jnp.sin(ar), -jnp.sin(ac), jnp.sin(ac)], -1)
        return cos, sin

    ones32, zeros32 = jnp.ones((T, 32), F32), jnp.zeros((T, 32), F32)
    cw, sw = group(16)
    cw, sw = jnp.tile(cw, (1, 2)), jnp.tile(sw, (1, 2))
    c8, s8 = group(8)
    cq = jnp.tile(jnp.concatenate([ones32, c8], -1), (1, 2))
    sq = jnp.tile(jnp.concatenate([zeros32, s8], -1), (1, 2))
    ck = jnp.concatenate([c8, ones32, ones32, ones32], -1)
    sk = jnp.concatenate([s8, zeros32, zeros32, zeros32], -1)

    def pad(a, ident):
        return jnp.concatenate([a, jnp.full((tile, 128), ident, F32)], 0)

    return [pad(cw, 1.0), pad(sw, 0.0), pad(cq, 1.0), pad(sq, 0.0), pad(ck, 1.0), pad(sk, 0.0)]


def _layer_weights(w_in, mla_w_uq, mla_w_ukv, gla_w_gf, gla_w_gb, gla_b_gf, gla_b_gb):
    D = w_in.shape[0]
    z = lambda n: jnp.zeros((D, n), F32)
    win = jnp.concatenate([
        w_in[:, 0:256] * 0.125, w_in[:, 256:768],
        w_in[:, 768:1024] * 0.125, w_in[:, 1024:1280],
        w_in[:, 1280:1696], z(96),
        w_in[:, 1696:2464],
        w_in[:, 2464:2496], z(96)], -1).astype(BF16)
    wuq = (mla_w_uq * 0.125).astype(BF16)
    ukv = mla_w_ukv.reshape(-1, N_HEADS, MLA_NOPE + HEAD_DIM)
    kv_rank = ukv.shape[0]
    wk = jnp.concatenate([ukv[:, :, :MLA_NOPE], jnp.zeros((kv_rank, N_HEADS, MLA_ROPE), F32)], -1)
    place = jnp.concatenate([jnp.zeros((MLA_ROPE, MLA_NOPE), F32), jnp.eye(MLA_ROPE, dtype=F32)], -1)
    place = jnp.concatenate([jnp.tile(place, (1, N_HEADS)), jnp.zeros((128 - MLA_ROPE, 256), F32)], 0)
    wkk = jnp.concatenate([wk.reshape(kv_rank, 256), place], 0).astype(BF16)
    wv = ukv[:, :, MLA_NOPE:].reshape(kv_rank, 256).astype(BF16)
    r = GLA_GATE_RANK
    wg = jnp.zeros((128, 256), F32).at[0:r, 0:128].set(gla_w_gf).at[r:2 * r, 128:256].set(gla_w_gb).astype(BF16)
    bg = jnp.concatenate([gla_b_gf, gla_b_gb])[None, :]
    return win, wuq, wkk, wv, wg, bg


def kernel(x, c, ctx, c_ctx, w_ada, b_ada, w_in, na_rpb, wa_sink, mla_g_q, mla_g_kv, mla_w_uq, mla_w_ukv,
           gla_w_gf, gla_b_gf, gla_w_gb, gla_b_gb, gla_g_norm, w_out, ln1_g, ln1_b, ln2_g, ln2_b,
           router_w, router_bias, exp_w1, exp_w3, exp_w2, sh_w1, sh_w3, sh_w2):
    B, T, D = x.shape
    C = ctx.shape[1]
    L = w_ada.shape[0]
    E = router_w.shape[-1]
    n_lat = B * T
    alpha = (2 * L) ** 0.25
    dims = dict(B=B, T=T, C=C)

    cc = jnp.zeros((8, D), F32).at[:B].set(c).at[B].set(c_ctx)
    mods = _mods(cc, w_ada, b_ada)
    tabs = _rope_tables(T, TOK_TILE)
    xall = jnp.concatenate([x.reshape(n_lat, D), ctx.reshape(B * C, D)], 0)

    for l in range(L):
        mods_l = mods[l].reshape(8 * 6, 1, D)
        win, wuq, wkk, wv, wg, bg = _layer_weights(w_in[l], mla_w_uq[l], mla_w_ukv[l], gla_w_gf[l], gla_w_gb[l],
                                                   gla_b_gf[l], gla_b_gb[l])
        pa, pb, pc, pd = _inproj(xall, mods_l, win, tabs, mla_g_q[l][None], mla_g_kv[l][None], wuq, wkk, wv, wg, bg,
                                 n_lat=n_lat, T=T)
        oa = _na_attention(pa, _na_bias_tables(na_rpb[l], T // GRID_W), **dims)
        ob = _wa_attention(pb, wa_sink[l], **dims)
        oc = _mla_attention(pc, **dims)
        o_ctx = _ctx_attention(wa_sink[l], pa, pb, pc, **dims)
        ofb = _gla(pd, **dims)

        rw1 = router_w[l].astype(BF16)
        rw2 = (router_w[l] - rw1.astype(F32)).astype(BF16)
        x1, hp, sel, gates, rank, counts = _outproj(
            xall, (oa, ob, oc), o_ctx, ofb, pd, jnp.tile(gla_g_norm[l], N_HEADS)[None], w_out[l].astype(BF16), mods_l,
            ln1_g[l][None], ln1_b[l][None], rw1, rw2, router_bias[l][None], n_lat=n_lat, T=T, alpha=alpha)

        dest_k, blk_exp, n_used, rows = _dispatch_plan(sel, rank, counts.reshape(E))
        xs = _sc_dispatch(hp, dest_k, rows)
        y = _experts(xs, blk_exp, n_used, exp_w1, exp_w3, exp_w2, l)
        yg = _sc_gather(y, dest_k.reshape(-1)).reshape(TOP_K, xall.shape[0], D // 2)
        xall = _combine(yg, gates, hp, sh_w1[l].astype(BF16), sh_w3[l].astype(BF16), sh_w2[l].astype(BF16),
                        x1, mods_l, ln2_g[l][None], ln2_b[l][None], n_lat=n_lat, T=T, alpha=alpha,
                        n_out=n_lat if l == L - 1 else x1.shape[0])

    return xall.reshape(B, T, D)
```

```python
import functools

import numpy as np
import jax
import jax.numpy as jnp
from jax import lax
from jax.experimental import pallas as pl
from jax.experimental.pallas import tpu as pltpu
from jax.experimental.pallas import tpu_sc as plsc

F32 = jnp.float32
BF16 = jnp.bfloat16
U32 = jnp.uint32
I32 = jnp.int32

GRID_W = 64
HEAD_DIM = 64
N_HEADS = 4
NA_KH, NA_KW = 8, 16
WA_WINDOW = 128
MLA_NOPE, MLA_ROPE = 32, 32
GLA_DK, GLA_DV = 32, 64
GLA_GATE_RANK = 16
GLA_GATE_NORM = 16.0
GLA_CHUNK = 64
TOP_K = 8
ROUTED_SCALE = 2.5
ROPE_THETA = 10000.0
LN_EPS = 1e-5
RMS_EPS = 1e-6
NEG = -1e30

VMEM_LIMIT = 56 * 1024 * 1024
TOK_TILE = 512
ATT_TQ = 512
MLA_TK = 1024
GLA_BLOCK = 256
MOE_BM = 512
MOE_SPLITS = 2
CMB_TILE = 256
SC_CORES = 2
SC_WORKERS = 32
SC_DISPATCH_CHUNK = 48
SC_GATHER_CHUNK = 64

PW = 2688
_NT = (((1,), (1,)), ((), ()))


def _cparams(sem, vmem=VMEM_LIMIT):
    return pltpu.CompilerParams(dimension_semantics=sem, vmem_limit_bytes=vmem)


def _sigmoid(x):
    return 1.0 / (1.0 + jnp.exp(-x))


def _layer_norm(x):
    mu = jnp.mean(x, -1, keepdims=True)
    xc = x - mu
    var = jnp.mean(xc * xc, -1, keepdims=True)
    return xc * lax.rsqrt(var + LN_EPS)


def _rms(x):
    return x * lax.rsqrt(jnp.mean(x * x, -1, keepdims=True) + RMS_EPS)


def _bdot(a, b):
    return jnp.dot(a.astype(BF16), b.astype(BF16), preferred_element_type=F32)


def _bdot_nt(a, b):
    return lax.dot_general(a.astype(BF16), b.astype(BF16), _NT, preferred_element_type=F32)


def _split3(a):
    a1 = a.astype(BF16)
    r = a - a1.astype(F32)
    a2 = r.astype(BF16)
    a3 = (r - a2.astype(F32)).astype(BF16)
    return a1, a2, a3


def _mods_kernel(cc_ref, w_ref, b_ref, o_ref):
    cc = cc_ref[...]
    s = cc * _sigmoid(cc)
    o_ref[0] = _bdot(s, w_ref[0]) + b_ref[0]


def _mods(cc, w_ada, b_ada):
    L, D, W = w_ada.shape
    tn = 1536
    return pl.pallas_call(
        _mods_kernel,
        grid=(L, W // tn),
        in_specs=[pl.BlockSpec((8, D), lambda l, j: (0, 0)),
                  pl.BlockSpec((1, D, tn), lambda l, j: (l, 0, j)),
                  pl.BlockSpec((1, 1, tn), lambda l, j: (l, 0, j))],
        out_specs=pl.BlockSpec((1, 8, tn), lambda l, j: (l, 0, j)),
        out_shape=jax.ShapeDtypeStruct((L, 8, W), F32),
        compiler_params=_cparams(("arbitrary", "arbitrary")),
        name="adaln_mods",
    )(cc, w_ada, b_ada.reshape(L, 1, W))


def _rope(z, cos, sin, half):
    w = z.shape[-1]
    lane = lax.broadcasted_iota(I32, z.shape, 1)
    first = (lane % (2 * half)) < half
    partner = jnp.where(first, pltpu.roll(z, w - half, 1), pltpu.roll(z, half, 1))
    return z * cos + partner * sin


def _log_sigmoid(x):
    return jnp.minimum(x, 0.0) - jnp.log(1.0 + jnp.exp(-jnp.abs(x)))


def _inproj_kernel(x_ref, shift_ref, scale_ref, win_ref, cw_ref, sw_ref, cq_ref, sq_ref, ck_ref, sk_ref,
                   gq_ref, gkv_ref, wuq_ref, wkk_ref, wv_ref, wg_ref, bg_ref,
                   pa_ref, pb_ref, pc_ref, pd_ref):
    h = _layer_norm(x_ref[...]) * (1.0 + scale_ref[0]) + shift_ref[0]
    p = jnp.dot(h.astype(BF16), win_ref[...], preferred_element_type=F32)
    pa_ref[...] = p[:, 0:768].astype(BF16)
    cw, sw = cw_ref[...], sw_ref[...]
    pb_ref[:, 0:128] = _rope(p[:, 768:896], cw, sw, 16).astype(BF16)
    pb_ref[:, 128:256] = _rope(p[:, 896:1024], cw, sw, 16).astype(BF16)
    pb_ref[:, 256:384] = _rope(p[:, 1024:1152], cw, sw, 16).astype(BF16)
    pb_ref[:, 384:512] = p[:, 1152:1280].astype(BF16)
    cqn = _rms(p[:, 1280:1536]) * gq_ref[...]
    q = jnp.dot(cqn.astype(BF16), wuq_ref[...], preferred_element_type=F32)
    cq, sq = cq_ref[...], sq_ref[...]
    pc_ref[:, 0:128] = _rope(q[:, 0:128], cq, sq, 8).astype(BF16)
    pc_ref[:, 128:256] = _rope(q[:, 128:256], cq, sq, 8).astype(BF16)
    ckvn = (_rms(p[:, 1536:1664]) * gkv_ref[...]).astype(BF16)
    kr = _rope(p[:, 1664:1792], ck_ref[...], sk_ref[...], 8).astype(BF16)
    kin = jnp.concatenate([ckvn, kr], axis=-1)
    pc_ref[:, 256:512] = jnp.dot(kin, wkk_ref[...], preferred_element_type=F32).astype(BF16)
    pc_ref[:, 512:768] = jnp.dot(ckvn, wv_ref[...], preferred_element_type=F32).astype(BF16)
    pd_ref[:, 0:128] = p[:, 1792:1920] * (GLA_DK ** -0.5)
    pd_ref[:, 128:768] = p[:, 1920:2560]
    pre = jnp.dot(p[:, 2560:2688].astype(BF16), wg_ref[...], preferred_element_type=F32) + bg_ref[...]
    pd_ref[:, 768:1024] = _log_sigmoid(pre) * (1.0 / GLA_GATE_NORM)


def _inproj(xall, mods_l, win, tabs, gq, gkv, wuq, wkk, wv, wg, bg, *, n_lat, T):
    N, D = xall.shape
    tm = TOK_TILE
    nlat_t = n_lat // tm
    per_b = T // tm
    nb = n_lat // T

    def midx(k):
        return lambda i: (jnp.where(i < nlat_t, i // per_b, nb) * 6 + k, 0, 0)

    def tidx(i):
        return (jnp.where(i < nlat_t, i % per_b, per_b), 0)

    const = lambda i: (0, 0)
    tab_spec = pl.BlockSpec((tm, 128), tidx)
    in_specs = [pl.BlockSpec((tm, D), lambda i: (i, 0)),
                pl.BlockSpec((1, 1, D), midx(0)), pl.BlockSpec((1, 1, D), midx(1)),
                pl.BlockSpec((D, PW), const)] + [tab_spec] * 6 + [
                pl.BlockSpec((1, 256), const), pl.BlockSpec((1, 128), const),
                pl.BlockSpec((256, 256), const), pl.BlockSpec((256, 256), const),
                pl.BlockSpec((128, 256), const), pl.BlockSpec((128, 256), const),
                pl.BlockSpec((1, 256), const)]
    out_specs = [pl.BlockSpec((tm, 768), lambda i: (i, 0)), pl.BlockSpec((tm, 512), lambda i: (i, 0)),
                 pl.BlockSpec((tm, 768), lambda i: (i, 0)), pl.BlockSpec((tm, 1024), lambda i: (i, 0))]
    out_shape = [jax.ShapeDtypeStruct((N, 768), BF16), jax.ShapeDtypeStruct((N, 512), BF16),
                 jax.ShapeDtypeStruct((N, 768), BF16), jax.ShapeDtypeStruct((N, 1024), F32)]
    return pl.pallas_call(
        _inproj_kernel, grid=(N // tm,), in_specs=in_specs, out_specs=out_specs, out_shape=out_shape,
        compiler_params=_cparams(("arbitrary",)), name="inproj",
    )(xall, mods_l, mods_l, win, *tabs, gq, gkv, wuq, wkk, wv, wg, bg)


def _softmax_av(parts, extra_logit=None):
    m = functools.reduce(jnp.maximum, [jnp.max(s, -1, keepdims=True) for s, _ in parts])
    if extra_logit is not None:
        m = jnp.maximum(m, extra_logit)
    l = 0.0 if extra_logit is None else jnp.exp(extra_logit - m)
    o = None
    for s, v in parts:
        e = jnp.exp(s - m)
        l = l + jnp.sum(e, -1, keepdims=True)
        c = jnp.dot(e.astype(BF16), v, preferred_element_type=F32)
        o = c if o is None else o + c
    return o * (1.0 / l)


def _na_kernel(q_ref, k_ref, v_ref, kc_ref, vc_ref, bias_ref, o_ref, *, rows):
    j = pl.program_id(1)
    ws = pl.multiple_of(jnp.clip(8 * j - 4, 0, rows - 16) * GRID_W, 256)
    win = 16 * GRID_W
    outs = []
    for h in range(N_HEADS):
        sl = slice(HEAD_DIM * h, HEAD_DIM * (h + 1))
        q = q_ref[:, sl]
        s = lax.dot_general(q, k_ref[pl.ds(ws, win), sl], _NT, preferred_element_type=F32) + bias_ref[0, h]
        sc = lax.dot_general(q, kc_ref[:, sl], _NT, preferred_element_type=F32)
        outs.append(_softmax_av([(s, v_ref[pl.ds(ws, win), sl]), (sc, vc_ref[:, sl])]))
    o_ref[...] = jnp.concatenate(outs, -1).astype(BF16)


def _na_bias_tables(rpb, rows):
    nj = rows // 8
    ro_all, ok_all = [], []
    for j in (0, 1, nj - 1):
        ws = int(np.clip(8 * j - 4, 0, rows - 16))
        r = 8 * j + np.arange(8)[:, None, None, None]
        qc = np.arange(GRID_W)[None, :, None, None]
        kr = ws + np.arange(16)[None, None, :, None]
        kc = np.arange(GRID_W)[None, None, None, :]
        rs = np.clip(r - NA_KH // 2, 0, rows - NA_KH)
        cs = np.clip(qc - NA_KW // 2, 0, GRID_W - NA_KW)
        ok = (kr >= rs) & (kr < rs + NA_KH) & (kc >= cs) & (kc < cs + NA_KW)
        ro = np.clip(kr - r + (NA_KH - 1), 0, 2 * NA_KH - 2)
        shp = (8, GRID_W, 16, GRID_W)
        ro_all.append(np.broadcast_to(ro, shp).reshape(512, 1024))
        ok_all.append(np.broadcast_to(ok, shp).reshape(512, 1024))
    ok = np.stack(ok_all)
    hi = lax.Precision.HIGHEST
    co = np.clip(np.arange(GRID_W)[None, :] - np.arange(GRID_W)[:, None] + (NA_KW - 1), 0, 2 * NA_KW - 2)
    col_hot = (co[..., None] == np.arange(2 * NA_KW - 1)).astype(np.float32)
    cb = jnp.einsum('hdo,qko->hdqk', rpb, col_hot, precision=hi)
    ro = np.stack(ro_all).reshape(3, 8, GRID_W, 16, GRID_W)[:, :, 0, :, 0]
    row_hot = (ro[..., None] == np.arange(2 * NA_KH - 1)).astype(np.float32)
    b = jnp.einsum('vrsd,hdqk->vhrqsk', row_hot, cb, precision=hi)
    b = b.reshape(3, rpb.shape[0], 512, 1024)
    return jnp.where(ok[:, None], b, NEG)


def _na_attention(pa, bias, *, B, T, C):
    rows = T // GRID_W
    nj = T // ATT_TQ
    cb = B * T // C
    kern = functools.partial(_na_kernel, rows=rows)
    return pl.pallas_call(
        kern, grid=(B, nj),
        in_specs=[pl.BlockSpec((ATT_TQ, 256), lambda b, j: (b * nj + j, 0)),
                  pl.BlockSpec((T, 256), lambda b, j: (b, 1)),
                  pl.BlockSpec((T, 256), lambda b, j: (b, 2)),
                  pl.BlockSpec((C, 256), lambda b, j: (cb + b, 1)),
                  pl.BlockSpec((C, 256), lambda b, j: (cb + b, 2)),
                  pl.BlockSpec((1, N_HEADS, 512, 1024),
                               lambda b, j: (jnp.where(j == 0, 0, jnp.where(j == nj - 1, 2, 1)), 0, 0, 0))],
        out_specs=pl.BlockSpec((ATT_TQ, 256), lambda b, j: (b * nj + j, 0)),
        out_shape=jax.ShapeDtypeStruct((B * T, 256), BF16),
        compiler_params=_cparams(("arbitrary", "arbitrary")), name="na_attention",
    )(pa, pa, pa, pa, pa, bias)


def _wa_kernel(sink_ref, q_ref, k_ref, v_ref, kc_ref, vc_ref, o_ref, *, T):
    i = pl.program_id(1)
    start = i * ATT_TQ
    win = ATT_TQ + 2 * WA_WINDOW
    ws = pl.multiple_of(jnp.clip(start - WA_WINDOW, 0, T - win), 128)
    rel = (lax.broadcasted_iota(I32, (ATT_TQ, win), 1) + (ws - start)) - lax.broadcasted_iota(I32, (ATT_TQ, win), 0)
    valid = jnp.abs(rel) <= WA_WINDOW
    outs = []
    for h in range(N_HEADS):
        sl = slice(HEAD_DIM * h, HEAD_DIM * (h + 1))
        g = h // 2
        ksl = slice(HEAD_DIM * g, HEAD_DIM * (g + 1))
        q = q_ref[:, sl]
        s = lax.dot_general(q, k_ref[pl.ds(ws, win), ksl], _NT, preferred_element_type=F32)
        s = jnp.where(valid, s, NEG)
        sc = lax.dot_general(q, kc_ref[:, ksl], _NT, preferred_element_type=F32)
        outs.append(_softmax_av([(s, v_ref[pl.ds(ws, win), ksl]), (sc, vc_ref[:, ksl])], extra_logit=sink_ref[h]))
    o_ref[...] = jnp.concatenate(outs, -1).astype(BF16)


def _wa_attention(pb, sink, *, B, T, C):
    nj = T // ATT_TQ
    cb = B * T // C
    kern = functools.partial(_wa_kernel, T=T)
    return pl.pallas_call(
        kern, grid=(B, nj),
        in_specs=[pl.BlockSpec(memory_space=pltpu.SMEM),
                  pl.BlockSpec((ATT_TQ, 256), lambda b, j: (b * nj + j, 0)),
                  pl.BlockSpec((T, 128), lambda b, j: (b, 2)),
                  pl.BlockSpec((T, 128), lambda b, j: (b, 3)),
                  pl.BlockSpec((C, 128), lambda b, j: (cb + b, 2)),
                  pl.BlockSpec((C, 128), lambda b, j: (cb + b, 3))],
        out_specs=pl.BlockSpec((ATT_TQ, 256), lambda b, j: (b * nj + j, 0)),
        out_shape=jax.ShapeDtypeStruct((B * T, 256), BF16),
        compiler_params=_cparams(("arbitrary", "arbitrary")), name="wa_attention",
    )(sink, pb, pb, pb, pb, pb)


def _mla_kernel(q_ref, k_ref, v_ref, kc_ref, vc_ref, o_ref, *, T):
    nk = T // MLA_TK
    heads = [slice(HEAD_DIM * h, HEAD_DIM * (h + 1)) for h in range(N_HEADS)]
    qs = [q_ref[:, sl] for sl in heads]
    init = []
    for q, sl in zip(qs, heads):
        sc = lax.dot_general(q, kc_ref[:, sl], _NT, preferred_element_type=F32)
        m0 = jnp.max(sc, -1, keepdims=True)
        e0 = jnp.exp(sc - m0)
        init += [m0, jnp.sum(e0, -1, keepdims=True),
                 jnp.dot(e0.astype(BF16), vc_ref[:, sl], preferred_element_type=F32)]

    def body(c, carry):
        ks = pl.multiple_of(c * MLA_TK, MLA_TK)
        out = []
        for h, (q, sl) in enumerate(zip(qs, heads)):
            m, l, acc = carry[3 * h:3 * h + 3]
            s = lax.dot_general(q, k_ref[pl.ds(ks, MLA_TK), sl], _NT, preferred_element_type=F32)
            mn = jnp.maximum(m, jnp.max(s, -1, keepdims=True))
            a = jnp.exp(m - mn)
            e = jnp.exp(s - mn)
            l = l * a + jnp.sum(e, -1, keepdims=True)
            acc = acc * a + jnp.dot(e.astype(BF16), v_ref[pl.ds(ks, MLA_TK), sl], preferred_element_type=F32)
            out += [mn, l, acc]
        return tuple(out)

    fin = lax.fori_loop(0, nk, body, tuple(init), unroll=2)
    outs = [fin[3 * h + 2] * (1.0 / fin[3 * h + 1]) for h in range(N_HEADS)]
    o_ref[...] = jnp.concatenate(outs, -1).astype(BF16)


def _mla_attention(pc, *, B, T, C):
    nj = T // ATT_TQ
    cb = B * T // C
    kern = functools.partial(_mla_kernel, T=T)
    return pl.pallas_call(
        kern, grid=(B, nj),
        in_specs=[pl.BlockSpec((ATT_TQ, 256), lambda b, j: (b * nj + j, 0)),
                  pl.BlockSpec((T, 256), lambda b, j: (b, 1)),
                  pl.BlockSpec((T, 256), lambda b, j: (b, 2)),
                  pl.BlockSpec((C, 256), lambda b, j: (cb + b, 1)),
                  pl.BlockSpec((C, 256), lambda b, j: (cb + b, 2))],
        out_specs=pl.BlockSpec((ATT_TQ, 256), lambda b, j: (b * nj + j, 0)),
        out_shape=jax.ShapeDtypeStruct((B * T, 256), BF16),
        compiler_params=_cparams(("arbitrary", "arbitrary")), name="mla_attention",
    )(pc, pc, pc, pc, pc)


def _ctx_kernel(sink_ref, pa_ref, pb_ref, pc_ref, oa_ref, ob_ref, oc_ref):
    def attend(p_ref, koff, voff, kv_heads, out_ref, sink):
        outs = []
        for h in range(N_HEADS):
            g = h * kv_heads // N_HEADS
            q = p_ref[:, HEAD_DIM * h:HEAD_DIM * (h + 1)]
            k = p_ref[:, koff + HEAD_DIM * g:koff + HEAD_DIM * (g + 1)]
            v = p_ref[:, voff + HEAD_DIM * g:voff + HEAD_DIM * (g + 1)]
            s = lax.dot_general(q, k, _NT, preferred_element_type=F32)
            outs.append(_softmax_av([(s, v)], extra_logit=sink_ref[h] if sink else None))
        out_ref[...] = jnp.concatenate(outs, -1).astype(BF16)

    attend(pa_ref, 256, 512, 4, oa_ref, False)
    attend(pb_ref, 256, 384, 2, ob_ref, True)
    attend(pc_ref, 256, 512, 4, oc_ref, False)


def _ctx_attention(sink, pa, pb, pc, *, B, T, C):
    cb = B * T // C
    row = lambda b: (cb + b, 0)
    return pl.pallas_call(
        _ctx_kernel, grid=(B,),
        in_specs=[pl.BlockSpec(memory_space=pltpu.SMEM),
                  pl.BlockSpec((C, 768), row), pl.BlockSpec((C, 512), row), pl.BlockSpec((C, 768), row)],
        out_specs=[pl.BlockSpec((C, 256), lambda b: (b, 0))] * 3,
        out_shape=[jax.ShapeDtypeStruct((B * C, 256), BF16)] * 3,
        compiler_params=_cparams(("arbitrary",)), name="ctx_attention",
    )(sink, pa, pb, pc)


def _gla_dir(pd_ref, o_ref, st_ref, reverse):
    nchunk = GLA_BLOCK // GLA_CHUNK
    L, DK, DV = GLA_CHUNK, N_HEADS * GLA_DK, N_HEADS * GLA_DV

    def iota(shape, dim):
        return lax.broadcasted_iota(I32, shape, dim)

    ri, ci = iota((L, L), 0), iota((L, L), 1)
    tri = jnp.where((ci >= ri) if reverse else (ci <= ri), 1.0, 0.0).astype(BF16)
    tri_t = jnp.where((ri >= ci) if reverse else (ri <= ci), 1.0, 0.0).astype(BF16)
    kcol, qrow = iota((L, N_HEADS * L), 1) % L, iota((L, N_HEADS * L), 0)
    keep = (kcol >= qrow) if reverse else (kcol <= qrow)
    k_owner = [iota((L, DK), 1) // GLA_DK == h for h in range(N_HEADS)]
    v_owner = [iota((L, DV), 1) // GLA_DV == h for h in range(N_HEADS)]
    s_owner = iota((DK, DV), 0) // GLA_DK == iota((DK, DV), 1) // GLA_DV
    gcol = 896 if reverse else 768
    end = 0 if reverse else L - 1
    order = range(nchunk - 1, -1, -1) if reverse else range(nchunk)

    def dot3(a_parts, b_parts):
        return sum(jnp.dot(a, b, preferred_element_type=F32) for a in a_parts for b in b_parts)

    per_chunk = []
    for c in order:
        rs = slice(c * L, (c + 1) * L)
        g = pd_ref[rs, gcol:gcol + DK]
        q = pd_ref[rs, 0:DK]
        k = pd_ref[rs, DK:2 * DK]
        vb = pd_ref[rs, 2 * DK:2 * DK + DV].astype(BF16)
        b = dot3([tri], _split3(g))
        bt = dot3(_split3(g.T), [tri_t])
        bend = bt[:, end:end + 1]
        qd = (q * jnp.exp(b)).astype(BF16)
        ki = (k * jnp.exp(-b)).astype(BF16)
        kdt = (k.T * jnp.exp(bend - bt)).astype(BF16)
        kbd = jnp.concatenate([jnp.where(m, ki, jnp.zeros_like(ki)) for m in k_owner], 0)
        vbd = jnp.concatenate([jnp.where(m, vb, jnp.zeros_like(vb)) for m in v_owner], 0)
        att = lax.dot_general(qd, kbd, _NT, preferred_element_type=F32)
        att = jnp.where(keep, att, 0.0).astype(BF16)
        intra = jnp.dot(att, vbd, preferred_element_type=F32)
        u = jnp.where(s_owner, jnp.dot(kdt, vb, preferred_element_type=F32), 0.0)
        per_chunk.append((rs, qd, intra, u, jnp.exp(bend)))
    st = st_ref[...]
    for rs, qd, intra, u, dec in per_chunk:
        o_ref[0, rs, :] = intra + jnp.dot(qd, st.astype(BF16), preferred_element_type=F32)
        st = st * dec + u
    st_ref[...] = st


def _gla_kernel(pd_ref, o_ref, st_ref):
    d = pl.program_id(1)
    s = pl.program_id(2)

    @pl.when(s == 0)
    def _():
        st_ref[...] = jnp.zeros_like(st_ref)

    @pl.when(d == 0)
    def _():
        _gla_dir(pd_ref, o_ref, st_ref, False)

    @pl.when(d == 1)
    def _():
        _gla_dir(pd_ref, o_ref, st_ref, True)


def _gla(pd, *, B, T, C):
    N = pd.shape[0]
    assert C == GLA_BLOCK
    nlb = T // GLA_BLOCK
    cb = B * T // GLA_BLOCK

    def rb(b, d, s):
        lat = b * nlb + jnp.where(d == 0, s - 1, nlb - s)
        return jnp.where(s == 0, cb + b, lat)

    return pl.pallas_call(
        _gla_kernel, grid=(B, 2, nlb + 1),
        in_specs=[pl.BlockSpec((GLA_BLOCK, 1024), lambda b, d, s: (rb(b, d, s), 0))],
        out_specs=pl.BlockSpec((1, GLA_BLOCK, 256), lambda b, d, s: (d, rb(b, d, s), 0)),
        out_shape=jax.ShapeDtypeStruct((2, N, 256), F32),
        scratch_shapes=[pltpu.VMEM((N_HEADS * GLA_DK, N_HEADS * GLA_DV), F32)],
        compiler_params=_cparams(("arbitrary", "arbitrary", "arbitrary")), name="gla_scan",
    )(pd)


def _pack_bf16_pairs(h):
    w = h.shape[-1] // 2
    bits = lax.bitcast_convert_type(h.astype(BF16).astype(F32), U32)
    return (bits[:, w:] & jnp.uint32(0xFFFF0000)) | (bits[:, :w] >> 16)


def _unpack_bf16_pairs(u):
    lo = lax.bitcast_convert_type(u << 16, F32)
    hi = lax.bitcast_convert_type(u & jnp.uint32(0xFFFF0000), F32)
    return jnp.concatenate([lo, hi], -1).astype(BF16)


def _outproj_kernel(x_ref, oa_ref, ob_ref, oc_ref, oac_ref, obc_ref, occ_ref, ofb_ref, r_ref, gn_ref, wout_ref,
                    g1_ref, sh2_ref, sc2_ref, lng_ref, lnb_ref, rw1_ref, rw2_ref, rb_ref,
                    x1_ref, hp_ref, sel_ref, gate_ref, rank_ref, count_ref, cnt_ref, *, alpha, nlat_t, tile0):
    is_lat = pl.program_id(0) + tile0 < nlat_t
    pick = lambda lat_ref, ctx_ref: jnp.where(is_lat, lat_ref[...], ctx_ref[...])
    s = ofb_ref[0] + ofb_ref[1]
    parts = []
    for h in range(N_HEADS):
        parts.append(_rms(s[:, GLA_DV * h:GLA_DV * (h + 1)]))
    r = r_ref[...]
    od = jnp.concatenate(parts, -1) * gn_ref[...] * (r * _sigmoid(r))
    ocat = jnp.concatenate([pick(oa_ref, oac_ref), pick(ob_ref, obc_ref), pick(oc_ref, occ_ref), od.astype(BF16)], -1)
    o = jnp.dot(ocat, wout_ref[...], preferred_element_type=F32)
    x1 = _layer_norm(alpha * x_ref[...] + g1_ref[0] * o) * lng_ref[...] + lnb_ref[...]
    x1_ref[...] = x1
    h2 = _layer_norm(x1) * (1.0 + sc2_ref[0]) + sh2_ref[0]
    hp_ref[...] = _pack_bf16_pairs(h2)
    a1, a2, _ = _split3(h2)
    w1, w2 = rw1_ref[...], rw2_ref[...]
    logits = (jnp.dot(a1, w1, preferred_element_type=F32) + jnp.dot(a1, w2, preferred_element_type=F32)
              + jnp.dot(a2, w1, preferred_element_type=F32))
    scores = _sigmoid(logits)
    work = scores + rb_ref[...]
    tm, ne = scores.shape
    lane = lax.broadcasted_iota(I32, (tm, ne), 1).astype(F32)
    lane8 = lax.broadcasted_iota(I32, (tm, TOP_K), 1)
    sel = jnp.zeros((tm, TOP_K), F32)
    gsel = jnp.zeros((tm, TOP_K), F32)
    chosen = jnp.zeros((tm, ne), F32)
    picks = []
    for k in range(TOP_K):
        m = jnp.max(work, -1, keepdims=True)
        idx = jnp.min(jnp.where(work == m, lane, float(ne)), -1, keepdims=True)
        hit = lane == idx
        picks.append(idx)
        sc = jnp.sum(jnp.where(hit, scores, 0.0), -1, keepdims=True)
        sel = jnp.where(lane8 == k, idx, sel)
        gsel = jnp.where(lane8 == k, sc, gsel)
        chosen = jnp.where(hit, 1.0, chosen)
        work = jnp.where(hit, -jnp.inf, work)
    sel_ref[...] = sel.astype(I32)
    gate_ref[...] = ROUTED_SCALE * gsel / jnp.sum(gsel, -1, keepdims=True)
    @pl.when(pl.program_id(0) == 0)
    def _():
        cnt_ref[...] = jnp.zeros_like(cnt_ref)

    ri = lax.broadcasted_iota(I32, (tm, tm), 0)
    ci = lax.broadcasted_iota(I32, (tm, tm), 1)
    before = jnp.where(ci < ri, 1.0, 0.0).astype(BF16)
    prefix = jnp.dot(before, chosen.astype(BF16), preferred_element_type=F32) + cnt_ref[...]
    rank = jnp.zeros((tm, TOP_K), F32)
    for k in range(TOP_K):
        rk = jnp.sum(jnp.where(lane == picks[k], prefix, 0.0), -1, keepdims=True)
        rank = jnp.where(lane8 == k, rk, rank)
    rank_ref[...] = rank.astype(I32)
    cnt_ref[...] = cnt_ref[...] + jnp.sum(chosen, 0, keepdims=True)
    count_ref[...] = cnt_ref[...].astype(I32)


def _outproj(xall, o_lat, o_ctx, ofb, pd, gn, wout, mods_l, lng, lnb, rw1, rw2, rb, *, n_lat, T, alpha, tile0, ntiles):
    D = xall.shape[1]
    tm = TOK_TILE
    N = ntiles * tm
    nlat_t, per_b, nb = n_lat // tm, T // tm, n_lat // T

    def midx(k):
        return lambda i: (jnp.where(i + tile0 < nlat_t, (i + tile0) // per_b, nb) * 6 + k, 0, 0)

    const = lambda i: (0, 0)
    row = lambda i: (i, 0)
    src_row = lambda i: (i + tile0, 0)
    lat_row = lambda i: (jnp.minimum(i + tile0, nlat_t - 1), 0)
    ctx_row = lambda i: (jnp.maximum(i + tile0 - nlat_t, 0), 0)
    E = rw1.shape[1]
    kern = functools.partial(_outproj_kernel, alpha=alpha, nlat_t=nlat_t, tile0=tile0)
    return pl.pallas_call(
        kern, grid=(ntiles,),
        in_specs=[pl.BlockSpec((tm, D), src_row)] + [pl.BlockSpec((tm, 256), lat_row)] * 3
                 + [pl.BlockSpec((tm, 256), ctx_row)] * 3 + [
                  pl.BlockSpec((2, tm, 256), lambda i: (0, i + tile0, 0)),
                  pl.BlockSpec((tm, 256), lambda i: (i + tile0, 2)), pl.BlockSpec((1, 256), const),
                  pl.BlockSpec((D, D), const),
                  pl.BlockSpec((1, 1, D), midx(2)), pl.BlockSpec((1, 1, D), midx(3)), pl.BlockSpec((1, 1, D), midx(4)),
                  pl.BlockSpec((1, D), const), pl.BlockSpec((1, D), const),
                  pl.BlockSpec((D, E), const), pl.BlockSpec((D, E), const), pl.BlockSpec((1, E), const)],
        out_specs=[pl.BlockSpec((tm, D), row), pl.BlockSpec((tm, D // 2), row),
                   pl.BlockSpec((tm, TOP_K), row), pl.BlockSpec((tm, TOP_K), row), pl.BlockSpec((tm, TOP_K), row),
                   pl.BlockSpec((1, E), const)],
        out_shape=[jax.ShapeDtypeStruct((N, D), F32), jax.ShapeDtypeStruct((N, D // 2), U32),
                   jax.ShapeDtypeStruct((N, TOP_K), I32), jax.ShapeDtypeStruct((N, TOP_K), F32),
                   jax.ShapeDtypeStruct((N, TOP_K), I32), jax.ShapeDtypeStruct((1, E), I32)],
        scratch_shapes=[pltpu.VMEM((1, E), F32)],
        compiler_params=_cparams(("arbitrary",)), name="outproj_router",
    )(xall, *o_lat, *o_ctx, ofb, pd, gn, wout, mods_l, mods_l, mods_l, lng, lnb, rw1, rw2, rb)


def _sc_mesh():
    return plsc.VectorSubcoreMesh(core_axis_name="c", subcore_axis_name="s")


def _sc_worker():
    return lax.axis_index("s") * SC_CORES + lax.axis_index("c")


def _sc_dispatch(hp, dest_k, rows):
    N, W = hp.shape
    tpw = N // SC_WORKERS
    ch = SC_DISPATCH_CHUNK
    assert N % SC_WORKERS == 0 and tpw % ch == 0
    nch = tpw // ch
    idx = dest_k.reshape(TOP_K, SC_WORKERS, nch, ch).transpose(1, 2, 0, 3)

    @functools.partial(
        pl.kernel, mesh=_sc_mesh(), out_type=jax.ShapeDtypeStruct((rows, W), hp.dtype),
        scratch_types=[pltpu.VMEM((nch, TOP_K, ch), I32), pltpu.VMEM((ch, W), hp.dtype), pltpu.SemaphoreType.DMA],
        compiler_params=pltpu.CompilerParams(use_tc_tiling_on_sc=True), name="sc_dispatch")
    def scatter(hp_hbm, idx_hbm, out_hbm, idx_v, rows_v, sem):
        wid = _sc_worker()
        base = wid * tpw
        pltpu.sync_copy(idx_hbm.at[wid], idx_v)

        @pl.loop(0, nch)
        def _(c):
            pltpu.sync_copy(hp_hbm.at[pl.ds(base + c * ch, ch)], rows_v)
            copies = [pltpu.async_copy(rows_v, out_hbm.at[idx_v.at[c, k]], sem) for k in range(TOP_K)]
            for cp in copies:
                cp.wait()

    return scatter(hp, idx)


def _sc_gather(src, idx):
    R, W = idx.shape[0], src.shape[1]
    per_w = R // SC_WORKERS
    ch = SC_GATHER_CHUNK
    assert R % SC_WORKERS == 0 and per_w % ch == 0
    nch = per_w // ch

    @functools.partial(
        pl.kernel, mesh=_sc_mesh(), out_type=jax.ShapeDtypeStruct((R, W), src.dtype),
        scratch_types=[pltpu.VMEM((nch, ch), I32), pltpu.VMEM((ch, W), src.dtype), pltpu.SemaphoreType.DMA],
        compiler_params=pltpu.CompilerParams(use_tc_tiling_on_sc=True), name="sc_gather")
    def gather(src_hbm, idx_hbm, out_hbm, idx_v, rows_v, sem):
        wid = _sc_worker()
        base = wid * per_w
        pltpu.sync_copy(idx_hbm.at[wid], idx_v)

        @pl.loop(0, nch)
        def _(c):
            pltpu.async_copy(src_hbm.at[idx_v.at[c]], rows_v, sem).wait()
            pltpu.sync_copy(rows_v, out_hbm.at[pl.ds(base + c * ch, ch)])

    return gather(src, idx.reshape(SC_WORKERS, nch, ch))


def _swiglu(xb, w1, w3, w2):
    a = jnp.dot(xb, w1, preferred_element_type=F32)
    b = jnp.dot(xb, w3, preferred_element_type=F32)
    return jnp.dot((a * _sigmoid(a) * b).astype(BF16), w2, preferred_element_type=F32)


def _expert_kernel(be_ref, nu_ref, xs_ref, w1_ref, w3_ref, w2_ref, y_ref, w1b_ref, w3b_ref, w2b_ref):
    i = pl.program_id(0)
    used = i < nu_ref[0]

    @pl.when(used & ((i == 0) | (be_ref[i] != be_ref[jnp.maximum(i - 1, 0)])))
    def _():
        w1b_ref[...] = w1_ref[0, 0].astype(BF16)
        w3b_ref[...] = w3_ref[0, 0].astype(BF16)
        w2b_ref[...] = w2_ref[0, 0].astype(BF16)

    @pl.when(used)
    def _():
        xb = _unpack_bf16_pairs(xs_ref[...])
        y_ref[...] = _pack_bf16_pairs(_swiglu(xb, w1b_ref[...], w3b_ref[...], w2b_ref[...]))

    @pl.when(jnp.logical_not(used))
    def _():
        y_ref[...] = jnp.zeros_like(y_ref)


def _experts(xs, blk_exp, n_used, w1, w3, w2, layer):
    rows, half = xs.shape
    _, E, D, F = w1.shape
    nblk = rows // MOE_BM
    wmap = lambda i, be, nu: (layer, be[i], 0, 0)
    grid_spec = pltpu.PrefetchScalarGridSpec(
        num_scalar_prefetch=2, grid=(nblk,),
        in_specs=[pl.BlockSpec((MOE_BM, half), lambda i, be, nu: (i, 0)),
                  pl.BlockSpec((1, 1, D, F), wmap), pl.BlockSpec((1, 1, D, F), wmap), pl.BlockSpec((1, 1, F, D), wmap)],
        out_specs=pl.BlockSpec((MOE_BM, half), lambda i, be, nu: (i, 0)),
        scratch_shapes=[pltpu.VMEM((D, F), BF16), pltpu.VMEM((D, F), BF16), pltpu.VMEM((F, D), BF16)])
    return pl.pallas_call(
        _expert_kernel, grid_spec=grid_spec,
        out_shape=jax.ShapeDtypeStruct((rows, half), U32),
        compiler_params=_cparams(("arbitrary",)), name="expert_ffn",
    )(blk_exp, n_used, xs, w1, w3, w2)


def _combine_kernel(yg_ref, gate_ref, hp_ref, sw1_ref, sw3_ref, sw2_ref, x_ref, g2_ref, lng_ref, lnb_ref,
                    o_ref, *, alpha):
    f = _swiglu(_unpack_bf16_pairs(hp_ref[...]), sw1_ref[...], sw3_ref[...], sw2_ref[...])
    gates = gate_ref[...]
    lo = hi = 0.0
    for k in range(TOP_K):
        u = yg_ref[k]
        g = gates[:, k:k + 1]
        lo = lo + g * lax.bitcast_convert_type(u << 16, F32)
        hi = hi + g * lax.bitcast_convert_type(u & jnp.uint32(0xFFFF0000), F32)
    f = f + jnp.concatenate([lo, hi], -1)
    o_ref[...] = _layer_norm(alpha * x_ref[...] + g2_ref[0] * f) * lng_ref[...] + lnb_ref[...]


def _combine(yg, gates, hp, sw1, sw3, sw2, x1, mods_l, lng, lnb, *, n_lat, T, alpha, n_out, tok0):
    D = x1.shape[1]
    N = n_out
    tm = CMB_TILE
    nlat_t, per_b, nb = n_lat // tm, T // tm, n_lat // T
    t0 = tok0 // tm
    F = sw1.shape[1]
    const = lambda i: (0, 0)
    row = lambda i: (i, 0)
    kern = functools.partial(_combine_kernel, alpha=alpha)
    return pl.pallas_call(
        kern, grid=(N // tm,),
        in_specs=[pl.BlockSpec((TOP_K, tm, D // 2), lambda i: (0, i, 0)), pl.BlockSpec((tm, TOP_K), row),
                  pl.BlockSpec((tm, D // 2), row),
                  pl.BlockSpec((D, F), const), pl.BlockSpec((D, F), const), pl.BlockSpec((F, D), const),
                  pl.BlockSpec((tm, D), row),
                  pl.BlockSpec((1, 1, D), lambda i: (jnp.where(i + t0 < nlat_t, (i + t0) // per_b, nb) * 6 + 5, 0, 0)),
                  pl.BlockSpec((1, D), const), pl.BlockSpec((1, D), const)],
        out_specs=pl.BlockSpec((tm, D), row),
        out_shape=jax.ShapeDtypeStruct((N, D), F32),
        compiler_params=_cparams(("arbitrary",)), name="moe_combine",
    )(yg, gates, hp, sw1, sw3, sw2, x1, mods_l, lng, lnb)


def _dest_kernel(start_ref, sel_ref, rank_ref, o_ref):
    sel = sel_ref[...]

    def body(e, acc):
        return jnp.where(sel == e, start_ref[e], acc)

    o_ref[...] = lax.fori_loop(0, start_ref.shape[0], body, jnp.zeros_like(sel)) + rank_ref[...]


def _dispatch_plan(sel, rank, counts):
    N = sel.shape[0]
    n_experts = counts.shape[0]
    padded = (counts + MOE_BM - 1) // MOE_BM * MOE_BM
    pad_end = jnp.cumsum(padded)
    pad_start = (pad_end - padded).astype(I32)
    dense = (N * TOP_K // 128, 128)
    dest = pl.pallas_call(
        _dest_kernel,
        in_specs=[pl.BlockSpec(memory_space=pltpu.SMEM), pl.BlockSpec(memory_space=pltpu.VMEM),
                  pl.BlockSpec(memory_space=pltpu.VMEM)],
        out_specs=pl.BlockSpec(memory_space=pltpu.VMEM),
        out_shape=jax.ShapeDtypeStruct(dense, I32), name="moe_dest_rows",
    )(pad_start, sel.reshape(dense), rank.reshape(dense))
    dest_k = dest.reshape(N, TOP_K).T
    n_blocks = -(-(N * TOP_K) // MOE_BM) + n_experts
    blk_first = jnp.arange(n_blocks, dtype=I32) * MOE_BM
    blk_exp = jnp.minimum(jnp.sum((pad_end[None, :] <= blk_first[:, None]).astype(I32), -1), n_experts - 1)
    n_used = (pad_end[-1] // MOE_BM).astype(I32).reshape(1)
    return dest_k, blk_exp, n_used, n_blocks * MOE_BM


def _rope_tables(T, tile):
    t = jnp.arange(T)
    row = (t // GRID_W).astype(F32)[:, None]
    col = (t % GRID_W).astype(F32)[:, None]

    def group(half):
        inv = ROPE_THETA ** (-jnp.arange(half, dtype=F32) / half)
        ar, ac = row * inv, col * inv
        cos = jnp.concatenate([jnp.cos(ar), jnp.cos(ar), jnp.cos(ac), jnp.cos(ac)], -1)
        sin = jnp.concatenate([-jnp.sin(ar), jnp.sin(ar), -jnp.sin(ac), jnp.sin(ac)], -1)
        return cos, sin

    ones32, zeros32 = jnp.ones((T, 32), F32), jnp.zeros((T, 32), F32)
    cw, sw = group(16)
    cw, sw = jnp.tile(cw, (1, 2)), jnp.tile(sw, (1, 2))
    c8, s8 = group(8)
    cq = jnp.tile(jnp.concatenate([ones32, c8], -1), (1, 2))
    sq = jnp.tile(jnp.concatenate([zeros32, s8], -1), (1, 2))
    ck = jnp.concatenate([c8, ones32, ones32, ones32], -1)
    sk = jnp.concatenate([s8, zeros32, zeros32, zeros32], -1)

    def pad(a, ident):
        return jnp.concatenate([a, jnp.full((tile, 128), ident, F32)], 0)

    return [pad(cw, 1.0), pad(sw, 0.0), pad(cq, 1.0), pad(sq, 0.0), pad(ck, 1.0), pad(sk, 0.0)]


def _layer_weights(w_in, mla_w_uq, mla_w_ukv, gla_w_gf, gla_w_gb, gla_b_gf, gla_b_gb):
    D = w_in.shape[0]
    z = lambda n: jnp.zeros((D, n), F32)
    win = jnp.concatenate([
        w_in[:, 0:256] * 0.125, w_in[:, 256:768],
        w_in[:, 768:1024] * 0.125, w_in[:, 1024:1280],
        w_in[:, 1280:1696], z(96),
        w_in[:, 1696:2464],
        w_in[:, 2464:2496], z(96)], -1).astype(BF16)
    wuq = (mla_w_uq * 0.125).astype(BF16)
    ukv = mla_w_ukv.reshape(-1, N_HEADS, MLA_NOPE + HEAD_DIM)
    kv_rank = ukv.shape[0]
    wk = jnp.concatenate([ukv[:, :, :MLA_NOPE], jnp.zeros((kv_rank, N_HEADS, MLA_ROPE), F32)], -1)
    place = jnp.concatenate([jnp.zeros((MLA_ROPE, MLA_NOPE), F32), jnp.eye(MLA_ROPE, dtype=F32)], -1)
    place = jnp.concatenate([jnp.tile(place, (1, N_HEADS)), jnp.zeros((128 - MLA_ROPE, 256), F32)], 0)
    wkk = jnp.concatenate([wk.reshape(kv_rank, 256), place], 0).astype(BF16)
    wv = ukv[:, :, MLA_NOPE:].reshape(kv_rank, 256).astype(BF16)
    r = GLA_GATE_RANK
    wg = jnp.zeros((128, 256), F32).at[0:r, 0:128].set(gla_w_gf).at[r:2 * r, 128:256].set(gla_w_gb).astype(BF16)
    bg = jnp.concatenate([gla_b_gf, gla_b_gb])[None, :]
    return win, wuq, wkk, wv, wg, bg


def kernel(x, c, ctx, c_ctx, w_ada, b_ada, w_in, na_rpb, wa_sink, mla_g_q, mla_g_kv, mla_w_uq, mla_w_ukv,
           gla_w_gf, gla_b_gf, gla_w_gb, gla_b_gb, gla_g_norm, w_out, ln1_g, ln1_b, ln2_g, ln2_b,
           router_w, router_bias, exp_w1, exp_w3, exp_w2, sh_w1, sh_w3, sh_w2):
    B, T, D = x.shape
    C = ctx.shape[1]
    L = w_ada.shape[0]
    E = router_w.shape[-1]
    n_lat = B * T
    alpha = (2 * L) ** 0.25
    dims = dict(B=B, T=T, C=C)

    cc = jnp.zeros((8, D), F32).at[:B].set(c).at[B].set(c_ctx)
    mods = _mods(cc, w_ada, b_ada)
    tabs = _rope_tables(T, TOK_TILE)
    xall = jnp.concatenate([x.reshape(n_lat, D), ctx.reshape(B * C, D)], 0)

    for l in range(L):
        mods_l = mods[l].reshape(8 * 6, 1, D)
        win, wuq, wkk, wv, wg, bg = _layer_weights(w_in[l], mla_w_uq[l], mla_w_ukv[l], gla_w_gf[l], gla_w_gb[l],
                                                   gla_b_gf[l], gla_b_gb[l])
        pa, pb, pc, pd = _inproj(xall, mods_l, win, tabs, mla_g_q[l][None], mla_g_kv[l][None], wuq, wkk, wv, wg, bg,
                                 n_lat=n_lat, T=T)
        oa = _na_attention(pa, _na_bias_tables(na_rpb[l], T // GRID_W), **dims)
        ob = _wa_attention(pb, wa_sink[l], **dims)
        oc = _mla_attention(pc, **dims)
        o_ctx = _ctx_attention(wa_sink[l], pa, pb, pc, **dims)
        ofb = _gla(pd, **dims)

        rw1 = router_w[l].astype(BF16)
        rw2 = (router_w[l] - rw1.astype(F32)).astype(BF16)
        n_all = xall.shape[0]
        part = n_all // MOE_SPLITS
        assert n_all % (MOE_SPLITS * TOK_TILE) == 0
        shared = (sh_w1[l].astype(BF16), sh_w3[l].astype(BF16), sh_w2[l].astype(BF16))
        gn, wout = jnp.tile(gla_g_norm[l], N_HEADS)[None], w_out[l].astype(BF16)
        routed = []
        for p in range(MOE_SPLITS):
            x1, hp, sel, gates, rank, counts = _outproj(
                xall, (oa, ob, oc), o_ctx, ofb, pd, gn, wout, mods_l, ln1_g[l][None], ln1_b[l][None], rw1, rw2,
                router_bias[l][None], n_lat=n_lat, T=T, alpha=alpha,
                tile0=p * part // TOK_TILE, ntiles=part // TOK_TILE)
            dest_k, blk_exp, n_used, rows = _dispatch_plan(sel, rank, counts.reshape(E))
            routed.append((x1, hp, gates, dest_k, blk_exp, n_used, _sc_dispatch(hp, dest_k, rows)))
        outs = []
        for p, (x1, hp, gates, dest_k, blk_exp, n_used, xs) in enumerate(routed):
            y = _experts(xs, blk_exp, n_used, exp_w1, exp_w3, exp_w2, l)
            yg = _sc_gather(y, dest_k.reshape(-1)).reshape(TOP_K, part, D // 2)
            n_out = part if l < L - 1 else max(min(n_lat - p * part, part), 0)
            if n_out:
                outs.append(_combine(yg, gates, hp, *shared, x1, mods_l, ln2_g[l][None], ln2_b[l][None], n_lat=n_lat,
                                     T=T, alpha=alpha, n_out=n_out, tok0=p * part))
        xall = jnp.concatenate(outs, 0)

    return xall.reshape(B, T, D)
```

```python
import functools

import numpy as np
import jax
import jax.numpy as jnp
from jax import lax
from jax.experimental import pallas as pl
from jax.experimental.pallas import tpu as pltpu
from jax.experimental.pallas import tpu_sc as plsc

F32 = jnp.float32
BF16 = jnp.bfloat16
U32 = jnp.uint32
I32 = jnp.int32

GRID_W = 64
HEAD_DIM = 64
N_HEADS = 4
NA_KH, NA_KW = 8, 16
WA_WINDOW = 128
WA_SUB = 256
MLA_NOPE, MLA_ROPE = 32, 32
GLA_DK, GLA_DV = 32, 64
GLA_GATE_RANK = 16
GLA_GATE_NORM = 16.0
GLA_CHUNK = 64
TOP_K = 8
ROUTED_SCALE = 2.5
ROPE_THETA = 10000.0
LN_EPS = 1e-5
RMS_EPS = 1e-6
NEG = -1e30

VMEM_LIMIT = 56 * 1024 * 1024
TOK_TILE = 512
ATT_TQ = 512
MLA_TQ = 1024
MLA_TK = 1024
GLA_BLOCK = 256
MOE_BM = 1024
MOE_SPLITS = 1
CMB_TILE = 256
SC_CORES = 2
SC_WORKERS = 32
SC_DISPATCH_CHUNK = 48
SC_GATHER_CHUNK = 64

PW = 2688
_NT = (((1,), (1,)), ((), ()))


def _cparams(sem, vmem=VMEM_LIMIT):
    return pltpu.CompilerParams(dimension_semantics=sem, vmem_limit_bytes=vmem)


def _sigmoid(x):
    return 1.0 / (1.0 + jnp.exp(-x))


def _layer_norm(x):
    mu = jnp.mean(x, -1, keepdims=True)
    xc = x - mu
    var = jnp.mean(xc * xc, -1, keepdims=True)
    return xc * lax.rsqrt(var + LN_EPS)


def _rms(x):
    return x * lax.rsqrt(jnp.mean(x * x, -1, keepdims=True) + RMS_EPS)


def _bdot(a, b):
    return jnp.dot(a.astype(BF16), b.astype(BF16), preferred_element_type=F32)


def _bdot_nt(a, b):
    return lax.dot_general(a.astype(BF16), b.astype(BF16), _NT, preferred_element_type=F32)


def _split3(a):
    a1 = a.astype(BF16)
    r = a - a1.astype(F32)
    a2 = r.astype(BF16)
    a3 = (r - a2.astype(F32)).astype(BF16)
    return a1, a2, a3


def _mods_kernel(cc_ref, w_ref, b_ref, o_ref):
    cc = cc_ref[...]
    s = cc * _sigmoid(cc)
    o_ref[0] = _bdot(s, w_ref[0]) + b_ref[0]


def _mods(cc, w_ada, b_ada):
    L, D, W = w_ada.shape
    tn = 1536
    return pl.pallas_call(
        _mods_kernel,
        grid=(L, W // tn),
        in_specs=[pl.BlockSpec((8, D), lambda l, j: (0, 0)),
                  pl.BlockSpec((1, D, tn), lambda l, j: (l, 0, j)),
                  pl.BlockSpec((1, 1, tn), lambda l, j: (l, 0, j))],
        out_specs=pl.BlockSpec((1, 8, tn), lambda l, j: (l, 0, j)),
        out_shape=jax.ShapeDtypeStruct((L, 8, W), F32),
        compiler_params=_cparams(("arbitrary", "arbitrary")),
        name="adaln_mods",
    )(cc, w_ada, b_ada.reshape(L, 1, W))


def _rope(z, cos, sin, half):
    w = z.shape[-1]
    lane = lax.broadcasted_iota(I32, z.shape, 1)
    first = (lane % (2 * half)) < half
    partner = jnp.where(first, pltpu.roll(z, w - half, 1), pltpu.roll(z, half, 1))
    return z * cos + partner * sin


def _log_sigmoid(x):
    return jnp.minimum(x, 0.0) - jnp.log(1.0 + jnp.exp(-jnp.abs(x)))


def _inproj_kernel(x_ref, shift_ref, scale_ref, win_ref, cw_ref, sw_ref, cq_ref, sq_ref, ck_ref, sk_ref,
                   gq_ref, gkv_ref, wuq_ref, wkk_ref, wv_ref, wg_ref, bg_ref,
                   pa_ref, pb_ref, pc_ref, pd_ref):
    h = _layer_norm(x_ref[...]) * (1.0 + scale_ref[0]) + shift_ref[0]
    p = jnp.dot(h.astype(BF16), win_ref[...], preferred_element_type=F32)
    pa_ref[...] = p[:, 0:768].astype(BF16)
    cw, sw = cw_ref[...], sw_ref[...]
    pb_ref[:, 0:128] = _rope(p[:, 768:896], cw, sw, 16).astype(BF16)
    pb_ref[:, 128:256] = _rope(p[:, 896:1024], cw, sw, 16).astype(BF16)
    pb_ref[:, 256:384] = _rope(p[:, 1024:1152], cw, sw, 16).astype(BF16)
    pb_ref[:, 384:512] = p[:, 1152:1280].astype(BF16)
    cqn = _rms(p[:, 1280:1536]) * gq_ref[...]
    q = jnp.dot(cqn.astype(BF16), wuq_ref[...], preferred_element_type=F32)
    cq, sq = cq_ref[...], sq_ref[...]
    pc_ref[:, 0:128] = _rope(q[:, 0:128], cq, sq, 8).astype(BF16)
    pc_ref[:, 128:256] = _rope(q[:, 128:256], cq, sq, 8).astype(BF16)
    ckvn = (_rms(p[:, 1536:1664]) * gkv_ref[...]).astype(BF16)
    kr = _rope(p[:, 1664:1792], ck_ref[...], sk_ref[...], 8).astype(BF16)
    kin = jnp.concatenate([ckvn, kr], axis=-1)
    pc_ref[:, 256:512] = jnp.dot(kin, wkk_ref[...], preferred_element_type=F32).astype(BF16)
    pc_ref[:, 512:768] = jnp.dot(ckvn, wv_ref[...], preferred_element_type=F32).astype(BF16)
    pd_ref[:, 0:128] = p[:, 1792:1920] * (GLA_DK ** -0.5)
    pd_ref[:, 128:768] = p[:, 1920:2560]
    pre = jnp.dot(p[:, 2560:2688].astype(BF16), wg_ref[...], preferred_element_type=F32) + bg_ref[...]
    pd_ref[:, 768:1024] = _log_sigmoid(pre) * (1.0 / GLA_GATE_NORM)


def _inproj(xall, mods_l, win, tabs, gq, gkv, wuq, wkk, wv, wg, bg, *, n_lat, T):
    N, D = xall.shape
    tm = TOK_TILE
    nlat_t = n_lat // tm
    per_b = T // tm
    nb = n_lat // T

    def midx(k):
        return lambda i: (jnp.where(i < nlat_t, i // per_b, nb) * 6 + k, 0, 0)

    def tidx(i):
        return (jnp.where(i < nlat_t, i % per_b, per_b), 0)

    const = lambda i: (0, 0)
    tab_spec = pl.BlockSpec((tm, 128), tidx)
    in_specs = [pl.BlockSpec((tm, D), lambda i: (i, 0)),
                pl.BlockSpec((1, 1, D), midx(0)), pl.BlockSpec((1, 1, D), midx(1)),
                pl.BlockSpec((D, PW), const)] + [tab_spec] * 6 + [
                pl.BlockSpec((1, 256), const), pl.BlockSpec((1, 128), const),
                pl.BlockSpec((256, 256), const), pl.BlockSpec((256, 256), const),
                pl.BlockSpec((128, 256), const), pl.BlockSpec((128, 256), const),
                pl.BlockSpec((1, 256), const)]
    out_specs = [pl.BlockSpec((tm, 768), lambda i: (i, 0)), pl.BlockSpec((tm, 512), lambda i: (i, 0)),
                 pl.BlockSpec((tm, 768), lambda i: (i, 0)), pl.BlockSpec((tm, 1024), lambda i: (i, 0))]
    out_shape = [jax.ShapeDtypeStruct((N, 768), BF16), jax.ShapeDtypeStruct((N, 512), BF16),
                 jax.ShapeDtypeStruct((N, 768), BF16), jax.ShapeDtypeStruct((N, 1024), F32)]
    return pl.pallas_call(
        _inproj_kernel, grid=(N // tm,), in_specs=in_specs, out_specs=out_specs, out_shape=out_shape,
        compiler_params=_cparams(("arbitrary",)), name="inproj",
    )(xall, mods_l, mods_l, win, *tabs, gq, gkv, wuq, wkk, wv, wg, bg)


def _softmax_av(parts, extra_logit=None):
    m = functools.reduce(jnp.maximum, [jnp.max(s, -1, keepdims=True) for s, _ in parts])
    if extra_logit is not None:
        m = jnp.maximum(m, extra_logit)
    l = 0.0 if extra_logit is None else jnp.exp(extra_logit - m)
    o = None
    for s, v in parts:
        e = jnp.exp(s - m)
        l = l + jnp.sum(e, -1, keepdims=True)
        c = jnp.dot(e.astype(BF16), v, preferred_element_type=F32)
        o = c if o is None else o + c
    return o * (1.0 / l)


def _na_kernel(q_ref, k_ref, v_ref, kc_ref, vc_ref, bias_ref, o_ref, *, rows):
    j = pl.program_id(1)
    ws = pl.multiple_of(jnp.clip(8 * j - 4, 0, rows - 16) * GRID_W, 256)
    win = 16 * GRID_W
    outs = []
    for h in range(N_HEADS):
        sl = slice(HEAD_DIM * h, HEAD_DIM * (h + 1))
        q = q_ref[:, sl]
        s = lax.dot_general(q, k_ref[pl.ds(ws, win), sl], _NT, preferred_element_type=F32) + bias_ref[0, h]
        sc = lax.dot_general(q, kc_ref[:, sl], _NT, preferred_element_type=F32)
        outs.append(_softmax_av([(s, v_ref[pl.ds(ws, win), sl]), (sc, vc_ref[:, sl])]))
    o_ref[...] = jnp.concatenate(outs, -1).astype(BF16)


def _na_bias_tables(rpb, rows):
    nj = rows // 8
    ro_all, ok_all = [], []
    for j in (0, 1, nj - 1):
        ws = int(np.clip(8 * j - 4, 0, rows - 16))
        r = 8 * j + np.arange(8)[:, None, None, None]
        qc = np.arange(GRID_W)[None, :, None, None]
        kr = ws + np.arange(16)[None, None, :, None]
        kc = np.arange(GRID_W)[None, None, None, :]
        rs = np.clip(r - NA_KH // 2, 0, rows - NA_KH)
        cs = np.clip(qc - NA_KW // 2, 0, GRID_W - NA_KW)
        ok = (kr >= rs) & (kr < rs + NA_KH) & (kc >= cs) & (kc < cs + NA_KW)
        ro = np.clip(kr - r + (NA_KH - 1), 0, 2 * NA_KH - 2)
        shp = (8, GRID_W, 16, GRID_W)
        ro_all.append(np.broadcast_to(ro, shp).reshape(512, 1024))
        ok_all.append(np.broadcast_to(ok, shp).reshape(512, 1024))
    ok = np.stack(ok_all)
    hi = lax.Precision.HIGHEST
    co = np.clip(np.arange(GRID_W)[None, :] - np.arange(GRID_W)[:, None] + (NA_KW - 1), 0, 2 * NA_KW - 2)
    col_hot = (co[..., None] == np.arange(2 * NA_KW - 1)).astype(np.float32)
    cb = jnp.einsum('hdo,qko->hdqk', rpb, col_hot, precision=hi)
    ro = np.stack(ro_all).reshape(3, 8, GRID_W, 16, GRID_W)[:, :, 0, :, 0]
    row_hot = (ro[..., None] == np.arange(2 * NA_KH - 1)).astype(np.float32)
    b = jnp.einsum('vrsd,hdqk->vhrqsk', row_hot, cb, precision=hi)
    b = b.reshape(3, rpb.shape[0], 512, 1024)
    return jnp.where(ok[:, None], b, NEG)


def _na_attention(pa, bias, *, B, T, C):
    rows = T // GRID_W
    nj = T // ATT_TQ
    cb = B * T // C
    kern = functools.partial(_na_kernel, rows=rows)
    return pl.pallas_call(
        kern, grid=(B, nj),
        in_specs=[pl.BlockSpec((ATT_TQ, 256), lambda b, j: (b * nj + j, 0)),
                  pl.BlockSpec((T, 256), lambda b, j: (b, 1)),
                  pl.BlockSpec((T, 256), lambda b, j: (b, 2)),
                  pl.BlockSpec((C, 256), lambda b, j: (cb + b, 1)),
                  pl.BlockSpec((C, 256), lambda b, j: (cb + b, 2)),
                  pl.BlockSpec((1, N_HEADS, 512, 1024),
                               lambda b, j: (jnp.where(j == 0, 0, jnp.where(j == nj - 1, 2, 1)), 0, 0, 0))],
        out_specs=pl.BlockSpec((ATT_TQ, 256), lambda b, j: (b * nj + j, 0)),
        out_shape=jax.ShapeDtypeStruct((B * T, 256), BF16),
        compiler_params=_cparams(("arbitrary", "arbitrary")), name="na_attention",
    )(pa, pa, pa, pa, pa, bias)


def _wa_kernel(sink_ref, q_ref, k_ref, v_ref, kc_ref, vc_ref, o_ref, *, T):
    i = pl.program_id(1)
    win = WA_SUB + 2 * WA_WINDOW
    for sub in range(ATT_TQ // WA_SUB):
        rows = slice(sub * WA_SUB, (sub + 1) * WA_SUB)
        start = i * ATT_TQ + sub * WA_SUB
        ws = pl.multiple_of(jnp.clip(start - WA_WINDOW, 0, T - win), 128)
        rel = ((lax.broadcasted_iota(I32, (WA_SUB, win), 1) + (ws - start))
               - lax.broadcasted_iota(I32, (WA_SUB, win), 0))
        valid = jnp.abs(rel) <= WA_WINDOW
        outs = []
        for h in range(N_HEADS):
            g = h // 2
            ksl = slice(HEAD_DIM * g, HEAD_DIM * (g + 1))
            q = q_ref[rows, HEAD_DIM * h:HEAD_DIM * (h + 1)]
            s = lax.dot_general(q, k_ref[pl.ds(ws, win), ksl], _NT, preferred_element_type=F32)
            s = jnp.where(valid, s, NEG)
            sc = lax.dot_general(q, kc_ref[:, ksl], _NT, preferred_element_type=F32)
            outs.append(_softmax_av([(s, v_ref[pl.ds(ws, win), ksl]), (sc, vc_ref[:, ksl])],
                                    extra_logit=sink_ref[h]))
        o_ref[rows, :] = jnp.concatenate(outs, -1).astype(BF16)


def _wa_attention(pb, sink, *, B, T, C):
    nj = T // ATT_TQ
    cb = B * T // C
    kern = functools.partial(_wa_kernel, T=T)
    return pl.pallas_call(
        kern, grid=(B, nj),
        in_specs=[pl.BlockSpec(memory_space=pltpu.SMEM),
                  pl.BlockSpec((ATT_TQ, 256), lambda b, j: (b * nj + j, 0)),
                  pl.BlockSpec((T, 128), lambda b, j: (b, 2)),
                  pl.BlockSpec((T, 128), lambda b, j: (b, 3)),
                  pl.BlockSpec((C, 128), lambda b, j: (cb + b, 2)),
                  pl.BlockSpec((C, 128), lambda b, j: (cb + b, 3))],
        out_specs=pl.BlockSpec((ATT_TQ, 256), lambda b, j: (b * nj + j, 0)),
        out_shape=jax.ShapeDtypeStruct((B * T, 256), BF16),
        compiler_params=_cparams(("arbitrary", "arbitrary")), name="wa_attention",
    )(sink, pb, pb, pb, pb, pb)


def _mla_kernel(q_ref, k_ref, v_ref, kc_ref, vc_ref, o_ref, *, T):
    nk = T // MLA_TK
    heads = [slice(HEAD_DIM * h, HEAD_DIM * (h + 1)) for h in range(N_HEADS)]
    qs = [q_ref[:, sl] for sl in heads]
    init = []
    for q, sl in zip(qs, heads):
        sc = lax.dot_general(q, kc_ref[:, sl], _NT, preferred_element_type=F32)
        m0 = jnp.max(sc, -1, keepdims=True)
        e0 = jnp.exp(sc - m0)
        init += [m0, jnp.sum(e0, -1, keepdims=True),
                 jnp.dot(e0.astype(BF16), vc_ref[:, sl], preferred_element_type=F32)]

    def body(c, carry):
        ks = pl.multiple_of(c * MLA_TK, MLA_TK)
        out = []
        for h, (q, sl) in enumerate(zip(qs, heads)):
            m, l, acc = carry[3 * h:3 * h + 3]
            s = lax.dot_general(q, k_ref[pl.ds(ks, MLA_TK), sl], _NT, preferred_element_type=F32)
            mn = jnp.maximum(m, jnp.max(s, -1, keepdims=True))
            a = jnp.exp(m - mn)
            e = jnp.exp(s - mn)
            l = l * a + jnp.sum(e, -1, keepdims=True)
            acc = acc * a + jnp.dot(e.astype(BF16), v_ref[pl.ds(ks, MLA_TK), sl], preferred_element_type=F32)
            out += [mn, l, acc]
        return tuple(out)

    fin = lax.fori_loop(0, nk, body, tuple(init), unroll=2)
    outs = [fin[3 * h + 2] * (1.0 / fin[3 * h + 1]) for h in range(N_HEADS)]
    o_ref[...] = jnp.concatenate(outs, -1).astype(BF16)


def _mla_attention(pc, *, B, T, C):
    nj = T // MLA_TQ
    cb = B * T // C
    kern = functools.partial(_mla_kernel, T=T)
    return pl.pallas_call(
        kern, grid=(B, nj),
        in_specs=[pl.BlockSpec((MLA_TQ, 256), lambda b, j: (b * nj + j, 0)),
                  pl.BlockSpec((T, 256), lambda b, j: (b, 1)),
                  pl.BlockSpec((T, 256), lambda b, j: (b, 2)),
                  pl.BlockSpec((C, 256), lambda b, j: (cb + b, 1)),
                  pl.BlockSpec((C, 256), lambda b, j: (cb + b, 2))],
        out_specs=pl.BlockSpec((MLA_TQ, 256), lambda b, j: (b * nj + j, 0)),
        out_shape=jax.ShapeDtypeStruct((B * T, 256), BF16),
        compiler_params=_cparams(("arbitrary", "arbitrary")), name="mla_attention",
    )(pc, pc, pc, pc, pc)


def _ctx_kernel(sink_ref, pa_ref, pb_ref, pc_ref, oa_ref, ob_ref, oc_ref):
    def attend(p_ref, koff, voff, kv_heads, out_ref, sink):
        outs = []
        for h in range(N_HEADS):
            g = h * kv_heads // N_HEADS
            q = p_ref[:, HEAD_DIM * h:HEAD_DIM * (h + 1)]
            k = p_ref[:, koff + HEAD_DIM * g:koff + HEAD_DIM * (g + 1)]
            v = p_ref[:, voff + HEAD_DIM * g:voff + HEAD_DIM * (g + 1)]
            s = lax.dot_general(q, k, _NT, preferred_element_type=F32)
            outs.append(_softmax_av([(s, v)], extra_logit=sink_ref[h] if sink else None))
        out_ref[...] = jnp.concatenate(outs, -1).astype(BF16)

    attend(pa_ref, 256, 512, 4, oa_ref, False)
    attend(pb_ref, 256, 384, 2, ob_ref, True)
    attend(pc_ref, 256, 512, 4, oc_ref, False)


def _ctx_attention(sink, pa, pb, pc, *, B, T, C):
    cb = B * T // C
    row = lambda b: (cb + b, 0)
    return pl.pallas_call(
        _ctx_kernel, grid=(B,),
        in_specs=[pl.BlockSpec(memory_space=pltpu.SMEM),
                  pl.BlockSpec((C, 768), row), pl.BlockSpec((C, 512), row), pl.BlockSpec((C, 768), row)],
        out_specs=[pl.BlockSpec((C, 256), lambda b: (b, 0))] * 3,
        out_shape=[jax.ShapeDtypeStruct((B * C, 256), BF16)] * 3,
        compiler_params=_cparams(("arbitrary",)), name="ctx_attention",
    )(sink, pa, pb, pc)


def _gla_dir(pd_ref, o_ref, st_ref, reverse):
    nchunk = GLA_BLOCK // GLA_CHUNK
    L, DK, DV = GLA_CHUNK, N_HEADS * GLA_DK, N_HEADS * GLA_DV

    def iota(shape, dim):
        return lax.broadcasted_iota(I32, shape, dim)

    ri, ci = iota((L, L), 0), iota((L, L), 1)
    tri = jnp.where((ci >= ri) if reverse else (ci <= ri), 1.0, 0.0).astype(BF16)
    tri_t = jnp.where((ri >= ci) if reverse else (ri <= ci), 1.0, 0.0).astype(BF16)
    kcol, qrow = iota((L, N_HEADS * L), 1) % L, iota((L, N_HEADS * L), 0)
    keep = (kcol >= qrow) if reverse else (kcol <= qrow)
    k_owner = [iota((L, DK), 1) // GLA_DK == h for h in range(N_HEADS)]
    v_owner = [iota((L, DV), 1) // GLA_DV == h for h in range(N_HEADS)]
    s_owner = iota((DK, DV), 0) // GLA_DK == iota((DK, DV), 1) // GLA_DV
    gcol = 896 if reverse else 768
    end = 0 if reverse else L - 1
    order = range(nchunk - 1, -1, -1) if reverse else range(nchunk)

    def dot3(a_parts, b_parts):
        return sum(jnp.dot(a, b, preferred_element_type=F32) for a in a_parts for b in b_parts)

    per_chunk = []
    for c in order:
        rs = slice(c * L, (c + 1) * L)
        g = pd_ref[rs, gcol:gcol + DK]
        q = pd_ref[rs, 0:DK]
        k = pd_ref[rs, DK:2 * DK]
        vb = pd_ref[rs, 2 * DK:2 * DK + DV].astype(BF16)
        b = dot3([tri], _split3(g))
        bt = dot3(_split3(g.T), [tri_t])
        bend = bt[:, end:end + 1]
        qd = (q * jnp.exp(b)).astype(BF16)
        ki = (k * jnp.exp(-b)).astype(BF16)
        kdt = (k.T * jnp.exp(bend - bt)).astype(BF16)
        kbd = jnp.concatenate([jnp.where(m, ki, jnp.zeros_like(ki)) for m in k_owner], 0)
        vbd = jnp.concatenate([jnp.where(m, vb, jnp.zeros_like(vb)) for m in v_owner], 0)
        att = lax.dot_general(qd, kbd, _NT, preferred_element_type=F32)
        att = jnp.where(keep, att, 0.0).astype(BF16)
        intra = jnp.dot(att, vbd, preferred_element_type=F32)
        u = jnp.where(s_owner, jnp.dot(kdt, vb, preferred_element_type=F32), 0.0)
        per_chunk.append((rs, qd, intra, u, jnp.exp(bend)))
    st = st_ref[...]
    for rs, qd, intra, u, dec in per_chunk:
        o_ref[0, rs, :] = intra + jnp.dot(qd, st.astype(BF16), preferred_element_type=F32)
        st = st * dec + u
    st_ref[...] = st


def _gla_kernel(pd_ref, o_ref, st_ref):
    d = pl.program_id(1)
    s = pl.program_id(2)

    @pl.when(s == 0)
    def _():
        st_ref[...] = jnp.zeros_like(st_ref)

    @pl.when(d == 0)
    def _():
        _gla_dir(pd_ref, o_ref, st_ref, False)

    @pl.when(d == 1)
    def _():
        _gla_dir(pd_ref, o_ref, st_ref, True)


def _gla(pd, *, B, T, C):
    N = pd.shape[0]
    assert C == GLA_BLOCK
    nlb = T // GLA_BLOCK
    cb = B * T // GLA_BLOCK

    def rb(b, d, s):
        lat = b * nlb + jnp.where(d == 0, s - 1, nlb - s)
        return jnp.where(s == 0, cb + b, lat)

    return pl.pallas_call(
        _gla_kernel, grid=(B, 2, nlb + 1),
        in_specs=[pl.BlockSpec((GLA_BLOCK, 1024), lambda b, d, s: (rb(b, d, s), 0))],
        out_specs=pl.BlockSpec((1, GLA_BLOCK, 256), lambda b, d, s: (d, rb(b, d, s), 0)),
        out_shape=jax.ShapeDtypeStruct((2, N, 256), F32),
        scratch_shapes=[pltpu.VMEM((N_HEADS * GLA_DK, N_HEADS * GLA_DV), F32)],
        compiler_params=_cparams(("arbitrary", "arbitrary", "arbitrary")), name="gla_scan",
    )(pd)


def _pack_bf16_pairs(h):
    w = h.shape[-1] // 2
    bits = lax.bitcast_convert_type(h.astype(BF16).astype(F32), U32)
    return (bits[:, w:] & jnp.uint32(0xFFFF0000)) | (bits[:, :w] >> 16)


def _unpack_bf16_pairs(u):
    lo = lax.bitcast_convert_type(u << 16, F32)
    hi = lax.bitcast_convert_type(u & jnp.uint32(0xFFFF0000), F32)
    return jnp.concatenate([lo, hi], -1).astype(BF16)


def _outproj_kernel(x_ref, oa_ref, ob_ref, oc_ref, oac_ref, obc_ref, occ_ref, ofb_ref, r_ref, gn_ref, wout_ref,
                    g1_ref, sh2_ref, sc2_ref, lng_ref, lnb_ref, rw1_ref, rw2_ref, rb_ref,
                    x1_ref, hp_ref, sel_ref, gate_ref, rank_ref, count_ref, cnt_ref, *, alpha, nlat_t, tile0):
    is_lat = pl.program_id(0) + tile0 < nlat_t
    pick = lambda lat_ref, ctx_ref: jnp.where(is_lat, lat_ref[...], ctx_ref[...])
    s = ofb_ref[0] + ofb_ref[1]
    w = s.shape[-1]
    same_head = (lax.broadcasted_iota(I32, (w, w), 0) // GLA_DV) == (lax.broadcasted_iota(I32, (w, w), 1) // GLA_DV)
    avg = jnp.where(same_head, 1.0 / GLA_DV, 0.0).astype(BF16)
    ms = sum(jnp.dot(part, avg, preferred_element_type=F32) for part in _split3(s * s))
    r = r_ref[...]
    od = s * lax.rsqrt(ms + RMS_EPS) * gn_ref[...] * (r * _sigmoid(r))
    ocat = jnp.concatenate([pick(oa_ref, oac_ref), pick(ob_ref, obc_ref), pick(oc_ref, occ_ref), od.astype(BF16)], -1)
    o = jnp.dot(ocat, wout_ref[...], preferred_element_type=F32)
    x1 = _layer_norm(alpha * x_ref[...] + g1_ref[0] * o) * lng_ref[...] + lnb_ref[...]
    x1_ref[...] = x1
    h2 = _layer_norm(x1) * (1.0 + sc2_ref[0]) + sh2_ref[0]
    hp_ref[...] = _pack_bf16_pairs(h2)
    a1, a2, _ = _split3(h2)
    w1, w2 = rw1_ref[...], rw2_ref[...]
    logits = (jnp.dot(a1, w1, preferred_element_type=F32) + jnp.dot(a1, w2, preferred_element_type=F32)
              + jnp.dot(a2, w1, preferred_element_type=F32))
    scores = _sigmoid(logits)
    work = scores + rb_ref[...]
    tm, ne = scores.shape
    lane = lax.broadcasted_iota(I32, (tm, ne), 1).astype(F32)
    lane8 = lax.broadcasted_iota(I32, (tm, TOP_K), 1)
    sel = jnp.zeros((tm, TOP_K), F32)
    gsel = jnp.zeros((tm, TOP_K), F32)
    chosen = jnp.zeros((tm, ne), F32)
    picks = []
    for k in range(TOP_K):
        m = jnp.max(work, -1, keepdims=True)
        idx = jnp.min(jnp.where(work == m, lane, float(ne)), -1, keepdims=True)
        hit = lane == idx
        picks.append(idx)
        sc = jnp.sum(jnp.where(hit, scores, 0.0), -1, keepdims=True)
        sel = jnp.where(lane8 == k, idx, sel)
        gsel = jnp.where(lane8 == k, sc, gsel)
        chosen = jnp.where(hit, 1.0, chosen)
        work = jnp.where(hit, -jnp.inf, work)
    sel_ref[...] = sel.astype(I32)
    gate_ref[...] = ROUTED_SCALE * gsel / jnp.sum(gsel, -1, keepdims=True)
    @pl.when(pl.program_id(0) == 0)
    def _():
        cnt_ref[...] = jnp.zeros_like(cnt_ref)

    ri = lax.broadcasted_iota(I32, (tm, tm), 0)
    ci = lax.broadcasted_iota(I32, (tm, tm), 1)
    before = jnp.where(ci < ri, 1.0, 0.0).astype(BF16)
    prefix = jnp.dot(before, chosen.astype(BF16), preferred_element_type=F32) + cnt_ref[...]
    rank = jnp.zeros((tm, TOP_K), F32)
    for k in range(TOP_K):
        rk = jnp.sum(jnp.where(lane == picks[k], prefix, 0.0), -1, keepdims=True)
        rank = jnp.where(lane8 == k, rk, rank)
    rank_ref[...] = rank.astype(I32)
    cnt_ref[...] = cnt_ref[...] + jnp.sum(chosen, 0, keepdims=True)
    count_ref[...] = cnt_ref[...].astype(I32)


def _outproj(xall, o_lat, o_ctx, ofb, pd, gn, wout, mods_l, lng, lnb, rw1, rw2, rb, *, n_lat, T, alpha, tile0, ntiles):
    D = xall.shape[1]
    tm = TOK_TILE
    N = ntiles * tm
    nlat_t, per_b, nb = n_lat // tm, T // tm, n_lat // T

    def midx(k):
        return lambda i: (jnp.where(i + tile0 < nlat_t, (i + tile0) // per_b, nb) * 6 + k, 0, 0)

    const = lambda i: (0, 0)
    row = lambda i: (i, 0)
    src_row = lambda i: (i + tile0, 0)
    lat_row = lambda i: (jnp.minimum(i + tile0, nlat_t - 1), 0)
    ctx_row = lambda i: (jnp.maximum(i + tile0 - nlat_t, 0), 0)
    E = rw1.shape[1]
    kern = functools.partial(_outproj_kernel, alpha=alpha, nlat_t=nlat_t, tile0=tile0)
    return pl.pallas_call(
        kern, grid=(ntiles,),
        in_specs=[pl.BlockSpec((tm, D), src_row)] + [pl.BlockSpec((tm, 256), lat_row)] * 3
                 + [pl.BlockSpec((tm, 256), ctx_row)] * 3 + [
                  pl.BlockSpec((2, tm, 256), lambda i: (0, i + tile0, 0)),
                  pl.BlockSpec((tm, 256), lambda i: (i + tile0, 2)), pl.BlockSpec((1, 256), const),
                  pl.BlockSpec((D, D), const),
                  pl.BlockSpec((1, 1, D), midx(2)), pl.BlockSpec((1, 1, D), midx(3)), pl.BlockSpec((1, 1, D), midx(4)),
                  pl.BlockSpec((1, D), const), pl.BlockSpec((1, D), const),
                  pl.BlockSpec((D, E), const), pl.BlockSpec((D, E), const), pl.BlockSpec((1, E), const)],
        out_specs=[pl.BlockSpec((tm, D), row), pl.BlockSpec((tm, D // 2), row),
                   pl.BlockSpec((tm, TOP_K), row), pl.BlockSpec((tm, TOP_K), row), pl.BlockSpec((tm, TOP_K), row),
                   pl.BlockSpec((1, E), const)],
        out_shape=[jax.ShapeDtypeStruct((N, D), F32), jax.ShapeDtypeStruct((N, D // 2), U32),
                   jax.ShapeDtypeStruct((N, TOP_K), I32), jax.ShapeDtypeStruct((N, TOP_K), F32),
                   jax.ShapeDtypeStruct((N, TOP_K), I32), jax.ShapeDtypeStruct((1, E), I32)],
        scratch_shapes=[pltpu.VMEM((1, E), F32)],
        compiler_params=_cparams(("arbitrary",)), name="outproj_router",
    )(xall, *o_lat, *o_ctx, ofb, pd, gn, wout, mods_l, mods_l, mods_l, lng, lnb, rw1, rw2, rb)


def _sc_mesh():
    return plsc.VectorSubcoreMesh(core_axis_name="c", subcore_axis_name="s")


def _sc_worker():
    return lax.axis_index("s") * SC_CORES + lax.axis_index("c")


def _sc_dispatch(hp, dest_k, rows):
    N, W = hp.shape
    tpw = N // SC_WORKERS
    ch = SC_DISPATCH_CHUNK
    assert N % SC_WORKERS == 0 and tpw % ch == 0
    nch = tpw // ch
    idx = dest_k.reshape(TOP_K, SC_WORKERS, nch, ch).transpose(1, 2, 0, 3)

    @functools.partial(
        pl.kernel, mesh=_sc_mesh(), out_type=jax.ShapeDtypeStruct((rows, W), hp.dtype),
        scratch_types=[pltpu.VMEM((nch, TOP_K, ch), I32), pltpu.VMEM((ch, W), hp.dtype), pltpu.SemaphoreType.DMA],
        compiler_params=pltpu.CompilerParams(use_tc_tiling_on_sc=True), name="sc_dispatch")
    def scatter(hp_hbm, idx_hbm, out_hbm, idx_v, rows_v, sem):
        wid = _sc_worker()
        base = wid * tpw
        pltpu.sync_copy(idx_hbm.at[wid], idx_v)

        @pl.loop(0, nch)
        def _(c):
            pltpu.sync_copy(hp_hbm.at[pl.ds(base + c * ch, ch)], rows_v)
            copies = [pltpu.async_copy(rows_v, out_hbm.at[idx_v.at[c, k]], sem) for k in range(TOP_K)]
            for cp in copies:
                cp.wait()

    return scatter(hp, idx)


def _sc_gather(src, idx):
    R, W = idx.shape[0], src.shape[1]
    per_w = R // SC_WORKERS
    ch = SC_GATHER_CHUNK
    assert R % SC_WORKERS == 0 and per_w % ch == 0
    nch = per_w // ch

    @functools.partial(
        pl.kernel, mesh=_sc_mesh(), out_type=jax.ShapeDtypeStruct((R, W), src.dtype),
        scratch_types=[pltpu.VMEM((nch, ch), I32), pltpu.VMEM((ch, W), src.dtype), pltpu.SemaphoreType.DMA],
        compiler_params=pltpu.CompilerParams(use_tc_tiling_on_sc=True), name="sc_gather")
    def gather(src_hbm, idx_hbm, out_hbm, idx_v, rows_v, sem):
        wid = _sc_worker()
        base = wid * per_w
        pltpu.sync_copy(idx_hbm.at[wid], idx_v)

        @pl.loop(0, nch)
        def _(c):
            pltpu.async_copy(src_hbm.at[idx_v.at[c]], rows_v, sem).wait()
            pltpu.sync_copy(rows_v, out_hbm.at[pl.ds(base + c * ch, ch)])

    return gather(src, idx.reshape(SC_WORKERS, nch, ch))


def _swiglu(xb, w1, w3, w2):
    a = jnp.dot(xb, w1, preferred_element_type=F32)
    b = jnp.dot(xb, w3, preferred_element_type=F32)
    return jnp.dot((a * _sigmoid(a) * b).astype(BF16), w2, preferred_element_type=F32)


def _expert_kernel(be_ref, nu_ref, xs_ref, w1_ref, w3_ref, w2_ref, y_ref, w1b_ref, w3b_ref, w2b_ref):
    i = pl.program_id(0)
    used = i < nu_ref[0]

    @pl.when(used & ((i == 0) | (be_ref[i] != be_ref[jnp.maximum(i - 1, 0)])))
    def _():
        w1b_ref[...] = w1_ref[0, 0].astype(BF16)
        w3b_ref[...] = w3_ref[0, 0].astype(BF16)
        w2b_ref[...] = w2_ref[0, 0].astype(BF16)

    @pl.when(used)
    def _():
        xb = _unpack_bf16_pairs(xs_ref[...])
        y_ref[...] = _pack_bf16_pairs(_swiglu(xb, w1b_ref[...], w3b_ref[...], w2b_ref[...]))

    @pl.when(jnp.logical_not(used))
    def _():
        y_ref[...] = jnp.zeros_like(y_ref)


def _experts(xs, blk_exp, n_used, w1, w3, w2, layer):
    rows, half = xs.shape
    _, E, D, F = w1.shape
    nblk = rows // MOE_BM
    wmap = lambda i, be, nu: (layer, be[i], 0, 0)
    grid_spec = pltpu.PrefetchScalarGridSpec(
        num_scalar_prefetch=2, grid=(nblk,),
        in_specs=[pl.BlockSpec((MOE_BM, half), lambda i, be, nu: (i, 0)),
                  pl.BlockSpec((1, 1, D, F), wmap), pl.BlockSpec((1, 1, D, F), wmap), pl.BlockSpec((1, 1, F, D), wmap)],
        out_specs=pl.BlockSpec((MOE_BM, half), lambda i, be, nu: (i, 0)),
        scratch_shapes=[pltpu.VMEM((D, F), BF16), pltpu.VMEM((D, F), BF16), pltpu.VMEM((F, D), BF16)])
    return pl.pallas_call(
        _expert_kernel, grid_spec=grid_spec,
        out_shape=jax.ShapeDtypeStruct((rows, half), U32),
        compiler_params=_cparams(("arbitrary",)), name="expert_ffn",
    )(blk_exp, n_used, xs, w1, w3, w2)


def _combine_kernel(yg_ref, gate_ref, hp_ref, sw1_ref, sw3_ref, sw2_ref, x_ref, g2_ref, lng_ref, lnb_ref,
                    o_ref, *, alpha):
    f = _swiglu(_unpack_bf16_pairs(hp_ref[...]), sw1_ref[...], sw3_ref[...], sw2_ref[...])
    gates = gate_ref[...]
    lo = hi = 0.0
    for k in range(TOP_K):
        u = yg_ref[k]
        g = gates[:, k:k + 1]
        lo = lo + g * lax.bitcast_convert_type(u << 16, F32)
        hi = hi + g * lax.bitcast_convert_type(u & jnp.uint32(0xFFFF0000), F32)
    f = f + jnp.concatenate([lo, hi], -1)
    o_ref[...] = _layer_norm(alpha * x_ref[...] + g2_ref[0] * f) * lng_ref[...] + lnb_ref[...]


def _combine(yg, gates, hp, sw1, sw3, sw2, x1, mods_l, lng, lnb, *, n_lat, T, alpha, n_out, tok0):
    D = x1.shape[1]
    N = n_out
    tm = CMB_TILE
    nlat_t, per_b, nb = n_lat // tm, T // tm, n_lat // T
    t0 = tok0 // tm
    F = sw1.shape[1]
    const = lambda i: (0, 0)
    row = lambda i: (i, 0)
    kern = functools.partial(_combine_kernel, alpha=alpha)
    return pl.pallas_call(
        kern, grid=(N // tm,),
        in_specs=[pl.BlockSpec((TOP_K, tm, D // 2), lambda i: (0, i, 0)), pl.BlockSpec((tm, TOP_K), row),
                  pl.BlockSpec((tm, D // 2), row),
                  pl.BlockSpec((D, F), const), pl.BlockSpec((D, F), const), pl.BlockSpec((F, D), const),
                  pl.BlockSpec((tm, D), row),
                  pl.BlockSpec((1, 1, D), lambda i: (jnp.where(i + t0 < nlat_t, (i + t0) // per_b, nb) * 6 + 5, 0, 0)),
                  pl.BlockSpec((1, D), const), pl.BlockSpec((1, D), const)],
        out_specs=pl.BlockSpec((tm, D), row),
        out_shape=jax.ShapeDtypeStruct((N, D), F32),
        compiler_params=_cparams(("arbitrary",)), name="moe_combine",
    )(yg, gates, hp, sw1, sw3, sw2, x1, mods_l, lng, lnb)


def _dest_kernel(start_ref, sel_ref, rank_ref, o_ref):
    sel = sel_ref[...]

    def body(e, acc):
        return jnp.where(sel == e, start_ref[e], acc)

    o_ref[...] = lax.fori_loop(0, start_ref.shape[0], body, jnp.zeros_like(sel)) + rank_ref[...]


def _dispatch_plan(sel, rank, counts):
    N = sel.shape[0]
    n_experts = counts.shape[0]
    padded = (counts + MOE_BM - 1) // MOE_BM * MOE_BM
    pad_end = jnp.cumsum(padded)
    pad_start = (pad_end - padded).astype(I32)
    dense = (N * TOP_K // 128, 128)
    dest = pl.pallas_call(
        _dest_kernel,
        in_specs=[pl.BlockSpec(memory_space=pltpu.SMEM), pl.BlockSpec(memory_space=pltpu.VMEM),
                  pl.BlockSpec(memory_space=pltpu.VMEM)],
        out_specs=pl.BlockSpec(memory_space=pltpu.VMEM),
        out_shape=jax.ShapeDtypeStruct(dense, I32), name="moe_dest_rows",
    )(pad_start, sel.reshape(dense), rank.reshape(dense))
    dest_k = dest.reshape(N, TOP_K).T
    n_blocks = -(-(N * TOP_K) // MOE_BM) + n_experts
    blk_first = jnp.arange(n_blocks, dtype=I32) * MOE_BM
    blk_exp = jnp.minimum(jnp.sum((pad_end[None, :] <= blk_first[:, None]).astype(I32), -1), n_experts - 1)
    n_used = (pad_end[-1] // MOE_BM).astype(I32).reshape(1)
    return dest_k, blk_exp, n_used, n_blocks * MOE_BM


def _rope_tables(T, tile):
    t = jnp.arange(T)
    row = (t // GRID_W).astype(F32)[:, None]
    col = (t % GRID_W).astype(F32)[:, None]

    def group(half):
        inv = ROPE_THETA ** (-jnp.arange(half, dtype=F32) / half)
        ar, ac = row * inv, col * inv
        cos = jnp.concatenate([jnp.cos(ar), jnp.cos(ar), jnp.cos(ac), jnp.cos(ac)], -1)
        sin = jnp.concatenate([-jnp.sin(ar), jnp.sin(ar), -jnp.sin(ac), jnp.sin(ac)], -1)
        return cos, sin

    ones32, zeros32 = jnp.ones((T, 32), F32), jnp.zeros((T, 32), F32)
    cw, sw = group(16)
    cw, sw = jnp.tile(cw, (1, 2)), jnp.tile(sw, (1, 2))
    c8, s8 = group(8)
    cq = jnp.tile(jnp.concatenate([ones32, c8], -1), (1, 2))
    sq = jnp.tile(jnp.concatenate([zeros32, s8], -1), (1, 2))
    ck = jnp.concatenate([c8, ones32, ones32, ones32], -1)
    sk = jnp.concatenate([s8, zeros32, zeros32, zeros32], -1)

    def pad(a, ident):
        return jnp.concatenate([a, jnp.full((tile, 128), ident, F32)], 0)

    return [pad(cw, 1.0), pad(sw, 0.0), pad(cq, 1.0), pad(sq, 0.0), pad(ck, 1.0), pad(sk, 0.0)]


def _layer_weights(w_in, mla_w_uq, mla_w_ukv, gla_w_gf, gla_w_gb, gla_b_gf, gla_b_gb):
    D = w_in.shape[0]
    z = lambda n: jnp.zeros((D, n), F32)
    win = jnp.concatenate([
        w_in[:, 0:256] * 0.125, w_in[:, 256:768],
        w_in[:, 768:1024] * 0.125, w_in[:, 1024:1280],
        w_in[:, 1280:1696], z(96),
        w_in[:, 1696:2464],
        w_in[:, 2464:2496], z(96)], -1).astype(BF16)
    wuq = (mla_w_uq * 0.125).astype(BF16)
    ukv = mla_w_ukv.reshape(-1, N_HEADS, MLA_NOPE + HEAD_DIM)
    kv_rank = ukv.shape[0]
    wk = jnp.concatenate([ukv[:, :, :MLA_NOPE], jnp.zeros((kv_rank, N_HEADS, MLA_ROPE), F32)], -1)
    place = jnp.concatenate([jnp.zeros((MLA_ROPE, MLA_NOPE), F32), jnp.eye(MLA_ROPE, dtype=F32)], -1)
    place = jnp.concatenate([jnp.tile(place, (1, N_HEADS)), jnp.zeros((128 - MLA_ROPE, 256), F32)], 0)
    wkk = jnp.concatenate([wk.reshape(kv_rank, 256), place], 0).astype(BF16)
    wv = ukv[:, :, MLA_NOPE:].reshape(kv_rank, 256).astype(BF16)
    r = GLA_GATE_RANK
    wg = jnp.zeros((128, 256), F32).at[0:r, 0:128].set(gla_w_gf).at[r:2 * r, 128:256].set(gla_w_gb).astype(BF16)
    bg = jnp.concatenate([gla_b_gf, gla_b_gb])[None, :]
    return win, wuq, wkk, wv, wg, bg


def kernel(x, c, ctx, c_ctx, w_ada, b_ada, w_in, na_rpb, wa_sink, mla_g_q, mla_g_kv, mla_w_uq, mla_w_ukv,
           gla_w_gf, gla_b_gf, gla_w_gb, gla_b_gb, gla_g_norm, w_out, ln1_g, ln1_b, ln2_g, ln2_b,
           router_w, router_bias, exp_w1, exp_w3, exp_w2, sh_w1, sh_w3, sh_w2):
    B, T, D = x.shape
    C = ctx.shape[1]
    L = w_ada.shape[0]
    E = router_w.shape[-1]
    n_lat = B * T
    alpha = (2 * L) ** 0.25
    dims = dict(B=B, T=T, C=C)

    cc = jnp.zeros((8, D), F32).at[:B].set(c).at[B].set(c_ctx)
    mods = _mods(cc, w_ada, b_ada)
    tabs = _rope_tables(T, TOK_TILE)
    xall = jnp.concatenate([x.reshape(n_lat, D), ctx.reshape(B * C, D)], 0)

    for l in range(L):
        mods_l = mods[l].reshape(8 * 6, 1, D)
        win, wuq, wkk, wv, wg, bg = _layer_weights(w_in[l], mla_w_uq[l], mla_w_ukv[l], gla_w_gf[l], gla_w_gb[l],
                                                   gla_b_gf[l], gla_b_gb[l])
        pa, pb, pc, pd = _inproj(xall, mods_l, win, tabs, mla_g_q[l][None], mla_g_kv[l][None], wuq, wkk, wv, wg, bg,
                                 n_lat=n_lat, T=T)
        oa = _na_attention(pa, _na_bias_tables(na_rpb[l], T // GRID_W), **dims)
        ob = _wa_attention(pb, wa_sink[l], **dims)
        oc = _mla_attention(pc, **dims)
        o_ctx = _ctx_attention(wa_sink[l], pa, pb, pc, **dims)
        ofb = _gla(pd, **dims)

        rw1 = router_w[l].astype(BF16)
        rw2 = (router_w[l] - rw1.astype(F32)).astype(BF16)
        n_all = xall.shape[0]
        part = n_all // MOE_SPLITS
        assert n_all % (MOE_SPLITS * TOK_TILE) == 0
        shared = (sh_w1[l].astype(BF16), sh_w3[l].astype(BF16), sh_w2[l].astype(BF16))
        gn, wout = jnp.tile(gla_g_norm[l], N_HEADS)[None], w_out[l].astype(BF16)
        routed = []
        for p in range(MOE_SPLITS):
            x1, hp, sel, gates, rank, counts = _outproj(
                xall, (oa, ob, oc), o_ctx, ofb, pd, gn, wout, mods_l, ln1_g[l][None], ln1_b[l][None], rw1, rw2,
                router_bias[l][None], n_lat=n_lat, T=T, alpha=alpha,
                tile0=p * part // TOK_TILE, ntiles=part // TOK_TILE)
            dest_k, blk_exp, n_used, rows = _dispatch_plan(sel, rank, counts.reshape(E))
            routed.append((x1, hp, gates, dest_k, blk_exp, n_used, _sc_dispatch(hp, dest_k, rows)))
        outs = []
        for p, (x1, hp, gates, dest_k, blk_exp, n_used, xs) in enumerate(routed):
            y = _experts(xs, blk_exp, n_used, exp_w1, exp_w3, exp_w2, l)
            yg = _sc_gather(y, dest_k.reshape(-1)).reshape(TOP_K, part, D // 2)
            n_out = part if l < L - 1 else max(min(n_lat - p * part, part), 0)
            if n_out:
                outs.append(_combine(yg, gates, hp, *shared, x1, mods_l, ln2_g[l][None], ln2_b[l][None], n_lat=n_lat,
                                     T=T, alpha=alpha, n_out=n_out, tok0=p * part))
        xall = jnp.concatenate(outs, 0)

    return xall.reshape(B, T, D)
```

```python
import functools

import numpy as np
import jax
import jax.numpy as jnp
from jax import lax
from jax.experimental import pallas as pl
from jax.experimental.pallas import tpu as pltpu
from jax.experimental.pallas import tpu_sc as plsc

F32 = jnp.float32
BF16 = jnp.bfloat16
U32 = jnp.uint32
I32 = jnp.int32

GRID_W = 64
HEAD_DIM = 64
N_HEADS = 4
NA_KH, NA_KW = 8, 16
WA_WINDOW = 128
WA_SUB = 256
MLA_NOPE, MLA_ROPE = 32, 32
GLA_DK, GLA_DV = 32, 64
GLA_GATE_RANK = 16
GLA_GATE_NORM = 16.0
GLA_CHUNK = 64
TOP_K = 8
ROUTED_SCALE = 2.5
ROPE_THETA = 10000.0
LN_EPS = 1e-5
RMS_EPS = 1e-6
NEG = -1e30
LOG2E = 1.4426950408889634
Q_SCALE = HEAD_DIM ** -0.5 * LOG2E

VMEM_LIMIT = 56 * 1024 * 1024
TOK_TILE = 512
ATT_TQ = 512
MLA_TQ = 1024
MLA_TK = 1024
GLA_BLOCK = 256
MOE_BM = 1024
MOE_SPLITS = 1
CMB_TILE = 256
SC_CORES = 2
SC_WORKERS = 32
SC_DISPATCH_CHUNK = 48
SC_GATHER_CHUNK = 64

PW = 2688
_NT = (((1,), (1,)), ((), ()))


def _cparams(sem, vmem=VMEM_LIMIT):
    return pltpu.CompilerParams(dimension_semantics=sem, vmem_limit_bytes=vmem)


def _sigmoid(x):
    return 1.0 / (1.0 + jnp.exp(-x))


def _layer_norm(x):
    mu = jnp.mean(x, -1, keepdims=True)
    xc = x - mu
    var = jnp.mean(xc * xc, -1, keepdims=True)
    return xc * lax.rsqrt(var + LN_EPS)


def _rms(x):
    return x * lax.rsqrt(jnp.mean(x * x, -1, keepdims=True) + RMS_EPS)


def _bdot(a, b):
    return jnp.dot(a.astype(BF16), b.astype(BF16), preferred_element_type=F32)


def _bdot_nt(a, b):
    return lax.dot_general(a.astype(BF16), b.astype(BF16), _NT, preferred_element_type=F32)


def _split3(a):
    a1 = a.astype(BF16)
    r = a - a1.astype(F32)
    a2 = r.astype(BF16)
    a3 = (r - a2.astype(F32)).astype(BF16)
    return a1, a2, a3


def _mods_kernel(cc_ref, w_ref, b_ref, o_ref):
    cc = cc_ref[...]
    s = cc * _sigmoid(cc)
    o_ref[0] = _bdot(s, w_ref[0]) + b_ref[0]


def _mods(cc, w_ada, b_ada):
    L, D, W = w_ada.shape
    tn = 1536
    return pl.pallas_call(
        _mods_kernel,
        grid=(L, W // tn),
        in_specs=[pl.BlockSpec((8, D), lambda l, j: (0, 0)),
                  pl.BlockSpec((1, D, tn), lambda l, j: (l, 0, j)),
                  pl.BlockSpec((1, 1, tn), lambda l, j: (l, 0, j))],
        out_specs=pl.BlockSpec((1, 8, tn), lambda l, j: (l, 0, j)),
        out_shape=jax.ShapeDtypeStruct((L, 8, W), F32),
        compiler_params=_cparams(("arbitrary", "arbitrary")),
        name="adaln_mods",
    )(cc, w_ada, b_ada.reshape(L, 1, W))


def _rope(z, cos, sin, half):
    w = z.shape[-1]
    lane = lax.broadcasted_iota(I32, z.shape, 1)
    first = (lane % (2 * half)) < half
    partner = jnp.where(first, pltpu.roll(z, w - half, 1), pltpu.roll(z, half, 1))
    return z * cos + partner * sin


def _log_sigmoid(x):
    return jnp.minimum(x, 0.0) - jnp.log(1.0 + jnp.exp(-jnp.abs(x)))


def _inproj_kernel(x_ref, shift_ref, scale_ref, win_ref, cw_ref, sw_ref, cq_ref, sq_ref, ck_ref, sk_ref,
                   gq_ref, gkv_ref, wuq_ref, wkk_ref, wv_ref, wg_ref, bg_ref,
                   pa_ref, pb_ref, pc_ref, pd_ref):
    h = _layer_norm(x_ref[...]) * (1.0 + scale_ref[0]) + shift_ref[0]
    p = jnp.dot(h.astype(BF16), win_ref[...], preferred_element_type=F32)
    pa_ref[...] = p[:, 0:768].astype(BF16)
    cw, sw = cw_ref[...], sw_ref[...]
    pb_ref[:, 0:128] = _rope(p[:, 768:896], cw, sw, 16).astype(BF16)
    pb_ref[:, 128:256] = _rope(p[:, 896:1024], cw, sw, 16).astype(BF16)
    pb_ref[:, 256:384] = _rope(p[:, 1024:1152], cw, sw, 16).astype(BF16)
    pb_ref[:, 384:512] = p[:, 1152:1280].astype(BF16)
    cqn = _rms(p[:, 1280:1536]) * gq_ref[...]
    q = jnp.dot(cqn.astype(BF16), wuq_ref[...], preferred_element_type=F32)
    cq, sq = cq_ref[...], sq_ref[...]
    pc_ref[:, 0:128] = _rope(q[:, 0:128], cq, sq, 8).astype(BF16)
    pc_ref[:, 128:256] = _rope(q[:, 128:256], cq, sq, 8).astype(BF16)
    ckvn = (_rms(p[:, 1536:1664]) * gkv_ref[...]).astype(BF16)
    kr = _rope(p[:, 1664:1792], ck_ref[...], sk_ref[...], 8).astype(BF16)
    kin = jnp.concatenate([ckvn, kr], axis=-1)
    pc_ref[:, 256:512] = jnp.dot(kin, wkk_ref[...], preferred_element_type=F32).astype(BF16)
    pc_ref[:, 512:768] = jnp.dot(ckvn, wv_ref[...], preferred_element_type=F32).astype(BF16)
    pd_ref[:, 0:128] = p[:, 1792:1920] * (GLA_DK ** -0.5)
    pd_ref[:, 128:768] = p[:, 1920:2560]
    pre = jnp.dot(p[:, 2560:2688].astype(BF16), wg_ref[...], preferred_element_type=F32) + bg_ref[...]
    pd_ref[:, 768:1024] = _log_sigmoid(pre) * (1.0 / GLA_GATE_NORM)


def _inproj(xall, mods_l, win, tabs, gq, gkv, wuq, wkk, wv, wg, bg, *, n_lat, T):
    N, D = xall.shape
    tm = TOK_TILE
    nlat_t = n_lat // tm
    per_b = T // tm
    nb = n_lat // T

    def midx(k):
        return lambda i: (jnp.where(i < nlat_t, i // per_b, nb) * 6 + k, 0, 0)

    def tidx(i):
        return (jnp.where(i < nlat_t, i % per_b, per_b), 0)

    const = lambda i: (0, 0)
    tab_spec = pl.BlockSpec((tm, 128), tidx)
    in_specs = [pl.BlockSpec((tm, D), lambda i: (i, 0)),
                pl.BlockSpec((1, 1, D), midx(0)), pl.BlockSpec((1, 1, D), midx(1)),
                pl.BlockSpec((D, PW), const)] + [tab_spec] * 6 + [
                pl.BlockSpec((1, 256), const), pl.BlockSpec((1, 128), const),
                pl.BlockSpec((256, 256), const), pl.BlockSpec((256, 256), const),
                pl.BlockSpec((128, 256), const), pl.BlockSpec((128, 256), const),
                pl.BlockSpec((1, 256), const)]
    out_specs = [pl.BlockSpec((tm, 768), lambda i: (i, 0)), pl.BlockSpec((tm, 512), lambda i: (i, 0)),
                 pl.BlockSpec((tm, 768), lambda i: (i, 0)), pl.BlockSpec((tm, 1024), lambda i: (i, 0))]
    out_shape = [jax.ShapeDtypeStruct((N, 768), BF16), jax.ShapeDtypeStruct((N, 512), BF16),
                 jax.ShapeDtypeStruct((N, 768), BF16), jax.ShapeDtypeStruct((N, 1024), F32)]
    return pl.pallas_call(
        _inproj_kernel, grid=(N // tm,), in_specs=in_specs, out_specs=out_specs, out_shape=out_shape,
        compiler_params=_cparams(("arbitrary",)), name="inproj",
    )(xall, mods_l, mods_l, win, *tabs, gq, gkv, wuq, wkk, wv, wg, bg)


def _softmax_av(parts, extra_logit=None):
    m = functools.reduce(jnp.maximum, [jnp.max(s, -1, keepdims=True) for s, _ in parts])
    if extra_logit is not None:
        m = jnp.maximum(m, extra_logit)
    l = 0.0 if extra_logit is None else jnp.exp2(extra_logit - m)
    o = None
    for s, v in parts:
        e = jnp.exp2(s - m)
        l = l + jnp.sum(e, -1, keepdims=True)
        c = jnp.dot(e.astype(BF16), v, preferred_element_type=F32)
        o = c if o is None else o + c
    return o * (1.0 / l)


def _na_kernel(q_ref, k_ref, v_ref, kc_ref, vc_ref, bias_ref, o_ref, *, rows):
    j = pl.program_id(1)
    ws = pl.multiple_of(jnp.clip(8 * j - 4, 0, rows - 16) * GRID_W, 256)
    win = 16 * GRID_W
    outs = []
    for h in range(N_HEADS):
        sl = slice(HEAD_DIM * h, HEAD_DIM * (h + 1))
        q = q_ref[:, sl]
        s = lax.dot_general(q, k_ref[pl.ds(ws, win), sl], _NT, preferred_element_type=F32) + bias_ref[0, h]
        sc = lax.dot_general(q, kc_ref[:, sl], _NT, preferred_element_type=F32)
        outs.append(_softmax_av([(s, v_ref[pl.ds(ws, win), sl]), (sc, vc_ref[:, sl])]))
    o_ref[...] = jnp.concatenate(outs, -1).astype(BF16)


def _na_bias_tables(rpb, rows):
    nj = rows // 8
    ro_all, ok_all = [], []
    for j in (0, 1, nj - 1):
        ws = int(np.clip(8 * j - 4, 0, rows - 16))
        r = 8 * j + np.arange(8)[:, None, None, None]
        qc = np.arange(GRID_W)[None, :, None, None]
        kr = ws + np.arange(16)[None, None, :, None]
        kc = np.arange(GRID_W)[None, None, None, :]
        rs = np.clip(r - NA_KH // 2, 0, rows - NA_KH)
        cs = np.clip(qc - NA_KW // 2, 0, GRID_W - NA_KW)
        ok = (kr >= rs) & (kr < rs + NA_KH) & (kc >= cs) & (kc < cs + NA_KW)
        ro = np.clip(kr - r + (NA_KH - 1), 0, 2 * NA_KH - 2)
        shp = (8, GRID_W, 16, GRID_W)
        ro_all.append(np.broadcast_to(ro, shp).reshape(512, 1024))
        ok_all.append(np.broadcast_to(ok, shp).reshape(512, 1024))
    ok = np.stack(ok_all)
    hi = lax.Precision.HIGHEST
    co = np.clip(np.arange(GRID_W)[None, :] - np.arange(GRID_W)[:, None] + (NA_KW - 1), 0, 2 * NA_KW - 2)
    col_hot = (co[..., None] == np.arange(2 * NA_KW - 1)).astype(np.float32)
    cb = jnp.einsum('hdo,qko->hdqk', rpb, col_hot, precision=hi)
    ro = np.stack(ro_all).reshape(3, 8, GRID_W, 16, GRID_W)[:, :, 0, :, 0]
    row_hot = (ro[..., None] == np.arange(2 * NA_KH - 1)).astype(np.float32)
    b = jnp.einsum('vrsd,hdqk->vhrqsk', row_hot, cb, precision=hi)
    b = b.reshape(3, rpb.shape[0], 512, 1024)
    return jnp.where(ok[:, None], b * LOG2E, NEG)


def _na_attention(pa, bias, *, B, T, C):
    rows = T // GRID_W
    nj = T // ATT_TQ
    cb = B * T // C
    kern = functools.partial(_na_kernel, rows=rows)
    return pl.pallas_call(
        kern, grid=(B, nj),
        in_specs=[pl.BlockSpec((ATT_TQ, 256), lambda b, j: (b * nj + j, 0)),
                  pl.BlockSpec((T, 256), lambda b, j: (b, 1)),
                  pl.BlockSpec((T, 256), lambda b, j: (b, 2)),
                  pl.BlockSpec((C, 256), lambda b, j: (cb + b, 1)),
                  pl.BlockSpec((C, 256), lambda b, j: (cb + b, 2)),
                  pl.BlockSpec((1, N_HEADS, 512, 1024),
                               lambda b, j: (jnp.where(j == 0, 0, jnp.where(j == nj - 1, 2, 1)), 0, 0, 0))],
        out_specs=pl.BlockSpec((ATT_TQ, 256), lambda b, j: (b * nj + j, 0)),
        out_shape=jax.ShapeDtypeStruct((B * T, 256), BF16),
        compiler_params=_cparams(("arbitrary", "arbitrary")), name="na_attention",
    )(pa, pa, pa, pa, pa, bias)


def _wa_kernel(sink_ref, q_ref, k_ref, v_ref, kc_ref, vc_ref, o_ref, *, T):
    i = pl.program_id(1)
    win = WA_SUB + 2 * WA_WINDOW
    for sub in range(ATT_TQ // WA_SUB):
        rows = slice(sub * WA_SUB, (sub + 1) * WA_SUB)
        start = i * ATT_TQ + sub * WA_SUB
        ws = pl.multiple_of(jnp.clip(start - WA_WINDOW, 0, T - win), 128)
        rel = ((lax.broadcasted_iota(I32, (WA_SUB, win), 1) + (ws - start))
               - lax.broadcasted_iota(I32, (WA_SUB, win), 0))
        valid = jnp.abs(rel) <= WA_WINDOW
        outs = []
        for h in range(N_HEADS):
            g = h // 2
            ksl = slice(HEAD_DIM * g, HEAD_DIM * (g + 1))
            q = q_ref[rows, HEAD_DIM * h:HEAD_DIM * (h + 1)]
            s = lax.dot_general(q, k_ref[pl.ds(ws, win), ksl], _NT, preferred_element_type=F32)
            s = jnp.where(valid, s, NEG)
            sc = lax.dot_general(q, kc_ref[:, ksl], _NT, preferred_element_type=F32)
            outs.append(_softmax_av([(s, v_ref[pl.ds(ws, win), ksl]), (sc, vc_ref[:, ksl])],
                                    extra_logit=sink_ref[h]))
        o_ref[rows, :] = jnp.concatenate(outs, -1).astype(BF16)


def _wa_attention(pb, sink, *, B, T, C):
    nj = T // ATT_TQ
    cb = B * T // C
    kern = functools.partial(_wa_kernel, T=T)
    return pl.pallas_call(
        kern, grid=(B, nj),
        in_specs=[pl.BlockSpec(memory_space=pltpu.SMEM),
                  pl.BlockSpec((ATT_TQ, 256), lambda b, j: (b * nj + j, 0)),
                  pl.BlockSpec((T, 128), lambda b, j: (b, 2)),
                  pl.BlockSpec((T, 128), lambda b, j: (b, 3)),
                  pl.BlockSpec((C, 128), lambda b, j: (cb + b, 2)),
                  pl.BlockSpec((C, 128), lambda b, j: (cb + b, 3))],
        out_specs=pl.BlockSpec((ATT_TQ, 256), lambda b, j: (b * nj + j, 0)),
        out_shape=jax.ShapeDtypeStruct((B * T, 256), BF16),
        compiler_params=_cparams(("arbitrary", "arbitrary")), name="wa_attention",
    )(sink, pb, pb, pb, pb, pb)


def _mla_kernel(q_ref, k_ref, v_ref, kc_ref, vc_ref, o_ref, *, T):
    nk = T // MLA_TK
    heads = [slice(HEAD_DIM * h, HEAD_DIM * (h + 1)) for h in range(N_HEADS)]
    qs = [q_ref[:, sl] for sl in heads]
    init = []
    for q, sl in zip(qs, heads):
        sc = lax.dot_general(q, kc_ref[:, sl], _NT, preferred_element_type=F32)
        m0 = jnp.max(sc, -1, keepdims=True)
        e0 = jnp.exp2(sc - m0)
        init += [m0, jnp.sum(e0, -1, keepdims=True),
                 jnp.dot(e0.astype(BF16), vc_ref[:, sl], preferred_element_type=F32)]

    def body(c, carry):
        ks = pl.multiple_of(c * MLA_TK, MLA_TK)
        out = []
        for h, (q, sl) in enumerate(zip(qs, heads)):
            m, l, acc = carry[3 * h:3 * h + 3]
            s = lax.dot_general(q, k_ref[pl.ds(ks, MLA_TK), sl], _NT, preferred_element_type=F32)
            mn = jnp.maximum(m, jnp.max(s, -1, keepdims=True))
            a = jnp.exp2(m - mn)
            e = jnp.exp2(s - mn)
            l = l * a + jnp.sum(e, -1, keepdims=True)
            acc = acc * a + jnp.dot(e.astype(BF16), v_ref[pl.ds(ks, MLA_TK), sl], preferred_element_type=F32)
            out += [mn, l, acc]
        return tuple(out)

    fin = lax.fori_loop(0, nk, body, tuple(init), unroll=2)
    outs = [fin[3 * h + 2] * (1.0 / fin[3 * h + 1]) for h in range(N_HEADS)]
    o_ref[...] = jnp.concatenate(outs, -1).astype(BF16)


def _mla_attention(pc, *, B, T, C):
    nj = T // MLA_TQ
    cb = B * T // C
    kern = functools.partial(_mla_kernel, T=T)
    return pl.pallas_call(
        kern, grid=(B, nj),
        in_specs=[pl.BlockSpec((MLA_TQ, 256), lambda b, j: (b * nj + j, 0)),
                  pl.BlockSpec((T, 256), lambda b, j: (b, 1)),
                  pl.BlockSpec((T, 256), lambda b, j: (b, 2)),
                  pl.BlockSpec((C, 256), lambda b, j: (cb + b, 1)),
                  pl.BlockSpec((C, 256), lambda b, j: (cb + b, 2))],
        out_specs=pl.BlockSpec((MLA_TQ, 256), lambda b, j: (b * nj + j, 0)),
        out_shape=jax.ShapeDtypeStruct((B * T, 256), BF16),
        compiler_params=_cparams(("arbitrary", "arbitrary")), name="mla_attention",
    )(pc, pc, pc, pc, pc)


def _ctx_kernel(sink_ref, pa_ref, pb_ref, pc_ref, oa_ref, ob_ref, oc_ref):
    def attend(p_ref, koff, voff, kv_heads, out_ref, sink):
        outs = []
        for h in range(N_HEADS):
            g = h * kv_heads // N_HEADS
            q = p_ref[:, HEAD_DIM * h:HEAD_DIM * (h + 1)]
            k = p_ref[:, koff + HEAD_DIM * g:koff + HEAD_DIM * (g + 1)]
            v = p_ref[:, voff + HEAD_DIM * g:voff + HEAD_DIM * (g + 1)]
            s = lax.dot_general(q, k, _NT, preferred_element_type=F32)
            outs.append(_softmax_av([(s, v)], extra_logit=sink_ref[h] if sink else None))
        out_ref[...] = jnp.concatenate(outs, -1).astype(BF16)

    attend(pa_ref, 256, 512, 4, oa_ref, False)
    attend(pb_ref, 256, 384, 2, ob_ref, True)
    attend(pc_ref, 256, 512, 4, oc_ref, False)


def _ctx_attention(sink, pa, pb, pc, *, B, T, C):
    cb = B * T // C
    row = lambda b: (cb + b, 0)
    return pl.pallas_call(
        _ctx_kernel, grid=(B,),
        in_specs=[pl.BlockSpec(memory_space=pltpu.SMEM),
                  pl.BlockSpec((C, 768), row), pl.BlockSpec((C, 512), row), pl.BlockSpec((C, 768), row)],
        out_specs=[pl.BlockSpec((C, 256), lambda b: (b, 0))] * 3,
        out_shape=[jax.ShapeDtypeStruct((B * C, 256), BF16)] * 3,
        compiler_params=_cparams(("arbitrary",)), name="ctx_attention",
    )(sink, pa, pb, pc)


def _gla_dir(pd_ref, o_ref, st_ref, reverse):
    nchunk = GLA_BLOCK // GLA_CHUNK
    L, DK, DV = GLA_CHUNK, N_HEADS * GLA_DK, N_HEADS * GLA_DV

    def iota(shape, dim):
        return lax.broadcasted_iota(I32, shape, dim)

    ri, ci = iota((L, L), 0), iota((L, L), 1)
    tri = jnp.where((ci >= ri) if reverse else (ci <= ri), 1.0, 0.0).astype(BF16)
    tri_t = jnp.where((ri >= ci) if reverse else (ri <= ci), 1.0, 0.0).astype(BF16)
    kcol, qrow = iota((L, N_HEADS * L), 1) % L, iota((L, N_HEADS * L), 0)
    keep = (kcol >= qrow) if reverse else (kcol <= qrow)
    k_owner = [iota((L, DK), 1) // GLA_DK == h for h in range(N_HEADS)]
    v_owner = [iota((L, DV), 1) // GLA_DV == h for h in range(N_HEADS)]
    s_owner = iota((DK, DV), 0) // GLA_DK == iota((DK, DV), 1) // GLA_DV
    gcol = 896 if reverse else 768
    end = 0 if reverse else L - 1
    order = range(nchunk - 1, -1, -1) if reverse else range(nchunk)

    def dot3(a_parts, b_parts):
        return sum(jnp.dot(a, b, preferred_element_type=F32) for a in a_parts for b in b_parts)

    per_chunk = []
    for c in order:
        rs = slice(c * L, (c + 1) * L)
        g = pd_ref[rs, gcol:gcol + DK]
        q = pd_ref[rs, 0:DK]
        k = pd_ref[rs, DK:2 * DK]
        vb = pd_ref[rs, 2 * DK:2 * DK + DV].astype(BF16)
        b = dot3([tri], _split3(g))
        bt = dot3(_split3(g.T), [tri_t])
        bend = bt[:, end:end + 1]
        qd = (q * jnp.exp(b)).astype(BF16)
        ki = (k * jnp.exp(-b)).astype(BF16)
        kdt = (k.T * jnp.exp(bend - bt)).astype(BF16)
        kbd = jnp.concatenate([jnp.where(m, ki, jnp.zeros_like(ki)) for m in k_owner], 0)
        vbd = jnp.concatenate([jnp.where(m, vb, jnp.zeros_like(vb)) for m in v_owner], 0)
        att = lax.dot_general(qd, kbd, _NT, preferred_element_type=F32)
        att = jnp.where(keep, att, 0.0).astype(BF16)
        intra = jnp.dot(att, vbd, preferred_element_type=F32)
        u = jnp.where(s_owner, jnp.dot(kdt, vb, preferred_element_type=F32), 0.0)
        per_chunk.append((rs, qd, intra, u, jnp.exp(bend)))
    st = st_ref[...]
    for rs, qd, intra, u, dec in per_chunk:
        o_ref[rs, :] = intra + jnp.dot(qd, st.astype(BF16), preferred_element_type=F32)
        st = st * dec + u
    st_ref[...] = st


def _gla_kernel(pdf_ref, pdb_ref, of_ref, ob_ref, stf_ref, stb_ref):
    @pl.when(pl.program_id(1) == 0)
    def _():
        stf_ref[...] = jnp.zeros_like(stf_ref)
        stb_ref[...] = jnp.zeros_like(stb_ref)

    _gla_dir(pdf_ref, of_ref, stf_ref, False)
    _gla_dir(pdb_ref, ob_ref, stb_ref, True)


def _gla(pd, *, B, T, C):
    N = pd.shape[0]
    assert C == GLA_BLOCK
    nlb = T // GLA_BLOCK
    cb = B * T // GLA_BLOCK

    def fwd(b, s):
        return (jnp.where(s == 0, cb + b, b * nlb + s - 1), 0)

    def bwd(b, s):
        return (jnp.where(s == 0, cb + b, b * nlb + nlb - s), 0)

    state = pltpu.VMEM((N_HEADS * GLA_DK, N_HEADS * GLA_DV), F32)
    return pl.pallas_call(
        _gla_kernel, grid=(B, nlb + 1),
        in_specs=[pl.BlockSpec((GLA_BLOCK, 1024), fwd), pl.BlockSpec((GLA_BLOCK, 1024), bwd)],
        out_specs=[pl.BlockSpec((GLA_BLOCK, 256), fwd), pl.BlockSpec((GLA_BLOCK, 256), bwd)],
        out_shape=[jax.ShapeDtypeStruct((N, 256), F32)] * 2,
        scratch_shapes=[state, state],
        compiler_params=_cparams(("arbitrary", "arbitrary")), name="gla_scan",
    )(pd, pd)


def _pack_bf16_pairs(h):
    w = h.shape[-1] // 2
    bits = lax.bitcast_convert_type(h.astype(BF16).astype(F32), U32)
    return (bits[:, w:] & jnp.uint32(0xFFFF0000)) | (bits[:, :w] >> 16)


def _unpack_bf16_pairs(u):
    lo = lax.bitcast_convert_type(u << 16, F32)
    hi = lax.bitcast_convert_type(u & jnp.uint32(0xFFFF0000), F32)
    return jnp.concatenate([lo, hi], -1).astype(BF16)


def _outproj_kernel(x_ref, oa_ref, ob_ref, oc_ref, oac_ref, obc_ref, occ_ref, gf_ref, gb_ref, r_ref, gn_ref, wout_ref,
                    g1_ref, sh2_ref, sc2_ref, lng_ref, lnb_ref, rw1_ref, rw2_ref, rb_ref,
                    x1_ref, hp_ref, sel_ref, gate_ref, rank_ref, count_ref, cnt_ref, *, alpha, nlat_t, tile0):
    is_lat = pl.program_id(0) + tile0 < nlat_t
    pick = lambda lat_ref, ctx_ref: jnp.where(is_lat, lat_ref[...], ctx_ref[...])
    s = gf_ref[...] + gb_ref[...]
    w = s.shape[-1]
    same_head = (lax.broadcasted_iota(I32, (w, w), 0) // GLA_DV) == (lax.broadcasted_iota(I32, (w, w), 1) // GLA_DV)
    avg = jnp.where(same_head, 1.0 / GLA_DV, 0.0).astype(BF16)
    ms = sum(jnp.dot(part, avg, preferred_element_type=F32) for part in _split3(s * s))
    r = r_ref[...]
    od = s * lax.rsqrt(ms + RMS_EPS) * gn_ref[...] * (r * _sigmoid(r))
    ocat = jnp.concatenate([pick(oa_ref, oac_ref), pick(ob_ref, obc_ref), pick(oc_ref, occ_ref), od.astype(BF16)], -1)
    o = jnp.dot(ocat, wout_ref[...], preferred_element_type=F32)
    x1 = _layer_norm(alpha * x_ref[...] + g1_ref[0] * o) * lng_ref[...] + lnb_ref[...]
    x1_ref[...] = x1
    h2 = _layer_norm(x1) * (1.0 + sc2_ref[0]) + sh2_ref[0]
    hp_ref[...] = _pack_bf16_pairs(h2)
    a1, a2, _ = _split3(h2)
    w1, w2 = rw1_ref[...], rw2_ref[...]
    logits = (jnp.dot(a1, w1, preferred_element_type=F32) + jnp.dot(a1, w2, preferred_element_type=F32)
              + jnp.dot(a2, w1, preferred_element_type=F32))
    scores = _sigmoid(logits)
    work = scores + rb_ref[...]
    tm, ne = scores.shape
    lane = lax.broadcasted_iota(I32, (tm, ne), 1).astype(F32)
    lane8 = lax.broadcasted_iota(I32, (tm, TOP_K), 1)
    sel = jnp.zeros((tm, TOP_K), F32)
    gsel = jnp.zeros((tm, TOP_K), F32)
    chosen = jnp.zeros((tm, ne), F32)
    picks = []
    for k in range(TOP_K):
        m = jnp.max(work, -1, keepdims=True)
        idx = jnp.min(jnp.where(work == m, lane, float(ne)), -1, keepdims=True)
        hit = lane == idx
        picks.append(idx)
        sc = jnp.sum(jnp.where(hit, scores, 0.0), -1, keepdims=True)
        sel = jnp.where(lane8 == k, idx, sel)
        gsel = jnp.where(lane8 == k, sc, gsel)
        chosen = jnp.where(hit, 1.0, chosen)
        work = jnp.where(hit, -jnp.inf, work)
    sel_ref[...] = sel.astype(I32)
    gate_ref[...] = ROUTED_SCALE * gsel / jnp.sum(gsel, -1, keepdims=True)
    @pl.when(pl.program_id(0) == 0)
    def _():
        cnt_ref[...] = jnp.zeros_like(cnt_ref)

    ri = lax.broadcasted_iota(I32, (tm, tm), 0)
    ci = lax.broadcasted_iota(I32, (tm, tm), 1)
    before = jnp.where(ci < ri, 1.0, 0.0).astype(BF16)
    prefix = jnp.dot(before, chosen.astype(BF16), preferred_element_type=F32) + cnt_ref[...]
    rank = jnp.zeros((tm, TOP_K), F32)
    for k in range(TOP_K):
        rk = jnp.sum(jnp.where(lane == picks[k], prefix, 0.0), -1, keepdims=True)
        rank = jnp.where(lane8 == k, rk, rank)
    rank_ref[...] = rank.astype(I32)
    cnt_ref[...] = cnt_ref[...] + jnp.sum(chosen, 0, keepdims=True)
    count_ref[...] = cnt_ref[...].astype(I32)


def _outproj(xall, o_lat, o_ctx, ofb, pd, gn, wout, mods_l, lng, lnb, rw1, rw2, rb, *, n_lat, T, alpha, tile0, ntiles):
    D = xall.shape[1]
    tm = TOK_TILE
    N = ntiles * tm
    nlat_t, per_b, nb = n_lat // tm, T // tm, n_lat // T

    def midx(k):
        return lambda i: (jnp.where(i + tile0 < nlat_t, (i + tile0) // per_b, nb) * 6 + k, 0, 0)

    const = lambda i: (0, 0)
    row = lambda i: (i, 0)
    src_row = lambda i: (i + tile0, 0)
    lat_row = lambda i: (jnp.minimum(i + tile0, nlat_t - 1), 0)
    ctx_row = lambda i: (jnp.maximum(i + tile0 - nlat_t, 0), 0)
    E = rw1.shape[1]
    kern = functools.partial(_outproj_kernel, alpha=alpha, nlat_t=nlat_t, tile0=tile0)
    return pl.pallas_call(
        kern, grid=(ntiles,),
        in_specs=[pl.BlockSpec((tm, D), src_row)] + [pl.BlockSpec((tm, 256), lat_row)] * 3
                 + [pl.BlockSpec((tm, 256), ctx_row)] * 3 + [
                  pl.BlockSpec((tm, 256), src_row), pl.BlockSpec((tm, 256), src_row),
                  pl.BlockSpec((tm, 256), lambda i: (i + tile0, 2)), pl.BlockSpec((1, 256), const),
                  pl.BlockSpec((D, D), const),
                  pl.BlockSpec((1, 1, D), midx(2)), pl.BlockSpec((1, 1, D), midx(3)), pl.BlockSpec((1, 1, D), midx(4)),
                  pl.BlockSpec((1, D), const), pl.BlockSpec((1, D), const),
                  pl.BlockSpec((D, E), const), pl.BlockSpec((D, E), const), pl.BlockSpec((1, E), const)],
        out_specs=[pl.BlockSpec((tm, D), row), pl.BlockSpec((tm, D // 2), row),
                   pl.BlockSpec((tm, TOP_K), row), pl.BlockSpec((tm, TOP_K), row), pl.BlockSpec((tm, TOP_K), row),
                   pl.BlockSpec((1, E), const)],
        out_shape=[jax.ShapeDtypeStruct((N, D), F32), jax.ShapeDtypeStruct((N, D // 2), U32),
                   jax.ShapeDtypeStruct((N, TOP_K), I32), jax.ShapeDtypeStruct((N, TOP_K), F32),
                   jax.ShapeDtypeStruct((N, TOP_K), I32), jax.ShapeDtypeStruct((1, E), I32)],
        scratch_shapes=[pltpu.VMEM((1, E), F32)],
        compiler_params=_cparams(("arbitrary",)), name="outproj_router",
    )(xall, *o_lat, *o_ctx, *ofb, pd, gn, wout, mods_l, mods_l, mods_l, lng, lnb, rw1, rw2, rb)


def _sc_mesh():
    return plsc.VectorSubcoreMesh(core_axis_name="c", subcore_axis_name="s")


def _sc_worker():
    return lax.axis_index("s") * SC_CORES + lax.axis_index("c")


def _sc_dispatch(hp, dest_k, rows):
    N, W = hp.shape
    tpw = N // SC_WORKERS
    ch = SC_DISPATCH_CHUNK
    assert N % SC_WORKERS == 0 and tpw % ch == 0
    nch = tpw // ch
    idx = dest_k.reshape(TOP_K, SC_WORKERS, nch, ch).transpose(1, 2, 0, 3)

    @functools.partial(
        pl.kernel, mesh=_sc_mesh(), out_type=jax.ShapeDtypeStruct((rows, W), hp.dtype),
        scratch_types=[pltpu.VMEM((nch, TOP_K, ch), I32), pltpu.VMEM((ch, W), hp.dtype), pltpu.SemaphoreType.DMA],
        compiler_params=pltpu.CompilerParams(use_tc_tiling_on_sc=True), name="sc_dispatch")
    def scatter(hp_hbm, idx_hbm, out_hbm, idx_v, rows_v, sem):
        wid = _sc_worker()
        base = wid * tpw
        pltpu.sync_copy(idx_hbm.at[wid], idx_v)

        @pl.loop(0, nch)
        def _(c):
            pltpu.sync_copy(hp_hbm.at[pl.ds(base + c * ch, ch)], rows_v)
            copies = [pltpu.async_copy(rows_v, out_hbm.at[idx_v.at[c, k]], sem) for k in range(TOP_K)]
            for cp in copies:
                cp.wait()

    return scatter(hp, idx)


def _sc_gather(src, idx):
    R, W = idx.shape[0], src.shape[1]
    per_w = R // SC_WORKERS
    ch = SC_GATHER_CHUNK
    assert R % SC_WORKERS == 0 and per_w % ch == 0
    nch = per_w // ch

    @functools.partial(
        pl.kernel, mesh=_sc_mesh(), out_type=jax.ShapeDtypeStruct((R, W), src.dtype),
        scratch_types=[pltpu.VMEM((nch, ch), I32), pltpu.VMEM((ch, W), src.dtype), pltpu.SemaphoreType.DMA],
        compiler_params=pltpu.CompilerParams(use_tc_tiling_on_sc=True), name="sc_gather")
    def gather(src_hbm, idx_hbm, out_hbm, idx_v, rows_v, sem):
        wid = _sc_worker()
        base = wid * per_w
        pltpu.sync_copy(idx_hbm.at[wid], idx_v)

        @pl.loop(0, nch)
        def _(c):
            pltpu.async_copy(src_hbm.at[idx_v.at[c]], rows_v, sem).wait()
            pltpu.sync_copy(rows_v, out_hbm.at[pl.ds(base + c * ch, ch)])

    return gather(src, idx.reshape(SC_WORKERS, nch, ch))


def _swiglu(xb, w1, w3, w2):
    a = jnp.dot(xb, w1, preferred_element_type=F32)
    b = jnp.dot(xb, w3, preferred_element_type=F32)
    return jnp.dot((a * _sigmoid(a) * b).astype(BF16), w2, preferred_element_type=F32)


def _expert_kernel(be_ref, nu_ref, xs_ref, w1_ref, w3_ref, w2_ref, y_ref, w1b_ref, w3b_ref, w2b_ref):
    i = pl.program_id(0)
    used = i < nu_ref[0]

    @pl.when(used & ((i == 0) | (be_ref[i] != be_ref[jnp.maximum(i - 1, 0)])))
    def _():
        w1b_ref[...] = w1_ref[0, 0].astype(BF16)
        w3b_ref[...] = w3_ref[0, 0].astype(BF16)
        w2b_ref[...] = w2_ref[0, 0].astype(BF16)

    @pl.when(used)
    def _():
        xb = _unpack_bf16_pairs(xs_ref[...])
        y_ref[...] = _pack_bf16_pairs(_swiglu(xb, w1b_ref[...], w3b_ref[...], w2b_ref[...]))

    @pl.when(jnp.logical_not(used))
    def _():
        y_ref[...] = jnp.zeros_like(y_ref)


def _experts(xs, blk_exp, n_used, w1, w3, w2, layer):
    rows, half = xs.shape
    _, E, D, F = w1.shape
    nblk = rows // MOE_BM
    wmap = lambda i, be, nu: (layer, be[i], 0, 0)
    grid_spec = pltpu.PrefetchScalarGridSpec(
        num_scalar_prefetch=2, grid=(nblk,),
        in_specs=[pl.BlockSpec((MOE_BM, half), lambda i, be, nu: (i, 0)),
                  pl.BlockSpec((1, 1, D, F), wmap), pl.BlockSpec((1, 1, D, F), wmap), pl.BlockSpec((1, 1, F, D), wmap)],
        out_specs=pl.BlockSpec((MOE_BM, half), lambda i, be, nu: (i, 0)),
        scratch_shapes=[pltpu.VMEM((D, F), BF16), pltpu.VMEM((D, F), BF16), pltpu.VMEM((F, D), BF16)])
    return pl.pallas_call(
        _expert_kernel, grid_spec=grid_spec,
        out_shape=jax.ShapeDtypeStruct((rows, half), U32),
        compiler_params=_cparams(("arbitrary",)), name="expert_ffn",
    )(blk_exp, n_used, xs, w1, w3, w2)


def _combine_kernel(yg_ref, gate_ref, hp_ref, sw1_ref, sw3_ref, sw2_ref, x_ref, g2_ref, lng_ref, lnb_ref,
                    o_ref, *, alpha):
    f = _swiglu(_unpack_bf16_pairs(hp_ref[...]), sw1_ref[...], sw3_ref[...], sw2_ref[...])
    gates = gate_ref[...]
    lo = hi = 0.0
    for k in range(TOP_K):
        u = yg_ref[k]
        g = gates[:, k:k + 1]
        lo = lo + g * lax.bitcast_convert_type(u << 16, F32)
        hi = hi + g * lax.bitcast_convert_type(u & jnp.uint32(0xFFFF0000), F32)
    f = f + jnp.concatenate([lo, hi], -1)
    o_ref[...] = _layer_norm(alpha * x_ref[...] + g2_ref[0] * f) * lng_ref[...] + lnb_ref[...]


def _combine(yg, gates, hp, sw1, sw3, sw2, x1, mods_l, lng, lnb, *, n_lat, T, alpha, n_out, tok0):
    D = x1.shape[1]
    N = n_out
    tm = CMB_TILE
    nlat_t, per_b, nb = n_lat // tm, T // tm, n_lat // T
    t0 = tok0 // tm
    F = sw1.shape[1]
    const = lambda i: (0, 0)
    row = lambda i: (i, 0)
    kern = functools.partial(_combine_kernel, alpha=alpha)
    return pl.pallas_call(
        kern, grid=(N // tm,),
        in_specs=[pl.BlockSpec((TOP_K, tm, D // 2), lambda i: (0, i, 0)), pl.BlockSpec((tm, TOP_K), row),
                  pl.BlockSpec((tm, D // 2), row),
                  pl.BlockSpec((D, F), const), pl.BlockSpec((D, F), const), pl.BlockSpec((F, D), const),
                  pl.BlockSpec((tm, D), row),
                  pl.BlockSpec((1, 1, D), lambda i: (jnp.where(i + t0 < nlat_t, (i + t0) // per_b, nb) * 6 + 5, 0, 0)),
                  pl.BlockSpec((1, D), const), pl.BlockSpec((1, D), const)],
        out_specs=pl.BlockSpec((tm, D), row),
        out_shape=jax.ShapeDtypeStruct((N, D), F32),
        compiler_params=_cparams(("arbitrary",)), name="moe_combine",
    )(yg, gates, hp, sw1, sw3, sw2, x1, mods_l, lng, lnb)


def _dest_kernel(start_ref, sel_ref, rank_ref, o_ref):
    sel = sel_ref[...]

    def body(e, acc):
        return jnp.where(sel == e, start_ref[e], acc)

    o_ref[...] = lax.fori_loop(0, start_ref.shape[0], body, jnp.zeros_like(sel)) + rank_ref[...]


def _dispatch_plan(sel, rank, counts):
    N = sel.shape[0]
    n_experts = counts.shape[0]
    padded = (counts + MOE_BM - 1) // MOE_BM * MOE_BM
    pad_end = jnp.cumsum(padded)
    pad_start = (pad_end - padded).astype(I32)
    dense = (N * TOP_K // 128, 128)
    dest = pl.pallas_call(
        _dest_kernel,
        in_specs=[pl.BlockSpec(memory_space=pltpu.SMEM), pl.BlockSpec(memory_space=pltpu.VMEM),
                  pl.BlockSpec(memory_space=pltpu.VMEM)],
        out_specs=pl.BlockSpec(memory_space=pltpu.VMEM),
        out_shape=jax.ShapeDtypeStruct(dense, I32), name="moe_dest_rows",
    )(pad_start, sel.reshape(dense), rank.reshape(dense))
    dest_k = dest.reshape(N, TOP_K).T
    n_blocks = -(-(N * TOP_K) // MOE_BM) + n_experts
    blk_first = jnp.arange(n_blocks, dtype=I32) * MOE_BM
    blk_exp = jnp.minimum(jnp.sum((pad_end[None, :] <= blk_first[:, None]).astype(I32), -1), n_experts - 1)
    n_used = (pad_end[-1] // MOE_BM).astype(I32).reshape(1)
    return dest_k, blk_exp, n_used, n_blocks * MOE_BM


def _rope_tables(T, tile):
    t = jnp.arange(T)
    row = (t // GRID_W).astype(F32)[:, None]
    col = (t % GRID_W).astype(F32)[:, None]

    def group(half):
        inv = ROPE_THETA ** (-jnp.arange(half, dtype=F32) / half)
        ar, ac = row * inv, col * inv
        cos = jnp.concatenate([jnp.cos(ar), jnp.cos(ar), jnp.cos(ac), jnp.cos(ac)], -1)
        sin = jnp.concatenate([-jnp.sin(ar), jnp.sin(ar), -jnp.sin(ac), jnp.sin(ac)], -1)
        return cos, sin

    ones32, zeros32 = jnp.ones((T, 32), F32), jnp.zeros((T, 32), F32)
    cw, sw = group(16)
    cw, sw = jnp.tile(cw, (1, 2)), jnp.tile(sw, (1, 2))
    c8, s8 = group(8)
    cq = jnp.tile(jnp.concatenate([ones32, c8], -1), (1, 2))
    sq = jnp.tile(jnp.concatenate([zeros32, s8], -1), (1, 2))
    ck = jnp.concatenate([c8, ones32, ones32, ones32], -1)
    sk = jnp.concatenate([s8, zeros32, zeros32, zeros32], -1)

    def pad(a, ident):
        return jnp.concatenate([a, jnp.full((tile, 128), ident, F32)], 0)

    return [pad(cw, 1.0), pad(sw, 0.0), pad(cq, 1.0), pad(sq, 0.0), pad(ck, 1.0), pad(sk, 0.0)]


def _layer_weights(w_in, mla_w_uq, mla_w_ukv, gla_w_gf, gla_w_gb, gla_b_gf, gla_b_gb):
    D = w_in.shape[0]
    z = lambda n: jnp.zeros((D, n), F32)
    win = jnp.concatenate([
        w_in[:, 0:256] * Q_SCALE, w_in[:, 256:768],
        w_in[:, 768:1024] * Q_SCALE, w_in[:, 1024:1280],
        w_in[:, 1280:1696], z(96),
        w_in[:, 1696:2464],
        w_in[:, 2464:2496], z(96)], -1).astype(BF16)
    wuq = (mla_w_uq * Q_SCALE).astype(BF16)
    ukv = mla_w_ukv.reshape(-1, N_HEADS, MLA_NOPE + HEAD_DIM)
    kv_rank = ukv.shape[0]
    wk = jnp.concatenate([ukv[:, :, :MLA_NOPE], jnp.zeros((kv_rank, N_HEADS, MLA_ROPE), F32)], -1)
    place = jnp.concatenate([jnp.zeros((MLA_ROPE, MLA_NOPE), F32), jnp.eye(MLA_ROPE, dtype=F32)], -1)
    place = jnp.concatenate([jnp.tile(place, (1, N_HEADS)), jnp.zeros((128 - MLA_ROPE, 256), F32)], 0)
    wkk = jnp.concatenate([wk.reshape(kv_rank, 256), place], 0).astype(BF16)
    wv = ukv[:, :, MLA_NOPE:].reshape(kv_rank, 256).astype(BF16)
    r = GLA_GATE_RANK
    wg = jnp.zeros((128, 256), F32).at[0:r, 0:128].set(gla_w_gf).at[r:2 * r, 128:256].set(gla_w_gb).astype(BF16)
    bg = jnp.concatenate([gla_b_gf, gla_b_gb])[None, :]
    return win, wuq, wkk, wv, wg, bg


def kernel(x, c, ctx, c_ctx, w_ada, b_ada, w_in, na_rpb, wa_sink, mla_g_q, mla_g_kv, mla_w_uq, mla_w_ukv,
           gla_w_gf, gla_b_gf, gla_w_gb, gla_b_gb, gla_g_norm, w_out, ln1_g, ln1_b, ln2_g, ln2_b,
           router_w, router_bias, exp_w1, exp_w3, exp_w2, sh_w1, sh_w3, sh_w2):
    B, T, D = x.shape
    C = ctx.shape[1]
    L = w_ada.shape[0]
    E = router_w.shape[-1]
    n_lat = B * T
    alpha = (2 * L) ** 0.25
    dims = dict(B=B, T=T, C=C)

    cc = jnp.zeros((8, D), F32).at[:B].set(c).at[B].set(c_ctx)
    mods = _mods(cc, w_ada, b_ada)
    tabs = _rope_tables(T, TOK_TILE)
    xall = jnp.concatenate([x.reshape(n_lat, D), ctx.reshape(B * C, D)], 0)

    for l in range(L):
        mods_l = mods[l].reshape(8 * 6, 1, D)
        win, wuq, wkk, wv, wg, bg = _layer_weights(w_in[l], mla_w_uq[l], mla_w_ukv[l], gla_w_gf[l], gla_w_gb[l],
                                                   gla_b_gf[l], gla_b_gb[l])
        pa, pb, pc, pd = _inproj(xall, mods_l, win, tabs, mla_g_q[l][None], mla_g_kv[l][None], wuq, wkk, wv, wg, bg,
                                 n_lat=n_lat, T=T)
        oa = _na_attention(pa, _na_bias_tables(na_rpb[l], T // GRID_W), **dims)
        sink = wa_sink[l] * LOG2E
        ob = _wa_attention(pb, sink, **dims)
        oc = _mla_attention(pc, **dims)
        o_ctx = _ctx_attention(sink, pa, pb, pc, **dims)
        ofb = _gla(pd, **dims)

        rw1 = router_w[l].astype(BF16)
        rw2 = (router_w[l] - rw1.astype(F32)).astype(BF16)
        n_all = xall.shape[0]
        part = n_all // MOE_SPLITS
        assert n_all % (MOE_SPLITS * TOK_TILE) == 0
        shared = (sh_w1[l].astype(BF16), sh_w3[l].astype(BF16), sh_w2[l].astype(BF16))
        gn, wout = jnp.tile(gla_g_norm[l], N_HEADS)[None], w_out[l].astype(BF16)
        routed = []
        for p in range(MOE_SPLITS):
            x1, hp, sel, gates, rank, counts = _outproj(
                xall, (oa, ob, oc), o_ctx, ofb, pd, gn, wout, mods_l, ln1_g[l][None], ln1_b[l][None], rw1, rw2,
                router_bias[l][None], n_lat=n_lat, T=T, alpha=alpha,
                tile0=p * part // TOK_TILE, ntiles=part // TOK_TILE)
            dest_k, blk_exp, n_used, rows = _dispatch_plan(sel, rank, counts.reshape(E))
            routed.append((x1, hp, gates, dest_k, blk_exp, n_used, _sc_dispatch(hp, dest_k, rows)))
        outs = []
        for p, (x1, hp, gates, dest_k, blk_exp, n_used, xs) in enumerate(routed):
            y = _experts(xs, blk_exp, n_used, exp_w1, exp_w3, exp_w2, l)
            yg = _sc_gather(y, dest_k.reshape(-1)).reshape(TOP_K, part, D // 2)
            n_out = part if l < L - 1 else max(min(n_lat - p * part, part), 0)
            if n_out:
                outs.append(_combine(yg, gates, hp, *shared, x1, mods_l, ln2_g[l][None], ln2_b[l][None], n_lat=n_lat,
                                     T=T, alpha=alpha, n_out=n_out, tok0=p * part))
        xall = jnp.concatenate(outs, 0)

    return xall.reshape(B, T, D)
```

```python
import functools

import numpy as np
import jax
import jax.numpy as jnp
from jax import lax
from jax.experimental import pallas as pl
from jax.experimental.pallas import tpu as pltpu
from jax.experimental.pallas import tpu_sc as plsc

F32 = jnp.float32
BF16 = jnp.bfloat16
U32 = jnp.uint32
I32 = jnp.int32

GRID_W = 64
HEAD_DIM = 64
N_HEADS = 4
NA_KH, NA_KW = 8, 16
WA_WINDOW = 128
WA_SUB = 256
MLA_NOPE, MLA_ROPE = 32, 32
GLA_DK, GLA_DV = 32, 64
GLA_GATE_RANK = 16
GLA_GATE_NORM = 16.0
GLA_CHUNK = 64
TOP_K = 8
ROUTED_SCALE = 2.5
ROPE_THETA = 10000.0
LN_EPS = 1e-5
RMS_EPS = 1e-6
NEG = -1e30
LOG2E = 1.4426950408889634
Q_SCALE = HEAD_DIM ** -0.5 * LOG2E

VMEM_LIMIT = 56 * 1024 * 1024
TOK_TILE = 512
ATT_TQ = 512
MLA_TQ = 1024
MLA_TK = 1024
GLA_BLOCK = 256
MOE_BM = 1024
MOE_RING = 3
MOE_SPLITS = 1
CMB_TILE = 256
SC_CORES = 2
SC_WORKERS = 32
SC_DISPATCH_CHUNK = 48
SC_GATHER_CHUNK = 64

PW = 2688
_NT = (((1,), (1,)), ((), ()))


def _cparams(sem, vmem=VMEM_LIMIT):
    return pltpu.CompilerParams(dimension_semantics=sem, vmem_limit_bytes=vmem)


def _sigmoid(x):
    return 1.0 / (1.0 + jnp.exp(-x))


def _layer_norm(x):
    mu = jnp.mean(x, -1, keepdims=True)
    xc = x - mu
    var = jnp.mean(xc * xc, -1, keepdims=True)
    return xc * lax.rsqrt(var + LN_EPS)


def _rms(x):
    return x * lax.rsqrt(jnp.mean(x * x, -1, keepdims=True) + RMS_EPS)


def _bdot(a, b):
    return jnp.dot(a.astype(BF16), b.astype(BF16), preferred_element_type=F32)


def _bdot_nt(a, b):
    return lax.dot_general(a.astype(BF16), b.astype(BF16), _NT, preferred_element_type=F32)


def _split3(a):
    a1 = a.astype(BF16)
    r = a - a1.astype(F32)
    a2 = r.astype(BF16)
    a3 = (r - a2.astype(F32)).astype(BF16)
    return a1, a2, a3


def _mods_kernel(cc_ref, w_ref, b_ref, o_ref):
    cc = cc_ref[...]
    s = cc * _sigmoid(cc)
    o_ref[0] = _bdot(s, w_ref[0]) + b_ref[0]


def _mods(cc, w_ada, b_ada):
    L, D, W = w_ada.shape
    tn = 1536
    return pl.pallas_call(
        _mods_kernel,
        grid=(L, W // tn),
        in_specs=[pl.BlockSpec((8, D), lambda l, j: (0, 0)),
                  pl.BlockSpec((1, D, tn), lambda l, j: (l, 0, j)),
                  pl.BlockSpec((1, 1, tn), lambda l, j: (l, 0, j))],
        out_specs=pl.BlockSpec((1, 8, tn), lambda l, j: (l, 0, j)),
        out_shape=jax.ShapeDtypeStruct((L, 8, W), F32),
        compiler_params=_cparams(("arbitrary", "arbitrary")),
        name="adaln_mods",
    )(cc, w_ada, b_ada.reshape(L, 1, W))


def _rope(z, cos, sin, half):
    w = z.shape[-1]
    lane = lax.broadcasted_iota(I32, z.shape, 1)
    first = (lane % (2 * half)) < half
    partner = jnp.where(first, pltpu.roll(z, w - half, 1), pltpu.roll(z, half, 1))
    return z * cos + partner * sin


def _log_sigmoid(x):
    return jnp.minimum(x, 0.0) - jnp.log(1.0 + jnp.exp(-jnp.abs(x)))


def _inproj_kernel(x_ref, shift_ref, scale_ref, win_ref, cw_ref, sw_ref, cq_ref, sq_ref, ck_ref, sk_ref,
                   gq_ref, gkv_ref, wuq_ref, wkk_ref, wv_ref, wg_ref, bg_ref,
                   pa_ref, pb_ref, pc_ref, pd_ref):
    h = _layer_norm(x_ref[...]) * (1.0 + scale_ref[0]) + shift_ref[0]
    p = jnp.dot(h.astype(BF16), win_ref[...], preferred_element_type=F32)
    pa_ref[...] = p[:, 0:768].astype(BF16)
    cw, sw = cw_ref[...], sw_ref[...]
    pb_ref[:, 0:128] = _rope(p[:, 768:896], cw, sw, 16).astype(BF16)
    pb_ref[:, 128:256] = _rope(p[:, 896:1024], cw, sw, 16).astype(BF16)
    pb_ref[:, 256:384] = _rope(p[:, 1024:1152], cw, sw, 16).astype(BF16)
    pb_ref[:, 384:512] = p[:, 1152:1280].astype(BF16)
    cqn = _rms(p[:, 1280:1536]) * gq_ref[...]
    q = jnp.dot(cqn.astype(BF16), wuq_ref[...], preferred_element_type=F32)
    cq, sq = cq_ref[...], sq_ref[...]
    pc_ref[:, 0:128] = _rope(q[:, 0:128], cq, sq, 8).astype(BF16)
    pc_ref[:, 128:256] = _rope(q[:, 128:256], cq, sq, 8).astype(BF16)
    ckvn = (_rms(p[:, 1536:1664]) * gkv_ref[...]).astype(BF16)
    kr = _rope(p[:, 1664:1792], ck_ref[...], sk_ref[...], 8).astype(BF16)
    kin = jnp.concatenate([ckvn, kr], axis=-1)
    pc_ref[:, 256:512] = jnp.dot(kin, wkk_ref[...], preferred_element_type=F32).astype(BF16)
    pc_ref[:, 512:768] = jnp.dot(ckvn, wv_ref[...], preferred_element_type=F32).astype(BF16)
    pd_ref[:, 0:128] = p[:, 1792:1920] * (GLA_DK ** -0.5)
    pd_ref[:, 128:768] = p[:, 1920:2560]
    pre = jnp.dot(p[:, 2560:2688].astype(BF16), wg_ref[...], preferred_element_type=F32) + bg_ref[...]
    pd_ref[:, 768:1024] = _log_sigmoid(pre) * (1.0 / GLA_GATE_NORM)


def _inproj(xall, mods_l, win, tabs, gq, gkv, wuq, wkk, wv, wg, bg, *, n_lat, T):
    N, D = xall.shape
    tm = TOK_TILE
    nlat_t = n_lat // tm
    per_b = T // tm
    nb = n_lat // T

    def midx(k):
        return lambda i: (jnp.where(i < nlat_t, i // per_b, nb) * 6 + k, 0, 0)

    def tidx(i):
        return (jnp.where(i < nlat_t, i % per_b, per_b), 0)

    const = lambda i: (0, 0)
    tab_spec = pl.BlockSpec((tm, 128), tidx)
    in_specs = [pl.BlockSpec((tm, D), lambda i: (i, 0)),
                pl.BlockSpec((1, 1, D), midx(0)), pl.BlockSpec((1, 1, D), midx(1)),
                pl.BlockSpec((D, PW), const)] + [tab_spec] * 6 + [
                pl.BlockSpec((1, 256), const), pl.BlockSpec((1, 128), const),
                pl.BlockSpec((256, 256), const), pl.BlockSpec((256, 256), const),
                pl.BlockSpec((128, 256), const), pl.BlockSpec((128, 256), const),
                pl.BlockSpec((1, 256), const)]
    out_specs = [pl.BlockSpec((tm, 768), lambda i: (i, 0)), pl.BlockSpec((tm, 512), lambda i: (i, 0)),
                 pl.BlockSpec((tm, 768), lambda i: (i, 0)), pl.BlockSpec((tm, 1024), lambda i: (i, 0))]
    out_shape = [jax.ShapeDtypeStruct((N, 768), BF16), jax.ShapeDtypeStruct((N, 512), BF16),
                 jax.ShapeDtypeStruct((N, 768), BF16), jax.ShapeDtypeStruct((N, 1024), F32)]
    return pl.pallas_call(
        _inproj_kernel, grid=(N // tm,), in_specs=in_specs, out_specs=out_specs, out_shape=out_shape,
        compiler_params=_cparams(("arbitrary",)), name="inproj",
    )(xall, mods_l, mods_l, win, *tabs, gq, gkv, wuq, wkk, wv, wg, bg)


def _softmax_av(parts, extra_logit=None):
    m = functools.reduce(jnp.maximum, [jnp.max(s, -1, keepdims=True) for s, _ in parts])
    if extra_logit is not None:
        m = jnp.maximum(m, extra_logit)
    l = 0.0 if extra_logit is None else jnp.exp2(extra_logit - m)
    o = None
    for s, v in parts:
        e = jnp.exp2(s - m)
        l = l + jnp.sum(e, -1, keepdims=True)
        c = jnp.dot(e.astype(BF16), v, preferred_element_type=F32)
        o = c if o is None else o + c
    return o * (1.0 / l)


def _na_kernel(q_ref, k_ref, v_ref, kc_ref, vc_ref, bias_ref, o_ref, *, rows):
    j = pl.program_id(1)
    ws = pl.multiple_of(jnp.clip(8 * j - 4, 0, rows - 16) * GRID_W, 256)
    win = 16 * GRID_W
    outs = []
    for h in range(N_HEADS):
        sl = slice(HEAD_DIM * h, HEAD_DIM * (h + 1))
        q = q_ref[:, sl]
        s = lax.dot_general(q, k_ref[pl.ds(ws, win), sl], _NT, preferred_element_type=F32) + bias_ref[0, h]
        sc = lax.dot_general(q, kc_ref[:, sl], _NT, preferred_element_type=F32)
        outs.append(_softmax_av([(s, v_ref[pl.ds(ws, win), sl]), (sc, vc_ref[:, sl])]))
    o_ref[...] = jnp.concatenate(outs, -1).astype(BF16)


def _na_bias_tables(rpb, rows):
    nj = rows // 8
    ro_all, ok_all = [], []
    for j in (0, 1, nj - 1):
        ws = int(np.clip(8 * j - 4, 0, rows - 16))
        r = 8 * j + np.arange(8)[:, None, None, None]
        qc = np.arange(GRID_W)[None, :, None, None]
        kr = ws + np.arange(16)[None, None, :, None]
        kc = np.arange(GRID_W)[None, None, None, :]
        rs = np.clip(r - NA_KH // 2, 0, rows - NA_KH)
        cs = np.clip(qc - NA_KW // 2, 0, GRID_W - NA_KW)
        ok = (kr >= rs) & (kr < rs + NA_KH) & (kc >= cs) & (kc < cs + NA_KW)
        ro = np.clip(kr - r + (NA_KH - 1), 0, 2 * NA_KH - 2)
        shp = (8, GRID_W, 16, GRID_W)
        ro_all.append(np.broadcast_to(ro, shp).reshape(512, 1024))
        ok_all.append(np.broadcast_to(ok, shp).reshape(512, 1024))
    ok = np.stack(ok_all)
    hi = lax.Precision.HIGHEST
    co = np.clip(np.arange(GRID_W)[None, :] - np.arange(GRID_W)[:, None] + (NA_KW - 1), 0, 2 * NA_KW - 2)
    col_hot = (co[..., None] == np.arange(2 * NA_KW - 1)).astype(np.float32)
    cb = jnp.einsum('hdo,qko->hdqk', rpb, col_hot, precision=hi)
    ro = np.stack(ro_all).reshape(3, 8, GRID_W, 16, GRID_W)[:, :, 0, :, 0]
    row_hot = (ro[..., None] == np.arange(2 * NA_KH - 1)).astype(np.float32)
    b = jnp.einsum('vrsd,hdqk->vhrqsk', row_hot, cb, precision=hi)
    b = b.reshape(3, rpb.shape[0], 512, 1024)
    return jnp.where(ok[:, None], b * LOG2E, NEG)


def _na_attention(pa, bias, *, B, T, C):
    rows = T // GRID_W
    nj = T // ATT_TQ
    cb = B * T // C
    kern = functools.partial(_na_kernel, rows=rows)
    return pl.pallas_call(
        kern, grid=(B, nj),
        in_specs=[pl.BlockSpec((ATT_TQ, 256), lambda b, j: (b * nj + j, 0)),
                  pl.BlockSpec((T, 256), lambda b, j: (b, 1)),
                  pl.BlockSpec((T, 256), lambda b, j: (b, 2)),
                  pl.BlockSpec((C, 256), lambda b, j: (cb + b, 1)),
                  pl.BlockSpec((C, 256), lambda b, j: (cb + b, 2)),
                  pl.BlockSpec((1, N_HEADS, 512, 1024),
                               lambda b, j: (jnp.where(j == 0, 0, jnp.where(j == nj - 1, 2, 1)), 0, 0, 0))],
        out_specs=pl.BlockSpec((ATT_TQ, 256), lambda b, j: (b * nj + j, 0)),
        out_shape=jax.ShapeDtypeStruct((B * T, 256), BF16),
        compiler_params=_cparams(("arbitrary", "arbitrary")), name="na_attention",
    )(pa, pa, pa, pa, pa, bias)


def _wa_kernel(sink_ref, q_ref, k_ref, v_ref, kc_ref, vc_ref, o_ref, *, T):
    i = pl.program_id(1)
    win = WA_SUB + 2 * WA_WINDOW
    for sub in range(ATT_TQ // WA_SUB):
        rows = slice(sub * WA_SUB, (sub + 1) * WA_SUB)
        start = i * ATT_TQ + sub * WA_SUB
        ws = pl.multiple_of(jnp.clip(start - WA_WINDOW, 0, T - win), 128)
        rel = ((lax.broadcasted_iota(I32, (WA_SUB, win), 1) + (ws - start))
               - lax.broadcasted_iota(I32, (WA_SUB, win), 0))
        valid = jnp.abs(rel) <= WA_WINDOW
        outs = []
        for h in range(N_HEADS):
            g = h // 2
            ksl = slice(HEAD_DIM * g, HEAD_DIM * (g + 1))
            q = q_ref[rows, HEAD_DIM * h:HEAD_DIM * (h + 1)]
            s = lax.dot_general(q, k_ref[pl.ds(ws, win), ksl], _NT, preferred_element_type=F32)
            s = jnp.where(valid, s, NEG)
            sc = lax.dot_general(q, kc_ref[:, ksl], _NT, preferred_element_type=F32)
            outs.append(_softmax_av([(s, v_ref[pl.ds(ws, win), ksl]), (sc, vc_ref[:, ksl])],
                                    extra_logit=sink_ref[h]))
        o_ref[rows, :] = jnp.concatenate(outs, -1).astype(BF16)


def _wa_attention(pb, sink, *, B, T, C):
    nj = T // ATT_TQ
    cb = B * T // C
    kern = functools.partial(_wa_kernel, T=T)
    return pl.pallas_call(
        kern, grid=(B, nj),
        in_specs=[pl.BlockSpec(memory_space=pltpu.SMEM),
                  pl.BlockSpec((ATT_TQ, 256), lambda b, j: (b * nj + j, 0)),
                  pl.BlockSpec((T, 128), lambda b, j: (b, 2)),
                  pl.BlockSpec((T, 128), lambda b, j: (b, 3)),
                  pl.BlockSpec((C, 128), lambda b, j: (cb + b, 2)),
                  pl.BlockSpec((C, 128), lambda b, j: (cb + b, 3))],
        out_specs=pl.BlockSpec((ATT_TQ, 256), lambda b, j: (b * nj + j, 0)),
        out_shape=jax.ShapeDtypeStruct((B * T, 256), BF16),
        compiler_params=_cparams(("arbitrary", "arbitrary")), name="wa_attention",
    )(sink, pb, pb, pb, pb, pb)


def _mla_kernel(q_ref, k_ref, v_ref, kc_ref, vc_ref, o_ref, *, T):
    nk = T // MLA_TK
    heads = [slice(HEAD_DIM * h, HEAD_DIM * (h + 1)) for h in range(N_HEADS)]
    qs = [q_ref[:, sl] for sl in heads]
    init = []
    for q, sl in zip(qs, heads):
        sc = lax.dot_general(q, kc_ref[:, sl], _NT, preferred_element_type=F32)
        m0 = jnp.max(sc, -1, keepdims=True)
        e0 = jnp.exp2(sc - m0)
        init += [m0, jnp.sum(e0, -1, keepdims=True),
                 jnp.dot(e0.astype(BF16), vc_ref[:, sl], preferred_element_type=F32)]

    def body(c, carry):
        ks = pl.multiple_of(c * MLA_TK, MLA_TK)
        out = []
        for h, (q, sl) in enumerate(zip(qs, heads)):
            m, l, acc = carry[3 * h:3 * h + 3]
            s = lax.dot_general(q, k_ref[pl.ds(ks, MLA_TK), sl], _NT, preferred_element_type=F32)
            mn = jnp.maximum(m, jnp.max(s, -1, keepdims=True))
            a = jnp.exp2(m - mn)
            e = jnp.exp2(s - mn)
            l = l * a + jnp.sum(e, -1, keepdims=True)
            acc = acc * a + jnp.dot(e.astype(BF16), v_ref[pl.ds(ks, MLA_TK), sl], preferred_element_type=F32)
            out += [mn, l, acc]
        return tuple(out)

    fin = lax.fori_loop(0, nk, body, tuple(init), unroll=2)
    outs = [fin[3 * h + 2] * (1.0 / fin[3 * h + 1]) for h in range(N_HEADS)]
    o_ref[...] = jnp.concatenate(outs, -1).astype(BF16)


def _mla_attention(pc, *, B, T, C):
    nj = T // MLA_TQ
    cb = B * T // C
    kern = functools.partial(_mla_kernel, T=T)
    return pl.pallas_call(
        kern, grid=(B, nj),
        in_specs=[pl.BlockSpec((MLA_TQ, 256), lambda b, j: (b * nj + j, 0)),
                  pl.BlockSpec((T, 256), lambda b, j: (b, 1)),
                  pl.BlockSpec((T, 256), lambda b, j: (b, 2)),
                  pl.BlockSpec((C, 256), lambda b, j: (cb + b, 1)),
                  pl.BlockSpec((C, 256), lambda b, j: (cb + b, 2))],
        out_specs=pl.BlockSpec((MLA_TQ, 256), lambda b, j: (b * nj + j, 0)),
        out_shape=jax.ShapeDtypeStruct((B * T, 256), BF16),
        compiler_params=_cparams(("arbitrary", "arbitrary")), name="mla_attention",
    )(pc, pc, pc, pc, pc)


def _ctx_kernel(sink_ref, pa_ref, pb_ref, pc_ref, oa_ref, ob_ref, oc_ref):
    def attend(p_ref, koff, voff, kv_heads, out_ref, sink):
        outs = []
        for h in range(N_HEADS):
            g = h * kv_heads // N_HEADS
            q = p_ref[:, HEAD_DIM * h:HEAD_DIM * (h + 1)]
            k = p_ref[:, koff + HEAD_DIM * g:koff + HEAD_DIM * (g + 1)]
            v = p_ref[:, voff + HEAD_DIM * g:voff + HEAD_DIM * (g + 1)]
            s = lax.dot_general(q, k, _NT, preferred_element_type=F32)
            outs.append(_softmax_av([(s, v)], extra_logit=sink_ref[h] if sink else None))
        out_ref[...] = jnp.concatenate(outs, -1).astype(BF16)

    attend(pa_ref, 256, 512, 4, oa_ref, False)
    attend(pb_ref, 256, 384, 2, ob_ref, True)
    attend(pc_ref, 256, 512, 4, oc_ref, False)


def _ctx_attention(sink, pa, pb, pc, *, B, T, C):
    cb = B * T // C
    row = lambda b: (cb + b, 0)
    return pl.pallas_call(
        _ctx_kernel, grid=(B,),
        in_specs=[pl.BlockSpec(memory_space=pltpu.SMEM),
                  pl.BlockSpec((C, 768), row), pl.BlockSpec((C, 512), row), pl.BlockSpec((C, 768), row)],
        out_specs=[pl.BlockSpec((C, 256), lambda b: (b, 0))] * 3,
        out_shape=[jax.ShapeDtypeStruct((B * C, 256), BF16)] * 3,
        compiler_params=_cparams(("arbitrary",)), name="ctx_attention",
    )(sink, pa, pb, pc)


def _gla_dir(pd_ref, o_ref, st_ref, reverse):
    nchunk = GLA_BLOCK // GLA_CHUNK
    L, DK, DV = GLA_CHUNK, N_HEADS * GLA_DK, N_HEADS * GLA_DV

    def iota(shape, dim):
        return lax.broadcasted_iota(I32, shape, dim)

    ri, ci = iota((L, L), 0), iota((L, L), 1)
    tri = jnp.where((ci >= ri) if reverse else (ci <= ri), 1.0, 0.0).astype(BF16)
    tri_t = jnp.where((ri >= ci) if reverse else (ri <= ci), 1.0, 0.0).astype(BF16)
    kcol, qrow = iota((L, N_HEADS * L), 1) % L, iota((L, N_HEADS * L), 0)
    keep = (kcol >= qrow) if reverse else (kcol <= qrow)
    k_owner = [iota((L, DK), 1) // GLA_DK == h for h in range(N_HEADS)]
    v_owner = [iota((L, DV), 1) // GLA_DV == h for h in range(N_HEADS)]
    s_owner = iota((DK, DV), 0) // GLA_DK == iota((DK, DV), 1) // GLA_DV
    gcol = 896 if reverse else 768
    end = 0 if reverse else L - 1
    order = range(nchunk - 1, -1, -1) if reverse else range(nchunk)

    def dot3(a_parts, b_parts):
        return sum(jnp.dot(a, b, preferred_element_type=F32) for a in a_parts for b in b_parts)

    per_chunk = []
    for c in order:
        rs = slice(c * L, (c + 1) * L)
        g = pd_ref[rs, gcol:gcol + DK]
        q = pd_ref[rs, 0:DK]
        k = pd_ref[rs, DK:2 * DK]
        vb = pd_ref[rs, 2 * DK:2 * DK + DV].astype(BF16)
        b = dot3([tri], _split3(g))
        bt = dot3(_split3(g.T), [tri_t])
        bend = bt[:, end:end + 1]
        qd = (q * jnp.exp(b)).astype(BF16)
        ki = (k * jnp.exp(-b)).astype(BF16)
        kdt = (k.T * jnp.exp(bend - bt)).astype(BF16)
        kbd = jnp.concatenate([jnp.where(m, ki, jnp.zeros_like(ki)) for m in k_owner], 0)
        vbd = jnp.concatenate([jnp.where(m, vb, jnp.zeros_like(vb)) for m in v_owner], 0)
        att = lax.dot_general(qd, kbd, _NT, preferred_element_type=F32)
        att = jnp.where(keep, att, 0.0).astype(BF16)
        intra = jnp.dot(att, vbd, preferred_element_type=F32)
        u = jnp.where(s_owner, jnp.dot(kdt, vb, preferred_element_type=F32), 0.0)
        per_chunk.append((rs, qd, intra, u, jnp.exp(bend)))
    st = st_ref[...]
    for rs, qd, intra, u, dec in per_chunk:
        o_ref[rs, :] = intra + jnp.dot(qd, st.astype(BF16), preferred_element_type=F32)
        st = st * dec + u
    st_ref[...] = st


def _gla_kernel(pdf_ref, pdb_ref, of_ref, ob_ref, stf_ref, stb_ref):
    @pl.when(pl.program_id(1) == 0)
    def _():
        stf_ref[...] = jnp.zeros_like(stf_ref)
        stb_ref[...] = jnp.zeros_like(stb_ref)

    _gla_dir(pdf_ref, of_ref, stf_ref, False)
    _gla_dir(pdb_ref, ob_ref, stb_ref, True)


def _gla(pd, *, B, T, C):
    N = pd.shape[0]
    assert C == GLA_BLOCK
    nlb = T // GLA_BLOCK
    cb = B * T // GLA_BLOCK

    def fwd(b, s):
        return (jnp.where(s == 0, cb + b, b * nlb + s - 1), 0)

    def bwd(b, s):
        return (jnp.where(s == 0, cb + b, b * nlb + nlb - s), 0)

    state = pltpu.VMEM((N_HEADS * GLA_DK, N_HEADS * GLA_DV), F32)
    return pl.pallas_call(
        _gla_kernel, grid=(B, nlb + 1),
        in_specs=[pl.BlockSpec((GLA_BLOCK, 1024), fwd), pl.BlockSpec((GLA_BLOCK, 1024), bwd)],
        out_specs=[pl.BlockSpec((GLA_BLOCK, 256), fwd), pl.BlockSpec((GLA_BLOCK, 256), bwd)],
        out_shape=[jax.ShapeDtypeStruct((N, 256), F32)] * 2,
        scratch_shapes=[state, state],
        compiler_params=_cparams(("arbitrary", "arbitrary")), name="gla_scan",
    )(pd, pd)


def _pack_bf16_pairs(h):
    w = h.shape[-1] // 2
    bits = lax.bitcast_convert_type(h.astype(BF16).astype(F32), U32)
    return (bits[:, w:] & jnp.uint32(0xFFFF0000)) | (bits[:, :w] >> 16)


def _unpack_bf16_pairs(u):
    lo = lax.bitcast_convert_type(u << 16, F32)
    hi = lax.bitcast_convert_type(u & jnp.uint32(0xFFFF0000), F32)
    return jnp.concatenate([lo, hi], -1).astype(BF16)


def _outproj_kernel(x_ref, oa_ref, ob_ref, oc_ref, oac_ref, obc_ref, occ_ref, gf_ref, gb_ref, r_ref, gn_ref, wout_ref,
                    g1_ref, sh2_ref, sc2_ref, lng_ref, lnb_ref, rw1_ref, rw2_ref, rb_ref,
                    x1_ref, hp_ref, sel_ref, gate_ref, rank_ref, count_ref, cnt_ref, *, alpha, nlat_t, tile0):
    is_lat = pl.program_id(0) + tile0 < nlat_t
    pick = lambda lat_ref, ctx_ref: jnp.where(is_lat, lat_ref[...], ctx_ref[...])
    s = gf_ref[...] + gb_ref[...]
    w = s.shape[-1]
    same_head = (lax.broadcasted_iota(I32, (w, w), 0) // GLA_DV) == (lax.broadcasted_iota(I32, (w, w), 1) // GLA_DV)
    avg = jnp.where(same_head, 1.0 / GLA_DV, 0.0).astype(BF16)
    ms = sum(jnp.dot(part, avg, preferred_element_type=F32) for part in _split3(s * s))
    r = r_ref[...]
    od = s * lax.rsqrt(ms + RMS_EPS) * gn_ref[...] * (r * _sigmoid(r))
    ocat = jnp.concatenate([pick(oa_ref, oac_ref), pick(ob_ref, obc_ref), pick(oc_ref, occ_ref), od.astype(BF16)], -1)
    o = jnp.dot(ocat, wout_ref[...], preferred_element_type=F32)
    x1 = _layer_norm(alpha * x_ref[...] + g1_ref[0] * o) * lng_ref[...] + lnb_ref[...]
    x1_ref[...] = x1
    h2 = _layer_norm(x1) * (1.0 + sc2_ref[0]) + sh2_ref[0]
    hp_ref[...] = _pack_bf16_pairs(h2)
    a1, a2, _ = _split3(h2)
    w1, w2 = rw1_ref[...], rw2_ref[...]
    logits = (jnp.dot(a1, w1, preferred_element_type=F32) + jnp.dot(a1, w2, preferred_element_type=F32)
              + jnp.dot(a2, w1, preferred_element_type=F32))
    scores = _sigmoid(logits)
    work = scores + rb_ref[...]
    tm, ne = scores.shape
    lane = lax.broadcasted_iota(I32, (tm, ne), 1).astype(F32)
    lane8 = lax.broadcasted_iota(I32, (tm, TOP_K), 1)
    sel = jnp.zeros((tm, TOP_K), F32)
    gsel = jnp.zeros((tm, TOP_K), F32)
    chosen = jnp.zeros((tm, ne), F32)
    picks = []
    for k in range(TOP_K):
        m = jnp.max(work, -1, keepdims=True)
        idx = jnp.min(jnp.where(work == m, lane, float(ne)), -1, keepdims=True)
        hit = lane == idx
        picks.append(idx)
        sc = jnp.sum(jnp.where(hit, scores, 0.0), -1, keepdims=True)
        sel = jnp.where(lane8 == k, idx, sel)
        gsel = jnp.where(lane8 == k, sc, gsel)
        chosen = jnp.where(hit, 1.0, chosen)
        work = jnp.where(hit, -jnp.inf, work)
    sel_ref[...] = sel.astype(I32)
    gate_ref[...] = ROUTED_SCALE * gsel / jnp.sum(gsel, -1, keepdims=True)
    @pl.when(pl.program_id(0) == 0)
    def _():
        cnt_ref[...] = jnp.zeros_like(cnt_ref)

    ri = lax.broadcasted_iota(I32, (tm, tm), 0)
    ci = lax.broadcasted_iota(I32, (tm, tm), 1)
    before = jnp.where(ci < ri, 1.0, 0.0).astype(BF16)
    prefix = jnp.dot(before, chosen.astype(BF16), preferred_element_type=F32) + cnt_ref[...]
    rank = jnp.zeros((tm, TOP_K), F32)
    for k in range(TOP_K):
        rk = jnp.sum(jnp.where(lane == picks[k], prefix, 0.0), -1, keepdims=True)
        rank = jnp.where(lane8 == k, rk, rank)
    rank_ref[...] = rank.astype(I32)
    cnt_ref[...] = cnt_ref[...] + jnp.sum(chosen, 0, keepdims=True)
    count_ref[...] = cnt_ref[...].astype(I32)


def _outproj(xall, o_lat, o_ctx, ofb, pd, gn, wout, mods_l, lng, lnb, rw1, rw2, rb, *, n_lat, T, alpha, tile0, ntiles):
    D = xall.shape[1]
    tm = TOK_TILE
    N = ntiles * tm
    nlat_t, per_b, nb = n_lat // tm, T // tm, n_lat // T

    def midx(k):
        return lambda i: (jnp.where(i + tile0 < nlat_t, (i + tile0) // per_b, nb) * 6 + k, 0, 0)

    const = lambda i: (0, 0)
    row = lambda i: (i, 0)
    src_row = lambda i: (i + tile0, 0)
    lat_row = lambda i: (jnp.minimum(i + tile0, nlat_t - 1), 0)
    ctx_row = lambda i: (jnp.maximum(i + tile0 - nlat_t, 0), 0)
    E = rw1.shape[1]
    kern = functools.partial(_outproj_kernel, alpha=alpha, nlat_t=nlat_t, tile0=tile0)
    return pl.pallas_call(
        kern, grid=(ntiles,),
        in_specs=[pl.BlockSpec((tm, D), src_row)] + [pl.BlockSpec((tm, 256), lat_row)] * 3
                 + [pl.BlockSpec((tm, 256), ctx_row)] * 3 + [
                  pl.BlockSpec((tm, 256), src_row), pl.BlockSpec((tm, 256), src_row),
                  pl.BlockSpec((tm, 256), lambda i: (i + tile0, 2)), pl.BlockSpec((1, 256), const),
                  pl.BlockSpec((D, D), const),
                  pl.BlockSpec((1, 1, D), midx(2)), pl.BlockSpec((1, 1, D), midx(3)), pl.BlockSpec((1, 1, D), midx(4)),
                  pl.BlockSpec((1, D), const), pl.BlockSpec((1, D), const),
                  pl.BlockSpec((D, E), const), pl.BlockSpec((D, E), const), pl.BlockSpec((1, E), const)],
        out_specs=[pl.BlockSpec((tm, D), row), pl.BlockSpec((tm, D // 2), row),
                   pl.BlockSpec((tm, TOP_K), row), pl.BlockSpec((tm, TOP_K), row), pl.BlockSpec((tm, TOP_K), row),
                   pl.BlockSpec((1, E), const)],
        out_shape=[jax.ShapeDtypeStruct((N, D), F32), jax.ShapeDtypeStruct((N, D // 2), U32),
                   jax.ShapeDtypeStruct((N, TOP_K), I32), jax.ShapeDtypeStruct((N, TOP_K), F32),
                   jax.ShapeDtypeStruct((N, TOP_K), I32), jax.ShapeDtypeStruct((1, E), I32)],
        scratch_shapes=[pltpu.VMEM((1, E), F32)],
        compiler_params=_cparams(("arbitrary",)), name="outproj_router",
    )(xall, *o_lat, *o_ctx, *ofb, pd, gn, wout, mods_l, mods_l, mods_l, lng, lnb, rw1, rw2, rb)


def _sc_mesh():
    return plsc.VectorSubcoreMesh(core_axis_name="c", subcore_axis_name="s")


def _sc_worker():
    return lax.axis_index("s") * SC_CORES + lax.axis_index("c")


def _sc_dispatch(hp, dest_k, rows):
    N, W = hp.shape
    tpw = N // SC_WORKERS
    ch = SC_DISPATCH_CHUNK
    assert N % SC_WORKERS == 0 and tpw % ch == 0
    nch = tpw // ch
    idx = dest_k.reshape(TOP_K, SC_WORKERS, nch, ch).transpose(1, 2, 0, 3)

    @functools.partial(
        pl.kernel, mesh=_sc_mesh(), out_type=jax.ShapeDtypeStruct((rows, W), hp.dtype),
        scratch_types=[pltpu.VMEM((nch, TOP_K, ch), I32), pltpu.VMEM((ch, W), hp.dtype), pltpu.SemaphoreType.DMA],
        compiler_params=pltpu.CompilerParams(use_tc_tiling_on_sc=True), name="sc_dispatch")
    def scatter(hp_hbm, idx_hbm, out_hbm, idx_v, rows_v, sem):
        wid = _sc_worker()
        base = wid * tpw
        pltpu.sync_copy(idx_hbm.at[wid], idx_v)

        @pl.loop(0, nch)
        def _(c):
            pltpu.sync_copy(hp_hbm.at[pl.ds(base + c * ch, ch)], rows_v)
            copies = [pltpu.async_copy(rows_v, out_hbm.at[idx_v.at[c, k]], sem) for k in range(TOP_K)]
            for cp in copies:
                cp.wait()

    return scatter(hp, idx)


def _sc_gather(src, idx):
    R, W = idx.shape[0], src.shape[1]
    per_w = R // SC_WORKERS
    ch = SC_GATHER_CHUNK
    assert R % SC_WORKERS == 0 and per_w % ch == 0
    nch = per_w // ch

    @functools.partial(
        pl.kernel, mesh=_sc_mesh(), out_type=jax.ShapeDtypeStruct((R, W), src.dtype),
        scratch_types=[pltpu.VMEM((nch, ch), I32), pltpu.VMEM((ch, W), src.dtype), pltpu.SemaphoreType.DMA],
        compiler_params=pltpu.CompilerParams(use_tc_tiling_on_sc=True), name="sc_gather")
    def gather(src_hbm, idx_hbm, out_hbm, idx_v, rows_v, sem):
        wid = _sc_worker()
        base = wid * per_w
        pltpu.sync_copy(idx_hbm.at[wid], idx_v)

        @pl.loop(0, nch)
        def _(c):
            pltpu.async_copy(src_hbm.at[idx_v.at[c]], rows_v, sem).wait()
            pltpu.sync_copy(rows_v, out_hbm.at[pl.ds(base + c * ch, ch)])

    return gather(src, idx.reshape(SC_WORKERS, nch, ch))


def _swiglu(xb, w1, w3, w2):
    a = jnp.dot(xb, w1, preferred_element_type=F32)
    b = jnp.dot(xb, w3, preferred_element_type=F32)
    return jnp.dot((a * _sigmoid(a) * b).astype(BF16), w2, preferred_element_type=F32)


def _expert_kernel(be_ref, nu_ref, xs_hbm, w1_ref, w3_ref, w2_ref, y_ref, w1b_ref, w3b_ref, w2b_ref, xbuf, sem):
    i = pl.program_id(0)
    n_used = nu_ref[0]
    used = i < n_used

    def fetch(j):
        slot = j % MOE_RING
        return pltpu.make_async_copy(xs_hbm.at[pl.ds(pl.multiple_of(j * MOE_BM, MOE_BM), MOE_BM)],
                                     xbuf.at[slot], sem.at[slot])

    for j in range(MOE_RING - 1):
        @pl.when((i == 0) & (j < n_used))
        def _(j=j):
            fetch(j).start()

    @pl.when(i + (MOE_RING - 1) < n_used)
    def _():
        fetch(i + (MOE_RING - 1)).start()

    @pl.when(used & ((i == 0) | (be_ref[i] != be_ref[jnp.maximum(i - 1, 0)])))
    def _():
        w1b_ref[...] = w1_ref[0, 0].astype(BF16)
        w3b_ref[...] = w3_ref[0, 0].astype(BF16)
        w2b_ref[...] = w2_ref[0, 0].astype(BF16)

    @pl.when(used)
    def _():
        fetch(i).wait()
        xb = _unpack_bf16_pairs(xbuf[i % MOE_RING])
        y_ref[...] = _pack_bf16_pairs(_swiglu(xb, w1b_ref[...], w3b_ref[...], w2b_ref[...]))

    @pl.when(jnp.logical_not(used))
    def _():
        y_ref[...] = jnp.zeros_like(y_ref)


def _experts(xs, blk_exp, n_used, w1, w3, w2, layer):
    rows, half = xs.shape
    _, E, D, F = w1.shape
    nblk = rows // MOE_BM
    wmap = lambda i, be, nu: (layer, be[i], 0, 0)
    grid_spec = pltpu.PrefetchScalarGridSpec(
        num_scalar_prefetch=2, grid=(nblk,),
        in_specs=[pl.BlockSpec(memory_space=pl.ANY),
                  pl.BlockSpec((1, 1, D, F), wmap), pl.BlockSpec((1, 1, D, F), wmap), pl.BlockSpec((1, 1, F, D), wmap)],
        out_specs=pl.BlockSpec((MOE_BM, half), lambda i, be, nu: (jnp.minimum(i, nu[0]), 0)),
        scratch_shapes=[pltpu.VMEM((D, F), BF16), pltpu.VMEM((D, F), BF16), pltpu.VMEM((F, D), BF16),
                        pltpu.VMEM((MOE_RING, MOE_BM, half), U32), pltpu.SemaphoreType.DMA((MOE_RING,))])
    return pl.pallas_call(
        _expert_kernel, grid_spec=grid_spec,
        out_shape=jax.ShapeDtypeStruct((rows, half), U32),
        compiler_params=_cparams(("arbitrary",)), name="expert_ffn",
    )(blk_exp, n_used, xs, w1, w3, w2)


def _combine_kernel(yg_ref, gate_ref, hp_ref, sw1_ref, sw3_ref, sw2_ref, x_ref, g2_ref, lng_ref, lnb_ref,
                    o_ref, *, alpha):
    f = _swiglu(_unpack_bf16_pairs(hp_ref[...]), sw1_ref[...], sw3_ref[...], sw2_ref[...])
    gates = gate_ref[...]
    lo = hi = 0.0
    for k in range(TOP_K):
        u = yg_ref[k]
        g = gates[:, k:k + 1]
        lo = lo + g * lax.bitcast_convert_type(u << 16, F32)
        hi = hi + g * lax.bitcast_convert_type(u & jnp.uint32(0xFFFF0000), F32)
    f = f + jnp.concatenate([lo, hi], -1)
    o_ref[...] = _layer_norm(alpha * x_ref[...] + g2_ref[0] * f) * lng_ref[...] + lnb_ref[...]


def _combine(yg, gates, hp, sw1, sw3, sw2, x1, mods_l, lng, lnb, *, n_lat, T, alpha, n_out, tok0):
    D = x1.shape[1]
    N = n_out
    tm = CMB_TILE
    nlat_t, per_b, nb = n_lat // tm, T // tm, n_lat // T
    t0 = tok0 // tm
    F = sw1.shape[1]
    const = lambda i: (0, 0)
    row = lambda i: (i, 0)
    kern = functools.partial(_combine_kernel, alpha=alpha)
    return pl.pallas_call(
        kern, grid=(N // tm,),
        in_specs=[pl.BlockSpec((TOP_K, tm, D // 2), lambda i: (0, i, 0)), pl.BlockSpec((tm, TOP_K), row),
                  pl.BlockSpec((tm, D // 2), row),
                  pl.BlockSpec((D, F), const), pl.BlockSpec((D, F), const), pl.BlockSpec((F, D), const),
                  pl.BlockSpec((tm, D), row),
                  pl.BlockSpec((1, 1, D), lambda i: (jnp.where(i + t0 < nlat_t, (i + t0) // per_b, nb) * 6 + 5, 0, 0)),
                  pl.BlockSpec((1, D), const), pl.BlockSpec((1, D), const)],
        out_specs=pl.BlockSpec((tm, D), row),
        out_shape=jax.ShapeDtypeStruct((N, D), F32),
        compiler_params=_cparams(("arbitrary",)), name="moe_combine",
    )(yg, gates, hp, sw1, sw3, sw2, x1, mods_l, lng, lnb)


def _dest_kernel(start_ref, sel_ref, rank_ref, o_ref):
    sel = sel_ref[...]

    def body(e, acc):
        return jnp.where(sel == e, start_ref[e], acc)

    o_ref[...] = lax.fori_loop(0, start_ref.shape[0], body, jnp.zeros_like(sel)) + rank_ref[...]


def _dispatch_plan(sel, rank, counts):
    N = sel.shape[0]
    n_experts = counts.shape[0]
    padded = (counts + MOE_BM - 1) // MOE_BM * MOE_BM
    pad_end = jnp.cumsum(padded)
    pad_start = (pad_end - padded).astype(I32)
    dense = (N * TOP_K // 128, 128)
    dest = pl.pallas_call(
        _dest_kernel,
        in_specs=[pl.BlockSpec(memory_space=pltpu.SMEM), pl.BlockSpec(memory_space=pltpu.VMEM),
                  pl.BlockSpec(memory_space=pltpu.VMEM)],
        out_specs=pl.BlockSpec(memory_space=pltpu.VMEM),
        out_shape=jax.ShapeDtypeStruct(dense, I32), name="moe_dest_rows",
    )(pad_start, sel.reshape(dense), rank.reshape(dense))
    dest_k = dest.reshape(N, TOP_K).T
    n_blocks = -(-(N * TOP_K) // MOE_BM) + n_experts
    blk_first = jnp.arange(n_blocks, dtype=I32) * MOE_BM
    blk_exp = jnp.minimum(jnp.sum((pad_end[None, :] <= blk_first[:, None]).astype(I32), -1), n_experts - 1)
    n_used = (pad_end[-1] // MOE_BM).astype(I32).reshape(1)
    return dest_k, blk_exp, n_used, n_blocks * MOE_BM


def _rope_tables(T, tile):
    t = jnp.arange(T)
    row = (t // GRID_W).astype(F32)[:, None]
    col = (t % GRID_W).astype(F32)[:, None]

    def group(half):
        inv = ROPE_THETA ** (-jnp.arange(half, dtype=F32) / half)
        ar, ac = row * inv, col * inv
        cos = jnp.concatenate([jnp.cos(ar), jnp.cos(ar), jnp.cos(ac), jnp.cos(ac)], -1)
        sin = jnp.concatenate([-jnp.sin(ar), jnp.sin(ar), -jnp.sin(ac), jnp.sin(ac)], -1)
        return cos, sin

    ones32, zeros32 = jnp.ones((T, 32), F32), jnp.zeros((T, 32), F32)
    cw, sw = group(16)
    cw, sw = jnp.tile(cw, (1, 2)), jnp.tile(sw, (1, 2))
    c8, s8 = group(8)
    cq = jnp.tile(jnp.concatenate([ones32, c8], -1), (1, 2))
    sq = jnp.tile(jnp.concatenate([zeros32, s8], -1), (1, 2))
    ck = jnp.concatenate([c8, ones32, ones32, ones32], -1)
    sk = jnp.concatenate([s8, zeros32, zeros32, zeros32], -1)

    def pad(a, ident):
        return jnp.concatenate([a, jnp.full((tile, 128), ident, F32)], 0)

    return [pad(cw, 1.0), pad(sw, 0.0), pad(cq, 1.0), pad(sq, 0.0), pad(ck, 1.0), pad(sk, 0.0)]


def _layer_weights(w_in, mla_w_uq, mla_w_ukv, gla_w_gf, gla_w_gb, gla_b_gf, gla_b_gb):
    D = w_in.shape[0]
    z = lambda n: jnp.zeros((D, n), F32)
    win = jnp.concatenate([
        w_in[:, 0:256] * Q_SCALE, w_in[:, 256:768],
        w_in[:, 768:1024] * Q_SCALE, w_in[:, 1024:1280],
        w_in[:, 1280:1696], z(96),
        w_in[:, 1696:2464],
        w_in[:, 2464:2496], z(96)], -1).astype(BF16)
    wuq = (mla_w_uq * Q_SCALE).astype(BF16)
    ukv = mla_w_ukv.reshape(-1, N_HEADS, MLA_NOPE + HEAD_DIM)
    kv_rank = ukv.shape[0]
    wk = jnp.concatenate([ukv[:, :, :MLA_NOPE], jnp.zeros((kv_rank, N_HEADS, MLA_ROPE), F32)], -1)
    place = jnp.concatenate([jnp.zeros((MLA_ROPE, MLA_NOPE), F32), jnp.eye(MLA_ROPE, dtype=F32)], -1)
    place = jnp.concatenate([jnp.tile(place, (1, N_HEADS)), jnp.zeros((128 - MLA_ROPE, 256), F32)], 0)
    wkk = jnp.concatenate([wk.reshape(kv_rank, 256), place], 0).astype(BF16)
    wv = ukv[:, :, MLA_NOPE:].reshape(kv_rank, 256).astype(BF16)
    r = GLA_GATE_RANK
    wg = jnp.zeros((128, 256), F32).at[0:r, 0:128].set(gla_w_gf).at[r:2 * r, 128:256].set(gla_w_gb).astype(BF16)
    bg = jnp.concatenate([gla_b_gf, gla_b_gb])[None, :]
    return win, wuq, wkk, wv, wg, bg


def kernel(x, c, ctx, c_ctx, w_ada, b_ada, w_in, na_rpb, wa_sink, mla_g_q, mla_g_kv, mla_w_uq, mla_w_ukv,
           gla_w_gf, gla_b_gf, gla_w_gb, gla_b_gb, gla_g_norm, w_out, ln1_g, ln1_b, ln2_g, ln2_b,
           router_w, router_bias, exp_w1, exp_w3, exp_w2, sh_w1, sh_w3, sh_w2):
    B, T, D = x.shape
    C = ctx.shape[1]
    L = w_ada.shape[0]
    E = router_w.shape[-1]
    n_lat = B * T
    alpha = (2 * L) ** 0.25
    dims = dict(B=B, T=T, C=C)

    cc = jnp.zeros((8, D), F32).at[:B].set(c).at[B].set(c_ctx)
    mods = _mods(cc, w_ada, b_ada)
    tabs = _rope_tables(T, TOK_TILE)
    xall = jnp.concatenate([x.reshape(n_lat, D), ctx.reshape(B * C, D)], 0)

    for l in range(L):
        mods_l = mods[l].reshape(8 * 6, 1, D)
        win, wuq, wkk, wv, wg, bg = _layer_weights(w_in[l], mla_w_uq[l], mla_w_ukv[l], gla_w_gf[l], gla_w_gb[l],
                                                   gla_b_gf[l], gla_b_gb[l])
        pa, pb, pc, pd = _inproj(xall, mods_l, win, tabs, mla_g_q[l][None], mla_g_kv[l][None], wuq, wkk, wv, wg, bg,
                                 n_lat=n_lat, T=T)
        oa = _na_attention(pa, _na_bias_tables(na_rpb[l], T // GRID_W), **dims)
        sink = wa_sink[l] * LOG2E
        ob = _wa_attention(pb, sink, **dims)
        oc = _mla_attention(pc, **dims)
        o_ctx = _ctx_attention(sink, pa, pb, pc, **dims)
        ofb = _gla(pd, **dims)

        rw1 = router_w[l].astype(BF16)
        rw2 = (router_w[l] - rw1.astype(F32)).astype(BF16)
        n_all = xall.shape[0]
        part = n_all // MOE_SPLITS
        assert n_all % (MOE_SPLITS * TOK_TILE) == 0
        shared = (sh_w1[l].astype(BF16), sh_w3[l].astype(BF16), sh_w2[l].astype(BF16))
        gn, wout = jnp.tile(gla_g_norm[l], N_HEADS)[None], w_out[l].astype(BF16)
        routed = []
        for p in range(MOE_SPLITS):
            x1, hp, sel, gates, rank, counts = _outproj(
                xall, (oa, ob, oc), o_ctx, ofb, pd, gn, wout, mods_l, ln1_g[l][None], ln1_b[l][None], rw1, rw2,
                router_bias[l][None], n_lat=n_lat, T=T, alpha=alpha,
                tile0=p * part // TOK_TILE, ntiles=part // TOK_TILE)
            dest_k, blk_exp, n_used, rows = _dispatch_plan(sel, rank, counts.reshape(E))
            routed.append((x1, hp, gates, dest_k, blk_exp, n_used, _sc_dispatch(hp, dest_k, rows)))
        outs = []
        for p, (x1, hp, gates, dest_k, blk_exp, n_used, xs) in enumerate(routed):
            y = _experts(xs, blk_exp, n_used, exp_w1, exp_w3, exp_w2, l)
            yg = _sc_gather(y, dest_k.reshape(-1)).reshape(TOP_K, part, D // 2)
            n_out = part if l < L - 1 else max(min(n_lat - p * part, part), 0)
            if n_out:
                outs.append(_combine(yg, gates, hp, *shared, x1, mods_l, ln2_g[l][None], ln2_b[l][None], n_lat=n_lat,
                                     T=T, alpha=alpha, n_out=n_out, tok0=p * part))
        xall = jnp.concatenate(outs, 0)

    return xall.reshape(B, T, D)
```

```python
import functools

import numpy as np
import jax
import jax.numpy as jnp
from jax import lax
from jax.experimental import pallas as pl
from jax.experimental.pallas import tpu as pltpu
from jax.experimental.pallas import tpu_sc as plsc

F32 = jnp.float32
BF16 = jnp.bfloat16
U32 = jnp.uint32
I32 = jnp.int32

GRID_W = 64
HEAD_DIM = 64
N_HEADS = 4
NA_KH, NA_KW = 8, 16
WA_WINDOW = 128
WA_SUB = 256
MLA_NOPE, MLA_ROPE = 32, 32
GLA_DK, GLA_DV = 32, 64
GLA_GATE_RANK = 16
GLA_GATE_NORM = 16.0
GLA_CHUNK = 64
TOP_K = 8
ROUTED_SCALE = 2.5
ROPE_THETA = 10000.0
LN_EPS = 1e-5
RMS_EPS = 1e-6
NEG = -1e30
LOG2E = 1.4426950408889634
Q_SCALE = HEAD_DIM ** -0.5 * LOG2E

VMEM_LIMIT = 56 * 1024 * 1024
TOK_TILE = 512
ATT_TQ = 512
MLA_TQ = 1024
MLA_TK = 1024
GLA_BLOCK = 256
MOE_BM = 512
MOE_RING = 3
MOE_SPLITS = 1
CMB_TILE = 256
SC_CORES = 2
SC_WORKERS = 32
SC_DISPATCH_CHUNK = 48
SC_GATHER_CHUNK = 64

PW = 2688
_NT = (((1,), (1,)), ((), ()))


def _cparams(sem, vmem=VMEM_LIMIT):
    return pltpu.CompilerParams(dimension_semantics=sem, vmem_limit_bytes=vmem)


def _sigmoid(x):
    return 1.0 / (1.0 + jnp.exp(-x))


def _layer_norm(x):
    mu = jnp.mean(x, -1, keepdims=True)
    xc = x - mu
    var = jnp.mean(xc * xc, -1, keepdims=True)
    return xc * lax.rsqrt(var + LN_EPS)


def _rms(x):
    return x * lax.rsqrt(jnp.mean(x * x, -1, keepdims=True) + RMS_EPS)


def _bdot(a, b):
    return jnp.dot(a.astype(BF16), b.astype(BF16), preferred_element_type=F32)


def _bdot_nt(a, b):
    return lax.dot_general(a.astype(BF16), b.astype(BF16), _NT, preferred_element_type=F32)


def _split3(a):
    a1 = a.astype(BF16)
    r = a - a1.astype(F32)
    a2 = r.astype(BF16)
    a3 = (r - a2.astype(F32)).astype(BF16)
    return a1, a2, a3


def _mods_kernel(cc_ref, w_ref, b_ref, o_ref):
    cc = cc_ref[...]
    s = cc * _sigmoid(cc)
    o_ref[0] = _bdot(s, w_ref[0]) + b_ref[0]


def _mods(cc, w_ada, b_ada):
    L, D, W = w_ada.shape
    tn = 1536
    return pl.pallas_call(
        _mods_kernel,
        grid=(L, W // tn),
        in_specs=[pl.BlockSpec((8, D), lambda l, j: (0, 0)),
                  pl.BlockSpec((1, D, tn), lambda l, j: (l, 0, j)),
                  pl.BlockSpec((1, 1, tn), lambda l, j: (l, 0, j))],
        out_specs=pl.BlockSpec((1, 8, tn), lambda l, j: (l, 0, j)),
        out_shape=jax.ShapeDtypeStruct((L, 8, W), F32),
        compiler_params=_cparams(("arbitrary", "arbitrary")),
        name="adaln_mods",
    )(cc, w_ada, b_ada.reshape(L, 1, W))


def _rope(z, cos, sin, half):
    w = z.shape[-1]
    lane = lax.broadcasted_iota(I32, z.shape, 1)
    first = (lane % (2 * half)) < half
    partner = jnp.where(first, pltpu.roll(z, w - half, 1), pltpu.roll(z, half, 1))
    return z * cos + partner * sin


def _log_sigmoid(x):
    return jnp.minimum(x, 0.0) - jnp.log(1.0 + jnp.exp(-jnp.abs(x)))


def _inproj_kernel(x_ref, shift_ref, scale_ref, win_ref, cw_ref, sw_ref, cq_ref, sq_ref, ck_ref, sk_ref,
                   gq_ref, gkv_ref, wuq_ref, wkk_ref, wv_ref, wg_ref, bg_ref,
                   pa_ref, pb_ref, pc_ref, pd_ref):
    h = _layer_norm(x_ref[...]) * (1.0 + scale_ref[0]) + shift_ref[0]
    p = jnp.dot(h.astype(BF16), win_ref[...], preferred_element_type=F32)
    pa_ref[...] = p[:, 0:768].astype(BF16)
    cw, sw = cw_ref[...], sw_ref[...]
    pb_ref[:, 0:128] = _rope(p[:, 768:896], cw, sw, 16).astype(BF16)
    pb_ref[:, 128:256] = _rope(p[:, 896:1024], cw, sw, 16).astype(BF16)
    pb_ref[:, 256:384] = _rope(p[:, 1024:1152], cw, sw, 16).astype(BF16)
    pb_ref[:, 384:512] = p[:, 1152:1280].astype(BF16)
    cqn = _rms(p[:, 1280:1536]) * gq_ref[...]
    q = jnp.dot(cqn.astype(BF16), wuq_ref[...], preferred_element_type=F32)
    cq, sq = cq_ref[...], sq_ref[...]
    pc_ref[:, 0:128] = _rope(q[:, 0:128], cq, sq, 8).astype(BF16)
    pc_ref[:, 128:256] = _rope(q[:, 128:256], cq, sq, 8).astype(BF16)
    ckvn = (_rms(p[:, 1536:1664]) * gkv_ref[...]).astype(BF16)
    kr = _rope(p[:, 1664:1792], ck_ref[...], sk_ref[...], 8).astype(BF16)
    kin = jnp.concatenate([ckvn, kr], axis=-1)
    pc_ref[:, 256:512] = jnp.dot(kin, wkk_ref[...], preferred_element_type=F32).astype(BF16)
    pc_ref[:, 512:768] = jnp.dot(ckvn, wv_ref[...], preferred_element_type=F32).astype(BF16)
    pd_ref[:, 0:128] = p[:, 1792:1920] * (GLA_DK ** -0.5)
    pd_ref[:, 128:768] = p[:, 1920:2560]
    pre = jnp.dot(p[:, 2560:2688].astype(BF16), wg_ref[...], preferred_element_type=F32) + bg_ref[...]
    pd_ref[:, 768:1024] = _log_sigmoid(pre) * (1.0 / GLA_GATE_NORM)


def _inproj(xall, mods_l, win, tabs, gq, gkv, wuq, wkk, wv, wg, bg, *, n_lat, T):
    N, D = xall.shape
    tm = TOK_TILE
    nlat_t = n_lat // tm
    per_b = T // tm
    nb = n_lat // T

    def midx(k):
        return lambda i: (jnp.where(i < nlat_t, i // per_b, nb) * 6 + k, 0, 0)

    def tidx(i):
        return (jnp.where(i < nlat_t, i % per_b, per_b), 0)

    const = lambda i: (0, 0)
    tab_spec = pl.BlockSpec((tm, 128), tidx)
    in_specs = [pl.BlockSpec((tm, D), lambda i: (i, 0)),
                pl.BlockSpec((1, 1, D), midx(0)), pl.BlockSpec((1, 1, D), midx(1)),
                pl.BlockSpec((D, PW), const)] + [tab_spec] * 6 + [
                pl.BlockSpec((1, 256), const), pl.BlockSpec((1, 128), const),
                pl.BlockSpec((256, 256), const), pl.BlockSpec((256, 256), const),
                pl.BlockSpec((128, 256), const), pl.BlockSpec((128, 256), const),
                pl.BlockSpec((1, 256), const)]
    out_specs = [pl.BlockSpec((tm, 768), lambda i: (i, 0)), pl.BlockSpec((tm, 512), lambda i: (i, 0)),
                 pl.BlockSpec((tm, 768), lambda i: (i, 0)), pl.BlockSpec((tm, 1024), lambda i: (i, 0))]
    out_shape = [jax.ShapeDtypeStruct((N, 768), BF16), jax.ShapeDtypeStruct((N, 512), BF16),
                 jax.ShapeDtypeStruct((N, 768), BF16), jax.ShapeDtypeStruct((N, 1024), F32)]
    return pl.pallas_call(
        _inproj_kernel, grid=(N // tm,), in_specs=in_specs, out_specs=out_specs, out_shape=out_shape,
        compiler_params=_cparams(("arbitrary",)), name="inproj",
    )(xall, mods_l, mods_l, win, *tabs, gq, gkv, wuq, wkk, wv, wg, bg)


def _softmax_av(parts, extra_logit=None):
    m = functools.reduce(jnp.maximum, [jnp.max(s, -1, keepdims=True) for s, _ in parts])
    if extra_logit is not None:
        m = jnp.maximum(m, extra_logit)
    l = 0.0 if extra_logit is None else jnp.exp2(extra_logit - m)
    o = None
    for s, v in parts:
        e = jnp.exp2(s - m)
        l = l + jnp.sum(e, -1, keepdims=True)
        c = jnp.dot(e.astype(BF16), v, preferred_element_type=F32)
        o = c if o is None else o + c
    return o * (1.0 / l)


def _na_kernel(q_ref, k_ref, v_ref, kc_ref, vc_ref, bias_ref, o_ref, *, rows):
    j = pl.program_id(1)
    ws = pl.multiple_of(jnp.clip(8 * j - 4, 0, rows - 16) * GRID_W, 256)
    win = 16 * GRID_W
    outs = []
    for h in range(N_HEADS):
        sl = slice(HEAD_DIM * h, HEAD_DIM * (h + 1))
        q = q_ref[:, sl]
        s = lax.dot_general(q, k_ref[pl.ds(ws, win), sl], _NT, preferred_element_type=F32) + bias_ref[0, h]
        sc = lax.dot_general(q, kc_ref[:, sl], _NT, preferred_element_type=F32)
        outs.append(_softmax_av([(s, v_ref[pl.ds(ws, win), sl]), (sc, vc_ref[:, sl])]))
    o_ref[...] = jnp.concatenate(outs, -1).astype(BF16)


def _na_bias_tables(rpb, rows):
    nj = rows // 8
    ro_all, ok_all = [], []
    for j in (0, 1, nj - 1):
        ws = int(np.clip(8 * j - 4, 0, rows - 16))
        r = 8 * j + np.arange(8)[:, None, None, None]
        qc = np.arange(GRID_W)[None, :, None, None]
        kr = ws + np.arange(16)[None, None, :, None]
        kc = np.arange(GRID_W)[None, None, None, :]
        rs = np.clip(r - NA_KH // 2, 0, rows - NA_KH)
        cs = np.clip(qc - NA_KW // 2, 0, GRID_W - NA_KW)
        ok = (kr >= rs) & (kr < rs + NA_KH) & (kc >= cs) & (kc < cs + NA_KW)
        ro = np.clip(kr - r + (NA_KH - 1), 0, 2 * NA_KH - 2)
        shp = (8, GRID_W, 16, GRID_W)
        ro_all.append(np.broadcast_to(ro, shp).reshape(512, 1024))
        ok_all.append(np.broadcast_to(ok, shp).reshape(512, 1024))
    ok = np.stack(ok_all)
    hi = lax.Precision.HIGHEST
    co = np.clip(np.arange(GRID_W)[None, :] - np.arange(GRID_W)[:, None] + (NA_KW - 1), 0, 2 * NA_KW - 2)
    col_hot = (co[..., None] == np.arange(2 * NA_KW - 1)).astype(np.float32)
    cb = jnp.einsum('hdo,qko->hdqk', rpb, col_hot, precision=hi)
    ro = np.stack(ro_all).reshape(3, 8, GRID_W, 16, GRID_W)[:, :, 0, :, 0]
    row_hot = (ro[..., None] == np.arange(2 * NA_KH - 1)).astype(np.float32)
    b = jnp.einsum('vrsd,hdqk->vhrqsk', row_hot, cb, precision=hi)
    b = b.reshape(3, rpb.shape[0], 512, 1024)
    return jnp.where(ok[:, None], b * LOG2E, NEG)


def _na_attention(pa, bias, *, B, T, C):
    rows = T // GRID_W
    nj = T // ATT_TQ
    cb = B * T // C
    kern = functools.partial(_na_kernel, rows=rows)
    return pl.pallas_call(
        kern, grid=(B, nj),
        in_specs=[pl.BlockSpec((ATT_TQ, 256), lambda b, j: (b * nj + j, 0)),
                  pl.BlockSpec((T, 256), lambda b, j: (b, 1)),
                  pl.BlockSpec((T, 256), lambda b, j: (b, 2)),
                  pl.BlockSpec((C, 256), lambda b, j: (cb + b, 1)),
                  pl.BlockSpec((C, 256), lambda b, j: (cb + b, 2)),
                  pl.BlockSpec((1, N_HEADS, 512, 1024),
                               lambda b, j: (jnp.where(j == 0, 0, jnp.where(j == nj - 1, 2, 1)), 0, 0, 0))],
        out_specs=pl.BlockSpec((ATT_TQ, 256), lambda b, j: (b * nj + j, 0)),
        out_shape=jax.ShapeDtypeStruct((B * T, 256), BF16),
        compiler_params=_cparams(("arbitrary", "arbitrary")), name="na_attention",
    )(pa, pa, pa, pa, pa, bias)


def _wa_kernel(sink_ref, q_ref, k_ref, v_ref, kc_ref, vc_ref, o_ref, *, T):
    i = pl.program_id(1)
    win = WA_SUB + 2 * WA_WINDOW
    for sub in range(ATT_TQ // WA_SUB):
        rows = slice(sub * WA_SUB, (sub + 1) * WA_SUB)
        start = i * ATT_TQ + sub * WA_SUB
        ws = pl.multiple_of(jnp.clip(start - WA_WINDOW, 0, T - win), 128)
        rel = ((lax.broadcasted_iota(I32, (WA_SUB, win), 1) + (ws - start))
               - lax.broadcasted_iota(I32, (WA_SUB, win), 0))
        valid = jnp.abs(rel) <= WA_WINDOW
        outs = []
        for h in range(N_HEADS):
            g = h // 2
            ksl = slice(HEAD_DIM * g, HEAD_DIM * (g + 1))
            q = q_ref[rows, HEAD_DIM * h:HEAD_DIM * (h + 1)]
            s = lax.dot_general(q, k_ref[pl.ds(ws, win), ksl], _NT, preferred_element_type=F32)
            s = jnp.where(valid, s, NEG)
            sc = lax.dot_general(q, kc_ref[:, ksl], _NT, preferred_element_type=F32)
            outs.append(_softmax_av([(s, v_ref[pl.ds(ws, win), ksl]), (sc, vc_ref[:, ksl])],
                                    extra_logit=sink_ref[h]))
        o_ref[rows, :] = jnp.concatenate(outs, -1).astype(BF16)


def _wa_attention(pb, sink, *, B, T, C):
    nj = T // ATT_TQ
    cb = B * T // C
    kern = functools.partial(_wa_kernel, T=T)
    return pl.pallas_call(
        kern, grid=(B, nj),
        in_specs=[pl.BlockSpec(memory_space=pltpu.SMEM),
                  pl.BlockSpec((ATT_TQ, 256), lambda b, j: (b * nj + j, 0)),
                  pl.BlockSpec((T, 128), lambda b, j: (b, 2)),
                  pl.BlockSpec((T, 128), lambda b, j: (b, 3)),
                  pl.BlockSpec((C, 128), lambda b, j: (cb + b, 2)),
                  pl.BlockSpec((C, 128), lambda b, j: (cb + b, 3))],
        out_specs=pl.BlockSpec((ATT_TQ, 256), lambda b, j: (b * nj + j, 0)),
        out_shape=jax.ShapeDtypeStruct((B * T, 256), BF16),
        compiler_params=_cparams(("arbitrary", "arbitrary")), name="wa_attention",
    )(sink, pb, pb, pb, pb, pb)


def _mla_kernel(q_ref, k_ref, v_ref, kc_ref, vc_ref, o_ref, *, T):
    nk = T // MLA_TK
    heads = [slice(HEAD_DIM * h, HEAD_DIM * (h + 1)) for h in range(N_HEADS)]
    qs = [q_ref[:, sl] for sl in heads]
    init = []
    for q, sl in zip(qs, heads):
        sc = lax.dot_general(q, kc_ref[:, sl], _NT, preferred_element_type=F32)
        m0 = jnp.max(sc, -1, keepdims=True)
        e0 = jnp.exp2(sc - m0)
        init += [m0, jnp.sum(e0, -1, keepdims=True),
                 jnp.dot(e0.astype(BF16), vc_ref[:, sl], preferred_element_type=F32)]

    def body(c, carry):
        ks = pl.multiple_of(c * MLA_TK, MLA_TK)
        out = []
        for h, (q, sl) in enumerate(zip(qs, heads)):
            m, l, acc = carry[3 * h:3 * h + 3]
            s = lax.dot_general(q, k_ref[pl.ds(ks, MLA_TK), sl], _NT, preferred_element_type=F32)
            mn = jnp.maximum(m, jnp.max(s, -1, keepdims=True))
            a = jnp.exp2(m - mn)
            e = jnp.exp2(s - mn)
            l = l * a + jnp.sum(e, -1, keepdims=True)
            acc = acc * a + jnp.dot(e.astype(BF16), v_ref[pl.ds(ks, MLA_TK), sl], preferred_element_type=F32)
            out += [mn, l, acc]
        return tuple(out)

    fin = lax.fori_loop(0, nk, body, tuple(init), unroll=2)
    outs = [fin[3 * h + 2] * (1.0 / fin[3 * h + 1]) for h in range(N_HEADS)]
    o_ref[...] = jnp.concatenate(outs, -1).astype(BF16)


def _mla_attention(pc, *, B, T, C):
    nj = T // MLA_TQ
    cb = B * T // C
    kern = functools.partial(_mla_kernel, T=T)
    return pl.pallas_call(
        kern, grid=(B, nj),
        in_specs=[pl.BlockSpec((MLA_TQ, 256), lambda b, j: (b * nj + j, 0)),
                  pl.BlockSpec((T, 256), lambda b, j: (b, 1)),
                  pl.BlockSpec((T, 256), lambda b, j: (b, 2)),
                  pl.BlockSpec((C, 256), lambda b, j: (cb + b, 1)),
                  pl.BlockSpec((C, 256), lambda b, j: (cb + b, 2))],
        out_specs=pl.BlockSpec((MLA_TQ, 256), lambda b, j: (b * nj + j, 0)),
        out_shape=jax.ShapeDtypeStruct((B * T, 256), BF16),
        compiler_params=_cparams(("arbitrary", "arbitrary")), name="mla_attention",
    )(pc, pc, pc, pc, pc)


def _ctx_kernel(sink_ref, pa_ref, pb_ref, pc_ref, oa_ref, ob_ref, oc_ref):
    def attend(p_ref, koff, voff, kv_heads, out_ref, sink):
        outs = []
        for h in range(N_HEADS):
            g = h * kv_heads // N_HEADS
            q = p_ref[:, HEAD_DIM * h:HEAD_DIM * (h + 1)]
            k = p_ref[:, koff + HEAD_DIM * g:koff + HEAD_DIM * (g + 1)]
            v = p_ref[:, voff + HEAD_DIM * g:voff + HEAD_DIM * (g + 1)]
            s = lax.dot_general(q, k, _NT, preferred_element_type=F32)
            outs.append(_softmax_av([(s, v)], extra_logit=sink_ref[h] if sink else None))
        out_ref[...] = jnp.concatenate(outs, -1).astype(BF16)

    attend(pa_ref, 256, 512, 4, oa_ref, False)
    attend(pb_ref, 256, 384, 2, ob_ref, True)
    attend(pc_ref, 256, 512, 4, oc_ref, False)


def _ctx_attention(sink, pa, pb, pc, *, B, T, C):
    cb = B * T // C
    row = lambda b: (cb + b, 0)
    return pl.pallas_call(
        _ctx_kernel, grid=(B,),
        in_specs=[pl.BlockSpec(memory_space=pltpu.SMEM),
                  pl.BlockSpec((C, 768), row), pl.BlockSpec((C, 512), row), pl.BlockSpec((C, 768), row)],
        out_specs=[pl.BlockSpec((C, 256), lambda b: (b, 0))] * 3,
        out_shape=[jax.ShapeDtypeStruct((B * C, 256), BF16)] * 3,
        compiler_params=_cparams(("arbitrary",)), name="ctx_attention",
    )(sink, pa, pb, pc)


def _gla_dir(pd_ref, o_ref, st_ref, reverse):
    nchunk = GLA_BLOCK // GLA_CHUNK
    L, DK, DV = GLA_CHUNK, N_HEADS * GLA_DK, N_HEADS * GLA_DV

    def iota(shape, dim):
        return lax.broadcasted_iota(I32, shape, dim)

    ri, ci = iota((L, L), 0), iota((L, L), 1)
    tri = jnp.where((ci >= ri) if reverse else (ci <= ri), 1.0, 0.0).astype(BF16)
    tri_t = jnp.where((ri >= ci) if reverse else (ri <= ci), 1.0, 0.0).astype(BF16)
    kcol, qrow = iota((L, N_HEADS * L), 1) % L, iota((L, N_HEADS * L), 0)
    keep = (kcol >= qrow) if reverse else (kcol <= qrow)
    k_owner = [iota((L, DK), 1) // GLA_DK == h for h in range(N_HEADS)]
    v_owner = [iota((L, DV), 1) // GLA_DV == h for h in range(N_HEADS)]
    s_owner = iota((DK, DV), 0) // GLA_DK == iota((DK, DV), 1) // GLA_DV
    gcol = 896 if reverse else 768
    end = 0 if reverse else L - 1
    order = range(nchunk - 1, -1, -1) if reverse else range(nchunk)

    def dot3(a_parts, b_parts):
        return sum(jnp.dot(a, b, preferred_element_type=F32) for a in a_parts for b in b_parts)

    per_chunk = []
    for c in order:
        rs = slice(c * L, (c + 1) * L)
        g = pd_ref[rs, gcol:gcol + DK]
        q = pd_ref[rs, 0:DK]
        k = pd_ref[rs, DK:2 * DK]
        vb = pd_ref[rs, 2 * DK:2 * DK + DV].astype(BF16)
        b = dot3([tri], _split3(g))
        bt = dot3(_split3(g.T), [tri_t])
        bend = bt[:, end:end + 1]
        qd = (q * jnp.exp(b)).astype(BF16)
        ki = (k * jnp.exp(-b)).astype(BF16)
        kdt = (k.T * jnp.exp(bend - bt)).astype(BF16)
        kbd = jnp.concatenate([jnp.where(m, ki, jnp.zeros_like(ki)) for m in k_owner], 0)
        vbd = jnp.concatenate([jnp.where(m, vb, jnp.zeros_like(vb)) for m in v_owner], 0)
        att = lax.dot_general(qd, kbd, _NT, preferred_element_type=F32)
        att = jnp.where(keep, att, 0.0).astype(BF16)
        intra = jnp.dot(att, vbd, preferred_element_type=F32)
        u = jnp.where(s_owner, jnp.dot(kdt, vb, preferred_element_type=F32), 0.0)
        per_chunk.append((rs, qd, intra, u, jnp.exp(bend)))
    st = st_ref[...]
    for rs, qd, intra, u, dec in per_chunk:
        o_ref[rs, :] = intra + jnp.dot(qd, st.astype(BF16), preferred_element_type=F32)
        st = st * dec + u
    st_ref[...] = st


def _gla_kernel(pdf_ref, pdb_ref, of_ref, ob_ref, stf_ref, stb_ref):
    @pl.when(pl.program_id(1) == 0)
    def _():
        stf_ref[...] = jnp.zeros_like(stf_ref)
        stb_ref[...] = jnp.zeros_like(stb_ref)

    _gla_dir(pdf_ref, of_ref, stf_ref, False)
    _gla_dir(pdb_ref, ob_ref, stb_ref, True)


def _gla(pd, *, B, T, C):
    N = pd.shape[0]
    assert C == GLA_BLOCK
    nlb = T // GLA_BLOCK
    cb = B * T // GLA_BLOCK

    def fwd(b, s):
        return (jnp.where(s == 0, cb + b, b * nlb + s - 1), 0)

    def bwd(b, s):
        return (jnp.where(s == 0, cb + b, b * nlb + nlb - s), 0)

    state = pltpu.VMEM((N_HEADS * GLA_DK, N_HEADS * GLA_DV), F32)
    return pl.pallas_call(
        _gla_kernel, grid=(B, nlb + 1),
        in_specs=[pl.BlockSpec((GLA_BLOCK, 1024), fwd), pl.BlockSpec((GLA_BLOCK, 1024), bwd)],
        out_specs=[pl.BlockSpec((GLA_BLOCK, 256), fwd), pl.BlockSpec((GLA_BLOCK, 256), bwd)],
        out_shape=[jax.ShapeDtypeStruct((N, 256), F32)] * 2,
        scratch_shapes=[state, state],
        compiler_params=_cparams(("arbitrary", "arbitrary")), name="gla_scan",
    )(pd, pd)


def _pack_bf16_pairs(h):
    w = h.shape[-1] // 2
    bits = lax.bitcast_convert_type(h.astype(BF16).astype(F32), U32)
    return (bits[:, w:] & jnp.uint32(0xFFFF0000)) | (bits[:, :w] >> 16)


def _unpack_bf16_pairs(u):
    lo = lax.bitcast_convert_type(u << 16, F32)
    hi = lax.bitcast_convert_type(u & jnp.uint32(0xFFFF0000), F32)
    return jnp.concatenate([lo, hi], -1).astype(BF16)


def _outproj_kernel(x_ref, oa_ref, ob_ref, oc_ref, oac_ref, obc_ref, occ_ref, gf_ref, gb_ref, r_ref, gn_ref, wout_ref,
                    g1_ref, sh2_ref, sc2_ref, lng_ref, lnb_ref, rw1_ref, rw2_ref, rb_ref,
                    x1_ref, hp_ref, sel_ref, gate_ref, rank_ref, count_ref, cnt_ref, *, alpha, nlat_t, tile0):
    is_lat = pl.program_id(0) + tile0 < nlat_t
    pick = lambda lat_ref, ctx_ref: jnp.where(is_lat, lat_ref[...], ctx_ref[...])
    s = gf_ref[...] + gb_ref[...]
    w = s.shape[-1]
    same_head = (lax.broadcasted_iota(I32, (w, w), 0) // GLA_DV) == (lax.broadcasted_iota(I32, (w, w), 1) // GLA_DV)
    avg = jnp.where(same_head, 1.0 / GLA_DV, 0.0).astype(BF16)
    ms = sum(jnp.dot(part, avg, preferred_element_type=F32) for part in _split3(s * s))
    r = r_ref[...]
    od = s * lax.rsqrt(ms + RMS_EPS) * gn_ref[...] * (r * _sigmoid(r))
    ocat = jnp.concatenate([pick(oa_ref, oac_ref), pick(ob_ref, obc_ref), pick(oc_ref, occ_ref), od.astype(BF16)], -1)
    o = jnp.dot(ocat, wout_ref[...], preferred_element_type=F32)
    x1 = _layer_norm(alpha * x_ref[...] + g1_ref[0] * o) * lng_ref[...] + lnb_ref[...]
    x1_ref[...] = x1
    h2 = _layer_norm(x1) * (1.0 + sc2_ref[0]) + sh2_ref[0]
    hp_ref[...] = _pack_bf16_pairs(h2)
    a1, a2, _ = _split3(h2)
    w1, w2 = rw1_ref[...], rw2_ref[...]
    logits = (jnp.dot(a1, w1, preferred_element_type=F32) + jnp.dot(a1, w2, preferred_element_type=F32)
              + jnp.dot(a2, w1, preferred_element_type=F32))
    scores = _sigmoid(logits)
    sc_t = scores.T
    work = (scores + rb_ref[...]).T
    ne, tm = sc_t.shape
    row = lax.broadcasted_iota(I32, (ne, tm), 0).astype(F32)
    chosen = jnp.zeros((ne, tm), F32)
    picks, picked = [], []
    for k in range(TOP_K):
        m = jnp.max(work, 0, keepdims=True)
        idx = jnp.min(jnp.where(work == m, row, float(ne)), 0, keepdims=True)
        hit = row == idx
        picks.append(idx)
        picked.append(jnp.sum(jnp.where(hit, sc_t, 0.0), 0, keepdims=True))
        chosen = jnp.where(hit, 1.0, chosen)
        work = jnp.where(hit, -jnp.inf, work)
    gsel = jnp.concatenate(picked, 0)
    sel_ref[...] = jnp.concatenate(picks, 0).astype(I32)
    gate_ref[...] = ROUTED_SCALE * gsel / jnp.sum(gsel, 0, keepdims=True)
    @pl.when(pl.program_id(0) == 0)
    def _():
        cnt_ref[...] = jnp.zeros_like(cnt_ref)

    earlier = (lax.broadcasted_iota(I32, (tm, tm), 0) < lax.broadcasted_iota(I32, (tm, tm), 1))
    prefix = jnp.dot(chosen.astype(BF16), jnp.where(earlier, 1.0, 0.0).astype(BF16),
                     preferred_element_type=F32) + cnt_ref[...]
    rank_ref[...] = jnp.concatenate(
        [jnp.sum(jnp.where(row == picks[k], prefix, 0.0), 0, keepdims=True) for k in range(TOP_K)], 0).astype(I32)
    cnt_ref[...] = cnt_ref[...] + jnp.sum(chosen, 1, keepdims=True)
    count_ref[...] = cnt_ref[...].astype(I32)


def _outproj(xall, o_lat, o_ctx, ofb, pd, gn, wout, mods_l, lng, lnb, rw1, rw2, rb, *, n_lat, T, alpha, tile0, ntiles):
    D = xall.shape[1]
    tm = TOK_TILE
    N = ntiles * tm
    nlat_t, per_b, nb = n_lat // tm, T // tm, n_lat // T

    def midx(k):
        return lambda i: (jnp.where(i + tile0 < nlat_t, (i + tile0) // per_b, nb) * 6 + k, 0, 0)

    const = lambda i: (0, 0)
    row = lambda i: (i, 0)
    src_row = lambda i: (i + tile0, 0)
    lat_row = lambda i: (jnp.minimum(i + tile0, nlat_t - 1), 0)
    ctx_row = lambda i: (jnp.maximum(i + tile0 - nlat_t, 0), 0)
    E = rw1.shape[1]
    kern = functools.partial(_outproj_kernel, alpha=alpha, nlat_t=nlat_t, tile0=tile0)
    return pl.pallas_call(
        kern, grid=(ntiles,),
        in_specs=[pl.BlockSpec((tm, D), src_row)] + [pl.BlockSpec((tm, 256), lat_row)] * 3
                 + [pl.BlockSpec((tm, 256), ctx_row)] * 3 + [
                  pl.BlockSpec((tm, 256), src_row), pl.BlockSpec((tm, 256), src_row),
                  pl.BlockSpec((tm, 256), lambda i: (i + tile0, 2)), pl.BlockSpec((1, 256), const),
                  pl.BlockSpec((D, D), const),
                  pl.BlockSpec((1, 1, D), midx(2)), pl.BlockSpec((1, 1, D), midx(3)), pl.BlockSpec((1, 1, D), midx(4)),
                  pl.BlockSpec((1, D), const), pl.BlockSpec((1, D), const),
                  pl.BlockSpec((D, E), const), pl.BlockSpec((D, E), const), pl.BlockSpec((1, E), const)],
        out_specs=[pl.BlockSpec((tm, D), row), pl.BlockSpec((tm, D // 2), row)]
                  + [pl.BlockSpec((TOP_K, tm), lambda i: (0, i))] * 3 + [pl.BlockSpec((E, 1), const)],
        out_shape=[jax.ShapeDtypeStruct((N, D), F32), jax.ShapeDtypeStruct((N, D // 2), U32),
                   jax.ShapeDtypeStruct((TOP_K, N), I32), jax.ShapeDtypeStruct((TOP_K, N), F32),
                   jax.ShapeDtypeStruct((TOP_K, N), I32), jax.ShapeDtypeStruct((E, 1), I32)],
        scratch_shapes=[pltpu.VMEM((E, 1), F32)],
        compiler_params=_cparams(("arbitrary",)), name="outproj_router",
    )(xall, *o_lat, *o_ctx, *ofb, pd, gn, wout, mods_l, mods_l, mods_l, lng, lnb, rw1, rw2, rb)


def _sc_mesh():
    return plsc.VectorSubcoreMesh(core_axis_name="c", subcore_axis_name="s")


def _sc_worker():
    return lax.axis_index("s") * SC_CORES + lax.axis_index("c")


def _sc_dispatch(hp, dest_k, rows):
    N, W = hp.shape
    tpw = N // SC_WORKERS
    ch = SC_DISPATCH_CHUNK
    assert N % SC_WORKERS == 0 and tpw % ch == 0
    nch = tpw // ch
    idx = dest_k.reshape(TOP_K, SC_WORKERS, nch, ch).transpose(1, 2, 0, 3)

    @functools.partial(
        pl.kernel, mesh=_sc_mesh(), out_type=jax.ShapeDtypeStruct((rows, W), hp.dtype),
        scratch_types=[pltpu.VMEM((nch, TOP_K, ch), I32), pltpu.VMEM((ch, W), hp.dtype), pltpu.SemaphoreType.DMA],
        compiler_params=pltpu.CompilerParams(use_tc_tiling_on_sc=True), name="sc_dispatch")
    def scatter(hp_hbm, idx_hbm, out_hbm, idx_v, rows_v, sem):
        wid = _sc_worker()
        base = wid * tpw
        pltpu.sync_copy(idx_hbm.at[wid], idx_v)

        @pl.loop(0, nch)
        def _(c):
            pltpu.sync_copy(hp_hbm.at[pl.ds(base + c * ch, ch)], rows_v)
            copies = [pltpu.async_copy(rows_v, out_hbm.at[idx_v.at[c, k]], sem) for k in range(TOP_K)]
            for cp in copies:
                cp.wait()

    return scatter(hp, idx)


def _sc_gather(src, idx):
    R, W = idx.shape[0], src.shape[1]
    per_w = R // SC_WORKERS
    ch = SC_GATHER_CHUNK
    assert R % SC_WORKERS == 0 and per_w % ch == 0
    nch = per_w // ch

    @functools.partial(
        pl.kernel, mesh=_sc_mesh(), out_type=jax.ShapeDtypeStruct((R, W), src.dtype),
        scratch_types=[pltpu.VMEM((nch, ch), I32), pltpu.VMEM((ch, W), src.dtype), pltpu.SemaphoreType.DMA],
        compiler_params=pltpu.CompilerParams(use_tc_tiling_on_sc=True), name="sc_gather")
    def gather(src_hbm, idx_hbm, out_hbm, idx_v, rows_v, sem):
        wid = _sc_worker()
        base = wid * per_w
        pltpu.sync_copy(idx_hbm.at[wid], idx_v)

        @pl.loop(0, nch)
        def _(c):
            pltpu.async_copy(src_hbm.at[idx_v.at[c]], rows_v, sem).wait()
            pltpu.sync_copy(rows_v, out_hbm.at[pl.ds(base + c * ch, ch)])

    return gather(src, idx.reshape(SC_WORKERS, nch, ch))


def _swiglu(xb, w1, w3, w2):
    a = jnp.dot(xb, w1, preferred_element_type=F32)
    b = jnp.dot(xb, w3, preferred_element_type=F32)
    return jnp.dot((a * _sigmoid(a) * b).astype(BF16), w2, preferred_element_type=F32)


def _expert_kernel(be_ref, nu_ref, xs_hbm, w1_ref, w3_ref, w2_ref, y_ref, w1b_ref, w3b_ref, w2b_ref, xbuf, sem):
    i = pl.program_id(0)
    n_used = nu_ref[0]
    used = i < n_used

    def fetch(j):
        slot = j % MOE_RING
        return pltpu.make_async_copy(xs_hbm.at[pl.ds(pl.multiple_of(j * MOE_BM, MOE_BM), MOE_BM)],
                                     xbuf.at[slot], sem.at[slot])

    for j in range(MOE_RING - 1):
        @pl.when((i == 0) & (j < n_used))
        def _(j=j):
            fetch(j).start()

    @pl.when(i + (MOE_RING - 1) < n_used)
    def _():
        fetch(i + (MOE_RING - 1)).start()

    @pl.when(used & ((i == 0) | (be_ref[i] != be_ref[jnp.maximum(i - 1, 0)])))
    def _():
        w1b_ref[...] = w1_ref[0, 0].astype(BF16)
        w3b_ref[...] = w3_ref[0, 0].astype(BF16)
        w2b_ref[...] = w2_ref[0, 0].astype(BF16)

    @pl.when(used)
    def _():
        fetch(i).wait()
        xb = _unpack_bf16_pairs(xbuf[i % MOE_RING])
        y_ref[...] = _pack_bf16_pairs(_swiglu(xb, w1b_ref[...], w3b_ref[...], w2b_ref[...]))

    @pl.when(jnp.logical_not(used))
    def _():
        y_ref[...] = jnp.zeros_like(y_ref)


def _experts(xs, blk_exp, n_used, w1, w3, w2, layer):
    rows, half = xs.shape
    _, E, D, F = w1.shape
    nblk = rows // MOE_BM
    wmap = lambda i, be, nu: (layer, be[i], 0, 0)
    grid_spec = pltpu.PrefetchScalarGridSpec(
        num_scalar_prefetch=2, grid=(nblk,),
        in_specs=[pl.BlockSpec(memory_space=pl.ANY),
                  pl.BlockSpec((1, 1, D, F), wmap), pl.BlockSpec((1, 1, D, F), wmap), pl.BlockSpec((1, 1, F, D), wmap)],
        out_specs=pl.BlockSpec((MOE_BM, half), lambda i, be, nu: (jnp.minimum(i, nu[0]), 0)),
        scratch_shapes=[pltpu.VMEM((D, F), BF16), pltpu.VMEM((D, F), BF16), pltpu.VMEM((F, D), BF16),
                        pltpu.VMEM((MOE_RING, MOE_BM, half), U32), pltpu.SemaphoreType.DMA((MOE_RING,))])
    return pl.pallas_call(
        _expert_kernel, grid_spec=grid_spec,
        out_shape=jax.ShapeDtypeStruct((rows, half), U32),
        compiler_params=_cparams(("arbitrary",)), name="expert_ffn",
    )(blk_exp, n_used, xs, w1, w3, w2)


def _combine_kernel(yg_ref, gate_ref, hp_ref, sw1_ref, sw3_ref, sw2_ref, x_ref, g2_ref, lng_ref, lnb_ref,
                    o_ref, *, alpha):
    f = _swiglu(_unpack_bf16_pairs(hp_ref[...]), sw1_ref[...], sw3_ref[...], sw2_ref[...])
    gates = gate_ref[...]
    lo = hi = 0.0
    for k in range(TOP_K):
        u = yg_ref[k]
        g = gates[:, k:k + 1]
        lo = lo + g * lax.bitcast_convert_type(u << 16, F32)
        hi = hi + g * lax.bitcast_convert_type(u & jnp.uint32(0xFFFF0000), F32)
    f = f + jnp.concatenate([lo, hi], -1)
    o_ref[...] = _layer_norm(alpha * x_ref[...] + g2_ref[0] * f) * lng_ref[...] + lnb_ref[...]


def _combine(yg, gates, hp, sw1, sw3, sw2, x1, mods_l, lng, lnb, *, n_lat, T, alpha, n_out, tok0):
    D = x1.shape[1]
    N = n_out
    tm = CMB_TILE
    nlat_t, per_b, nb = n_lat // tm, T // tm, n_lat // T
    t0 = tok0 // tm
    F = sw1.shape[1]
    const = lambda i: (0, 0)
    row = lambda i: (i, 0)
    kern = functools.partial(_combine_kernel, alpha=alpha)
    return pl.pallas_call(
        kern, grid=(N // tm,),
        in_specs=[pl.BlockSpec((TOP_K, tm, D // 2), lambda i: (0, i, 0)), pl.BlockSpec((tm, TOP_K), row),
                  pl.BlockSpec((tm, D // 2), row),
                  pl.BlockSpec((D, F), const), pl.BlockSpec((D, F), const), pl.BlockSpec((F, D), const),
                  pl.BlockSpec((tm, D), row),
                  pl.BlockSpec((1, 1, D), lambda i: (jnp.where(i + t0 < nlat_t, (i + t0) // per_b, nb) * 6 + 5, 0, 0)),
                  pl.BlockSpec((1, D), const), pl.BlockSpec((1, D), const)],
        out_specs=pl.BlockSpec((tm, D), row),
        out_shape=jax.ShapeDtypeStruct((N, D), F32),
        compiler_params=_cparams(("arbitrary",)), name="moe_combine",
    )(yg, gates, hp, sw1, sw3, sw2, x1, mods_l, lng, lnb)


def _dest_kernel(start_ref, sel_ref, rank_ref, o_ref):
    sel = sel_ref[...]

    def body(e, acc):
        return jnp.where(sel == e, start_ref[e], acc)

    o_ref[...] = lax.fori_loop(0, start_ref.shape[0], body, jnp.zeros_like(sel)) + rank_ref[...]


def _dispatch_plan(sel, rank, counts):
    N = sel.shape[1]
    n_experts = counts.shape[0]
    padded = (counts + MOE_BM - 1) // MOE_BM * MOE_BM
    pad_end = jnp.cumsum(padded)
    pad_start = (pad_end - padded).astype(I32)
    dense = (N * TOP_K // 128, 128)
    dest = pl.pallas_call(
        _dest_kernel,
        in_specs=[pl.BlockSpec(memory_space=pltpu.SMEM), pl.BlockSpec(memory_space=pltpu.VMEM),
                  pl.BlockSpec(memory_space=pltpu.VMEM)],
        out_specs=pl.BlockSpec(memory_space=pltpu.VMEM),
        out_shape=jax.ShapeDtypeStruct(dense, I32), name="moe_dest_rows",
    )(pad_start, sel.reshape(dense), rank.reshape(dense))
    dest_k = dest.reshape(TOP_K, N)
    n_blocks = -(-(N * TOP_K) // MOE_BM) + n_experts
    blk_first = jnp.arange(n_blocks, dtype=I32) * MOE_BM
    blk_exp = jnp.minimum(jnp.sum((pad_end[None, :] <= blk_first[:, None]).astype(I32), -1), n_experts - 1)
    n_used = (pad_end[-1] // MOE_BM).astype(I32).reshape(1)
    return dest_k, blk_exp, n_used, n_blocks * MOE_BM


def _rope_tables(T, tile):
    t = jnp.arange(T)
    row = (t // GRID_W).astype(F32)[:, None]
    col = (t % GRID_W).astype(F32)[:, None]

    def group(half):
        inv = ROPE_THETA ** (-jnp.arange(half, dtype=F32) / half)
        ar, ac = row * inv, col * inv
        cos = jnp.concatenate([jnp.cos(ar), jnp.cos(ar), jnp.cos(ac), jnp.cos(ac)], -1)
        sin = jnp.concatenate([-jnp.sin(ar), jnp.sin(ar), -jnp.sin(ac), jnp.sin(ac)], -1)
        return cos, sin

    ones32, zeros32 = jnp.ones((T, 32), F32), jnp.zeros((T, 32), F32)
    cw, sw = group(16)
    cw, sw = jnp.tile(cw, (1, 2)), jnp.tile(sw, (1, 2))
    c8, s8 = group(8)
    cq = jnp.tile(jnp.concatenate([ones32, c8], -1), (1, 2))
    sq = jnp.tile(jnp.concatenate([zeros32, s8], -1), (1, 2))
    ck = jnp.concatenate([c8, ones32, ones32, ones32], -1)
    sk = jnp.concatenate([s8, zeros32, zeros32, zeros32], -1)

    def pad(a, ident):
        return jnp.concatenate([a, jnp.full((tile, 128), ident, F32)], 0)

    return [pad(cw, 1.0), pad(sw, 0.0), pad(cq, 1.0), pad(sq, 0.0), pad(ck, 1.0), pad(sk, 0.0)]


def _layer_weights(w_in, mla_w_uq, mla_w_ukv, gla_w_gf, gla_w_gb, gla_b_gf, gla_b_gb):
    D = w_in.shape[0]
    z = lambda n: jnp.zeros((D, n), F32)
    win = jnp.concatenate([
        w_in[:, 0:256] * Q_SCALE, w_in[:, 256:768],
        w_in[:, 768:1024] * Q_SCALE, w_in[:, 1024:1280],
        w_in[:, 1280:1696], z(96),
        w_in[:, 1696:2464],
        w_in[:, 2464:2496], z(96)], -1).astype(BF16)
    wuq = (mla_w_uq * Q_SCALE).astype(BF16)
    ukv = mla_w_ukv.reshape(-1, N_HEADS, MLA_NOPE + HEAD_DIM)
    kv_rank = ukv.shape[0]
    wk = jnp.concatenate([ukv[:, :, :MLA_NOPE], jnp.zeros((kv_rank, N_HEADS, MLA_ROPE), F32)], -1)
    place = jnp.concatenate([jnp.zeros((MLA_ROPE, MLA_NOPE), F32), jnp.eye(MLA_ROPE, dtype=F32)], -1)
    place = jnp.concatenate([jnp.tile(place, (1, N_HEADS)), jnp.zeros((128 - MLA_ROPE, 256), F32)], 0)
    wkk = jnp.concatenate([wk.reshape(kv_rank, 256), place], 0).astype(BF16)
    wv = ukv[:, :, MLA_NOPE:].reshape(kv_rank, 256).astype(BF16)
    r = GLA_GATE_RANK
    wg = jnp.zeros((128, 256), F32).at[0:r, 0:128].set(gla_w_gf).at[r:2 * r, 128:256].set(gla_w_gb).astype(BF16)
    bg = jnp.concatenate([gla_b_gf, gla_b_gb])[None, :]
    return win, wuq, wkk, wv, wg, bg


def kernel(x, c, ctx, c_ctx, w_ada, b_ada, w_in, na_rpb, wa_sink, mla_g_q, mla_g_kv, mla_w_uq, mla_w_ukv,
           gla_w_gf, gla_b_gf, gla_w_gb, gla_b_gb, gla_g_norm, w_out, ln1_g, ln1_b, ln2_g, ln2_b,
           router_w, router_bias, exp_w1, exp_w3, exp_w2, sh_w1, sh_w3, sh_w2):
    B, T, D = x.shape
    C = ctx.shape[1]
    L = w_ada.shape[0]
    E = router_w.shape[-1]
    n_lat = B * T
    alpha = (2 * L) ** 0.25
    dims = dict(B=B, T=T, C=C)

    cc = jnp.zeros((8, D), F32).at[:B].set(c).at[B].set(c_ctx)
    mods = _mods(cc, w_ada, b_ada)
    tabs = _rope_tables(T, TOK_TILE)
    xall = jnp.concatenate([x.reshape(n_lat, D), ctx.reshape(B * C, D)], 0)

    for l in range(L):
        mods_l = mods[l].reshape(8 * 6, 1, D)
        win, wuq, wkk, wv, wg, bg = _layer_weights(w_in[l], mla_w_uq[l], mla_w_ukv[l], gla_w_gf[l], gla_w_gb[l],
                                                   gla_b_gf[l], gla_b_gb[l])
        pa, pb, pc, pd = _inproj(xall, mods_l, win, tabs, mla_g_q[l][None], mla_g_kv[l][None], wuq, wkk, wv, wg, bg,
                                 n_lat=n_lat, T=T)
        oa = _na_attention(pa, _na_bias_tables(na_rpb[l], T // GRID_W), **dims)
        sink = wa_sink[l] * LOG2E
        ob = _wa_attention(pb, sink, **dims)
        oc = _mla_attention(pc, **dims)
        o_ctx = _ctx_attention(sink, pa, pb, pc, **dims)
        ofb = _gla(pd, **dims)

        rw1 = router_w[l].astype(BF16)
        rw2 = (router_w[l] - rw1.astype(F32)).astype(BF16)
        n_all = xall.shape[0]
        part = n_all // MOE_SPLITS
        assert n_all % (MOE_SPLITS * TOK_TILE) == 0
        shared = (sh_w1[l].astype(BF16), sh_w3[l].astype(BF16), sh_w2[l].astype(BF16))
        gn, wout = jnp.tile(gla_g_norm[l], N_HEADS)[None], w_out[l].astype(BF16)
        routed = []
        for p in range(MOE_SPLITS):
            x1, hp, sel, gates, rank, counts = _outproj(
                xall, (oa, ob, oc), o_ctx, ofb, pd, gn, wout, mods_l, ln1_g[l][None], ln1_b[l][None], rw1, rw2,
                router_bias[l][None], n_lat=n_lat, T=T, alpha=alpha,
                tile0=p * part // TOK_TILE, ntiles=part // TOK_TILE)
            dest_k, blk_exp, n_used, rows = _dispatch_plan(sel, rank, counts.reshape(E))
            routed.append((x1, hp, gates.T, dest_k, blk_exp, n_used, _sc_dispatch(hp, dest_k, rows)))
        outs = []
        for p, (x1, hp, gates, dest_k, blk_exp, n_used, xs) in enumerate(routed):
            y = _experts(xs, blk_exp, n_used, exp_w1, exp_w3, exp_w2, l)
            yg = _sc_gather(y, dest_k.reshape(-1)).reshape(TOP_K, part, D // 2)
            n_out = part if l < L - 1 else max(min(n_lat - p * part, part), 0)
            if n_out:
                outs.append(_combine(yg, gates, hp, *shared, x1, mods_l, ln2_g[l][None], ln2_b[l][None], n_lat=n_lat,
                                     T=T, alpha=alpha, n_out=n_out, tok0=p * part))
        xall = jnp.concatenate(outs, 0)

    return xall.reshape(B, T, D)
```

```python
import functools

import numpy as np
import jax
import jax.numpy as jnp
from jax import lax
from jax.experimental import pallas as pl
from jax.experimental.pallas import tpu as pltpu
from jax.experimental.pallas import tpu_sc as plsc

F32 = jnp.float32
BF16 = jnp.bfloat16
U32 = jnp.uint32
I32 = jnp.int32

GRID_W = 64
HEAD_DIM = 64
N_HEADS = 4
NA_KH, NA_KW = 8, 16
WA_WINDOW = 128
WA_SUB = 256
MLA_NOPE, MLA_ROPE = 32, 32
GLA_DK, GLA_DV = 32, 64
GLA_GATE_RANK = 16
GLA_GATE_NORM = 16.0
GLA_CHUNK = 64
TOP_K = 8
ROUTED_SCALE = 2.5
ROPE_THETA = 10000.0
LN_EPS = 1e-5
RMS_EPS = 1e-6
NEG = -1e30
LOG2E = 1.4426950408889634
Q_SCALE = HEAD_DIM ** -0.5 * LOG2E

VMEM_LIMIT = 56 * 1024 * 1024
TOK_TILE = 512
ATT_TQ = 512
MLA_TQ = 1024
MLA_TK = 1024
GLA_BLOCK = 256
MOE_BM = 1024
MOE_RING = 3
MOE_SPLITS = 1
CMB_TILE = 256
SC_CORES = 2
SC_WORKERS = 32
SC_DISPATCH_CHUNK = 48
SC_GATHER_CHUNK = 64

PW = 2688
_NT = (((1,), (1,)), ((), ()))


def _cparams(sem, vmem=VMEM_LIMIT):
    return pltpu.CompilerParams(dimension_semantics=sem, vmem_limit_bytes=vmem)


def _sigmoid(x):
    return 1.0 / (1.0 + jnp.exp(-x))


def _layer_norm(x):
    mu = jnp.mean(x, -1, keepdims=True)
    xc = x - mu
    var = jnp.mean(xc * xc, -1, keepdims=True)
    return xc * lax.rsqrt(var + LN_EPS)


def _rms(x):
    return x * lax.rsqrt(jnp.mean(x * x, -1, keepdims=True) + RMS_EPS)


def _bdot(a, b):
    return jnp.dot(a.astype(BF16), b.astype(BF16), preferred_element_type=F32)


def _bdot_nt(a, b):
    return lax.dot_general(a.astype(BF16), b.astype(BF16), _NT, preferred_element_type=F32)


def _split3(a):
    a1 = a.astype(BF16)
    r = a - a1.astype(F32)
    a2 = r.astype(BF16)
    a3 = (r - a2.astype(F32)).astype(BF16)
    return a1, a2, a3


def _mods_kernel(cc_ref, w_ref, b_ref, o_ref):
    cc = cc_ref[...]
    s = cc * _sigmoid(cc)
    o_ref[0] = _bdot(s, w_ref[0]) + b_ref[0]


def _mods(cc, w_ada, b_ada):
    L, D, W = w_ada.shape
    tn = 1536
    return pl.pallas_call(
        _mods_kernel,
        grid=(L, W // tn),
        in_specs=[pl.BlockSpec((8, D), lambda l, j: (0, 0)),
                  pl.BlockSpec((1, D, tn), lambda l, j: (l, 0, j)),
                  pl.BlockSpec((1, 1, tn), lambda l, j: (l, 0, j))],
        out_specs=pl.BlockSpec((1, 8, tn), lambda l, j: (l, 0, j)),
        out_shape=jax.ShapeDtypeStruct((L, 8, W), F32),
        compiler_params=_cparams(("arbitrary", "arbitrary")),
        name="adaln_mods",
    )(cc, w_ada, b_ada.reshape(L, 1, W))


def _rope(z, cos, sin, half):
    w = z.shape[-1]
    lane = lax.broadcasted_iota(I32, z.shape, 1)
    first = (lane % (2 * half)) < half
    partner = jnp.where(first, pltpu.roll(z, w - half, 1), pltpu.roll(z, half, 1))
    return z * cos + partner * sin


def _log_sigmoid(x):
    return jnp.minimum(x, 0.0) - jnp.log(1.0 + jnp.exp(-jnp.abs(x)))


def _inproj_kernel(x_ref, shift_ref, scale_ref, win_ref, cw_ref, sw_ref, cq_ref, sq_ref, ck_ref, sk_ref,
                   gq_ref, gkv_ref, wuq_ref, wkk_ref, wv_ref, wg_ref, bg_ref,
                   pa_ref, pb_ref, pc_ref, pd_ref):
    h = _layer_norm(x_ref[...]) * (1.0 + scale_ref[0]) + shift_ref[0]
    p = jnp.dot(h.astype(BF16), win_ref[...], preferred_element_type=F32)
    pa_ref[...] = p[:, 0:768].astype(BF16)
    cw, sw = cw_ref[...], sw_ref[...]
    pb_ref[:, 0:128] = _rope(p[:, 768:896], cw, sw, 16).astype(BF16)
    pb_ref[:, 128:256] = _rope(p[:, 896:1024], cw, sw, 16).astype(BF16)
    pb_ref[:, 256:384] = _rope(p[:, 1024:1152], cw, sw, 16).astype(BF16)
    pb_ref[:, 384:512] = p[:, 1152:1280].astype(BF16)
    cqn = _rms(p[:, 1280:1536]) * gq_ref[...]
    q = jnp.dot(cqn.astype(BF16), wuq_ref[...], preferred_element_type=F32)
    cq, sq = cq_ref[...], sq_ref[...]
    pc_ref[:, 0:128] = _rope(q[:, 0:128], cq, sq, 8).astype(BF16)
    pc_ref[:, 128:256] = _rope(q[:, 128:256], cq, sq, 8).astype(BF16)
    ckvn = (_rms(p[:, 1536:1664]) * gkv_ref[...]).astype(BF16)
    kr = _rope(p[:, 1664:1792], ck_ref[...], sk_ref[...], 8).astype(BF16)
    kin = jnp.concatenate([ckvn, kr], axis=-1)
    pc_ref[:, 256:512] = jnp.dot(kin, wkk_ref[...], preferred_element_type=F32).astype(BF16)
    pc_ref[:, 512:768] = jnp.dot(ckvn, wv_ref[...], preferred_element_type=F32).astype(BF16)
    pd_ref[:, 0:128] = p[:, 1792:1920] * (GLA_DK ** -0.5)
    pd_ref[:, 128:768] = p[:, 1920:2560]
    pre = jnp.dot(p[:, 2560:2688].astype(BF16), wg_ref[...], preferred_element_type=F32) + bg_ref[...]
    pd_ref[:, 768:1024] = _log_sigmoid(pre) * (1.0 / GLA_GATE_NORM)


def _inproj(xall, mods_l, win, tabs, gq, gkv, wuq, wkk, wv, wg, bg, *, n_lat, T):
    N, D = xall.shape
    tm = TOK_TILE
    nlat_t = n_lat // tm
    per_b = T // tm
    nb = n_lat // T

    def midx(k):
        return lambda i: (jnp.where(i < nlat_t, i // per_b, nb) * 6 + k, 0, 0)

    def tidx(i):
        return (jnp.where(i < nlat_t, i % per_b, per_b), 0)

    const = lambda i: (0, 0)
    tab_spec = pl.BlockSpec((tm, 128), tidx)
    in_specs = [pl.BlockSpec((tm, D), lambda i: (i, 0)),
                pl.BlockSpec((1, 1, D), midx(0)), pl.BlockSpec((1, 1, D), midx(1)),
                pl.BlockSpec((D, PW), const)] + [tab_spec] * 6 + [
                pl.BlockSpec((1, 256), const), pl.BlockSpec((1, 128), const),
                pl.BlockSpec((256, 256), const), pl.BlockSpec((256, 256), const),
                pl.BlockSpec((128, 256), const), pl.BlockSpec((128, 256), const),
                pl.BlockSpec((1, 256), const)]
    out_specs = [pl.BlockSpec((tm, 768), lambda i: (i, 0)), pl.BlockSpec((tm, 512), lambda i: (i, 0)),
                 pl.BlockSpec((tm, 768), lambda i: (i, 0)), pl.BlockSpec((tm, 1024), lambda i: (i, 0))]
    out_shape = [jax.ShapeDtypeStruct((N, 768), BF16), jax.ShapeDtypeStruct((N, 512), BF16),
                 jax.ShapeDtypeStruct((N, 768), BF16), jax.ShapeDtypeStruct((N, 1024), F32)]
    return pl.pallas_call(
        _inproj_kernel, grid=(N // tm,), in_specs=in_specs, out_specs=out_specs, out_shape=out_shape,
        compiler_params=_cparams(("arbitrary",)), name="inproj",
    )(xall, mods_l, mods_l, win, *tabs, gq, gkv, wuq, wkk, wv, wg, bg)


def _softmax_av(parts, extra_logit=None):
    m = functools.reduce(jnp.maximum, [jnp.max(s, -1, keepdims=True) for s, _ in parts])
    if extra_logit is not None:
        m = jnp.maximum(m, extra_logit)
    l = 0.0 if extra_logit is None else jnp.exp2(extra_logit - m)
    o = None
    for s, v in parts:
        e = jnp.exp2(s - m)
        l = l + jnp.sum(e, -1, keepdims=True)
        c = jnp.dot(e.astype(BF16), v, preferred_element_type=F32)
        o = c if o is None else o + c
    return o * (1.0 / l)


def _na_kernel(q_ref, k_ref, v_ref, kc_ref, vc_ref, bias_ref, o_ref, *, rows):
    j = pl.program_id(1)
    ws = pl.multiple_of(jnp.clip(8 * j - 4, 0, rows - 16) * GRID_W, 256)
    win = 16 * GRID_W
    outs = []
    for h in range(N_HEADS):
        sl = slice(HEAD_DIM * h, HEAD_DIM * (h + 1))
        q = q_ref[:, sl]
        s = lax.dot_general(q, k_ref[pl.ds(ws, win), sl], _NT, preferred_element_type=F32) + bias_ref[0, h]
        sc = lax.dot_general(q, kc_ref[:, sl], _NT, preferred_element_type=F32)
        outs.append(_softmax_av([(s, v_ref[pl.ds(ws, win), sl]), (sc, vc_ref[:, sl])]))
    o_ref[...] = jnp.concatenate(outs, -1).astype(BF16)


def _na_bias_tables(rpb, rows):
    nj = rows // 8
    H = rpb.shape[0]
    masked = 2 * NA_KH - 1
    qc, kc = np.arange(GRID_W)[:, None], np.arange(GRID_W)[None, :]
    cs = np.clip(qc - NA_KW // 2, 0, GRID_W - NA_KW)
    col_ok = (kc >= cs) & (kc < cs + NA_KW)
    co = np.clip(kc - qc + (NA_KW - 1), 0, 2 * NA_KW - 2)
    col_hot = (co[..., None] == np.arange(2 * NA_KW - 1)).astype(np.float32)
    cb = jnp.einsum('hdo,qko->hdqk', rpb, col_hot, precision=lax.Precision.HIGHEST)
    blocks = jnp.concatenate([jnp.where(col_ok, cb * LOG2E, NEG), jnp.full((H, 1, GRID_W, GRID_W), NEG, F32)], 1)
    which = []
    for j in (0, 1, nj - 1):
        ws = int(np.clip(8 * j - 4, 0, rows - 16))
        r = 8 * j + np.arange(8)[:, None]
        kr = ws + np.arange(16)[None, :]
        rs = np.clip(r - NA_KH // 2, 0, rows - NA_KH)
        which.append(np.where((kr >= rs) & (kr < rs + NA_KH), kr - r + (NA_KH - 1), masked))
    t = jnp.take(blocks, np.stack(which).reshape(-1), axis=1)
    t = t.reshape(H, 3, 8, 16, GRID_W, GRID_W).transpose(1, 0, 2, 4, 3, 5)
    return t.reshape(3, H, 512, 1024)


def _na_attention(pa, bias, *, B, T, C):
    rows = T // GRID_W
    nj = T // ATT_TQ
    cb = B * T // C
    kern = functools.partial(_na_kernel, rows=rows)
    return pl.pallas_call(
        kern, grid=(B, nj),
        in_specs=[pl.BlockSpec((ATT_TQ, 256), lambda b, j: (b * nj + j, 0)),
                  pl.BlockSpec((T, 256), lambda b, j: (b, 1)),
                  pl.BlockSpec((T, 256), lambda b, j: (b, 2)),
                  pl.BlockSpec((C, 256), lambda b, j: (cb + b, 1)),
                  pl.BlockSpec((C, 256), lambda b, j: (cb + b, 2)),
                  pl.BlockSpec((1, N_HEADS, 512, 1024),
                               lambda b, j: (jnp.where(j == 0, 0, jnp.where(j == nj - 1, 2, 1)), 0, 0, 0))],
        out_specs=pl.BlockSpec((ATT_TQ, 256), lambda b, j: (b * nj + j, 0)),
        out_shape=jax.ShapeDtypeStruct((B * T, 256), BF16),
        compiler_params=_cparams(("arbitrary", "arbitrary")), name="na_attention",
    )(pa, pa, pa, pa, pa, bias)


def _wa_kernel(sink_ref, q_ref, k_ref, v_ref, kc_ref, vc_ref, o_ref, *, T):
    i = pl.program_id(1)
    win = WA_SUB + 2 * WA_WINDOW
    for sub in range(ATT_TQ // WA_SUB):
        rows = slice(sub * WA_SUB, (sub + 1) * WA_SUB)
        start = i * ATT_TQ + sub * WA_SUB
        ws = pl.multiple_of(jnp.clip(start - WA_WINDOW, 0, T - win), 128)
        rel = ((lax.broadcasted_iota(I32, (WA_SUB, win), 1) + (ws - start))
               - lax.broadcasted_iota(I32, (WA_SUB, win), 0))
        valid = jnp.abs(rel) <= WA_WINDOW
        outs = []
        for h in range(N_HEADS):
            g = h // 2
            ksl = slice(HEAD_DIM * g, HEAD_DIM * (g + 1))
            q = q_ref[rows, HEAD_DIM * h:HEAD_DIM * (h + 1)]
            s = lax.dot_general(q, k_ref[pl.ds(ws, win), ksl], _NT, preferred_element_type=F32)
            s = jnp.where(valid, s, NEG)
            sc = lax.dot_general(q, kc_ref[:, ksl], _NT, preferred_element_type=F32)
            outs.append(_softmax_av([(s, v_ref[pl.ds(ws, win), ksl]), (sc, vc_ref[:, ksl])],
                                    extra_logit=sink_ref[h]))
        o_ref[rows, :] = jnp.concatenate(outs, -1).astype(BF16)


def _wa_attention(pb, sink, *, B, T, C):
    nj = T // ATT_TQ
    cb = B * T // C
    kern = functools.partial(_wa_kernel, T=T)
    return pl.pallas_call(
        kern, grid=(B, nj),
        in_specs=[pl.BlockSpec(memory_space=pltpu.SMEM),
                  pl.BlockSpec((ATT_TQ, 256), lambda b, j: (b * nj + j, 0)),
                  pl.BlockSpec((T, 128), lambda b, j: (b, 2)),
                  pl.BlockSpec((T, 128), lambda b, j: (b, 3)),
                  pl.BlockSpec((C, 128), lambda b, j: (cb + b, 2)),
                  pl.BlockSpec((C, 128), lambda b, j: (cb + b, 3))],
        out_specs=pl.BlockSpec((ATT_TQ, 256), lambda b, j: (b * nj + j, 0)),
        out_shape=jax.ShapeDtypeStruct((B * T, 256), BF16),
        compiler_params=_cparams(("arbitrary", "arbitrary")), name="wa_attention",
    )(sink, pb, pb, pb, pb, pb)


def _mla_kernel(q_ref, k_ref, v_ref, kc_ref, vc_ref, o_ref, *, T):
    nk = T // MLA_TK
    heads = [slice(HEAD_DIM * h, HEAD_DIM * (h + 1)) for h in range(N_HEADS)]
    qs = [q_ref[:, sl] for sl in heads]
    init = []
    for q, sl in zip(qs, heads):
        sc = lax.dot_general(q, kc_ref[:, sl], _NT, preferred_element_type=F32)
        m0 = jnp.max(sc, -1, keepdims=True)
        e0 = jnp.exp2(sc - m0)
        init += [m0, jnp.sum(e0, -1, keepdims=True),
                 jnp.dot(e0.astype(BF16), vc_ref[:, sl], preferred_element_type=F32)]

    def body(c, carry):
        ks = pl.multiple_of(c * MLA_TK, MLA_TK)
        out = []
        for h, (q, sl) in enumerate(zip(qs, heads)):
            m, l, acc = carry[3 * h:3 * h + 3]
            s = lax.dot_general(q, k_ref[pl.ds(ks, MLA_TK), sl], _NT, preferred_element_type=F32)
            mn = jnp.maximum(m, jnp.max(s, -1, keepdims=True))
            a = jnp.exp2(m - mn)
            e = jnp.exp2(s - mn)
            l = l * a + jnp.sum(e, -1, keepdims=True)
            acc = acc * a + jnp.dot(e.astype(BF16), v_ref[pl.ds(ks, MLA_TK), sl], preferred_element_type=F32)
            out += [mn, l, acc]
        return tuple(out)

    fin = lax.fori_loop(0, nk, body, tuple(init), unroll=2)
    outs = [fin[3 * h + 2] * (1.0 / fin[3 * h + 1]) for h in range(N_HEADS)]
    o_ref[...] = jnp.concatenate(outs, -1).astype(BF16)


def _mla_attention(pc, *, B, T, C):
    nj = T // MLA_TQ
    cb = B * T // C
    kern = functools.partial(_mla_kernel, T=T)
    return pl.pallas_call(
        kern, grid=(B, nj),
        in_specs=[pl.BlockSpec((MLA_TQ, 256), lambda b, j: (b * nj + j, 0)),
                  pl.BlockSpec((T, 256), lambda b, j: (b, 1)),
                  pl.BlockSpec((T, 256), lambda b, j: (b, 2)),
                  pl.BlockSpec((C, 256), lambda b, j: (cb + b, 1)),
                  pl.BlockSpec((C, 256), lambda b, j: (cb + b, 2))],
        out_specs=pl.BlockSpec((MLA_TQ, 256), lambda b, j: (b * nj + j, 0)),
        out_shape=jax.ShapeDtypeStruct((B * T, 256), BF16),
        compiler_params=_cparams(("arbitrary", "arbitrary")), name="mla_attention",
    )(pc, pc, pc, pc, pc)


def _ctx_kernel(sink_ref, pa_ref, pb_ref, pc_ref, oa_ref, ob_ref, oc_ref):
    def attend(p_ref, koff, voff, kv_heads, out_ref, sink):
        outs = []
        for h in range(N_HEADS):
            g = h * kv_heads // N_HEADS
            q = p_ref[:, HEAD_DIM * h:HEAD_DIM * (h + 1)]
            k = p_ref[:, koff + HEAD_DIM * g:koff + HEAD_DIM * (g + 1)]
            v = p_ref[:, voff + HEAD_DIM * g:voff + HEAD_DIM * (g + 1)]
            s = lax.dot_general(q, k, _NT, preferred_element_type=F32)
            outs.append(_softmax_av([(s, v)], extra_logit=sink_ref[h] if sink else None))
        out_ref[...] = jnp.concatenate(outs, -1).astype(BF16)

    attend(pa_ref, 256, 512, 4, oa_ref, False)
    attend(pb_ref, 256, 384, 2, ob_ref, True)
    attend(pc_ref, 256, 512, 4, oc_ref, False)


def _ctx_attention(sink, pa, pb, pc, *, B, T, C):
    cb = B * T // C
    row = lambda b: (cb + b, 0)
    return pl.pallas_call(
        _ctx_kernel, grid=(B,),
        in_specs=[pl.BlockSpec(memory_space=pltpu.SMEM),
                  pl.BlockSpec((C, 768), row), pl.BlockSpec((C, 512), row), pl.BlockSpec((C, 768), row)],
        out_specs=[pl.BlockSpec((C, 256), lambda b: (b, 0))] * 3,
        out_shape=[jax.ShapeDtypeStruct((B * C, 256), BF16)] * 3,
        compiler_params=_cparams(("arbitrary",)), name="ctx_attention",
    )(sink, pa, pb, pc)


def _gla_dir(pd_ref, o_ref, st_ref, reverse):
    nchunk = GLA_BLOCK // GLA_CHUNK
    L, DK, DV = GLA_CHUNK, N_HEADS * GLA_DK, N_HEADS * GLA_DV

    def iota(shape, dim):
        return lax.broadcasted_iota(I32, shape, dim)

    ri, ci = iota((L, L), 0), iota((L, L), 1)
    tri = jnp.where((ci >= ri) if reverse else (ci <= ri), 1.0, 0.0).astype(BF16)
    tri_t = jnp.where((ri >= ci) if reverse else (ri <= ci), 1.0, 0.0).astype(BF16)
    kcol, qrow = iota((L, N_HEADS * L), 1) % L, iota((L, N_HEADS * L), 0)
    keep = (kcol >= qrow) if reverse else (kcol <= qrow)
    k_owner = [iota((L, DK), 1) // GLA_DK == h for h in range(N_HEADS)]
    v_owner = [iota((L, DV), 1) // GLA_DV == h for h in range(N_HEADS)]
    s_owner = iota((DK, DV), 0) // GLA_DK == iota((DK, DV), 1) // GLA_DV
    gcol = 896 if reverse else 768
    end = 0 if reverse else L - 1
    order = range(nchunk - 1, -1, -1) if reverse else range(nchunk)

    def dot3(a_parts, b_parts):
        return sum(jnp.dot(a, b, preferred_element_type=F32) for a in a_parts for b in b_parts)

    per_chunk = []
    for c in order:
        rs = slice(c * L, (c + 1) * L)
        g = pd_ref[rs, gcol:gcol + DK]
        q = pd_ref[rs, 0:DK]
        k = pd_ref[rs, DK:2 * DK]
        vb = pd_ref[rs, 2 * DK:2 * DK + DV].astype(BF16)
        b = dot3([tri], _split3(g))
        bt = dot3(_split3(g.T), [tri_t])
        bend = bt[:, end:end + 1]
        qd = (q * jnp.exp(b)).astype(BF16)
        ki = (k * jnp.exp(-b)).astype(BF16)
        kdt = (k.T * jnp.exp(bend - bt)).astype(BF16)
        kbd = jnp.concatenate([jnp.where(m, ki, jnp.zeros_like(ki)) for m in k_owner], 0)
        vbd = jnp.concatenate([jnp.where(m, vb, jnp.zeros_like(vb)) for m in v_owner], 0)
        att = lax.dot_general(qd, kbd, _NT, preferred_element_type=F32)
        att = jnp.where(keep, att, 0.0).astype(BF16)
        intra = jnp.dot(att, vbd, preferred_element_type=F32)
        u = jnp.where(s_owner, jnp.dot(kdt, vb, preferred_element_type=F32), 0.0)
        per_chunk.append((rs, qd, intra, u, jnp.exp(bend)))
    st = st_ref[...]
    for rs, qd, intra, u, dec in per_chunk:
        o_ref[rs, :] = intra + jnp.dot(qd, st.astype(BF16), preferred_element_type=F32)
        st = st * dec + u
    st_ref[...] = st


def _gla_kernel(pdf_ref, pdb_ref, of_ref, ob_ref, stf_ref, stb_ref):
    @pl.when(pl.program_id(1) == 0)
    def _():
        stf_ref[...] = jnp.zeros_like(stf_ref)
        stb_ref[...] = jnp.zeros_like(stb_ref)

    _gla_dir(pdf_ref, of_ref, stf_ref, False)
    _gla_dir(pdb_ref, ob_ref, stb_ref, True)


def _gla(pd, *, B, T, C):
    N = pd.shape[0]
    assert C == GLA_BLOCK
    nlb = T // GLA_BLOCK
    cb = B * T // GLA_BLOCK

    def fwd(b, s):
        return (jnp.where(s == 0, cb + b, b * nlb + s - 1), 0)

    def bwd(b, s):
        return (jnp.where(s == 0, cb + b, b * nlb + nlb - s), 0)

    state = pltpu.VMEM((N_HEADS * GLA_DK, N_HEADS * GLA_DV), F32)
    return pl.pallas_call(
        _gla_kernel, grid=(B, nlb + 1),
        in_specs=[pl.BlockSpec((GLA_BLOCK, 1024), fwd), pl.BlockSpec((GLA_BLOCK, 1024), bwd)],
        out_specs=[pl.BlockSpec((GLA_BLOCK, 256), fwd), pl.BlockSpec((GLA_BLOCK, 256), bwd)],
        out_shape=[jax.ShapeDtypeStruct((N, 256), F32)] * 2,
        scratch_shapes=[state, state],
        compiler_params=_cparams(("arbitrary", "arbitrary")), name="gla_scan",
    )(pd, pd)


def _pack_bf16_pairs(h):
    w = h.shape[-1] // 2
    bits = lax.bitcast_convert_type(h.astype(BF16).astype(F32), U32)
    return (bits[:, w:] & jnp.uint32(0xFFFF0000)) | (bits[:, :w] >> 16)


def _unpack_bf16_pairs(u):
    lo = lax.bitcast_convert_type(u << 16, F32)
    hi = lax.bitcast_convert_type(u & jnp.uint32(0xFFFF0000), F32)
    return jnp.concatenate([lo, hi], -1).astype(BF16)


def _outproj_kernel(x_ref, oa_ref, ob_ref, oc_ref, oac_ref, obc_ref, occ_ref, gf_ref, gb_ref, r_ref, gn_ref, wout_ref,
                    g1_ref, sh2_ref, sc2_ref, lng_ref, lnb_ref, rw1_ref, rw2_ref, rb_ref,
                    x1_ref, hp_ref, sel_ref, gate_ref, rank_ref, count_ref, cnt_ref, *, alpha, nlat_t, tile0):
    is_lat = pl.program_id(0) + tile0 < nlat_t
    pick = lambda lat_ref, ctx_ref: jnp.where(is_lat, lat_ref[...], ctx_ref[...])
    s = gf_ref[...] + gb_ref[...]
    w = s.shape[-1]
    same_head = (lax.broadcasted_iota(I32, (w, w), 0) // GLA_DV) == (lax.broadcasted_iota(I32, (w, w), 1) // GLA_DV)
    avg = jnp.where(same_head, 1.0 / GLA_DV, 0.0).astype(BF16)
    ms = sum(jnp.dot(part, avg, preferred_element_type=F32) for part in _split3(s * s))
    r = r_ref[...]
    od = s * lax.rsqrt(ms + RMS_EPS) * gn_ref[...] * (r * _sigmoid(r))
    ocat = jnp.concatenate([pick(oa_ref, oac_ref), pick(ob_ref, obc_ref), pick(oc_ref, occ_ref), od.astype(BF16)], -1)
    o = jnp.dot(ocat, wout_ref[...], preferred_element_type=F32)
    x1 = _layer_norm(alpha * x_ref[...] + g1_ref[0] * o) * lng_ref[...] + lnb_ref[...]
    x1_ref[...] = x1
    h2 = _layer_norm(x1) * (1.0 + sc2_ref[0]) + sh2_ref[0]
    hp_ref[...] = _pack_bf16_pairs(h2)
    a1, a2, _ = _split3(h2)
    w1, w2 = rw1_ref[...], rw2_ref[...]
    logits = (jnp.dot(a1, w1, preferred_element_type=F32) + jnp.dot(a1, w2, preferred_element_type=F32)
              + jnp.dot(a2, w1, preferred_element_type=F32))
    scores = _sigmoid(logits)
    sc_t = scores.T
    work = (scores + rb_ref[...]).T
    ne, tm = sc_t.shape
    row = lax.broadcasted_iota(I32, (ne, tm), 0).astype(F32)
    chosen = jnp.zeros((ne, tm), F32)
    picks, picked = [], []
    for k in range(TOP_K):
        m = jnp.max(work, 0, keepdims=True)
        idx = jnp.min(jnp.where(work == m, row, float(ne)), 0, keepdims=True)
        hit = row == idx
        picks.append(idx)
        picked.append(jnp.sum(jnp.where(hit, sc_t, 0.0), 0, keepdims=True))
        chosen = jnp.where(hit, 1.0, chosen)
        work = jnp.where(hit, -jnp.inf, work)
    gsel = jnp.concatenate(picked, 0)
    sel_ref[...] = jnp.concatenate(picks, 0).astype(I32)
    gate_ref[...] = ROUTED_SCALE * gsel / jnp.sum(gsel, 0, keepdims=True)
    @pl.when(pl.program_id(0) == 0)
    def _():
        cnt_ref[...] = jnp.zeros_like(cnt_ref)

    earlier = (lax.broadcasted_iota(I32, (tm, tm), 0) < lax.broadcasted_iota(I32, (tm, tm), 1))
    prefix = jnp.dot(chosen.astype(BF16), jnp.where(earlier, 1.0, 0.0).astype(BF16),
                     preferred_element_type=F32) + cnt_ref[...]
    rank_ref[...] = jnp.concatenate(
        [jnp.sum(jnp.where(row == picks[k], prefix, 0.0), 0, keepdims=True) for k in range(TOP_K)], 0).astype(I32)
    cnt_ref[...] = cnt_ref[...] + jnp.sum(chosen, 1, keepdims=True)
    count_ref[...] = cnt_ref[...].astype(I32)


def _outproj(xall, o_lat, o_ctx, ofb, pd, gn, wout, mods_l, lng, lnb, rw1, rw2, rb, *, n_lat, T, alpha, tile0, ntiles):
    D = xall.shape[1]
    tm = TOK_TILE
    N = ntiles * tm
    nlat_t, per_b, nb = n_lat // tm, T // tm, n_lat // T

    def midx(k):
        return lambda i: (jnp.where(i + tile0 < nlat_t, (i + tile0) // per_b, nb) * 6 + k, 0, 0)

    const = lambda i: (0, 0)
    row = lambda i: (i, 0)
    src_row = lambda i: (i + tile0, 0)
    lat_row = lambda i: (jnp.minimum(i + tile0, nlat_t - 1), 0)
    ctx_row = lambda i: (jnp.maximum(i + tile0 - nlat_t, 0), 0)
    E = rw1.shape[1]
    kern = functools.partial(_outproj_kernel, alpha=alpha, nlat_t=nlat_t, tile0=tile0)
    return pl.pallas_call(
        kern, grid=(ntiles,),
        in_specs=[pl.BlockSpec((tm, D), src_row)] + [pl.BlockSpec((tm, 256), lat_row)] * 3
                 + [pl.BlockSpec((tm, 256), ctx_row)] * 3 + [
                  pl.BlockSpec((tm, 256), src_row), pl.BlockSpec((tm, 256), src_row),
                  pl.BlockSpec((tm, 256), lambda i: (i + tile0, 2)), pl.BlockSpec((1, 256), const),
                  pl.BlockSpec((D, D), const),
                  pl.BlockSpec((1, 1, D), midx(2)), pl.BlockSpec((1, 1, D), midx(3)), pl.BlockSpec((1, 1, D), midx(4)),
                  pl.BlockSpec((1, D), const), pl.BlockSpec((1, D), const),
                  pl.BlockSpec((D, E), const), pl.BlockSpec((D, E), const), pl.BlockSpec((1, E), const)],
        out_specs=[pl.BlockSpec((tm, D), row), pl.BlockSpec((tm, D // 2), row)]
                  + [pl.BlockSpec((TOP_K, tm), lambda i: (0, i))] * 3 + [pl.BlockSpec((E, 1), const)],
        out_shape=[jax.ShapeDtypeStruct((N, D), F32), jax.ShapeDtypeStruct((N, D // 2), U32),
                   jax.ShapeDtypeStruct((TOP_K, N), I32), jax.ShapeDtypeStruct((TOP_K, N), F32),
                   jax.ShapeDtypeStruct((TOP_K, N), I32), jax.ShapeDtypeStruct((E, 1), I32)],
        scratch_shapes=[pltpu.VMEM((E, 1), F32)],
        compiler_params=_cparams(("arbitrary",)), name="outproj_router",
    )(xall, *o_lat, *o_ctx, *ofb, pd, gn, wout, mods_l, mods_l, mods_l, lng, lnb, rw1, rw2, rb)


def _sc_mesh():
    return plsc.VectorSubcoreMesh(core_axis_name="c", subcore_axis_name="s")


def _sc_worker():
    return lax.axis_index("s") * SC_CORES + lax.axis_index("c")


def _sc_dispatch(hp, dest_k, rows):
    N, W = hp.shape
    tpw = N // SC_WORKERS
    ch = SC_DISPATCH_CHUNK
    assert N % SC_WORKERS == 0 and tpw % ch == 0
    nch = tpw // ch
    idx = dest_k.reshape(TOP_K, SC_WORKERS, nch, ch).transpose(1, 2, 0, 3)

    @functools.partial(
        pl.kernel, mesh=_sc_mesh(), out_type=jax.ShapeDtypeStruct((rows, W), hp.dtype),
        scratch_types=[pltpu.VMEM((nch, TOP_K, ch), I32), pltpu.VMEM((ch, W), hp.dtype), pltpu.SemaphoreType.DMA],
        compiler_params=pltpu.CompilerParams(use_tc_tiling_on_sc=True), name="sc_dispatch")
    def scatter(hp_hbm, idx_hbm, out_hbm, idx_v, rows_v, sem):
        wid = _sc_worker()
        base = wid * tpw
        pltpu.sync_copy(idx_hbm.at[wid], idx_v)

        @pl.loop(0, nch)
        def _(c):
            pltpu.sync_copy(hp_hbm.at[pl.ds(base + c * ch, ch)], rows_v)
            copies = [pltpu.async_copy(rows_v, out_hbm.at[idx_v.at[c, k]], sem) for k in range(TOP_K)]
            for cp in copies:
                cp.wait()

    return scatter(hp, idx)


def _sc_gather(src, idx):
    R, W = idx.shape[0], src.shape[1]
    per_w = R // SC_WORKERS
    ch = SC_GATHER_CHUNK
    assert R % SC_WORKERS == 0 and per_w % ch == 0
    nch = per_w // ch

    @functools.partial(
        pl.kernel, mesh=_sc_mesh(), out_type=jax.ShapeDtypeStruct((R, W), src.dtype),
        scratch_types=[pltpu.VMEM((nch, ch), I32), pltpu.VMEM((ch, W), src.dtype), pltpu.SemaphoreType.DMA],
        compiler_params=pltpu.CompilerParams(use_tc_tiling_on_sc=True), name="sc_gather")
    def gather(src_hbm, idx_hbm, out_hbm, idx_v, rows_v, sem):
        wid = _sc_worker()
        base = wid * per_w
        pltpu.sync_copy(idx_hbm.at[wid], idx_v)

        @pl.loop(0, nch)
        def _(c):
            pltpu.async_copy(src_hbm.at[idx_v.at[c]], rows_v, sem).wait()
            pltpu.sync_copy(rows_v, out_hbm.at[pl.ds(base + c * ch, ch)])

    return gather(src, idx.reshape(SC_WORKERS, nch, ch))


def _swiglu(xb, w1, w3, w2):
    a = jnp.dot(xb, w1, preferred_element_type=F32)
    b = jnp.dot(xb, w3, preferred_element_type=F32)
    return jnp.dot((a * _sigmoid(a) * b).astype(BF16), w2, preferred_element_type=F32)


def _expert_kernel(be_ref, nu_ref, xs_hbm, w1_ref, w3_ref, w2_ref, y_ref, w1b_ref, w3b_ref, w2b_ref, xbuf, sem):
    i = pl.program_id(0)
    n_used = nu_ref[0]
    used = i < n_used

    def fetch(j):
        slot = j % MOE_RING
        return pltpu.make_async_copy(xs_hbm.at[pl.ds(pl.multiple_of(j * MOE_BM, MOE_BM), MOE_BM)],
                                     xbuf.at[slot], sem.at[slot])

    for j in range(MOE_RING - 1):
        @pl.when((i == 0) & (j < n_used))
        def _(j=j):
            fetch(j).start()

    @pl.when(i + (MOE_RING - 1) < n_used)
    def _():
        fetch(i + (MOE_RING - 1)).start()

    @pl.when(used & ((i == 0) | (be_ref[i] != be_ref[jnp.maximum(i - 1, 0)])))
    def _():
        w1b_ref[...] = w1_ref[0, 0].astype(BF16)
        w3b_ref[...] = w3_ref[0, 0].astype(BF16)
        w2b_ref[...] = w2_ref[0, 0].astype(BF16)

    @pl.when(used)
    def _():
        fetch(i).wait()
        xb = _unpack_bf16_pairs(xbuf[i % MOE_RING])
        y_ref[...] = _pack_bf16_pairs(_swiglu(xb, w1b_ref[...], w3b_ref[...], w2b_ref[...]))

    @pl.when(jnp.logical_not(used))
    def _():
        y_ref[...] = jnp.zeros_like(y_ref)


def _experts(xs, blk_exp, n_used, w1, w3, w2, layer):
    rows, half = xs.shape
    _, E, D, F = w1.shape
    nblk = rows // MOE_BM
    wmap = lambda i, be, nu: (layer, be[i], 0, 0)
    grid_spec = pltpu.PrefetchScalarGridSpec(
        num_scalar_prefetch=2, grid=(nblk,),
        in_specs=[pl.BlockSpec(memory_space=pl.ANY),
                  pl.BlockSpec((1, 1, D, F), wmap), pl.BlockSpec((1, 1, D, F), wmap), pl.BlockSpec((1, 1, F, D), wmap)],
        out_specs=pl.BlockSpec((MOE_BM, half), lambda i, be, nu: (jnp.minimum(i, nu[0]), 0)),
        scratch_shapes=[pltpu.VMEM((D, F), BF16), pltpu.VMEM((D, F), BF16), pltpu.VMEM((F, D), BF16),
                        pltpu.VMEM((MOE_RING, MOE_BM, half), U32), pltpu.SemaphoreType.DMA((MOE_RING,))])
    return pl.pallas_call(
        _expert_kernel, grid_spec=grid_spec,
        out_shape=jax.ShapeDtypeStruct((rows, half), U32),
        compiler_params=_cparams(("arbitrary",)), name="expert_ffn",
    )(blk_exp, n_used, xs, w1, w3, w2)


def _combine_kernel(yg_ref, gate_ref, hp_ref, sw1_ref, sw3_ref, sw2_ref, x_ref, g2_ref, lng_ref, lnb_ref,
                    o_ref, *, alpha):
    f = _swiglu(_unpack_bf16_pairs(hp_ref[...]), sw1_ref[...], sw3_ref[...], sw2_ref[...])
    gates = gate_ref[...]
    lo = hi = 0.0
    for k in range(TOP_K):
        u = yg_ref[k]
        g = gates[:, k:k + 1]
        lo = lo + g * lax.bitcast_convert_type(u << 16, F32)
        hi = hi + g * lax.bitcast_convert_type(u & jnp.uint32(0xFFFF0000), F32)
    f = f + jnp.concatenate([lo, hi], -1)
    o_ref[...] = _layer_norm(alpha * x_ref[...] + g2_ref[0] * f) * lng_ref[...] + lnb_ref[...]


def _combine(yg, gates, hp, sw1, sw3, sw2, x1, mods_l, lng, lnb, *, n_lat, T, alpha, n_out, tok0):
    D = x1.shape[1]
    N = n_out
    tm = CMB_TILE
    nlat_t, per_b, nb = n_lat // tm, T // tm, n_lat // T
    t0 = tok0 // tm
    F = sw1.shape[1]
    const = lambda i: (0, 0)
    row = lambda i: (i, 0)
    kern = functools.partial(_combine_kernel, alpha=alpha)
    return pl.pallas_call(
        kern, grid=(N // tm,),
        in_specs=[pl.BlockSpec((TOP_K, tm, D // 2), lambda i: (0, i, 0)), pl.BlockSpec((tm, TOP_K), row),
                  pl.BlockSpec((tm, D // 2), row),
                  pl.BlockSpec((D, F), const), pl.BlockSpec((D, F), const), pl.BlockSpec((F, D), const),
                  pl.BlockSpec((tm, D), row),
                  pl.BlockSpec((1, 1, D), lambda i: (jnp.where(i + t0 < nlat_t, (i + t0) // per_b, nb) * 6 + 5, 0, 0)),
                  pl.BlockSpec((1, D), const), pl.BlockSpec((1, D), const)],
        out_specs=pl.BlockSpec((tm, D), row),
        out_shape=jax.ShapeDtypeStruct((N, D), F32),
        compiler_params=_cparams(("arbitrary",)), name="moe_combine",
    )(yg, gates, hp, sw1, sw3, sw2, x1, mods_l, lng, lnb)


def _dest_kernel(start_ref, sel_ref, rank_ref, o_ref):
    sel = sel_ref[...]

    def body(e, acc):
        return jnp.where(sel == e, start_ref[e], acc)

    o_ref[...] = lax.fori_loop(0, start_ref.shape[0], body, jnp.zeros_like(sel)) + rank_ref[...]


def _dispatch_plan(sel, rank, counts):
    N = sel.shape[1]
    n_experts = counts.shape[0]
    padded = (counts + MOE_BM - 1) // MOE_BM * MOE_BM
    pad_end = jnp.cumsum(padded)
    pad_start = (pad_end - padded).astype(I32)
    dense = (N * TOP_K // 128, 128)
    dest = pl.pallas_call(
        _dest_kernel,
        in_specs=[pl.BlockSpec(memory_space=pltpu.SMEM), pl.BlockSpec(memory_space=pltpu.VMEM),
                  pl.BlockSpec(memory_space=pltpu.VMEM)],
        out_specs=pl.BlockSpec(memory_space=pltpu.VMEM),
        out_shape=jax.ShapeDtypeStruct(dense, I32), name="moe_dest_rows",
    )(pad_start, sel.reshape(dense), rank.reshape(dense))
    dest_k = dest.reshape(TOP_K, N)
    n_blocks = -(-(N * TOP_K) // MOE_BM) + n_experts
    blk_first = jnp.arange(n_blocks, dtype=I32) * MOE_BM
    blk_exp = jnp.minimum(jnp.sum((pad_end[None, :] <= blk_first[:, None]).astype(I32), -1), n_experts - 1)
    n_used = (pad_end[-1] // MOE_BM).astype(I32).reshape(1)
    return dest_k, blk_exp, n_used, n_blocks * MOE_BM


def _rope_tables(T, tile):
    t = jnp.arange(T)
    row = (t // GRID_W).astype(F32)[:, None]
    col = (t % GRID_W).astype(F32)[:, None]

    def group(half):
        inv = ROPE_THETA ** (-jnp.arange(half, dtype=F32) / half)
        ar, ac = row * inv, col * inv
        cos = jnp.concatenate([jnp.cos(ar), jnp.cos(ar), jnp.cos(ac), jnp.cos(ac)], -1)
        sin = jnp.concatenate([-jnp.sin(ar), jnp.sin(ar), -jnp.sin(ac), jnp.sin(ac)], -1)
        return cos, sin

    ones32, zeros32 = jnp.ones((T, 32), F32), jnp.zeros((T, 32), F32)
    cw, sw = group(16)
    cw, sw = jnp.tile(cw, (1, 2)), jnp.tile(sw, (1, 2))
    c8, s8 = group(8)
    cq = jnp.tile(jnp.concatenate([ones32, c8], -1), (1, 2))
    sq = jnp.tile(jnp.concatenate([zeros32, s8], -1), (1, 2))
    ck = jnp.concatenate([c8, ones32, ones32, ones32], -1)
    sk = jnp.concatenate([s8, zeros32, zeros32, zeros32], -1)

    def pad(a, ident):
        return jnp.concatenate([a, jnp.full((tile, 128), ident, F32)], 0)

    return [pad(cw, 1.0), pad(sw, 0.0), pad(cq, 1.0), pad(sq, 0.0), pad(ck, 1.0), pad(sk, 0.0)]


def _layer_weights(w_in, mla_w_uq, mla_w_ukv, gla_w_gf, gla_w_gb, gla_b_gf, gla_b_gb):
    D = w_in.shape[0]
    z = lambda n: jnp.zeros((D, n), F32)
    win = jnp.concatenate([
        w_in[:, 0:256] * Q_SCALE, w_in[:, 256:768],
        w_in[:, 768:1024] * Q_SCALE, w_in[:, 1024:1280],
        w_in[:, 1280:1696], z(96),
        w_in[:, 1696:2464],
        w_in[:, 2464:2496], z(96)], -1).astype(BF16)
    wuq = (mla_w_uq * Q_SCALE).astype(BF16)
    ukv = mla_w_ukv.reshape(-1, N_HEADS, MLA_NOPE + HEAD_DIM)
    kv_rank = ukv.shape[0]
    wk = jnp.concatenate([ukv[:, :, :MLA_NOPE], jnp.zeros((kv_rank, N_HEADS, MLA_ROPE), F32)], -1)
    place = jnp.concatenate([jnp.zeros((MLA_ROPE, MLA_NOPE), F32), jnp.eye(MLA_ROPE, dtype=F32)], -1)
    place = jnp.concatenate([jnp.tile(place, (1, N_HEADS)), jnp.zeros((128 - MLA_ROPE, 256), F32)], 0)
    wkk = jnp.concatenate([wk.reshape(kv_rank, 256), place], 0).astype(BF16)
    wv = ukv[:, :, MLA_NOPE:].reshape(kv_rank, 256).astype(BF16)
    r = GLA_GATE_RANK
    wg = jnp.zeros((128, 256), F32).at[0:r, 0:128].set(gla_w_gf).at[r:2 * r, 128:256].set(gla_w_gb).astype(BF16)
    bg = jnp.concatenate([gla_b_gf, gla_b_gb])[None, :]
    return win, wuq, wkk, wv, wg, bg


def kernel(x, c, ctx, c_ctx, w_ada, b_ada, w_in, na_rpb, wa_sink, mla_g_q, mla_g_kv, mla_w_uq, mla_w_ukv,
           gla_w_gf, gla_b_gf, gla_w_gb, gla_b_gb, gla_g_norm, w_out, ln1_g, ln1_b, ln2_g, ln2_b,
           router_w, router_bias, exp_w1, exp_w3, exp_w2, sh_w1, sh_w3, sh_w2):
    B, T, D = x.shape
    C = ctx.shape[1]
    L = w_ada.shape[0]
    E = router_w.shape[-1]
    n_lat = B * T
    alpha = (2 * L) ** 0.25
    dims = dict(B=B, T=T, C=C)

    cc = jnp.zeros((8, D), F32).at[:B].set(c).at[B].set(c_ctx)
    mods = _mods(cc, w_ada, b_ada)
    tabs = _rope_tables(T, TOK_TILE)
    xall = jnp.concatenate([x.reshape(n_lat, D), ctx.reshape(B * C, D)], 0)

    for l in range(L):
        mods_l = mods[l].reshape(8 * 6, 1, D)
        win, wuq, wkk, wv, wg, bg = _layer_weights(w_in[l], mla_w_uq[l], mla_w_ukv[l], gla_w_gf[l], gla_w_gb[l],
                                                   gla_b_gf[l], gla_b_gb[l])
        pa, pb, pc, pd = _inproj(xall, mods_l, win, tabs, mla_g_q[l][None], mla_g_kv[l][None], wuq, wkk, wv, wg, bg,
                                 n_lat=n_lat, T=T)
        oa = _na_attention(pa, _na_bias_tables(na_rpb[l], T // GRID_W), **dims)
        sink = wa_sink[l] * LOG2E
        ob = _wa_attention(pb, sink, **dims)
        oc = _mla_attention(pc, **dims)
        o_ctx = _ctx_attention(sink, pa, pb, pc, **dims)
        ofb = _gla(pd, **dims)

        rw1 = router_w[l].astype(BF16)
        rw2 = (router_w[l] - rw1.astype(F32)).astype(BF16)
        n_all = xall.shape[0]
        part = n_all // MOE_SPLITS
        assert n_all % (MOE_SPLITS * TOK_TILE) == 0
        shared = (sh_w1[l].astype(BF16), sh_w3[l].astype(BF16), sh_w2[l].astype(BF16))
        gn, wout = jnp.tile(gla_g_norm[l], N_HEADS)[None], w_out[l].astype(BF16)
        routed = []
        for p in range(MOE_SPLITS):
            x1, hp, sel, gates, rank, counts = _outproj(
                xall, (oa, ob, oc), o_ctx, ofb, pd, gn, wout, mods_l, ln1_g[l][None], ln1_b[l][None], rw1, rw2,
                router_bias[l][None], n_lat=n_lat, T=T, alpha=alpha,
                tile0=p * part // TOK_TILE, ntiles=part // TOK_TILE)
            dest_k, blk_exp, n_used, rows = _dispatch_plan(sel, rank, counts.reshape(E))
            routed.append((x1, hp, gates.T, dest_k, blk_exp, n_used, _sc_dispatch(hp, dest_k, rows)))
        outs = []
        for p, (x1, hp, gates, dest_k, blk_exp, n_used, xs) in enumerate(routed):
            y = _experts(xs, blk_exp, n_used, exp_w1, exp_w3, exp_w2, l)
            yg = _sc_gather(y, dest_k.reshape(-1)).reshape(TOP_K, part, D // 2)
            n_out = part if l < L - 1 else max(min(n_lat - p * part, part), 0)
            if n_out:
                outs.append(_combine(yg, gates, hp, *shared, x1, mods_l, ln2_g[l][None], ln2_b[l][None], n_lat=n_lat,
                                     T=T, alpha=alpha, n_out=n_out, tok0=p * part))
        xall = jnp.concatenate(outs, 0)

    return xall.reshape(B, T, D)
```

```python
import functools

import numpy as np
import jax
import jax.numpy as jnp
from jax import lax
from jax.experimental import pallas as pl
from jax.experimental.pallas import tpu as pltpu
from jax.experimental.pallas import tpu_sc as plsc

F32 = jnp.float32
BF16 = jnp.bfloat16
U32 = jnp.uint32
I32 = jnp.int32

GRID_W = 64
HEAD_DIM = 64
N_HEADS = 4
NA_KH, NA_KW = 8, 16
WA_WINDOW = 128
WA_SUB = 256
MLA_NOPE, MLA_ROPE = 32, 32
GLA_DK, GLA_DV = 32, 64
GLA_GATE_RANK = 16
GLA_GATE_NORM = 16.0
GLA_CHUNK = 64
TOP_K = 8
ROUTED_SCALE = 2.5
ROPE_THETA = 10000.0
LN_EPS = 1e-5
RMS_EPS = 1e-6
NEG = float("-inf")
LOG2E = 1.4426950408889634
Q_SCALE = HEAD_DIM ** -0.5 * LOG2E

VMEM_LIMIT = 56 * 1024 * 1024
TOK_TILE = 512
ATT_TQ = 512
MLA_TQ = 1024
MLA_TK = 1024
GLA_BLOCK = 256
MOE_BM = 1024
MOE_RING = 3
CMB_TILE = 512
SC_CORES = 2
SC_SUBCORES = 16
SC_WORKERS = SC_CORES * SC_SUBCORES
SC_DISPATCH_CHUNK = 96
SC_GATHER_CHUNK = 128

PW = 2688
_NT = (((1,), (1,)), ((), ()))


def _cparams(sem, vmem=VMEM_LIMIT):
    return pltpu.CompilerParams(dimension_semantics=sem, vmem_limit_bytes=vmem)


def _sigmoid(x):
    return 1.0 / (1.0 + jnp.exp(-x))


def _layer_norm(x):
    mu = jnp.mean(x, -1, keepdims=True)
    xc = x - mu
    var = jnp.mean(xc * xc, -1, keepdims=True)
    return xc * lax.rsqrt(var + LN_EPS)


def _rms(x):
    return x * lax.rsqrt(jnp.mean(x * x, -1, keepdims=True) + RMS_EPS)


def _bdot(a, b):
    return jnp.dot(a.astype(BF16), b.astype(BF16), preferred_element_type=F32)


def _bdot_nt(a, b):
    return lax.dot_general(a.astype(BF16), b.astype(BF16), _NT, preferred_element_type=F32)


def _split3(a):
    a1 = a.astype(BF16)
    r = a - a1.astype(F32)
    a2 = r.astype(BF16)
    a3 = (r - a2.astype(F32)).astype(BF16)
    return a1, a2, a3


def _mods_kernel(cc_ref, w_ref, b_ref, o_ref):
    cc = cc_ref[...]
    s = cc * _sigmoid(cc)
    o_ref[0] = _bdot(s, w_ref[0]) + b_ref[0]


def _mods(cc, w_ada, b_ada):
    L, D, W = w_ada.shape
    tn = 1536
    return pl.pallas_call(
        _mods_kernel,
        grid=(L, W // tn),
        in_specs=[pl.BlockSpec((8, D), lambda l, j: (0, 0)),
                  pl.BlockSpec((1, D, tn), lambda l, j: (l, 0, j)),
                  pl.BlockSpec((1, 1, tn), lambda l, j: (l, 0, j))],
        out_specs=pl.BlockSpec((1, 8, tn), lambda l, j: (l, 0, j)),
        out_shape=jax.ShapeDtypeStruct((L, 8, W), F32),
        compiler_params=_cparams(("arbitrary", "arbitrary")),
        name="adaln_mods",
    )(cc, w_ada, b_ada.reshape(L, 1, W))


def _rope(z, cos, sin, half):
    w = z.shape[-1]
    lane = lax.broadcasted_iota(I32, z.shape, 1)
    first = (lane % (2 * half)) < half
    partner = jnp.where(first, pltpu.roll(z, w - half, 1), pltpu.roll(z, half, 1))
    return z * cos + partner * sin


def _log_sigmoid(x):
    return jnp.minimum(x, 0.0) - jnp.log(1.0 + jnp.exp(-jnp.abs(x)))


def _inproj_kernel(x_ref, shift_ref, scale_ref, win_ref, cw_ref, sw_ref, cq_ref, sq_ref, ck_ref, sk_ref,
                   gq_ref, gkv_ref, wuq_ref, wkk_ref, wv_ref, wg_ref, bg_ref,
                   pa_ref, pb_ref, pc_ref, pd_ref):
    h = _layer_norm(x_ref[...]) * (1.0 + scale_ref[0]) + shift_ref[0]
    p = jnp.dot(h.astype(BF16), win_ref[...], preferred_element_type=F32)
    pa_ref[...] = p[:, 0:768].astype(BF16)
    cw, sw = cw_ref[...], sw_ref[...]
    pb_ref[:, 0:128] = _rope(p[:, 768:896], cw, sw, 16).astype(BF16)
    pb_ref[:, 128:256] = _rope(p[:, 896:1024], cw, sw, 16).astype(BF16)
    pb_ref[:, 256:384] = _rope(p[:, 1024:1152], cw, sw, 16).astype(BF16)
    pb_ref[:, 384:512] = p[:, 1152:1280].astype(BF16)
    cqn = _rms(p[:, 1280:1536]) * gq_ref[...]
    q = jnp.dot(cqn.astype(BF16), wuq_ref[...], preferred_element_type=F32)
    cq, sq = cq_ref[...], sq_ref[...]
    pc_ref[:, 0:128] = _rope(q[:, 0:128], cq, sq, 8).astype(BF16)
    pc_ref[:, 128:256] = _rope(q[:, 128:256], cq, sq, 8).astype(BF16)
    ckvn = (_rms(p[:, 1536:1664]) * gkv_ref[...]).astype(BF16)
    kr = _rope(p[:, 1664:1792], ck_ref[...], sk_ref[...], 8).astype(BF16)
    kin = jnp.concatenate([ckvn, kr], axis=-1)
    pc_ref[:, 256:512] = jnp.dot(kin, wkk_ref[...], preferred_element_type=F32).astype(BF16)
    pc_ref[:, 512:768] = jnp.dot(ckvn, wv_ref[...], preferred_element_type=F32).astype(BF16)
    pd_ref[:, 0:128] = p[:, 1792:1920] * (GLA_DK ** -0.5)
    pd_ref[:, 128:768] = p[:, 1920:2560]
    pre = jnp.dot(p[:, 2560:2688].astype(BF16), wg_ref[...], preferred_element_type=F32) + bg_ref[...]
    pd_ref[:, 768:1024] = _log_sigmoid(pre) * (1.0 / GLA_GATE_NORM)


def _inproj(xall, mods_l, win, tabs, gq, gkv, wuq, wkk, wv, wg, bg, *, n_lat, T):
    N, D = xall.shape
    tm = TOK_TILE
    nlat_t = n_lat // tm
    per_b = T // tm
    nb = n_lat // T

    def midx(k):
        return lambda i: (jnp.where(i < nlat_t, i // per_b, nb) * 6 + k, 0, 0)

    def tidx(i):
        return (jnp.where(i < nlat_t, i % per_b, per_b), 0)

    const = lambda i: (0, 0)
    tab_spec = pl.BlockSpec((tm, 128), tidx)
    in_specs = [pl.BlockSpec((tm, D), lambda i: (i, 0)),
                pl.BlockSpec((1, 1, D), midx(0)), pl.BlockSpec((1, 1, D), midx(1)),
                pl.BlockSpec((D, PW), const)] + [tab_spec] * 6 + [
                pl.BlockSpec((1, 256), const), pl.BlockSpec((1, 128), const),
                pl.BlockSpec((256, 256), const), pl.BlockSpec((256, 256), const),
                pl.BlockSpec((128, 256), const), pl.BlockSpec((128, 256), const),
                pl.BlockSpec((1, 256), const)]
    out_specs = [pl.BlockSpec((tm, 768), lambda i: (i, 0)), pl.BlockSpec((tm, 512), lambda i: (i, 0)),
                 pl.BlockSpec((tm, 768), lambda i: (i, 0)), pl.BlockSpec((tm, 1024), lambda i: (i, 0))]
    out_shape = [jax.ShapeDtypeStruct((N, 768), BF16), jax.ShapeDtypeStruct((N, 512), BF16),
                 jax.ShapeDtypeStruct((N, 768), BF16), jax.ShapeDtypeStruct((N, 1024), F32)]
    return pl.pallas_call(
        _inproj_kernel, grid=(N // tm,), in_specs=in_specs, out_specs=out_specs, out_shape=out_shape,
        compiler_params=_cparams(("arbitrary",)), name="inproj",
    )(xall, mods_l, mods_l, win, *tabs, gq, gkv, wuq, wkk, wv, wg, bg)


def _softmax_av(parts, extra_logit=None):
    m = functools.reduce(jnp.maximum, [jnp.max(s, -1, keepdims=True) for s, _ in parts])
    if extra_logit is not None:
        m = jnp.maximum(m, extra_logit)
    l = 0.0 if extra_logit is None else jnp.exp2(extra_logit - m)
    o = None
    for s, v in parts:
        e = jnp.exp2(s - m)
        l = l + jnp.sum(e, -1, keepdims=True)
        c = jnp.dot(e.astype(BF16), v, preferred_element_type=F32)
        o = c if o is None else o + c
    return o * (1.0 / l)


def _na_kernel(q_ref, k_ref, v_ref, kc_ref, vc_ref, bias_ref, o_ref, *, rows):
    j = pl.program_id(1)
    ws = pl.multiple_of(jnp.clip(8 * j - 4, 0, rows - 16) * GRID_W, 256)
    win = 16 * GRID_W
    outs = []
    for h in range(N_HEADS):
        sl = slice(HEAD_DIM * h, HEAD_DIM * (h + 1))
        q = q_ref[:, sl]
        s = lax.dot_general(q, k_ref[pl.ds(ws, win), sl], _NT, preferred_element_type=F32) + bias_ref[0, h]
        sc = lax.dot_general(q, kc_ref[:, sl], _NT, preferred_element_type=F32)
        outs.append(_softmax_av([(s, v_ref[pl.ds(ws, win), sl]), (sc, vc_ref[:, sl])]))
    o_ref[...] = jnp.concatenate(outs, -1).astype(BF16)


def _na_bias_tables(rpb, rows):
    nj = rows // 8
    H = rpb.shape[0]
    masked = 2 * NA_KH - 1
    qc, kc = np.arange(GRID_W)[:, None], np.arange(GRID_W)[None, :]
    cs = np.clip(qc - NA_KW // 2, 0, GRID_W - NA_KW)
    col_ok = (kc >= cs) & (kc < cs + NA_KW)
    co = np.clip(kc - qc + (NA_KW - 1), 0, 2 * NA_KW - 2)
    col_hot = (co[..., None] == np.arange(2 * NA_KW - 1)).astype(np.float32)
    cb = jnp.einsum('hdo,qko->hdqk', rpb, col_hot, precision=lax.Precision.HIGHEST)
    blocks = jnp.concatenate([jnp.where(col_ok, cb * LOG2E, NEG), jnp.full((H, 1, GRID_W, GRID_W), NEG, F32)], 1)
    which = []
    for j in (0, 1, nj - 1):
        ws = int(np.clip(8 * j - 4, 0, rows - 16))
        r = 8 * j + np.arange(8)[:, None]
        kr = ws + np.arange(16)[None, :]
        rs = np.clip(r - NA_KH // 2, 0, rows - NA_KH)
        which.append(np.where((kr >= rs) & (kr < rs + NA_KH), kr - r + (NA_KH - 1), masked))
    t = jnp.take(blocks, np.stack(which).reshape(-1), axis=1)
    t = t.reshape(H, 3, 8, 16, GRID_W, GRID_W).transpose(1, 0, 2, 4, 3, 5)
    return t.reshape(3, H, 512, 1024)


def _na_attention(pa, bias, *, B, T, C):
    rows = T // GRID_W
    nj = T // ATT_TQ
    cb = B * T // C
    kern = functools.partial(_na_kernel, rows=rows)
    return pl.pallas_call(
        kern, grid=(B, nj),
        in_specs=[pl.BlockSpec((ATT_TQ, 256), lambda b, j: (b * nj + j, 0)),
                  pl.BlockSpec((T, 256), lambda b, j: (b, 1)),
                  pl.BlockSpec((T, 256), lambda b, j: (b, 2)),
                  pl.BlockSpec((C, 256), lambda b, j: (cb + b, 1)),
                  pl.BlockSpec((C, 256), lambda b, j: (cb + b, 2)),
                  pl.BlockSpec((1, N_HEADS, 512, 1024),
                               lambda b, j: (jnp.where(j == 0, 0, jnp.where(j == nj - 1, 2, 1)), 0, 0, 0))],
        out_specs=pl.BlockSpec((ATT_TQ, 256), lambda b, j: (b * nj + j, 0)),
        out_shape=jax.ShapeDtypeStruct((B * T, 256), BF16),
        compiler_params=_cparams(("arbitrary", "arbitrary")), name="na_attention",
    )(pa, pa, pa, pa, pa, bias)


def _wa_kernel(sink_ref, q_ref, k_ref, v_ref, kc_ref, vc_ref, o_ref, *, T):
    i = pl.program_id(1)
    win = WA_SUB + 2 * WA_WINDOW
    for sub in range(ATT_TQ // WA_SUB):
        rows = slice(sub * WA_SUB, (sub + 1) * WA_SUB)
        start = i * ATT_TQ + sub * WA_SUB
        ws = pl.multiple_of(jnp.clip(start - WA_WINDOW, 0, T - win), 128)
        rel = ((lax.broadcasted_iota(I32, (WA_SUB, win), 1) + (ws - start))
               - lax.broadcasted_iota(I32, (WA_SUB, win), 0))
        valid = jnp.abs(rel) <= WA_WINDOW
        outs = []
        for h in range(N_HEADS):
            g = h // 2
            ksl = slice(HEAD_DIM * g, HEAD_DIM * (g + 1))
            q = q_ref[rows, HEAD_DIM * h:HEAD_DIM * (h + 1)]
            s = lax.dot_general(q, k_ref[pl.ds(ws, win), ksl], _NT, preferred_element_type=F32)
            s = jnp.where(valid, s, NEG)
            sc = lax.dot_general(q, kc_ref[:, ksl], _NT, preferred_element_type=F32)
            outs.append(_softmax_av([(s, v_ref[pl.ds(ws, win), ksl]), (sc, vc_ref[:, ksl])],
                                    extra_logit=sink_ref[h]))
        o_ref[rows, :] = jnp.concatenate(outs, -1).astype(BF16)


def _wa_attention(pb, sink, *, B, T, C):
    nj = T // ATT_TQ
    cb = B * T // C
    kern = functools.partial(_wa_kernel, T=T)
    return pl.pallas_call(
        kern, grid=(B, nj),
        in_specs=[pl.BlockSpec(memory_space=pltpu.SMEM),
                  pl.BlockSpec((ATT_TQ, 256), lambda b, j: (b * nj + j, 0)),
                  pl.BlockSpec((T, 128), lambda b, j: (b, 2)),
                  pl.BlockSpec((T, 128), lambda b, j: (b, 3)),
                  pl.BlockSpec((C, 128), lambda b, j: (cb + b, 2)),
                  pl.BlockSpec((C, 128), lambda b, j: (cb + b, 3))],
        out_specs=pl.BlockSpec((ATT_TQ, 256), lambda b, j: (b * nj + j, 0)),
        out_shape=jax.ShapeDtypeStruct((B * T, 256), BF16),
        compiler_params=_cparams(("arbitrary", "arbitrary")), name="wa_attention",
    )(sink, pb, pb, pb, pb, pb)


def _mla_kernel(q_ref, k_ref, v_ref, kc_ref, vc_ref, o_ref, *, T):
    nk = T // MLA_TK
    heads = [slice(HEAD_DIM * h, HEAD_DIM * (h + 1)) for h in range(N_HEADS)]
    qs = [q_ref[:, sl] for sl in heads]
    init = []
    for q, sl in zip(qs, heads):
        sc = lax.dot_general(q, kc_ref[:, sl], _NT, preferred_element_type=F32)
        m0 = jnp.max(sc, -1, keepdims=True)
        e0 = jnp.exp2(sc - m0)
        init += [m0, jnp.sum(e0, -1, keepdims=True),
                 jnp.dot(e0.astype(BF16), vc_ref[:, sl], preferred_element_type=F32)]

    def body(c, carry):
        ks = pl.multiple_of(c * MLA_TK, MLA_TK)
        out = []
        for h, (q, sl) in enumerate(zip(qs, heads)):
            m, l, acc = carry[3 * h:3 * h + 3]
            s = lax.dot_general(q, k_ref[pl.ds(ks, MLA_TK), sl], _NT, preferred_element_type=F32)
            mn = jnp.maximum(m, jnp.max(s, -1, keepdims=True))
            a = jnp.exp2(m - mn)
            e = jnp.exp2(s - mn)
            l = l * a + jnp.sum(e, -1, keepdims=True)
            acc = acc * a + jnp.dot(e.astype(BF16), v_ref[pl.ds(ks, MLA_TK), sl], preferred_element_type=F32)
            out += [mn, l, acc]
        return tuple(out)

    fin = lax.fori_loop(0, nk, body, tuple(init), unroll=2)
    outs = [fin[3 * h + 2] * (1.0 / fin[3 * h + 1]) for h in range(N_HEADS)]
    o_ref[...] = jnp.concatenate(outs, -1).astype(BF16)


def _mla_attention(pc, *, B, T, C):
    nj = T // MLA_TQ
    cb = B * T // C
    kern = functools.partial(_mla_kernel, T=T)
    return pl.pallas_call(
        kern, grid=(B, nj),
        in_specs=[pl.BlockSpec((MLA_TQ, 256), lambda b, j: (b * nj + j, 0)),
                  pl.BlockSpec((T, 256), lambda b, j: (b, 1)),
                  pl.BlockSpec((T, 256), lambda b, j: (b, 2)),
                  pl.BlockSpec((C, 256), lambda b, j: (cb + b, 1)),
                  pl.BlockSpec((C, 256), lambda b, j: (cb + b, 2))],
        out_specs=pl.BlockSpec((MLA_TQ, 256), lambda b, j: (b * nj + j, 0)),
        out_shape=jax.ShapeDtypeStruct((B * T, 256), BF16),
        compiler_params=_cparams(("arbitrary", "arbitrary")), name="mla_attention",
    )(pc, pc, pc, pc, pc)


def _ctx_kernel(sink_ref, pa_ref, pb_ref, pc_ref, oa_ref, ob_ref, oc_ref):
    def attend(p_ref, koff, voff, kv_heads, out_ref, sink):
        outs = []
        for h in range(N_HEADS):
            g = h * kv_heads // N_HEADS
            q = p_ref[:, HEAD_DIM * h:HEAD_DIM * (h + 1)]
            k = p_ref[:, koff + HEAD_DIM * g:koff + HEAD_DIM * (g + 1)]
            v = p_ref[:, voff + HEAD_DIM * g:voff + HEAD_DIM * (g + 1)]
            s = lax.dot_general(q, k, _NT, preferred_element_type=F32)
            outs.append(_softmax_av([(s, v)], extra_logit=sink_ref[h] if sink else None))
        out_ref[...] = jnp.concatenate(outs, -1).astype(BF16)

    attend(pa_ref, 256, 512, 4, oa_ref, False)
    attend(pb_ref, 256, 384, 2, ob_ref, True)
    attend(pc_ref, 256, 512, 4, oc_ref, False)


def _ctx_attention(sink, pa, pb, pc, *, B, T, C):
    cb = B * T // C
    row = lambda b: (cb + b, 0)
    return pl.pallas_call(
        _ctx_kernel, grid=(B,),
        in_specs=[pl.BlockSpec(memory_space=pltpu.SMEM),
                  pl.BlockSpec((C, 768), row), pl.BlockSpec((C, 512), row), pl.BlockSpec((C, 768), row)],
        out_specs=[pl.BlockSpec((C, 256), lambda b: (b, 0))] * 3,
        out_shape=[jax.ShapeDtypeStruct((B * C, 256), BF16)] * 3,
        compiler_params=_cparams(("arbitrary",)), name="ctx_attention",
    )(sink, pa, pb, pc)


def _gla_dir(pd_ref, o_ref, st_ref, reverse):
    nchunk = GLA_BLOCK // GLA_CHUNK
    L, DK, DV = GLA_CHUNK, N_HEADS * GLA_DK, N_HEADS * GLA_DV

    def iota(shape, dim):
        return lax.broadcasted_iota(I32, shape, dim)

    ri, ci = iota((GLA_BLOCK, GLA_BLOCK), 0), iota((GLA_BLOCK, GLA_BLOCK), 1)
    tri = jnp.where((ri // L == ci // L) & ((ci >= ri) if reverse else (ci <= ri)), 1.0, 0.0).astype(BF16)
    kcol, qrow = iota((L, N_HEADS * L), 1) % L, iota((L, N_HEADS * L), 0)
    keep = (kcol >= qrow) if reverse else (kcol <= qrow)
    k_owner = [iota((L, DK), 1) // GLA_DK == h for h in range(N_HEADS)]
    v_owner = [iota((L, DV), 1) // GLA_DV == h for h in range(N_HEADS)]
    s_owner = iota((DK, DV), 0) // GLA_DK == iota((DK, DV), 1) // GLA_DV
    gcol = 896 if reverse else 768
    end = 0 if reverse else L - 1
    order = range(nchunk - 1, -1, -1) if reverse else range(nchunk)

    def dot3(a_parts, b_parts):
        return sum(jnp.dot(a, b, preferred_element_type=F32) for a in a_parts for b in b_parts)

    g_all = pd_ref[:, gcol:gcol + DK]
    b_all = dot3([tri], _split3(g_all))
    per_chunk = []
    for c in order:
        rs = slice(c * L, (c + 1) * L)
        q = pd_ref[rs, 0:DK]
        k = pd_ref[rs, DK:2 * DK]
        vb = pd_ref[rs, 2 * DK:2 * DK + DV].astype(BF16)
        b = b_all[rs]
        bend = b[end:end + 1, :]
        qd = (q * jnp.exp(b)).astype(BF16)
        ki = (k * jnp.exp(-b)).astype(BF16)
        kdt = (k * jnp.exp(bend - b)).T.astype(BF16)
        dec = jnp.exp(jnp.broadcast_to(bend, (8, DK)).T[:, 0:1])
        kbd = jnp.concatenate([jnp.where(m, ki, jnp.zeros_like(ki)) for m in k_owner], 0)
        vbd = jnp.concatenate([jnp.where(m, vb, jnp.zeros_like(vb)) for m in v_owner], 0)
        att = lax.dot_general(qd, kbd, _NT, preferred_element_type=F32)
        att = jnp.where(keep, att, 0.0).astype(BF16)
        intra = jnp.dot(att, vbd, preferred_element_type=F32)
        u = jnp.where(s_owner, jnp.dot(kdt, vb, preferred_element_type=F32), 0.0)
        per_chunk.append((rs, qd, intra, u, dec))
    st = st_ref[...]
    for rs, qd, intra, u, dec in per_chunk:
        o_ref[rs, :] = intra + jnp.dot(qd, st.astype(BF16), preferred_element_type=F32)
        st = st * dec + u
    st_ref[...] = st


def _gla_kernel(pdf_ref, pdb_ref, of_ref, ob_ref, stf_ref, stb_ref):
    @pl.when(pl.program_id(1) == 0)
    def _():
        stf_ref[...] = jnp.zeros_like(stf_ref)
        stb_ref[...] = jnp.zeros_like(stb_ref)

    _gla_dir(pdf_ref, of_ref, stf_ref, False)
    _gla_dir(pdb_ref, ob_ref, stb_ref, True)


def _gla(pd, *, B, T, C):
    N = pd.shape[0]
    assert C == GLA_BLOCK
    nlb = T // GLA_BLOCK
    cb = B * T // GLA_BLOCK

    def fwd(b, s):
        return (jnp.where(s == 0, cb + b, b * nlb + s - 1), 0)

    def bwd(b, s):
        return (jnp.where(s == 0, cb + b, b * nlb + nlb - s), 0)

    state = pltpu.VMEM((N_HEADS * GLA_DK, N_HEADS * GLA_DV), F32)
    return pl.pallas_call(
        _gla_kernel, grid=(B, nlb + 1),
        in_specs=[pl.BlockSpec((GLA_BLOCK, 1024), fwd), pl.BlockSpec((GLA_BLOCK, 1024), bwd)],
        out_specs=[pl.BlockSpec((GLA_BLOCK, 256), fwd), pl.BlockSpec((GLA_BLOCK, 256), bwd)],
        out_shape=[jax.ShapeDtypeStruct((N, 256), F32)] * 2,
        scratch_shapes=[state, state],
        compiler_params=_cparams(("arbitrary", "arbitrary")), name="gla_scan",
    )(pd, pd)


def _pack_bf16_pairs(h):
    w = h.shape[-1] // 2
    bits = lax.bitcast_convert_type(h.astype(BF16).astype(F32), U32)
    return (bits[:, w:] & jnp.uint32(0xFFFF0000)) | (bits[:, :w] >> 16)


def _unpack_bf16_pairs(u):
    lo = lax.bitcast_convert_type(u << 16, F32)
    hi = lax.bitcast_convert_type(u & jnp.uint32(0xFFFF0000), F32)
    return jnp.concatenate([lo, hi], -1).astype(BF16)


def _outproj_kernel(x_ref, oa_ref, ob_ref, oc_ref, oac_ref, obc_ref, occ_ref, gf_ref, gb_ref, r_ref, gn_ref, wout_ref,
                    g1_ref, sh2_ref, sc2_ref, lng_ref, lnb_ref, rw1_ref, rw2_ref, rb_ref,
                    x1_ref, hp_ref, sel_ref, gate_ref, rank_ref, count_ref, cnt_ref, *, alpha, nlat_t):
    is_lat = pl.program_id(0) < nlat_t
    pick = lambda lat_ref, ctx_ref: jnp.where(is_lat, lat_ref[...], ctx_ref[...])
    s = gf_ref[...] + gb_ref[...]
    w = s.shape[-1]
    same_head = (lax.broadcasted_iota(I32, (w, w), 0) // GLA_DV) == (lax.broadcasted_iota(I32, (w, w), 1) // GLA_DV)
    avg = jnp.where(same_head, 1.0 / GLA_DV, 0.0).astype(BF16)
    ms = sum(jnp.dot(part, avg, preferred_element_type=F32) for part in _split3(s * s))
    r = r_ref[...]
    od = s * lax.rsqrt(ms + RMS_EPS) * gn_ref[...] * (r * _sigmoid(r))
    ocat = jnp.concatenate([pick(oa_ref, oac_ref), pick(ob_ref, obc_ref), pick(oc_ref, occ_ref), od.astype(BF16)], -1)
    o = jnp.dot(ocat, wout_ref[...], preferred_element_type=F32)
    x1 = _layer_norm(alpha * x_ref[...] + g1_ref[0] * o) * lng_ref[...] + lnb_ref[...]
    x1_ref[...] = x1
    h2 = _layer_norm(x1) * (1.0 + sc2_ref[0]) + sh2_ref[0]
    hp_ref[...] = _pack_bf16_pairs(h2)
    a1, a2, _ = _split3(h2)
    w1, w2 = rw1_ref[...], rw2_ref[...]
    logits = (jnp.dot(a1, w1, preferred_element_type=F32) + jnp.dot(a1, w2, preferred_element_type=F32)
              + jnp.dot(a2, w1, preferred_element_type=F32))
    scores = _sigmoid(logits)
    sc_t = scores.T
    work = (scores + rb_ref[...]).T
    ne, tm = sc_t.shape
    row = lax.broadcasted_iota(I32, (ne, tm), 0).astype(F32)
    chosen = jnp.zeros((ne, tm), F32)
    picks, picked = [], []
    for k in range(TOP_K):
        m = jnp.max(work, 0, keepdims=True)
        idx = jnp.min(jnp.where(work == m, row, float(ne)), 0, keepdims=True)
        hit = row == idx
        picks.append(idx)
        picked.append(jnp.sum(jnp.where(hit, sc_t, 0.0), 0, keepdims=True))
        chosen = jnp.where(hit, 1.0, chosen)
        work = jnp.where(hit, -jnp.inf, work)
    gsel = jnp.concatenate(picked, 0)
    sel_ref[...] = jnp.concatenate(picks, 0).astype(I32)
    gate_ref[...] = ROUTED_SCALE * gsel / jnp.sum(gsel, 0, keepdims=True)
    @pl.when(pl.program_id(0) == 0)
    def _():
        cnt_ref[...] = jnp.zeros_like(cnt_ref)

    earlier = (lax.broadcasted_iota(I32, (tm, tm), 0) < lax.broadcasted_iota(I32, (tm, tm), 1))
    prefix = jnp.dot(chosen.astype(BF16), jnp.where(earlier, 1.0, 0.0).astype(BF16),
                     preferred_element_type=F32) + cnt_ref[...]
    rank_ref[...] = jnp.concatenate(
        [jnp.sum(jnp.where(row == picks[k], prefix, 0.0), 0, keepdims=True) for k in range(TOP_K)], 0).astype(I32)
    cnt_ref[...] = cnt_ref[...] + jnp.sum(chosen, 1, keepdims=True)
    count_ref[...] = cnt_ref[...].astype(I32)


def _outproj(xall, o_lat, o_ctx, ofb, pd, gn, wout, mods_l, lng, lnb, rw1, rw2, rb, *, n_lat, T, alpha):
    N, D = xall.shape
    tm = TOK_TILE
    nlat_t, per_b, nb = n_lat // tm, T // tm, n_lat // T

    def midx(k):
        return lambda i: (jnp.where(i < nlat_t, i // per_b, nb) * 6 + k, 0, 0)

    const = lambda i: (0, 0)
    row = lambda i: (i, 0)
    lat_row = lambda i: (jnp.minimum(i, nlat_t - 1), 0)
    ctx_row = lambda i: (jnp.maximum(i - nlat_t, 0), 0)
    E = rw1.shape[1]
    kern = functools.partial(_outproj_kernel, alpha=alpha, nlat_t=nlat_t)
    return pl.pallas_call(
        kern, grid=(N // tm,),
        in_specs=[pl.BlockSpec((tm, D), row)] + [pl.BlockSpec((tm, 256), lat_row)] * 3
                 + [pl.BlockSpec((tm, 256), ctx_row)] * 3 + [
                  pl.BlockSpec((tm, 256), row), pl.BlockSpec((tm, 256), row),
                  pl.BlockSpec((tm, 256), lambda i: (i, 2)), pl.BlockSpec((1, 256), const),
                  pl.BlockSpec((D, D), const),
                  pl.BlockSpec((1, 1, D), midx(2)), pl.BlockSpec((1, 1, D), midx(3)), pl.BlockSpec((1, 1, D), midx(4)),
                  pl.BlockSpec((1, D), const), pl.BlockSpec((1, D), const),
                  pl.BlockSpec((D, E), const), pl.BlockSpec((D, E), const), pl.BlockSpec((1, E), const)],
        out_specs=[pl.BlockSpec((tm, D), row), pl.BlockSpec((tm, D // 2), row)]
                  + [pl.BlockSpec((TOP_K, tm), lambda i: (0, i))] * 3 + [pl.BlockSpec((E, 1), const)],
        out_shape=[jax.ShapeDtypeStruct((N, D), F32), jax.ShapeDtypeStruct((N, D // 2), U32),
                   jax.ShapeDtypeStruct((TOP_K, N), I32), jax.ShapeDtypeStruct((TOP_K, N), F32),
                   jax.ShapeDtypeStruct((TOP_K, N), I32), jax.ShapeDtypeStruct((E, 1), I32)],
        scratch_shapes=[pltpu.VMEM((E, 1), F32)],
        compiler_params=_cparams(("arbitrary",)), name="outproj_router",
    )(xall, *o_lat, *o_ctx, *ofb, pd, gn, wout, mods_l, mods_l, mods_l, lng, lnb, rw1, rw2, rb)


def _sc_mesh():
    return plsc.VectorSubcoreMesh(core_axis_name="c", subcore_axis_name="s")


def _sc_worker():
    return lax.axis_index("s") * SC_CORES + lax.axis_index("c")


def _sc_dispatch(hp, dest_k, rows):
    N, W = hp.shape
    tpw = N // SC_WORKERS
    ch = SC_DISPATCH_CHUNK
    assert N % SC_WORKERS == 0 and tpw % ch == 0
    nch = tpw // ch
    idx = dest_k.reshape(TOP_K, SC_WORKERS, nch, ch).transpose(1, 2, 0, 3)

    @functools.partial(
        pl.kernel, mesh=_sc_mesh(), out_type=jax.ShapeDtypeStruct((rows, W), hp.dtype),
        scratch_types=[pltpu.VMEM((nch, TOP_K, ch), I32), pltpu.VMEM((ch, W), hp.dtype), pltpu.SemaphoreType.DMA],
        compiler_params=pltpu.CompilerParams(use_tc_tiling_on_sc=True), name="sc_dispatch")
    def scatter(hp_hbm, idx_hbm, out_hbm, idx_v, rows_v, sem):
        wid = _sc_worker()
        base = wid * tpw
        pltpu.sync_copy(idx_hbm.at[wid], idx_v)

        @pl.loop(0, nch)
        def _(c):
            pltpu.sync_copy(hp_hbm.at[pl.ds(base + c * ch, ch)], rows_v)
            copies = [pltpu.async_copy(rows_v, out_hbm.at[idx_v.at[c, k]], sem) for k in range(TOP_K)]
            for cp in copies:
                cp.wait()

    return scatter(hp, idx)


def _sc_gather(src, idx):
    R, W = idx.shape[0], src.shape[1]
    per_w = R // SC_WORKERS
    ch = SC_GATHER_CHUNK
    assert R % SC_WORKERS == 0 and per_w % ch == 0
    nch = per_w // ch

    @functools.partial(
        pl.kernel, mesh=_sc_mesh(), out_type=jax.ShapeDtypeStruct((R, W), src.dtype),
        scratch_types=[pltpu.VMEM((nch, ch), I32), pltpu.VMEM((ch, W), src.dtype), pltpu.SemaphoreType.DMA],
        compiler_params=pltpu.CompilerParams(use_tc_tiling_on_sc=True), name="sc_gather")
    def gather(src_hbm, idx_hbm, out_hbm, idx_v, rows_v, sem):
        wid = _sc_worker()
        base = wid * per_w
        pltpu.sync_copy(idx_hbm.at[wid], idx_v)

        @pl.loop(0, nch)
        def _(c):
            pltpu.async_copy(src_hbm.at[idx_v.at[c]], rows_v, sem).wait()
            pltpu.sync_copy(rows_v, out_hbm.at[pl.ds(base + c * ch, ch)])

    return gather(src, idx.reshape(SC_WORKERS, nch, ch))


def _swiglu(xb, w1, w3, w2):
    a = jnp.dot(xb, w1, preferred_element_type=F32)
    b = jnp.dot(xb, w3, preferred_element_type=F32)
    return jnp.dot((a * _sigmoid(a) * b).astype(BF16), w2, preferred_element_type=F32)


def _expert_kernel(be_ref, nu_ref, xs_hbm, w1_ref, w3_ref, w2_ref, y_ref, w1b_ref, w3b_ref, w2b_ref, xbuf, sem):
    i = pl.program_id(0)
    n_used = nu_ref[0]
    used = i < n_used

    def fetch(j):
        slot = j % MOE_RING
        return pltpu.make_async_copy(xs_hbm.at[pl.ds(pl.multiple_of(j * MOE_BM, MOE_BM), MOE_BM)],
                                     xbuf.at[slot], sem.at[slot])

    for j in range(MOE_RING - 1):
        @pl.when((i == 0) & (j < n_used))
        def _(j=j):
            fetch(j).start()

    @pl.when(i + (MOE_RING - 1) < n_used)
    def _():
        fetch(i + (MOE_RING - 1)).start()

    @pl.when(used & ((i == 0) | (be_ref[i] != be_ref[jnp.maximum(i - 1, 0)])))
    def _():
        w1b_ref[...] = w1_ref[0, 0].astype(BF16)
        w3b_ref[...] = w3_ref[0, 0].astype(BF16)
        w2b_ref[...] = w2_ref[0, 0].astype(BF16)

    @pl.when(used)
    def _():
        fetch(i).wait()
        xb = _unpack_bf16_pairs(xbuf[i % MOE_RING])
        y_ref[...] = _pack_bf16_pairs(_swiglu(xb, w1b_ref[...], w3b_ref[...], w2b_ref[...]))

    @pl.when(jnp.logical_not(used))
    def _():
        y_ref[...] = jnp.zeros_like(y_ref)


def _experts(xs, blk_exp, n_used, w1, w3, w2, layer):
    rows, half = xs.shape
    _, E, D, F = w1.shape
    nblk = rows // MOE_BM
    wmap = lambda i, be, nu: (layer, be[i], 0, 0)
    grid_spec = pltpu.PrefetchScalarGridSpec(
        num_scalar_prefetch=2, grid=(nblk,),
        in_specs=[pl.BlockSpec(memory_space=pl.ANY),
                  pl.BlockSpec((1, 1, D, F), wmap), pl.BlockSpec((1, 1, D, F), wmap), pl.BlockSpec((1, 1, F, D), wmap)],
        out_specs=pl.BlockSpec((MOE_BM, half), lambda i, be, nu: (jnp.minimum(i, nu[0]), 0)),
        scratch_shapes=[pltpu.VMEM((D, F), BF16), pltpu.VMEM((D, F), BF16), pltpu.VMEM((F, D), BF16),
                        pltpu.VMEM((MOE_RING, MOE_BM, half), U32), pltpu.SemaphoreType.DMA((MOE_RING,))])
    return pl.pallas_call(
        _expert_kernel, grid_spec=grid_spec,
        out_shape=jax.ShapeDtypeStruct((rows, half), U32),
        compiler_params=_cparams(("arbitrary",)), name="expert_ffn",
    )(blk_exp, n_used, xs, w1, w3, w2)


def _combine_kernel(yg_ref, gate_ref, hp_ref, sw1_ref, sw3_ref, sw2_ref, x_ref, g2_ref, lng_ref, lnb_ref,
                    o_ref, *, alpha):
    f = _swiglu(_unpack_bf16_pairs(hp_ref[...]), sw1_ref[...], sw3_ref[...], sw2_ref[...])
    gates = gate_ref[...]
    lo = hi = 0.0
    for k in range(TOP_K):
        u = yg_ref[k]
        g = gates[:, k:k + 1]
        lo = lo + g * lax.bitcast_convert_type(u << 16, F32)
        hi = hi + g * lax.bitcast_convert_type(u & jnp.uint32(0xFFFF0000), F32)
    f = f + jnp.concatenate([lo, hi], -1)
    o_ref[...] = _layer_norm(alpha * x_ref[...] + g2_ref[0] * f) * lng_ref[...] + lnb_ref[...]


def _combine(yg, gates, hp, sw1, sw3, sw2, x1, mods_l, lng, lnb, *, n_lat, T, alpha, n_out):
    D = x1.shape[1]
    N = n_out
    tm = CMB_TILE
    nlat_t, per_b, nb = n_lat // tm, T // tm, n_lat // T
    F = sw1.shape[1]
    const = lambda i: (0, 0)
    row = lambda i: (i, 0)
    kern = functools.partial(_combine_kernel, alpha=alpha)
    return pl.pallas_call(
        kern, grid=(N // tm,),
        in_specs=[pl.BlockSpec((TOP_K, tm, D // 2), lambda i: (0, i, 0)), pl.BlockSpec((tm, TOP_K), row),
                  pl.BlockSpec((tm, D // 2), row),
                  pl.BlockSpec((D, F), const), pl.BlockSpec((D, F), const), pl.BlockSpec((F, D), const),
                  pl.BlockSpec((tm, D), row),
                  pl.BlockSpec((1, 1, D), lambda i: (jnp.where(i < nlat_t, i // per_b, nb) * 6 + 5, 0, 0)),
                  pl.BlockSpec((1, D), const), pl.BlockSpec((1, D), const)],
        out_specs=pl.BlockSpec((tm, D), row),
        out_shape=jax.ShapeDtypeStruct((N, D), F32),
        compiler_params=_cparams(("arbitrary",)), name="moe_combine",
    )(yg, gates, hp, sw1, sw3, sw2, x1, mods_l, lng, lnb)


def _dest_kernel(start_ref, sel_ref, rank_ref, o_ref):
    sel = sel_ref[...]

    def body(e, acc):
        return jnp.where(sel == e, start_ref[e], acc)

    o_ref[...] = lax.fori_loop(0, start_ref.shape[0], body, jnp.zeros_like(sel)) + rank_ref[...]


def _dispatch_plan(sel, rank, counts):
    N = sel.shape[1]
    n_experts = counts.shape[0]
    padded = (counts + MOE_BM - 1) // MOE_BM * MOE_BM
    pad_end = jnp.cumsum(padded)
    pad_start = (pad_end - padded).astype(I32)
    dense = (N * TOP_K // 128, 128)
    dest = pl.pallas_call(
        _dest_kernel,
        in_specs=[pl.BlockSpec(memory_space=pltpu.SMEM), pl.BlockSpec(memory_space=pltpu.VMEM),
                  pl.BlockSpec(memory_space=pltpu.VMEM)],
        out_specs=pl.BlockSpec(memory_space=pltpu.VMEM),
        out_shape=jax.ShapeDtypeStruct(dense, I32), name="moe_dest_rows",
    )(pad_start, sel.reshape(dense), rank.reshape(dense))
    dest_k = dest.reshape(TOP_K, N)
    n_blocks = -(-(N * TOP_K) // MOE_BM) + n_experts
    blk_first = jnp.arange(n_blocks, dtype=I32) * MOE_BM
    blk_exp = jnp.minimum(jnp.sum((pad_end[None, :] <= blk_first[:, None]).astype(I32), -1), n_experts - 1)
    n_used = (pad_end[-1] // MOE_BM).astype(I32).reshape(1)
    return dest_k, blk_exp, n_used, n_blocks * MOE_BM


def _rope_tables(T, tile):
    t = jnp.arange(T)
    row = (t // GRID_W).astype(F32)[:, None]
    col = (t % GRID_W).astype(F32)[:, None]

    def group(half):
        inv = ROPE_THETA ** (-jnp.arange(half, dtype=F32) / half)
        ar, ac = row * inv, col * inv
        cos = jnp.concatenate([jnp.cos(ar), jnp.cos(ar), jnp.cos(ac), jnp.cos(ac)], -1)
        sin = jnp.concatenate([-jnp.sin(ar), jnp.sin(ar), -jnp.sin(ac), jnp.sin(ac)], -1)
        return cos, sin

    ones32, zeros32 = jnp.ones((T, 32), F32), jnp.zeros((T, 32), F32)
    cw, sw = group(16)
    cw, sw = jnp.tile(cw, (1, 2)), jnp.tile(sw, (1, 2))
    c8, s8 = group(8)
    cq = jnp.tile(jnp.concatenate([ones32, c8], -1), (1, 2))
    sq = jnp.tile(jnp.concatenate([zeros32, s8], -1), (1, 2))
    ck = jnp.concatenate([c8, ones32, ones32, ones32], -1)
    sk = jnp.concatenate([s8, zeros32, zeros32, zeros32], -1)

    def pad(a, ident):
        return jnp.concatenate([a, jnp.full((tile, 128), ident, F32)], 0)

    return [pad(cw, 1.0), pad(sw, 0.0), pad(cq, 1.0), pad(sq, 0.0), pad(ck, 1.0), pad(sk, 0.0)]


def _layer_weights(w_in, mla_w_uq, mla_w_ukv, gla_w_gf, gla_w_gb, gla_b_gf, gla_b_gb):
    D = w_in.shape[0]
    z = lambda n: jnp.zeros((D, n), F32)
    win = jnp.concatenate([
        w_in[:, 0:256] * Q_SCALE, w_in[:, 256:768],
        w_in[:, 768:1024] * Q_SCALE, w_in[:, 1024:1280],
        w_in[:, 1280:1696], z(96),
        w_in[:, 1696:2464],
        w_in[:, 2464:2496], z(96)], -1).astype(BF16)
    wuq = (mla_w_uq * Q_SCALE).astype(BF16)
    ukv = mla_w_ukv.reshape(-1, N_HEADS, MLA_NOPE + HEAD_DIM)
    kv_rank = ukv.shape[0]
    wk = jnp.concatenate([ukv[:, :, :MLA_NOPE], jnp.zeros((kv_rank, N_HEADS, MLA_ROPE), F32)], -1)
    place = jnp.concatenate([jnp.zeros((MLA_ROPE, MLA_NOPE), F32), jnp.eye(MLA_ROPE, dtype=F32)], -1)
    place = jnp.concatenate([jnp.tile(place, (1, N_HEADS)), jnp.zeros((128 - MLA_ROPE, 256), F32)], 0)
    wkk = jnp.concatenate([wk.reshape(kv_rank, 256), place], 0).astype(BF16)
    wv = ukv[:, :, MLA_NOPE:].reshape(kv_rank, 256).astype(BF16)
    r = GLA_GATE_RANK
    wg = jnp.zeros((128, 256), F32).at[0:r, 0:128].set(gla_w_gf).at[r:2 * r, 128:256].set(gla_w_gb).astype(BF16)
    bg = jnp.concatenate([gla_b_gf, gla_b_gb])[None, :]
    return win, wuq, wkk, wv, wg, bg


def kernel(x, c, ctx, c_ctx, w_ada, b_ada, w_in, na_rpb, wa_sink, mla_g_q, mla_g_kv, mla_w_uq, mla_w_ukv,
           gla_w_gf, gla_b_gf, gla_w_gb, gla_b_gb, gla_g_norm, w_out, ln1_g, ln1_b, ln2_g, ln2_b,
           router_w, router_bias, exp_w1, exp_w3, exp_w2, sh_w1, sh_w3, sh_w2):
    B, T, D = x.shape
    C = ctx.shape[1]
    L = w_ada.shape[0]
    E = router_w.shape[-1]
    n_lat = B * T
    alpha = (2 * L) ** 0.25
    dims = dict(B=B, T=T, C=C)

    cc = jnp.zeros((8, D), F32).at[:B].set(c).at[B].set(c_ctx)
    mods = _mods(cc, w_ada, b_ada)
    tabs = _rope_tables(T, TOK_TILE)
    xall = jnp.concatenate([x.reshape(n_lat, D), ctx.reshape(B * C, D)], 0)

    for l in range(L):
        mods_l = mods[l].reshape(8 * 6, 1, D)
        win, wuq, wkk, wv, wg, bg = _layer_weights(w_in[l], mla_w_uq[l], mla_w_ukv[l], gla_w_gf[l], gla_w_gb[l],
                                                   gla_b_gf[l], gla_b_gb[l])
        pa, pb, pc, pd = _inproj(xall, mods_l, win, tabs, mla_g_q[l][None], mla_g_kv[l][None], wuq, wkk, wv, wg, bg,
                                 n_lat=n_lat, T=T)
        oa = _na_attention(pa, _na_bias_tables(na_rpb[l], T // GRID_W), **dims)
        sink = wa_sink[l] * LOG2E
        ob = _wa_attention(pb, sink, **dims)
        oc = _mla_attention(pc, **dims)
        o_ctx = _ctx_attention(sink, pa, pb, pc, **dims)
        ofb = _gla(pd, **dims)

        rw1 = router_w[l].astype(BF16)
        rw2 = (router_w[l] - rw1.astype(F32)).astype(BF16)
        n_all = xall.shape[0]
        x1, hp, sel, gates, rank, counts = _outproj(
            xall, (oa, ob, oc), o_ctx, ofb, pd, jnp.tile(gla_g_norm[l], N_HEADS)[None], w_out[l].astype(BF16), mods_l,
            ln1_g[l][None], ln1_b[l][None], rw1, rw2, router_bias[l][None], n_lat=n_lat, T=T, alpha=alpha)

        dest_k, blk_exp, n_used, rows = _dispatch_plan(sel, rank, counts.reshape(E))
        xs = _sc_dispatch(hp, dest_k, rows)
        y = _experts(xs, blk_exp, n_used, exp_w1, exp_w3, exp_w2, l)
        yg = _sc_gather(y, dest_k.reshape(-1)).reshape(TOP_K, n_all, D // 2)
        xall = _combine(yg, gates.T, hp, sh_w1[l].astype(BF16), sh_w3[l].astype(BF16), sh_w2[l].astype(BF16),
                        x1, mods_l, ln2_g[l][None], ln2_b[l][None], n_lat=n_lat, T=T, alpha=alpha,
                        n_out=n_lat if l == L - 1 else n_all)

    return xall.reshape(B, T, D)
```

```python
import functools

import numpy as np
import jax
import jax.numpy as jnp
from jax import lax
from jax.experimental import pallas as pl
from jax.experimental.pallas import tpu as pltpu
from jax.experimental.pallas import tpu_sc as plsc

F32 = jnp.float32
BF16 = jnp.bfloat16
U32 = jnp.uint32
I32 = jnp.int32

GRID_W = 64
HEAD_DIM = 64
N_HEADS = 4
NA_KH, NA_KW = 8, 16
WA_WINDOW = 128
WA_SUB = 256
MLA_NOPE, MLA_ROPE = 32, 32
GLA_DK, GLA_DV = 32, 64
GLA_GATE_RANK = 16
GLA_GATE_NORM = 16.0
GLA_CHUNK = 64
TOP_K = 8
ROUTED_SCALE = 2.5
ROPE_THETA = 10000.0
LN_EPS = 1e-5
RMS_EPS = 1e-6
NEG = float("-inf")
LOG2E = 1.4426950408889634
Q_SCALE = HEAD_DIM ** -0.5 * LOG2E

VMEM_LIMIT = 56 * 1024 * 1024
TOK_TILE = 512
ATT_TQ = 512
MLA_TQ = 1024
MLA_TK = 1024
GLA_BLOCK = 256
MOE_BM = 1024
MOE_RING = 3
CMB_TILE = 512
SC_CORES = 2
SC_SUBCORES = 16
SC_WORKERS = SC_CORES * SC_SUBCORES
SC_DISPATCH_CHUNK = 96
SC_GATHER_CHUNK = 128

PW = 2688
_NT = (((1,), (1,)), ((), ()))


def _cparams(sem, vmem=VMEM_LIMIT):
    return pltpu.CompilerParams(dimension_semantics=sem, vmem_limit_bytes=vmem)


def _sigmoid(x):
    return 1.0 / (1.0 + jnp.exp(-x))


def _layer_norm(x):
    mu = jnp.mean(x, -1, keepdims=True)
    xc = x - mu
    var = jnp.mean(xc * xc, -1, keepdims=True)
    return xc * lax.rsqrt(var + LN_EPS)


def _rms(x):
    return x * lax.rsqrt(jnp.mean(x * x, -1, keepdims=True) + RMS_EPS)


def _bdot(a, b):
    return jnp.dot(a.astype(BF16), b.astype(BF16), preferred_element_type=F32)


def _bdot_nt(a, b):
    return lax.dot_general(a.astype(BF16), b.astype(BF16), _NT, preferred_element_type=F32)


def _split3(a):
    a1 = a.astype(BF16)
    r = a - a1.astype(F32)
    a2 = r.astype(BF16)
    a3 = (r - a2.astype(F32)).astype(BF16)
    return a1, a2, a3


def _mods_kernel(cc_ref, w_ref, b_ref, o_ref):
    cc = cc_ref[...]
    s = cc * _sigmoid(cc)
    o_ref[0] = _bdot(s, w_ref[0]) + b_ref[0]


def _mods(cc, w_ada, b_ada):
    L, D, W = w_ada.shape
    tn = 1536
    return pl.pallas_call(
        _mods_kernel,
        grid=(L, W // tn),
        in_specs=[pl.BlockSpec((8, D), lambda l, j: (0, 0)),
                  pl.BlockSpec((1, D, tn), lambda l, j: (l, 0, j)),
                  pl.BlockSpec((1, 1, tn), lambda l, j: (l, 0, j))],
        out_specs=pl.BlockSpec((1, 8, tn), lambda l, j: (l, 0, j)),
        out_shape=jax.ShapeDtypeStruct((L, 8, W), F32),
        compiler_params=_cparams(("arbitrary", "arbitrary")),
        name="adaln_mods",
    )(cc, w_ada, b_ada.reshape(L, 1, W))


def _rope(z, cos, sin, half):
    w = z.shape[-1]
    lane = lax.broadcasted_iota(I32, z.shape, 1)
    first = (lane % (2 * half)) < half
    partner = jnp.where(first, pltpu.roll(z, w - half, 1), pltpu.roll(z, half, 1))
    return z * cos + partner * sin


def _log_sigmoid(x):
    return jnp.minimum(x, 0.0) - jnp.log(1.0 + jnp.exp(-jnp.abs(x)))


def _inproj_kernel(x_ref, shift_ref, scale_ref, win_ref, cw_ref, sw_ref, cq_ref, sq_ref, ck_ref, sk_ref,
                   gq_ref, gkv_ref, wuq_ref, wkk_ref, wv_ref, wg_ref, bg_ref,
                   pa_ref, pb_ref, pc_ref, pd_ref):
    h = _layer_norm(x_ref[...]) * (1.0 + scale_ref[0]) + shift_ref[0]
    p = jnp.dot(h.astype(BF16), win_ref[...], preferred_element_type=F32)
    pa_ref[...] = p[:, 0:768].astype(BF16)
    cw, sw = cw_ref[...], sw_ref[...]
    pb_ref[:, 0:128] = _rope(p[:, 768:896], cw, sw, 16).astype(BF16)
    pb_ref[:, 128:256] = _rope(p[:, 896:1024], cw, sw, 16).astype(BF16)
    pb_ref[:, 256:384] = _rope(p[:, 1024:1152], cw, sw, 16).astype(BF16)
    pb_ref[:, 384:512] = p[:, 1152:1280].astype(BF16)
    cqn = _rms(p[:, 1280:1536]) * gq_ref[...]
    q = jnp.dot(cqn.astype(BF16), wuq_ref[...], preferred_element_type=F32)
    cq, sq = cq_ref[...], sq_ref[...]
    pc_ref[:, 0:128] = _rope(q[:, 0:128], cq, sq, 8).astype(BF16)
    pc_ref[:, 128:256] = _rope(q[:, 128:256], cq, sq, 8).astype(BF16)
    ckvn = (_rms(p[:, 1536:1664]) * gkv_ref[...]).astype(BF16)
    kr = _rope(p[:, 1664:1792], ck_ref[...], sk_ref[...], 8).astype(BF16)
    kin = jnp.concatenate([ckvn, kr], axis=-1)
    pc_ref[:, 256:512] = jnp.dot(kin, wkk_ref[...], preferred_element_type=F32).astype(BF16)
    pc_ref[:, 512:768] = jnp.dot(ckvn, wv_ref[...], preferred_element_type=F32).astype(BF16)
    pd_ref[:, 0:128] = p[:, 1792:1920] * (GLA_DK ** -0.5)
    pd_ref[:, 128:768] = p[:, 1920:2560]
    pre = jnp.dot(p[:, 2560:2688].astype(BF16), wg_ref[...], preferred_element_type=F32) + bg_ref[...]
    pd_ref[:, 768:1024] = _log_sigmoid(pre) * (1.0 / GLA_GATE_NORM)


def _inproj(xall, mods_l, win, tabs, gq, gkv, wuq, wkk, wv, wg, bg, *, n_lat, T):
    N, D = xall.shape
    tm = TOK_TILE
    nlat_t = n_lat // tm
    per_b = T // tm
    nb = n_lat // T

    def midx(k):
        return lambda i: (jnp.where(i < nlat_t, i // per_b, nb) * 6 + k, 0, 0)

    def tidx(i):
        return (jnp.where(i < nlat_t, i % per_b, per_b), 0)

    const = lambda i: (0, 0)
    tab_spec = pl.BlockSpec((tm, 128), tidx)
    in_specs = [pl.BlockSpec((tm, D), lambda i: (i, 0)),
                pl.BlockSpec((1, 1, D), midx(0)), pl.BlockSpec((1, 1, D), midx(1)),
                pl.BlockSpec((D, PW), const)] + [tab_spec] * 6 + [
                pl.BlockSpec((1, 256), const), pl.BlockSpec((1, 128), const),
                pl.BlockSpec((256, 256), const), pl.BlockSpec((256, 256), const),
                pl.BlockSpec((128, 256), const), pl.BlockSpec((128, 256), const),
                pl.BlockSpec((1, 256), const)]
    out_specs = [pl.BlockSpec((tm, 768), lambda i: (i, 0)), pl.BlockSpec((tm, 512), lambda i: (i, 0)),
                 pl.BlockSpec((tm, 768), lambda i: (i, 0)), pl.BlockSpec((tm, 1024), lambda i: (i, 0))]
    out_shape = [jax.ShapeDtypeStruct((N, 768), BF16), jax.ShapeDtypeStruct((N, 512), BF16),
                 jax.ShapeDtypeStruct((N, 768), BF16), jax.ShapeDtypeStruct((N, 1024), F32)]
    return pl.pallas_call(
        _inproj_kernel, grid=(N // tm,), in_specs=in_specs, out_specs=out_specs, out_shape=out_shape,
        compiler_params=_cparams(("arbitrary",)), name="inproj",
    )(xall, mods_l, mods_l, win, *tabs, gq, gkv, wuq, wkk, wv, wg, bg)


def _softmax_av(parts, extra_logit=None):
    m = functools.reduce(jnp.maximum, [jnp.max(s, -1, keepdims=True) for s, _ in parts])
    if extra_logit is not None:
        m = jnp.maximum(m, extra_logit)
    l = 0.0 if extra_logit is None else jnp.exp2(extra_logit - m)
    o = None
    for s, v in parts:
        e = jnp.exp2(s - m)
        l = l + jnp.sum(e, -1, keepdims=True)
        c = jnp.dot(e.astype(BF16), v, preferred_element_type=F32)
        o = c if o is None else o + c
    return o * (1.0 / l)


def _na_kernel(q_ref, k_ref, v_ref, kc_ref, vc_ref, bias_ref, o_ref, *, rows):
    j = pl.program_id(1)
    ws = pl.multiple_of(jnp.clip(8 * j - 4, 0, rows - 16) * GRID_W, 256)
    win = 16 * GRID_W
    outs = []
    for h in range(N_HEADS):
        sl = slice(HEAD_DIM * h, HEAD_DIM * (h + 1))
        q = q_ref[:, sl]
        s = lax.dot_general(q, k_ref[pl.ds(ws, win), sl], _NT, preferred_element_type=F32) + bias_ref[0, h]
        sc = lax.dot_general(q, kc_ref[:, sl], _NT, preferred_element_type=F32)
        outs.append(_softmax_av([(s, v_ref[pl.ds(ws, win), sl]), (sc, vc_ref[:, sl])]))
    o_ref[...] = jnp.concatenate(outs, -1).astype(BF16)


def _na_bias_tables(rpb, rows):
    nj = rows // 8
    H = rpb.shape[0]
    masked = 2 * NA_KH - 1
    qc, kc = np.arange(GRID_W)[:, None], np.arange(GRID_W)[None, :]
    cs = np.clip(qc - NA_KW // 2, 0, GRID_W - NA_KW)
    col_ok = (kc >= cs) & (kc < cs + NA_KW)
    co = np.clip(kc - qc + (NA_KW - 1), 0, 2 * NA_KW - 2)
    col_hot = (co[..., None] == np.arange(2 * NA_KW - 1)).astype(np.float32)
    cb = jnp.einsum('hdo,qko->hdqk', rpb, col_hot, precision=lax.Precision.HIGHEST)
    blocks = jnp.concatenate([jnp.where(col_ok, cb * LOG2E, NEG), jnp.full((H, 1, GRID_W, GRID_W), NEG, F32)], 1)
    which = []
    for j in (0, 1, nj - 1):
        ws = int(np.clip(8 * j - 4, 0, rows - 16))
        r = 8 * j + np.arange(8)[:, None]
        kr = ws + np.arange(16)[None, :]
        rs = np.clip(r - NA_KH // 2, 0, rows - NA_KH)
        which.append(np.where((kr >= rs) & (kr < rs + NA_KH), kr - r + (NA_KH - 1), masked))
    t = jnp.take(blocks, np.stack(which).reshape(-1), axis=1)
    t = t.reshape(H, 3, 8, 16, GRID_W, GRID_W).transpose(1, 0, 2, 4, 3, 5)
    return t.reshape(3, H, 512, 1024)


def _na_attention(pa, bias, *, B, T, C):
    rows = T // GRID_W
    nj = T // ATT_TQ
    cb = B * T // C
    kern = functools.partial(_na_kernel, rows=rows)
    return pl.pallas_call(
        kern, grid=(B, nj),
        in_specs=[pl.BlockSpec((ATT_TQ, 256), lambda b, j: (b * nj + j, 0)),
                  pl.BlockSpec((T, 256), lambda b, j: (b, 1)),
                  pl.BlockSpec((T, 256), lambda b, j: (b, 2)),
                  pl.BlockSpec((C, 256), lambda b, j: (cb + b, 1)),
                  pl.BlockSpec((C, 256), lambda b, j: (cb + b, 2)),
                  pl.BlockSpec((1, N_HEADS, 512, 1024),
                               lambda b, j: (jnp.where(j == 0, 0, jnp.where(j == nj - 1, 2, 1)), 0, 0, 0))],
        out_specs=pl.BlockSpec((ATT_TQ, 256), lambda b, j: (b * nj + j, 0)),
        out_shape=jax.ShapeDtypeStruct((B * T, 256), BF16),
        compiler_params=_cparams(("arbitrary", "arbitrary")), name="na_attention",
    )(pa, pa, pa, pa, pa, bias)


def _wa_kernel(sink_ref, q_ref, k_ref, v_ref, kc_ref, vc_ref, o_ref, *, T):
    i = pl.program_id(1)
    win = WA_SUB + 2 * WA_WINDOW
    for sub in range(ATT_TQ // WA_SUB):
        rows = slice(sub * WA_SUB, (sub + 1) * WA_SUB)
        start = i * ATT_TQ + sub * WA_SUB
        ws = pl.multiple_of(jnp.clip(start - WA_WINDOW, 0, T - win), 128)
        rel = ((lax.broadcasted_iota(I32, (WA_SUB, win), 1) + (ws - start))
               - lax.broadcasted_iota(I32, (WA_SUB, win), 0))
        valid = jnp.abs(rel) <= WA_WINDOW
        outs = []
        for h in range(N_HEADS):
            g = h // 2
            ksl = slice(HEAD_DIM * g, HEAD_DIM * (g + 1))
            q = q_ref[rows, HEAD_DIM * h:HEAD_DIM * (h + 1)]
            s = lax.dot_general(q, k_ref[pl.ds(ws, win), ksl], _NT, preferred_element_type=F32)
            s = jnp.where(valid, s, NEG)
            sc = lax.dot_general(q, kc_ref[:, ksl], _NT, preferred_element_type=F32)
            outs.append(_softmax_av([(s, v_ref[pl.ds(ws, win), ksl]), (sc, vc_ref[:, ksl])],
                                    extra_logit=sink_ref[h]))
        o_ref[rows, :] = jnp.concatenate(outs, -1).astype(BF16)


def _wa_attention(pb, sink, *, B, T, C):
    nj = T // ATT_TQ
    cb = B * T // C
    kern = functools.partial(_wa_kernel, T=T)
    return pl.pallas_call(
        kern, grid=(B, nj),
        in_specs=[pl.BlockSpec(memory_space=pltpu.SMEM),
                  pl.BlockSpec((ATT_TQ, 256), lambda b, j: (b * nj + j, 0)),
                  pl.BlockSpec((T, 128), lambda b, j: (b, 2)),
                  pl.BlockSpec((T, 128), lambda b, j: (b, 3)),
                  pl.BlockSpec((C, 128), lambda b, j: (cb + b, 2)),
                  pl.BlockSpec((C, 128), lambda b, j: (cb + b, 3))],
        out_specs=pl.BlockSpec((ATT_TQ, 256), lambda b, j: (b * nj + j, 0)),
        out_shape=jax.ShapeDtypeStruct((B * T, 256), BF16),
        compiler_params=_cparams(("arbitrary", "arbitrary")), name="wa_attention",
    )(sink, pb, pb, pb, pb, pb)


def _mla_kernel(q_ref, k_ref, v_ref, kc_ref, vc_ref, o_ref, *, T):
    nk = T // MLA_TK
    heads = [slice(HEAD_DIM * h, HEAD_DIM * (h + 1)) for h in range(N_HEADS)]
    qs = [q_ref[:, sl] for sl in heads]
    init = []
    for q, sl in zip(qs, heads):
        sc = lax.dot_general(q, kc_ref[:, sl], _NT, preferred_element_type=F32)
        m0 = jnp.max(sc, -1, keepdims=True)
        e0 = jnp.exp2(sc - m0)
        init += [m0, jnp.sum(e0, -1, keepdims=True),
                 jnp.dot(e0.astype(BF16), vc_ref[:, sl], preferred_element_type=F32)]

    def body(c, carry):
        ks = pl.multiple_of(c * MLA_TK, MLA_TK)
        out = []
        for h, (q, sl) in enumerate(zip(qs, heads)):
            m, l, acc = carry[3 * h:3 * h + 3]
            s = lax.dot_general(q, k_ref[pl.ds(ks, MLA_TK), sl], _NT, preferred_element_type=F32)
            mn = jnp.maximum(m, jnp.max(s, -1, keepdims=True))
            a = jnp.exp2(m - mn)
            e = jnp.exp2(s - mn)
            l = l * a + jnp.sum(e, -1, keepdims=True)
            acc = acc * a + jnp.dot(e.astype(BF16), v_ref[pl.ds(ks, MLA_TK), sl], preferred_element_type=F32)
            out += [mn, l, acc]
        return tuple(out)

    fin = lax.fori_loop(0, nk, body, tuple(init), unroll=2)
    outs = [fin[3 * h + 2] * (1.0 / fin[3 * h + 1]) for h in range(N_HEADS)]
    o_ref[...] = jnp.concatenate(outs, -1).astype(BF16)


def _mla_attention(pc, *, B, T, C):
    nj = T // MLA_TQ
    cb = B * T // C
    kern = functools.partial(_mla_kernel, T=T)
    return pl.pallas_call(
        kern, grid=(B, nj),
        in_specs=[pl.BlockSpec((MLA_TQ, 256), lambda b, j: (b * nj + j, 0)),
                  pl.BlockSpec((T, 256), lambda b, j: (b, 1)),
                  pl.BlockSpec((T, 256), lambda b, j: (b, 2)),
                  pl.BlockSpec((C, 256), lambda b, j: (cb + b, 1)),
                  pl.BlockSpec((C, 256), lambda b, j: (cb + b, 2))],
        out_specs=pl.BlockSpec((MLA_TQ, 256), lambda b, j: (b * nj + j, 0)),
        out_shape=jax.ShapeDtypeStruct((B * T, 256), BF16),
        compiler_params=_cparams(("arbitrary", "arbitrary")), name="mla_attention",
    )(pc, pc, pc, pc, pc)


def _ctx_kernel(sink_ref, pa_ref, pb_ref, pc_ref, oa_ref, ob_ref, oc_ref):
    def attend(p_ref, koff, voff, kv_heads, out_ref, sink):
        outs = []
        for h in range(N_HEADS):
            g = h * kv_heads // N_HEADS
            q = p_ref[:, HEAD_DIM * h:HEAD_DIM * (h + 1)]
            k = p_ref[:, koff + HEAD_DIM * g:koff + HEAD_DIM * (g + 1)]
            v = p_ref[:, voff + HEAD_DIM * g:voff + HEAD_DIM * (g + 1)]
            s = lax.dot_general(q, k, _NT, preferred_element_type=F32)
            outs.append(_softmax_av([(s, v)], extra_logit=sink_ref[h] if sink else None))
        out_ref[...] = jnp.concatenate(outs, -1).astype(BF16)

    attend(pa_ref, 256, 512, 4, oa_ref, False)
    attend(pb_ref, 256, 384, 2, ob_ref, True)
    attend(pc_ref, 256, 512, 4, oc_ref, False)


def _ctx_attention(sink, pa, pb, pc, *, B, T, C):
    cb = B * T // C
    row = lambda b: (cb + b, 0)
    return pl.pallas_call(
        _ctx_kernel, grid=(B,),
        in_specs=[pl.BlockSpec(memory_space=pltpu.SMEM),
                  pl.BlockSpec((C, 768), row), pl.BlockSpec((C, 512), row), pl.BlockSpec((C, 768), row)],
        out_specs=[pl.BlockSpec((C, 256), lambda b: (b, 0))] * 3,
        out_shape=[jax.ShapeDtypeStruct((B * C, 256), BF16)] * 3,
        compiler_params=_cparams(("arbitrary",)), name="ctx_attention",
    )(sink, pa, pb, pc)


def _gla_dir(pd_ref, o_ref, st_ref, reverse):
    nchunk = GLA_BLOCK // GLA_CHUNK
    L, DK, DV = GLA_CHUNK, N_HEADS * GLA_DK, N_HEADS * GLA_DV

    def iota(shape, dim):
        return lax.broadcasted_iota(I32, shape, dim)

    ri, ci = iota((GLA_BLOCK, GLA_BLOCK), 0), iota((GLA_BLOCK, GLA_BLOCK), 1)
    tri = jnp.where((ri // L == ci // L) & ((ci >= ri) if reverse else (ci <= ri)), 1.0, 0.0).astype(BF16)
    kcol, qrow = iota((L, N_HEADS * L), 1) % L, iota((L, N_HEADS * L), 0)
    keep = (kcol >= qrow) if reverse else (kcol <= qrow)
    k_owner = [iota((L, DK), 1) // GLA_DK == h for h in range(N_HEADS)]
    v_owner = [iota((L, DV), 1) // GLA_DV == h for h in range(N_HEADS)]
    s_owner = iota((DK, DV), 0) // GLA_DK == iota((DK, DV), 1) // GLA_DV
    gcol = 896 if reverse else 768
    end = 0 if reverse else L - 1
    order = range(nchunk - 1, -1, -1) if reverse else range(nchunk)

    def dot3(a_parts, b_parts):
        return sum(jnp.dot(a, b, preferred_element_type=F32) for a in a_parts for b in b_parts)

    g_all = pd_ref[:, gcol:gcol + DK]
    b_all = dot3([tri], _split3(g_all))
    per_chunk = []
    for c in order:
        rs = slice(c * L, (c + 1) * L)
        q = pd_ref[rs, 0:DK]
        k = pd_ref[rs, DK:2 * DK]
        vb = pd_ref[rs, 2 * DK:2 * DK + DV].astype(BF16)
        b = b_all[rs]
        bend = b[end:end + 1, :]
        qd = (q * jnp.exp(b)).astype(BF16)
        ki = (k * jnp.exp(-b)).astype(BF16)
        kdt = (k * jnp.exp(bend - b)).T.astype(BF16)
        dec = jnp.exp(jnp.broadcast_to(bend, (8, DK)).T[:, 0:1])
        kbd = jnp.concatenate([jnp.where(m, ki, jnp.zeros_like(ki)) for m in k_owner], 0)
        vbd = jnp.concatenate([jnp.where(m, vb, jnp.zeros_like(vb)) for m in v_owner], 0)
        att = lax.dot_general(qd, kbd, _NT, preferred_element_type=F32)
        att = jnp.where(keep, att, 0.0).astype(BF16)
        intra = jnp.dot(att, vbd, preferred_element_type=F32)
        u = jnp.where(s_owner, jnp.dot(kdt, vb, preferred_element_type=F32), 0.0)
        per_chunk.append((rs, qd, intra, u, dec))
    st = st_ref[...]
    for rs, qd, intra, u, dec in per_chunk:
        o_ref[rs, :] = intra + jnp.dot(qd, st.astype(BF16), preferred_element_type=F32)
        st = st * dec + u
    st_ref[...] = st


def _gla_kernel(pdf_ref, pdb_ref, of_ref, ob_ref, stf_ref, stb_ref):
    @pl.when(pl.program_id(1) == 0)
    def _():
        stf_ref[...] = jnp.zeros_like(stf_ref)
        stb_ref[...] = jnp.zeros_like(stb_ref)

    _gla_dir(pdf_ref, of_ref, stf_ref, False)
    _gla_dir(pdb_ref, ob_ref, stb_ref, True)


def _gla(pd, *, B, T, C):
    N = pd.shape[0]
    assert C == GLA_BLOCK
    nlb = T // GLA_BLOCK
    cb = B * T // GLA_BLOCK

    def fwd(b, s):
        return (jnp.where(s == 0, cb + b, b * nlb + s - 1), 0)

    def bwd(b, s):
        return (jnp.where(s == 0, cb + b, b * nlb + nlb - s), 0)

    state = pltpu.VMEM((N_HEADS * GLA_DK, N_HEADS * GLA_DV), F32)
    return pl.pallas_call(
        _gla_kernel, grid=(B, nlb + 1),
        in_specs=[pl.BlockSpec((GLA_BLOCK, 1024), fwd), pl.BlockSpec((GLA_BLOCK, 1024), bwd)],
        out_specs=[pl.BlockSpec((GLA_BLOCK, 256), fwd), pl.BlockSpec((GLA_BLOCK, 256), bwd)],
        out_shape=[jax.ShapeDtypeStruct((N, 256), F32)] * 2,
        scratch_shapes=[state, state],
        compiler_params=_cparams(("arbitrary", "arbitrary")), name="gla_scan",
    )(pd, pd)


def _pack_bf16_pairs(h):
    w = h.shape[-1] // 2
    bits = lax.bitcast_convert_type(h.astype(BF16).astype(F32), U32)
    return (bits[:, w:] & jnp.uint32(0xFFFF0000)) | (bits[:, :w] >> 16)


def _unpack_bf16_pairs(u):
    lo = lax.bitcast_convert_type(u << 16, F32)
    hi = lax.bitcast_convert_type(u & jnp.uint32(0xFFFF0000), F32)
    return jnp.concatenate([lo, hi], -1).astype(BF16)


def _outproj_kernel(x_ref, oa_ref, ob_ref, oc_ref, oac_ref, obc_ref, occ_ref, gf_ref, gb_ref, r_ref, gn_ref, wout_ref,
                    g1_ref, sh2_ref, sc2_ref, lng_ref, lnb_ref, rw1_ref, rw2_ref, rb_ref,
                    x1_ref, hp_ref, sel_ref, gate_ref, rank_ref, count_ref, cnt_ref, *, alpha, nlat_t):
    is_lat = pl.program_id(0) < nlat_t
    pick = lambda lat_ref, ctx_ref: jnp.where(is_lat, lat_ref[...], ctx_ref[...])
    s = gf_ref[...] + gb_ref[...]
    w = s.shape[-1]
    same_head = (lax.broadcasted_iota(I32, (w, w), 0) // GLA_DV) == (lax.broadcasted_iota(I32, (w, w), 1) // GLA_DV)
    avg = jnp.where(same_head, 1.0 / GLA_DV, 0.0).astype(BF16)
    ms = sum(jnp.dot(part, avg, preferred_element_type=F32) for part in _split3(s * s))
    r = r_ref[...]
    od = s * lax.rsqrt(ms + RMS_EPS) * gn_ref[...] * (r * _sigmoid(r))
    ocat = jnp.concatenate([pick(oa_ref, oac_ref), pick(ob_ref, obc_ref), pick(oc_ref, occ_ref), od.astype(BF16)], -1)
    o = jnp.dot(ocat, wout_ref[...], preferred_element_type=F32)
    x1 = _layer_norm(alpha * x_ref[...] + g1_ref[0] * o) * lng_ref[...] + lnb_ref[...]
    x1_ref[...] = x1
    h2 = _layer_norm(x1) * (1.0 + sc2_ref[0]) + sh2_ref[0]
    hp_ref[...] = _pack_bf16_pairs(h2)
    a1, a2, _ = _split3(h2)
    w1, w2 = rw1_ref[...], rw2_ref[...]
    logits = (jnp.dot(a1, w1, preferred_element_type=F32) + jnp.dot(a1, w2, preferred_element_type=F32)
              + jnp.dot(a2, w1, preferred_element_type=F32))
    scores = _sigmoid(logits)
    sc_t = scores.T
    work = (scores + rb_ref[...]).T
    ne, tm = sc_t.shape
    row = lax.broadcasted_iota(I32, (ne, tm), 0).astype(F32)
    chosen = jnp.zeros((ne, tm), F32)
    picks, picked = [], []
    for k in range(TOP_K):
        m = jnp.max(work, 0, keepdims=True)
        idx = jnp.min(jnp.where(work == m, row, float(ne)), 0, keepdims=True)
        hit = row == idx
        picks.append(idx)
        picked.append(jnp.sum(jnp.where(hit, sc_t, 0.0), 0, keepdims=True))
        chosen = jnp.where(hit, 1.0, chosen)
        work = jnp.where(hit, -jnp.inf, work)
    gsel = jnp.concatenate(picked, 0)
    sel_ref[...] = jnp.concatenate(picks, 0).astype(I32)
    gate_ref[...] = ROUTED_SCALE * gsel / jnp.sum(gsel, 0, keepdims=True)
    @pl.when(pl.program_id(0) == 0)
    def _():
        cnt_ref[...] = jnp.zeros_like(cnt_ref)

    earlier = (lax.broadcasted_iota(I32, (tm, tm), 0) < lax.broadcasted_iota(I32, (tm, tm), 1))
    prefix = jnp.dot(chosen.astype(BF16), jnp.where(earlier, 1.0, 0.0).astype(BF16),
                     preferred_element_type=F32) + cnt_ref[...]
    rank_ref[...] = jnp.concatenate(
        [jnp.sum(jnp.where(row == picks[k], prefix, 0.0), 0, keepdims=True) for k in range(TOP_K)], 0).astype(I32)
    cnt_ref[...] = cnt_ref[...] + jnp.sum(chosen, 1, keepdims=True)
    count_ref[...] = cnt_ref[...].astype(I32)


def _outproj(xall, o_lat, o_ctx, ofb, pd, gn, wout, mods_l, lng, lnb, rw1, rw2, rb, *, n_lat, T, alpha):
    N, D = xall.shape
    tm = TOK_TILE
    nlat_t, per_b, nb = n_lat // tm, T // tm, n_lat // T

    def midx(k):
        return lambda i: (jnp.where(i < nlat_t, i // per_b, nb) * 6 + k, 0, 0)

    const = lambda i: (0, 0)
    row = lambda i: (i, 0)
    lat_row = lambda i: (jnp.minimum(i, nlat_t - 1), 0)
    ctx_row = lambda i: (jnp.maximum(i - nlat_t, 0), 0)
    E = rw1.shape[1]
    kern = functools.partial(_outproj_kernel, alpha=alpha, nlat_t=nlat_t)
    return pl.pallas_call(
        kern, grid=(N // tm,),
        in_specs=[pl.BlockSpec((tm, D), row)] + [pl.BlockSpec((tm, 256), lat_row)] * 3
                 + [pl.BlockSpec((tm, 256), ctx_row)] * 3 + [
                  pl.BlockSpec((tm, 256), row), pl.BlockSpec((tm, 256), row),
                  pl.BlockSpec((tm, 256), lambda i: (i, 2)), pl.BlockSpec((1, 256), const),
                  pl.BlockSpec((D, D), const),
                  pl.BlockSpec((1, 1, D), midx(2)), pl.BlockSpec((1, 1, D), midx(3)), pl.BlockSpec((1, 1, D), midx(4)),
                  pl.BlockSpec((1, D), const), pl.BlockSpec((1, D), const),
                  pl.BlockSpec((D, E), const), pl.BlockSpec((D, E), const), pl.BlockSpec((1, E), const)],
        out_specs=[pl.BlockSpec((tm, D), row), pl.BlockSpec((tm, D // 2), row)]
                  + [pl.BlockSpec((TOP_K, tm), lambda i: (0, i))] * 3 + [pl.BlockSpec((E, 1), const)],
        out_shape=[jax.ShapeDtypeStruct((N, D), F32), jax.ShapeDtypeStruct((N, D // 2), U32),
                   jax.ShapeDtypeStruct((TOP_K, N), I32), jax.ShapeDtypeStruct((TOP_K, N), F32),
                   jax.ShapeDtypeStruct((TOP_K, N), I32), jax.ShapeDtypeStruct((E, 1), I32)],
        scratch_shapes=[pltpu.VMEM((E, 1), F32)],
        compiler_params=_cparams(("arbitrary",)), name="outproj_router",
    )(xall, *o_lat, *o_ctx, *ofb, pd, gn, wout, mods_l, mods_l, mods_l, lng, lnb, rw1, rw2, rb)


def _sc_mesh():
    return plsc.VectorSubcoreMesh(core_axis_name="c", subcore_axis_name="s")


def _sc_worker():
    return lax.axis_index("s") * SC_CORES + lax.axis_index("c")


def _sc_dispatch(hp, dest_k, rows):
    N, W = hp.shape
    tpw = N // SC_WORKERS
    ch = SC_DISPATCH_CHUNK
    assert N % SC_WORKERS == 0 and tpw % ch == 0
    nch = tpw // ch
    idx = dest_k.reshape(TOP_K, SC_WORKERS, nch, ch).transpose(1, 2, 0, 3)

    @functools.partial(
        pl.kernel, mesh=_sc_mesh(), out_type=jax.ShapeDtypeStruct((rows, W), hp.dtype),
        scratch_types=[pltpu.VMEM((nch, TOP_K, ch), I32), pltpu.VMEM((2, ch, W), hp.dtype),
                       pltpu.SemaphoreType.DMA, pltpu.SemaphoreType.DMA],
        compiler_params=pltpu.CompilerParams(use_tc_tiling_on_sc=True), name="sc_dispatch")
    def scatter(hp_hbm, idx_hbm, out_hbm, idx_v, rows_v, rsem, wsem):
        wid = _sc_worker()
        base = wid * tpw
        pltpu.sync_copy(idx_hbm.at[wid], idx_v)

        def read(c, slot):
            return pltpu.make_async_copy(hp_hbm.at[pl.ds(base + c * ch, ch)], rows_v.at[slot], rsem)

        def writes(c, slot):
            return [pltpu.make_async_copy(rows_v.at[slot], out_hbm.at[idx_v.at[c, k]], wsem) for k in range(TOP_K)]

        def step(c, slot):
            read(c, slot).wait()

            @pl.when(c > 0)
            def _():
                for cp in writes(c - 1, 1 - slot):
                    cp.wait()

            @pl.when(c + 1 < nch)
            def _():
                read(c + 1, 1 - slot).start()

            for cp in writes(c, slot):
                cp.start()

        read(0, 0).start()

        @pl.loop(0, nch // 2)
        def _(p):
            step(2 * p, 0)
            step(2 * p + 1, 1)

        if nch % 2:
            step(jnp.int32(nch - 1), 0)
        for cp in writes(nch - 1, (nch - 1) % 2):
            cp.wait()

    return scatter(hp, idx)


def _sc_gather(src, idx):
    R, W = idx.shape[0], src.shape[1]
    per_w = R // SC_WORKERS
    ch = SC_GATHER_CHUNK
    assert R % SC_WORKERS == 0 and per_w % ch == 0
    nch = per_w // ch

    @functools.partial(
        pl.kernel, mesh=_sc_mesh(), out_type=jax.ShapeDtypeStruct((R, W), src.dtype),
        scratch_types=[pltpu.VMEM((nch, ch), I32), pltpu.VMEM((ch, W), src.dtype), pltpu.SemaphoreType.DMA],
        compiler_params=pltpu.CompilerParams(use_tc_tiling_on_sc=True), name="sc_gather")
    def gather(src_hbm, idx_hbm, out_hbm, idx_v, rows_v, sem):
        wid = _sc_worker()
        base = wid * per_w
        pltpu.sync_copy(idx_hbm.at[wid], idx_v)

        @pl.loop(0, nch)
        def _(c):
            pltpu.async_copy(src_hbm.at[idx_v.at[c]], rows_v, sem).wait()
            pltpu.sync_copy(rows_v, out_hbm.at[pl.ds(base + c * ch, ch)])

    return gather(src, idx.reshape(SC_WORKERS, nch, ch))


def _swiglu(xb, w1, w3, w2):
    a = jnp.dot(xb, w1, preferred_element_type=F32)
    b = jnp.dot(xb, w3, preferred_element_type=F32)
    return jnp.dot((a * _sigmoid(a) * b).astype(BF16), w2, preferred_element_type=F32)


def _expert_kernel(be_ref, nu_ref, xs_hbm, w1_ref, w3_ref, w2_ref, y_ref, w1b_ref, w3b_ref, w2b_ref, xbuf, sem):
    i = pl.program_id(0)
    n_used = nu_ref[0]
    used = i < n_used

    def fetch(j):
        slot = j % MOE_RING
        return pltpu.make_async_copy(xs_hbm.at[pl.ds(pl.multiple_of(j * MOE_BM, MOE_BM), MOE_BM)],
                                     xbuf.at[slot], sem.at[slot])

    for j in range(MOE_RING - 1):
        @pl.when((i == 0) & (j < n_used))
        def _(j=j):
            fetch(j).start()

    @pl.when(i + (MOE_RING - 1) < n_used)
    def _():
        fetch(i + (MOE_RING - 1)).start()

    @pl.when(used & ((i == 0) | (be_ref[i] != be_ref[jnp.maximum(i - 1, 0)])))
    def _():
        w1b_ref[...] = w1_ref[0, 0].astype(BF16)
        w3b_ref[...] = w3_ref[0, 0].astype(BF16)
        w2b_ref[...] = w2_ref[0, 0].astype(BF16)

    @pl.when(used)
    def _():
        fetch(i).wait()
        xb = _unpack_bf16_pairs(xbuf[i % MOE_RING])
        y_ref[...] = _pack_bf16_pairs(_swiglu(xb, w1b_ref[...], w3b_ref[...], w2b_ref[...]))

    @pl.when(jnp.logical_not(used))
    def _():
        y_ref[...] = jnp.zeros_like(y_ref)


def _experts(xs, blk_exp, n_used, w1, w3, w2, layer):
    rows, half = xs.shape
    _, E, D, F = w1.shape
    nblk = rows // MOE_BM
    wmap = lambda i, be, nu: (layer, be[i], 0, 0)
    grid_spec = pltpu.PrefetchScalarGridSpec(
        num_scalar_prefetch=2, grid=(nblk,),
        in_specs=[pl.BlockSpec(memory_space=pl.ANY),
                  pl.BlockSpec((1, 1, D, F), wmap), pl.BlockSpec((1, 1, D, F), wmap), pl.BlockSpec((1, 1, F, D), wmap)],
        out_specs=pl.BlockSpec((MOE_BM, half), lambda i, be, nu: (jnp.minimum(i, nu[0]), 0)),
        scratch_shapes=[pltpu.VMEM((D, F), BF16), pltpu.VMEM((D, F), BF16), pltpu.VMEM((F, D), BF16),
                        pltpu.VMEM((MOE_RING, MOE_BM, half), U32), pltpu.SemaphoreType.DMA((MOE_RING,))])
    return pl.pallas_call(
        _expert_kernel, grid_spec=grid_spec,
        out_shape=jax.ShapeDtypeStruct((rows, half), U32),
        compiler_params=_cparams(("arbitrary",)), name="expert_ffn",
    )(blk_exp, n_used, xs, w1, w3, w2)


def _combine_kernel(yg_ref, gate_ref, hp_ref, sw1_ref, sw3_ref, sw2_ref, x_ref, g2_ref, lng_ref, lnb_ref,
                    o_ref, *, alpha):
    f = _swiglu(_unpack_bf16_pairs(hp_ref[...]), sw1_ref[...], sw3_ref[...], sw2_ref[...])
    gates = gate_ref[...]
    lo = hi = 0.0
    for k in range(TOP_K):
        u = yg_ref[k]
        g = gates[:, k:k + 1]
        lo = lo + g * lax.bitcast_convert_type(u << 16, F32)
        hi = hi + g * lax.bitcast_convert_type(u & jnp.uint32(0xFFFF0000), F32)
    f = f + jnp.concatenate([lo, hi], -1)
    o_ref[...] = _layer_norm(alpha * x_ref[...] + g2_ref[0] * f) * lng_ref[...] + lnb_ref[...]


def _combine(yg, gates, hp, sw1, sw3, sw2, x1, mods_l, lng, lnb, *, n_lat, T, alpha, n_out):
    D = x1.shape[1]
    N = n_out
    tm = CMB_TILE
    nlat_t, per_b, nb = n_lat // tm, T // tm, n_lat // T
    F = sw1.shape[1]
    const = lambda i: (0, 0)
    row = lambda i: (i, 0)
    kern = functools.partial(_combine_kernel, alpha=alpha)
    return pl.pallas_call(
        kern, grid=(N // tm,),
        in_specs=[pl.BlockSpec((TOP_K, tm, D // 2), lambda i: (0, i, 0)), pl.BlockSpec((tm, TOP_K), row),
                  pl.BlockSpec((tm, D // 2), row),
                  pl.BlockSpec((D, F), const), pl.BlockSpec((D, F), const), pl.BlockSpec((F, D), const),
                  pl.BlockSpec((tm, D), row),
                  pl.BlockSpec((1, 1, D), lambda i: (jnp.where(i < nlat_t, i // per_b, nb) * 6 + 5, 0, 0)),
                  pl.BlockSpec((1, D), const), pl.BlockSpec((1, D), const)],
        out_specs=pl.BlockSpec((tm, D), row),
        out_shape=jax.ShapeDtypeStruct((N, D), F32),
        compiler_params=_cparams(("arbitrary",)), name="moe_combine",
    )(yg, gates, hp, sw1, sw3, sw2, x1, mods_l, lng, lnb)


def _dest_kernel(start_ref, sel_ref, rank_ref, o_ref):
    sel = sel_ref[...]

    def body(e, acc):
        return jnp.where(sel == e, start_ref[e], acc)

    o_ref[...] = lax.fori_loop(0, start_ref.shape[0], body, jnp.zeros_like(sel)) + rank_ref[...]


def _dispatch_plan(sel, rank, counts):
    N = sel.shape[1]
    n_experts = counts.shape[0]
    padded = (counts + MOE_BM - 1) // MOE_BM * MOE_BM
    pad_end = jnp.cumsum(padded)
    pad_start = (pad_end - padded).astype(I32)
    dense = (N * TOP_K // 128, 128)
    dest = pl.pallas_call(
        _dest_kernel,
        in_specs=[pl.BlockSpec(memory_space=pltpu.SMEM), pl.BlockSpec(memory_space=pltpu.VMEM),
                  pl.BlockSpec(memory_space=pltpu.VMEM)],
        out_specs=pl.BlockSpec(memory_space=pltpu.VMEM),
        out_shape=jax.ShapeDtypeStruct(dense, I32), name="moe_dest_rows",
    )(pad_start, sel.reshape(dense), rank.reshape(dense))
    dest_k = dest.reshape(TOP_K, N)
    n_blocks = -(-(N * TOP_K) // MOE_BM) + n_experts
    blk_first = jnp.arange(n_blocks, dtype=I32) * MOE_BM
    blk_exp = jnp.minimum(jnp.sum((pad_end[None, :] <= blk_first[:, None]).astype(I32), -1), n_experts - 1)
    n_used = (pad_end[-1] // MOE_BM).astype(I32).reshape(1)
    return dest_k, blk_exp, n_used, n_blocks * MOE_BM


def _rope_tables(T, tile):
    t = jnp.arange(T)
    row = (t // GRID_W).astype(F32)[:, None]
    col = (t % GRID_W).astype(F32)[:, None]

    def group(half):
        inv = ROPE_THETA ** (-jnp.arange(half, dtype=F32) / half)
        ar, ac = row * inv, col * inv
        cos = jnp.concatenate([jnp.cos(ar), jnp.cos(ar), jnp.cos(ac), jnp.cos(ac)], -1)
        sin = jnp.concatenate([-jnp.sin(ar), jnp.sin(ar), -jnp.sin(ac), jnp.sin(ac)], -1)
        return cos, sin

    ones32, zeros32 = jnp.ones((T, 32), F32), jnp.zeros((T, 32), F32)
    cw, sw = group(16)
    cw, sw = jnp.tile(cw, (1, 2)), jnp.tile(sw, (1, 2))
    c8, s8 = group(8)
    cq = jnp.tile(jnp.concatenate([ones32, c8], -1), (1, 2))
    sq = jnp.tile(jnp.concatenate([zeros32, s8], -1), (1, 2))
    ck = jnp.concatenate([c8, ones32, ones32, ones32], -1)
    sk = jnp.concatenate([s8, zeros32, zeros32, zeros32], -1)

    def pad(a, ident):
        return jnp.concatenate([a, jnp.full((tile, 128), ident, F32)], 0)

    return [pad(cw, 1.0), pad(sw, 0.0), pad(cq, 1.0), pad(sq, 0.0), pad(ck, 1.0), pad(sk, 0.0)]


def _layer_weights(w_in, mla_w_uq, mla_w_ukv, gla_w_gf, gla_w_gb, gla_b_gf, gla_b_gb):
    D = w_in.shape[0]
    z = lambda n: jnp.zeros((D, n), F32)
    win = jnp.concatenate([
        w_in[:, 0:256] * Q_SCALE, w_in[:, 256:768],
        w_in[:, 768:1024] * Q_SCALE, w_in[:, 1024:1280],
        w_in[:, 1280:1696], z(96),
        w_in[:, 1696:2464],
        w_in[:, 2464:2496], z(96)], -1).astype(BF16)
    wuq = (mla_w_uq * Q_SCALE).astype(BF16)
    ukv = mla_w_ukv.reshape(-1, N_HEADS, MLA_NOPE + HEAD_DIM)
    kv_rank = ukv.shape[0]
    wk = jnp.concatenate([ukv[:, :, :MLA_NOPE], jnp.zeros((kv_rank, N_HEADS, MLA_ROPE), F32)], -1)
    place = jnp.concatenate([jnp.zeros((MLA_ROPE, MLA_NOPE), F32), jnp.eye(MLA_ROPE, dtype=F32)], -1)
    place = jnp.concatenate([jnp.tile(place, (1, N_HEADS)), jnp.zeros((128 - MLA_ROPE, 256), F32)], 0)
    wkk = jnp.concatenate([wk.reshape(kv_rank, 256), place], 0).astype(BF16)
    wv = ukv[:, :, MLA_NOPE:].reshape(kv_rank, 256).astype(BF16)
    r = GLA_GATE_RANK
    wg = jnp.zeros((128, 256), F32).at[0:r, 0:128].set(gla_w_gf).at[r:2 * r, 128:256].set(gla_w_gb).astype(BF16)
    bg = jnp.concatenate([gla_b_gf, gla_b_gb])[None, :]
    return win, wuq, wkk, wv, wg, bg


def kernel(x, c, ctx, c_ctx, w_ada, b_ada, w_in, na_rpb, wa_sink, mla_g_q, mla_g_kv, mla_w_uq, mla_w_ukv,
           gla_w_gf, gla_b_gf, gla_w_gb, gla_b_gb, gla_g_norm, w_out, ln1_g, ln1_b, ln2_g, ln2_b,
           router_w, router_bias, exp_w1, exp_w3, exp_w2, sh_w1, sh_w3, sh_w2):
    B, T, D = x.shape
    C = ctx.shape[1]
    L = w_ada.shape[0]
    E = router_w.shape[-1]
    n_lat = B * T
    alpha = (2 * L) ** 0.25
    dims = dict(B=B, T=T, C=C)

    cc = jnp.zeros((8, D), F32).at[:B].set(c).at[B].set(c_ctx)
    mods = _mods(cc, w_ada, b_ada)
    tabs = _rope_tables(T, TOK_TILE)
    xall = jnp.concatenate([x.reshape(n_lat, D), ctx.reshape(B * C, D)], 0)

    for l in range(L):
        mods_l = mods[l].reshape(8 * 6, 1, D)
        win, wuq, wkk, wv, wg, bg = _layer_weights(w_in[l], mla_w_uq[l], mla_w_ukv[l], gla_w_gf[l], gla_w_gb[l],
                                                   gla_b_gf[l], gla_b_gb[l])
        pa, pb, pc, pd = _inproj(xall, mods_l, win, tabs, mla_g_q[l][None], mla_g_kv[l][None], wuq, wkk, wv, wg, bg,
                                 n_lat=n_lat, T=T)
        oa = _na_attention(pa, _na_bias_tables(na_rpb[l], T // GRID_W), **dims)
        sink = wa_sink[l] * LOG2E
        ob = _wa_attention(pb, sink, **dims)
        oc = _mla_attention(pc, **dims)
        o_ctx = _ctx_attention(sink, pa, pb, pc, **dims)
        ofb = _gla(pd, **dims)

        rw1 = router_w[l].astype(BF16)
        rw2 = (router_w[l] - rw1.astype(F32)).astype(BF16)
        n_all = xall.shape[0]
        x1, hp, sel, gates, rank, counts = _outproj(
            xall, (oa, ob, oc), o_ctx, ofb, pd, jnp.tile(gla_g_norm[l], N_HEADS)[None], w_out[l].astype(BF16), mods_l,
            ln1_g[l][None], ln1_b[l][None], rw1, rw2, router_bias[l][None], n_lat=n_lat, T=T, alpha=alpha)

        dest_k, blk_exp, n_used, rows = _dispatch_plan(sel, rank, counts.reshape(E))
        xs = _sc_dispatch(hp, dest_k, rows)
        y = _experts(xs, blk_exp, n_used, exp_w1, exp_w3, exp_w2, l)
        yg = _sc_gather(y, dest_k.reshape(-1)).reshape(TOP_K, n_all, D // 2)
        xall = _combine(yg, gates.T, hp, sh_w1[l].astype(BF16), sh_w3[l].astype(BF16), sh_w2[l].astype(BF16),
                        x1, mods_l, ln2_g[l][None], ln2_b[l][None], n_lat=n_lat, T=T, alpha=alpha,
                        n_out=n_lat if l == L - 1 else n_all)

    return xall.reshape(B, T, D)
```
